```python
import math
import jax, jax.numpy as jnp
from jax import lax
import numpy as np

D_MODEL = 2048
BATCH = 8
SEQ = 2048
DEPTH = 4
DEC_BATCH = 32
DEC_SEQ = 16
PAST_LEN = 1024

CHUNK = 64
N_MIXERS = 3
N_A = (DEPTH + 2) // 3
N_B = (DEPTH + 1) // 3
N_C = DEPTH // 3
N_DENSE = (DEPTH + 1) // 2
N_MOE = DEPTH // 2
PLE_DIM = 256
EPS = 1e-6

FOX_HEADS = 16
FOX_HD = D_MODEL // FOX_HEADS
Q_BLOCK = 128

S5_GROUP = 16
S5_GROUPS = D_MODEL // S5_GROUP
S5_STATE = 64

GDN_QK_HEADS = 16
GDN_V_HEADS = 32
GDN_HD = 128
GDN_KEY_DIM = GDN_QK_HEADS * GDN_HD
GDN_VAL_DIM = GDN_V_HEADS * GDN_HD
GDN_QKV_DIM = 2 * GDN_KEY_DIM + GDN_VAL_DIM
CONV_W = 4

D_FF = 4096
N_EXPERTS = 8
TOP_K = 2
D_FF_EXPERT = 1024

kernel_name = 'streaming_hybrid_fox_s5_gdn'

F32 = jnp.float32


def rmsnorm(x, g):
    xf = x.astype(F32)
    y = xf * lax.rsqrt(jnp.mean(xf * xf, axis=-1, keepdims=True) + EPS)
    return (y * g.astype(F32)).astype(x.dtype)


def l2norm(x):
    xf = x.astype(F32)
    return xf * lax.rsqrt(jnp.sum(xf * xf, axis=-1, keepdims=True) + EPS)


def fox_attend(q, k, v, fq, fk, qpos, kpos):
    s = jnp.einsum('bqhd,bkhd->bhqk', q, k).astype(F32) * (FOX_HD ** -0.5)
    s = s + jnp.swapaxes(fq, 1, 2)[..., :, None] - jnp.swapaxes(fk, 1, 2)[..., None, :]
    s = jnp.where(qpos[:, None] >= kpos[None, :], s, -jnp.inf)
    p = jax.nn.softmax(s, axis=-1)
    return jnp.einsum('bhqk,bkhd->bqhd', p.astype(v.dtype), v)


def fox_mixer(u, past, w_in, b_f, w_out):
    B_, L, _ = u.shape
    proj = u @ w_in
    q = proj[..., :D_MODEL].reshape(B_, L, FOX_HEADS, FOX_HD)
    k = proj[..., D_MODEL:2 * D_MODEL].reshape(B_, L, FOX_HEADS, FOX_HD)
    v = proj[..., 2 * D_MODEL:3 * D_MODEL].reshape(B_, L, FOX_HEADS, FOX_HD)
    logf = jax.nn.log_sigmoid(proj[..., 3 * D_MODEL:].astype(F32) + b_f.astype(F32))
    if past is None:
        k_all, v_all, logf_all = k, v, logf
    else:
        k_past, v_past, logf_past = past
        k_all = jnp.concatenate([k_past.astype(k.dtype), k], axis=1)
        v_all = jnp.concatenate([v_past.astype(v.dtype), v], axis=1)
        logf_all = jnp.concatenate([logf_past.astype(F32), logf], axis=1)
    P = k_all.shape[1] - L
    f_all = jnp.cumsum(logf_all, axis=1)
    f_q = f_all[:, P:]
    kpos = jnp.arange(P + L)
    qpos = P + jnp.arange(L)
    bq = min(Q_BLOCK, L)
    nb = L // bq

    def block(args):
        qb, fqb, qposb = args
        return fox_attend(qb, k_all, v_all, fqb, f_all, qposb, kpos)

    o = lax.map(block, (jnp.swapaxes(q.reshape(B_, nb, bq, FOX_HEADS, FOX_HD), 0, 1),
                        jnp.swapaxes(f_q.reshape(B_, nb, bq, FOX_HEADS), 0, 1),
                        qpos.reshape(nb, bq)))
    o = jnp.swapaxes(o, 0, 1).reshape(B_, L, D_MODEL)
    return o @ w_out, k, v, logf


def s5_combine(e1, e2):
    a1, b1 = e1
    a2, b2 = e2
    return a2 * a1, a2 * b1 + b2


def s5_mixer(u, h0, a_re, a_im, b_re, b_im, c_re, c_im, d_skip, log_dt, w_glu):
    B_, L, _ = u.shape
    a = lax.complex(a_re.astype(F32), a_im.astype(F32))
    dt = jnp.exp(log_dt.astype(F32))[:, None]
    a_bar = jnp.exp(a * dt)
    b_bar = ((a_bar - 1.0) / a)[..., None] * lax.complex(b_re.astype(F32), b_im.astype(F32))
    c_mat = lax.complex(c_re.astype(F32), c_im.astype(F32))
    uf = u.astype(F32)
    bu = jnp.einsum('blgc,gnc->blgn', uf.reshape(B_, L, S5_GROUPS, S5_GROUP).astype(jnp.complex64), b_bar)
    c = min(CHUNK, L)
    nc = L // c
    bu = bu.reshape(B_, nc, c, S5_GROUPS, S5_STATE).transpose(1, 2, 0, 3, 4)

    def step(h, bu_n):
        a_n = jnp.broadcast_to(a_bar, bu_n.shape)
        a_cum, b_cum = lax.associative_scan(s5_combine, (a_n, bu_n), axis=0)
        states = a_cum * h[None] + b_cum
        y_n = jnp.einsum('gcn,tbgn->tbgc', c_mat, states).real
        return states[-1], y_n

    h_last, y = lax.scan(step, h0, bu)
    y = y.transpose(2, 0, 1, 3, 4).reshape(B_, L, D_MODEL) + d_skip.astype(F32) * uf
    y = jax.nn.gelu(y).astype(u.dtype)
    z = y @ w_glu
    return z[..., :D_MODEL] * jax.nn.sigmoid(z[..., D_MODEL:]), h_last


def causal_conv(x, buf, w):
    L = x.shape[1]
    xp = jnp.concatenate([buf.astype(x.dtype), x], axis=1)
    y = xp[:, 0:L] * w[0]
    for j in range(1, CONV_W):
        y = y + xp[:, j:j + L] * w[j]
    return y, xp[:, -(CONV_W - 1):]


def gated_delta_chunked(q, k, v, g, beta, s0):
    B_, L, H, DK = k.shape
    DV = v.shape[-1]
    c = min(CHUNK, L)
    nc = L // c

    def heads_chunks(t):
        t = jnp.moveaxis(t, 2, 1)
        return t.reshape((B_, H, nc, c) + t.shape[3:])

    qc = heads_chunks(q) * (DK ** -0.5)
    kc = heads_chunks(k)
    vc = heads_chunks(v.astype(F32))
    gcum = jnp.cumsum(heads_chunks(g), axis=-1)
    bc = heads_chunks(beta)
    idx = jnp.arange(c)
    lower = idx[:, None] >= idx[None, :]
    strict = idx[:, None] > idx[None, :]
    decay = jnp.exp(jnp.where(lower, gcum[..., :, None] - gcum[..., None, :], -jnp.inf))
    kk = jnp.einsum('bhnid,bhnjd->bhnij', kc, kc)
    t_mat = jnp.where(strict, bc[..., :, None] * kk * decay, 0.0) + jnp.eye(c, dtype=F32)
    u_vals = lax.linalg.triangular_solve(t_mat, vc * bc[..., None], left_side=True, lower=True, unit_diagonal=True)
    w_keys = lax.linalg.triangular_solve(t_mat, kc * (bc * jnp.exp(gcum))[..., None], left_side=True, lower=True, unit_diagonal=True)
    qk = jnp.einsum('bhnid,bhnjd->bhnij', qc, kc) * decay
    q_dec = qc * jnp.exp(gcum)[..., None]
    k_dec = kc * jnp.exp(gcum[..., -1:] - gcum)[..., None]
    g_tot = jnp.exp(gcum[..., -1])
    xs = tuple(jnp.moveaxis(t, 2, 0) for t in (u_vals, w_keys, qk, q_dec, k_dec, g_tot))

    def step(s, inp):
        u_n, w_n, qk_n, qd_n, kd_n, gt_n = inp
        v_new = u_n - jnp.einsum('bhcd,bhde->bhce', w_n, s)
        o_n = jnp.einsum('bhcd,bhde->bhce', qd_n, s) + jnp.einsum('bhij,bhje->bhie', qk_n, v_new)
        s = s * gt_n[..., None, None] + jnp.einsum('bhcd,bhce->bhde', kd_n, v_new)
        return s, o_n

    s_last, o = lax.scan(step, s0.astype(F32), xs)
    o = o.transpose(1, 0, 3, 2, 4).reshape(B_, L, H, DV)
    return o, s_last


def gdn_mixer(u, s0, conv_buf, w_in, conv_w, a_log, dt_bias, norm_g, w_out):
    B_, L, _ = u.shape
    proj = u @ w_in
    o1 = GDN_QKV_DIM
    o2 = o1 + GDN_VAL_DIM
    o3 = o2 + GDN_V_HEADS
    qkv = proj[..., :o1]
    z = proj[..., o1:o2].reshape(B_, L, GDN_V_HEADS, GDN_HD)
    b_in = proj[..., o2:o3]
    a_in = proj[..., o3:]
    qkv_c, new_buf = causal_conv(qkv, conv_buf, conv_w)
    qkv_c = jax.nn.silu(qkv_c)
    rep = GDN_V_HEADS // GDN_QK_HEADS
    q = jnp.repeat(l2norm(qkv_c[..., :GDN_KEY_DIM].reshape(B_, L, GDN_QK_HEADS, GDN_HD)), rep, axis=2)
    k = jnp.repeat(l2norm(qkv_c[..., GDN_KEY_DIM:2 * GDN_KEY_DIM].reshape(B_, L, GDN_QK_HEADS, GDN_HD)), rep, axis=2)
    v = qkv_c[..., 2 * GDN_KEY_DIM:].reshape(B_, L, GDN_V_HEADS, GDN_HD)
    beta = jax.nn.sigmoid(b_in.astype(F32))
    g = -jnp.exp(a_log.astype(F32)) * jax.nn.softplus(a_in.astype(F32) + dt_bias.astype(F32))
    o, s_last = gated_delta_chunked(q, k, v, g, beta, s0)
    o = rmsnorm(o, norm_g) * jax.nn.silu(z.astype(F32))
    return o.reshape(B_, L, GDN_VAL_DIM).astype(u.dtype) @ w_out, s_last, new_buf


def swiglu(x, w_gate, w_up, w_down):
    return (jax.nn.silu(x @ w_gate) * (x @ w_up)) @ w_down


def moe_swiglu(x, w_router, w_gate, w_up, w_down):
    B_, L, D = x.shape
    t = x.reshape(B_ * L, D)
    logits = (t @ w_router).astype(F32)
    top_vals, top_idx = lax.top_k(logits, TOP_K)
    gates = jax.nn.softmax(top_vals, axis=-1)
    comb = jnp.sum(jax.nn.one_hot(top_idx, N_EXPERTS, dtype=F32) * gates[..., None], axis=1)
    h = jax.nn.silu(jnp.einsum('td,edf->tef', t, w_gate)) * jnp.einsum('td,edf->tef', t, w_up)
    h = h * comb[..., None].astype(h.dtype)
    return jnp.einsum('tef,efd->td', h, w_down).reshape(B_, L, D)


def per_layer_embed(h, p_i, g_ple, w_gate, w_up):
    gate = jax.nn.sigmoid(rmsnorm(h, g_ple) @ w_gate)
    return (p_i @ w_up) * gate


def trunk(x, p, fox_k, fox_v, fox_logf, s5_re, s5_im, gdn_s, gdn_conv,
          norm_mix, norm_ffn, norm_ple, norm_final,
          fox_w_in, fox_b_f, fox_w_out,
          s5_a_re, s5_a_im, s5_b_re, s5_b_im, s5_c_re, s5_c_im, s5_d, s5_log_dt, s5_w_glu,
          gdn_w_in, gdn_conv_w, gdn_a_log, gdn_dt_bias, gdn_norm, gdn_w_out,
          ffn_w_gate, ffn_w_up, ffn_w_down,
          moe_w_router, moe_w_gate, moe_w_up, moe_w_down,
          ple_w_up, ple_w_gate):
    B_ = x.shape[0]
    nk, nv, nlf, nre, nim, ns, nconv = [], [], [], [], [], [], []
    for i in range(DEPTH):
        j = i // N_MIXERS
        u = rmsnorm(x, norm_mix[i])
        if i % N_MIXERS == 0:
            past = None if fox_k is None else (fox_k[j], fox_v[j], fox_logf[j])
            y, k_new, v_new, lf_new = fox_mixer(u, past, fox_w_in[j], fox_b_f[j], fox_w_out[j])
            nk.append(k_new)
            nv.append(v_new)
            nlf.append(lf_new)
        elif i % N_MIXERS == 1:
            if s5_re is None:
                h0 = jnp.zeros((B_, S5_GROUPS, S5_STATE), jnp.complex64)
            else:
                h0 = lax.complex(s5_re[j].astype(F32), s5_im[j].astype(F32))
            y, h_last = s5_mixer(u, h0, s5_a_re[j], s5_a_im[j], s5_b_re[j], s5_b_im[j],
                                 s5_c_re[j], s5_c_im[j], s5_d[j], s5_log_dt[j], s5_w_glu[j])
            nre.append(h_last.real)
            nim.append(h_last.imag)
        else:
            if gdn_s is None:
                s0 = jnp.zeros((B_, GDN_V_HEADS, GDN_HD, GDN_HD), F32)
                buf = jnp.zeros((B_, CONV_W - 1, GDN_QKV_DIM), x.dtype)
            else:
                s0 = gdn_s[j]
                buf = gdn_conv[j]
            y, s_last, new_buf = gdn_mixer(u, s0, buf, gdn_w_in[j], gdn_conv_w[j], gdn_a_log[j],
                                           gdn_dt_bias[j], gdn_norm[j], gdn_w_out[j])
            ns.append(s_last)
            nconv.append(new_buf)
        x = x + y
        hn = rmsnorm(x, norm_ffn[i])
        if i % 2 == 0:
            f = swiglu(hn, ffn_w_gate[i // 2], ffn_w_up[i // 2], ffn_w_down[i // 2])
        else:
            f = moe_swiglu(hn, moe_w_router[i // 2], moe_w_gate[i // 2], moe_w_up[i // 2], moe_w_down[i // 2])
        x = x + f
        x = x + per_layer_embed(x, p[i], norm_ple[i], ple_w_gate[i], ple_w_up[i])
    x = rmsnorm(x, norm_final)
    return (x, jnp.stack(nk), jnp.stack(nv), jnp.stack(nlf), jnp.stack(nre), jnp.stack(nim),
            jnp.stack(ns), jnp.stack(nconv))


def setup_inputs(seed: int = 0) -> dict:
    key = jax.random.key(seed)
    ks = iter(jax.random.split(key, 64))

    def nrm(shape, scale):
        return scale * jax.random.normal(next(ks), shape, F32)

    def unif(shape, lo, hi):
        return jax.random.uniform(next(ks), shape, F32, lo, hi)

    D = D_MODEL
    fox_in_dim = 3 * D + FOX_HEADS
    gdn_in_dim = GDN_QKV_DIM + GDN_VAL_DIM + 2 * GDN_V_HEADS
    dt_gdn = jnp.exp(unif((N_C, GDN_V_HEADS), math.log(1e-3), math.log(1e-1)))
    n_idx = jnp.arange(S5_STATE, dtype=F32)
    return {
        'x_prompt': nrm((BATCH, SEQ, D), 1.0),
        'x_sample': nrm((DEC_BATCH, DEC_SEQ, D), 1.0),
        'cache_fox_k': nrm((N_A, DEC_BATCH, PAST_LEN, FOX_HEADS, FOX_HD), 1.0),
        'cache_fox_v': nrm((N_A, DEC_BATCH, PAST_LEN, FOX_HEADS, FOX_HD), 1.0),
        'cache_fox_logf': jax.nn.log_sigmoid(3.0 + nrm((N_A, DEC_BATCH, PAST_LEN, FOX_HEADS), 1.0)),
        'state_s5_re': nrm((N_B, DEC_BATCH, S5_GROUPS, S5_STATE), 0.5),
        'state_s5_im': nrm((N_B, DEC_BATCH, S5_GROUPS, S5_STATE), 0.5),
        'state_gdn': nrm((N_C, DEC_BATCH, GDN_V_HEADS, GDN_HD, GDN_HD), 0.1),
        'state_gdn_conv': nrm((N_C, DEC_BATCH, CONV_W - 1, GDN_QKV_DIM), 1.0),
        'p_prompt': nrm((DEPTH, BATCH, SEQ, PLE_DIM), 1.0),
        'p_sample': nrm((DEPTH, DEC_BATCH, DEC_SEQ, PLE_DIM), 1.0),
        'norm_mix': 1.0 + nrm((DEPTH, D), 0.02),
        'norm_ffn': 1.0 + nrm((DEPTH, D), 0.02),
        'norm_ple': 1.0 + nrm((DEPTH, D), 0.02),
        'norm_final': 1.0 + nrm((D,), 0.02),
        'fox_w_in': nrm((N_A, D, fox_in_dim), D ** -0.5),
        'fox_b_f': unif((N_A, FOX_HEADS), 2.0, 4.0),
        'fox_w_out': nrm((N_A, D, D), D ** -0.5),
        's5_a_re': -0.5 + nrm((N_B, S5_GROUPS, S5_STATE), 0.01),
        's5_a_im': math.pi * n_idx + nrm((N_B, S5_GROUPS, S5_STATE), 0.01),
        's5_b_re': nrm((N_B, S5_GROUPS, S5_STATE, S5_GROUP), (2 * S5_GROUP) ** -0.5),
        's5_b_im': nrm((N_B, S5_GROUPS, S5_STATE, S5_GROUP), (2 * S5_GROUP) ** -0.5),
        's5_c_re': nrm((N_B, S5_GROUPS, S5_GROUP, S5_STATE), (2 * S5_STATE) ** -0.5),
        's5_c_im': nrm((N_B, S5_GROUPS, S5_GROUP, S5_STATE), (2 * S5_STATE) ** -0.5),
        's5_d': nrm((N_B, D), 1.0),
        's5_log_dt': unif((N_B, S5_GROUPS), math.log(1e-3), math.log(1e-1)),
        's5_w_glu': nrm((N_B, D, 2 * D), D ** -0.5),
        'gdn_w_in': nrm((N_C, D, gdn_in_dim), D ** -0.5),
        'gdn_conv_w': nrm((N_C, CONV_W, GDN_QKV_DIM), CONV_W ** -0.5),
        'gdn_a_log': jnp.log(unif((N_C, GDN_V_HEADS), 1.0, 16.0)),
        'gdn_dt_bias': dt_gdn + jnp.log(-jnp.expm1(-dt_gdn)),
        'gdn_norm': 1.0 + nrm((N_C, GDN_HD), 0.02),
        'gdn_w_out': nrm((N_C, GDN_VAL_DIM, D), GDN_VAL_DIM ** -0.5),
        'ffn_w_gate': nrm((N_DENSE, D, D_FF), D ** -0.5),
        'ffn_w_up': nrm((N_DENSE, D, D_FF), D ** -0.5),
        'ffn_w_down': nrm((N_DENSE, D_FF, D), D_FF ** -0.5),
        'moe_w_router': nrm((N_MOE, D, N_EXPERTS), D ** -0.5),
        'moe_w_gate': nrm((N_MOE, N_EXPERTS, D, D_FF_EXPERT), D ** -0.5),
        'moe_w_up': nrm((N_MOE, N_EXPERTS, D, D_FF_EXPERT), D ** -0.5),
        'moe_w_down': nrm((N_MOE, N_EXPERTS, D_FF_EXPERT, D), D_FF_EXPERT ** -0.5),
        'ple_w_up': nrm((DEPTH, PLE_DIM, D), PLE_DIM ** -0.5),
        'ple_w_gate': nrm((DEPTH, D, D), D ** -0.5),
    }


def reference(x_prompt, x_sample, cache_fox_k, cache_fox_v, cache_fox_logf, state_s5_re, state_s5_im,
              state_gdn, state_gdn_conv, p_prompt, p_sample,
              norm_mix, norm_ffn, norm_ple, norm_final,
              fox_w_in, fox_b_f, fox_w_out,
              s5_a_re, s5_a_im, s5_b_re, s5_b_im, s5_c_re, s5_c_im, s5_d, s5_log_dt, s5_w_glu,
              gdn_w_in, gdn_conv_w, gdn_a_log, gdn_dt_bias, gdn_norm, gdn_w_out,
              ffn_w_gate, ffn_w_up, ffn_w_down,
              moe_w_router, moe_w_gate, moe_w_up, moe_w_down,
              ple_w_up, ple_w_gate):
    weights = (norm_mix, norm_ffn, norm_ple, norm_final,
               fox_w_in, fox_b_f, fox_w_out,
               s5_a_re, s5_a_im, s5_b_re, s5_b_im, s5_c_re, s5_c_im, s5_d, s5_log_dt, s5_w_glu,
               gdn_w_in, gdn_conv_w, gdn_a_log, gdn_dt_bias, gdn_norm, gdn_w_out,
               ffn_w_gate, ffn_w_up, ffn_w_down,
               moe_w_router, moe_w_gate, moe_w_up, moe_w_down,
               ple_w_up, ple_w_gate)
    (y_prompt, fk_p, fv_p, flf_p, sre_p, sim_p, gs_p, gc_p) = trunk(
        x_prompt, p_prompt, None, None, None, None, None, None, None, *weights)
    (y_sample, fk_s, fv_s, flf_s, sre_s, sim_s, gs_s, gc_s) = trunk(
        x_sample, p_sample, cache_fox_k, cache_fox_v, cache_fox_logf, state_s5_re, state_s5_im,
        state_gdn, state_gdn_conv, *weights)
    return (y_prompt, y_sample,
            fk_p, fv_p, flf_p, fk_s, fv_s, flf_s,
            sre_p, sim_p, sre_s, sim_s,
            gs_p, gc_p, gs_s, gc_s)
```

```python
import functools
import math

import jax
import jax.numpy as jnp
from jax import lax
from jax.experimental import pallas as pl
from jax.experimental.pallas import tpu as pltpu

F32 = jnp.float32
BF16 = jnp.bfloat16
EPS = 1e-6
LANES = 128
SUBLANES = 8
VMEM_LIMIT = 56 * 1024 * 1024
CHUNK = 64
N_EXPERTS = 8
NEG_INF = float("-inf")


def _cparams(*sem):
    return pltpu.CompilerParams(dimension_semantics=sem, vmem_limit_bytes=VMEM_LIMIT)


def _rms_scale(x, g):
    ms = jnp.mean(x * x, axis=-1, keepdims=True)
    return x * lax.rsqrt(ms + EPS) * g


def _sigmoid(x):
    return 1.0 / (1.0 + jnp.exp(-x))


def _silu(x):
    return x * _sigmoid(x)


def _softplus(x):
    return jnp.maximum(x, 0.0) + jnp.log1p(jnp.exp(-jnp.abs(x)))


def _split_bf16(x):
    hi = x.astype(BF16)
    lo = (x - hi.astype(F32)).astype(BF16)
    return hi, lo


def _dot(a, b):
    return jnp.dot(a, b, preferred_element_type=F32)


def _dot_nt(a, b):
    return lax.dot_general(a, b, (((1,), (1,)), ((), ())), preferred_element_type=F32)


def _dot_tn(a, b):
    return lax.dot_general(a, b, (((0,), (0,)), ((), ())), preferred_element_type=F32)


def _parts(x, precise):
    return _split_bf16(x.astype(F32)) if precise else (x.astype(BF16),)


def _mm(xp, wp):
    if len(wp) == 1:
        return _dot(xp[0], wp[0])
    return _dot(xp[0], wp[0]) + _dot(xp[1], wp[0]) + _dot(xp[0], wp[1])


def _mm_nt(ap, bp):
    if len(ap) == 1:
        return _dot_nt(ap[0], bp[0])
    return _dot_nt(ap[0], bp[0]) + _dot_nt(ap[1], bp[0]) + _dot_nt(ap[0], bp[1])


def _read(refs):
    return tuple(r[...] for r in refs)


def _wparts(w):
    return tuple(w) if isinstance(w, (tuple, list)) else (w,)


def _row_tile(m, n_weight_parts=1):
    cap = 1024 // n_weight_parts
    return next(t for t in (1024, 512, 256, 128) if t <= cap and m % t == 0)


def _store_norm(x_ref, g_ref, xn_refs):
    xn = _rms_scale(x_ref[...], g_ref[...])
    for ref, part in zip(xn_refs, _parts(xn, len(xn_refs) == 2)):
        ref[...] = part


def _norm_mm_kernel(*refs, ranges, nw):
    x_ref, g_ref = refs[:2]
    w_refs = refs[2:2 + nw]
    outs = refs[2 + nw:2 + nw + len(ranges)]
    xn_refs = refs[2 + nw + len(ranges):]
    j = pl.program_id(1)

    @pl.when(j == 0)
    def _():
        _store_norm(x_ref, g_ref, xn_refs)

    acc = _mm(_read(xn_refs), _read(w_refs))
    for o_ref, (a, b) in zip(outs, ranges):
        @pl.when((j >= a) & (j < b))
        def _(o_ref=o_ref):
            o_ref[...] = acc.astype(o_ref.dtype)


def _range_map(i, j, *, a, n):
    return (i, jnp.clip(j - a, 0, n - 1))


def norm_mm(x, g, w, outs, *, tn):
    M, D = x.shape
    wp = _wparts(w)
    N = wp[0].shape[1]
    tm = _row_tile(M, len(wp))
    ranges = tuple((a, b) for _, a, b in outs)
    return pl.pallas_call(
        functools.partial(_norm_mm_kernel, ranges=ranges, nw=len(wp)),
        grid=(M // tm, N // tn),
        in_specs=[pl.BlockSpec((tm, D), lambda i, j: (i, 0)),
                  pl.BlockSpec((1, D), lambda i, j: (0, 0))]
                 + [pl.BlockSpec((D, tn), lambda i, j: (0, j))] * len(wp),
        out_specs=[pl.BlockSpec((tm, tn), functools.partial(_range_map, a=a, n=b - a))
                   for _, a, b in outs],
        out_shape=[jax.ShapeDtypeStruct((M, (b - a) * tn), dt) for dt, a, b in outs],
        scratch_shapes=[pltpu.VMEM((tm, D), BF16)] * len(wp),
        compiler_params=_cparams("parallel", "arbitrary"),
        name="norm_mm",
    )(x, g.reshape(1, D), *wp)


def _glu_up_kernel(*refs, tiles_per_expert, nw):
    x_ref, g_ref = refs[:2]
    wg_refs = refs[2:2 + nw]
    wu_refs = refs[2 + nw:2 + 2 * nw]
    rest = refs[2 + 2 * nw:]
    comb_ref = rest[0] if tiles_per_expert else None
    o_ref = rest[1] if tiles_per_expert else rest[0]
    xn_refs = rest[2:] if tiles_per_expert else rest[1:]
    j = pl.program_id(1)

    @pl.when(j == 0)
    def _():
        _store_norm(x_ref, g_ref, xn_refs)

    xn = _read(xn_refs)
    h = _silu(_mm(xn, _read(wg_refs))) * _mm(xn, _read(wu_refs))
    if tiles_per_expert:
        comb = comb_ref[...]
        lane = lax.broadcasted_iota(jnp.int32, comb.shape, 1)
        col = jnp.sum(jnp.where(lane == j // tiles_per_expert, comb, 0.0), axis=1, keepdims=True)
        h = h * col
    o_ref[...] = h.astype(o_ref.dtype)


def glu_up(x, g, w_gate, w_up, comb=None, *, tn):
    M, D = x.shape
    wg, wu = _wparts(w_gate), _wparts(w_up)
    nw = len(wg)
    tm = _row_tile(M, nw)
    if comb is None:
        n_out, tpe = wg[0].shape[1], 0
        w_spec = pl.BlockSpec((D, tn), lambda i, j: (0, j))
        extra_in, extra_specs = (), []
    else:
        E, _, F = wg[0].shape
        n_out, tpe = E * F, F // tn
        w_spec = pl.BlockSpec((None, D, tn), lambda i, j: (j // tpe, 0, j % tpe))
        extra_in = (comb,)
        extra_specs = [pl.BlockSpec((tm, LANES), lambda i, j: (i, 0))]
    return pl.pallas_call(
        functools.partial(_glu_up_kernel, tiles_per_expert=tpe, nw=nw),
        grid=(M // tm, n_out // tn),
        in_specs=[pl.BlockSpec((tm, D), lambda i, j: (i, 0)),
                  pl.BlockSpec((1, D), lambda i, j: (0, 0))] + [w_spec] * (2 * nw) + extra_specs,
        out_specs=pl.BlockSpec((tm, tn), lambda i, j: (i, j)),
        out_shape=jax.ShapeDtypeStruct((M, n_out), F32 if nw == 2 else BF16),
        scratch_shapes=[pltpu.VMEM((tm, D), BF16)] * nw,
        compiler_params=_cparams("parallel", "arbitrary"),
        name="glu_up",
    )(x, g.reshape(1, D), *wg, *wu, *extra_in)


def _mm_res_kernel(*refs, nk, nw):
    x_ref = refs[0]
    w_refs = refs[1:1 + nw]
    r_ref, o_ref, acc_ref = refs[1 + nw:]
    k = pl.program_id(2)

    @pl.when(k == 0)
    def _():
        acc_ref[...] = jnp.zeros_like(acc_ref)

    acc_ref[...] += _mm(_parts(x_ref[...], nw == 2), _read(w_refs))

    @pl.when(k == nk - 1)
    def _():
        o_ref[...] = r_ref[...] + acc_ref[...]


def mm_res(x, w, res, *, tn, tk):
    M, K = x.shape
    wp = _wparts(w)
    N = wp[0].shape[1]
    tm = _row_tile(M, len(wp))
    nk = K // tk
    return pl.pallas_call(
        functools.partial(_mm_res_kernel, nk=nk, nw=len(wp)),
        grid=(M // tm, N // tn, nk),
        in_specs=[pl.BlockSpec((tm, tk), lambda i, j, k: (i, k))]
                 + [pl.BlockSpec((tk, tn), lambda i, j, k: (k, j))] * len(wp)
                 + [pl.BlockSpec((tm, tn), lambda i, j, k: (i, j))],
        out_specs=pl.BlockSpec((tm, tn), lambda i, j, k: (i, j)),
        out_shape=jax.ShapeDtypeStruct((M, N), F32),
        scratch_shapes=[pltpu.VMEM((tm, tn), F32)],
        compiler_params=_cparams("parallel", "parallel", "arbitrary"),
        name="mm_res",
    )(x, *wp, res)


def _glu_res_kernel(*refs, nw):
    y_ref = refs[0]
    wa_refs = refs[1:1 + nw]
    wb_refs = refs[1 + nw:1 + 2 * nw]
    r_ref, o_ref = refs[1 + 2 * nw:]
    y = _parts(y_ref[...], nw == 2)
    o_ref[...] = r_ref[...] + _mm(y, _read(wa_refs)) * _sigmoid(_mm(y, _read(wb_refs)))


def glu_res(y, w, res, *, tn):
    M, K = y.shape
    wp = _wparts(w)
    N = wp[0].shape[1] // 2
    nj = N // tn
    tm = _row_tile(M, len(wp))
    return pl.pallas_call(
        functools.partial(_glu_res_kernel, nw=len(wp)),
        grid=(M // tm, nj),
        in_specs=[pl.BlockSpec((tm, K), lambda i, j: (i, 0))]
                 + [pl.BlockSpec((K, tn), lambda i, j: (0, j))] * len(wp)
                 + [pl.BlockSpec((K, tn), lambda i, j: (0, j + nj))] * len(wp)
                 + [pl.BlockSpec((tm, tn), lambda i, j: (i, j))],
        out_specs=pl.BlockSpec((tm, tn), lambda i, j: (i, j)),
        out_shape=jax.ShapeDtypeStruct((M, N), F32),
        compiler_params=_cparams("parallel", "arbitrary"),
        name="glu_res",
    )(y, *wp, *wp, res)


def _ple_kernel(*refs, tn, nw):
    x_ref, g_ref = refs[:2]
    wg_refs = refs[2:2 + nw]
    p_ref = refs[2 + nw]
    wu_refs = refs[3 + nw:3 + 2 * nw]
    o_ref = refs[3 + 2 * nw]
    xn_refs = refs[4 + 2 * nw:]
    j = pl.program_id(1)

    @pl.when(j == 0)
    def _():
        _store_norm(x_ref, g_ref, xn_refs)

    gate = _sigmoid(_mm(_read(xn_refs), _read(wg_refs)))
    up = _mm(_parts(p_ref[...], nw == 2), _read(wu_refs))
    x_tile = x_ref[:, pl.ds(pl.multiple_of(j * tn, tn), tn)]
    o_ref[...] = x_tile + up * gate


def ple(x, g, w_gate, p, w_up, *, tn):
    M, D = x.shape
    P = p.shape[1]
    wg, wu = _wparts(w_gate), _wparts(w_up)
    nw = len(wg)
    tm = _row_tile(M, nw)
    return pl.pallas_call(
        functools.partial(_ple_kernel, tn=tn, nw=nw),
        grid=(M // tm, D // tn),
        in_specs=[pl.BlockSpec((tm, D), lambda i, j: (i, 0)),
                  pl.BlockSpec((1, D), lambda i, j: (0, 0))]
                 + [pl.BlockSpec((D, tn), lambda i, j: (0, j))] * nw
                 + [pl.BlockSpec((tm, P), lambda i, j: (i, 0))]
                 + [pl.BlockSpec((P, tn), lambda i, j: (0, j))] * nw,
        out_specs=pl.BlockSpec((tm, tn), lambda i, j: (i, j)),
        out_shape=jax.ShapeDtypeStruct((M, D), F32),
        scratch_shapes=[pltpu.VMEM((tm, D), BF16)] * nw,
        compiler_params=_cparams("parallel", "arbitrary"),
        name="ple",
    )(x, g.reshape(1, D), *wg, p, *wu)


def _rmsnorm_kernel(x_ref, g_ref, o_ref):
    o_ref[...] = _rms_scale(x_ref[...], g_ref[...]).astype(o_ref.dtype)


def rmsnorm(x, g):
    M, D = x.shape
    tm = _row_tile(M)
    return pl.pallas_call(
        _rmsnorm_kernel,
        grid=(M // tm,),
        in_specs=[pl.BlockSpec((tm, D), lambda i: (i, 0)),
                  pl.BlockSpec((1, D), lambda i: (0, 0))],
        out_specs=pl.BlockSpec((tm, D), lambda i: (i, 0)),
        out_shape=jax.ShapeDtypeStruct((M, D), F32),
        compiler_params=_cparams("parallel"),
        name="rmsnorm",
    )(x, g.reshape(1, D))


def _small_proj_kernel(x_ref, g_ref, whi_ref, wlo_ref, aux_ref, o_ref, *, mode, nh):
    xn = _rms_scale(x_ref[...], g_ref[...])
    acc = _mm(_parts(xn, True), (whi_ref[...], wlo_ref[...]))
    lane = lax.broadcasted_iota(jnp.int32, acc.shape, 1)
    if mode == "logf":
        o_ref[...] = -_softplus(-(acc + aux_ref[0:1, :]))
    elif mode == "gdn":
        beta = _sigmoid(acc)
        gdec = -jnp.exp(aux_ref[1:2, :]) * _softplus(acc + aux_ref[0:1, :])
        o_ref[...] = jnp.where(lane < nh, beta, gdec)
    else:
        logits = jnp.where(lane < N_EXPERTS, acc, NEG_INF)
        v1 = jnp.max(logits, axis=1, keepdims=True)
        i1 = jnp.min(jnp.where(logits == v1, lane, LANES), axis=1, keepdims=True)
        rest = jnp.where(lane == i1, NEG_INF, logits)
        v2 = jnp.max(rest, axis=1, keepdims=True)
        i2 = jnp.min(jnp.where(rest == v2, lane, LANES), axis=1, keepdims=True)
        e2 = jnp.exp(v2 - v1)
        g1 = 1.0 / (1.0 + e2)
        o_ref[...] = jnp.where(lane == i1, g1, 0.0) + jnp.where(lane == i2, e2 * g1, 0.0)


def small_proj(x, g, w_parts, aux, mode, *, nh=0):
    M, D = x.shape
    tm = _row_tile(M)
    return pl.pallas_call(
        functools.partial(_small_proj_kernel, mode=mode, nh=nh),
        grid=(M // tm,),
        in_specs=[pl.BlockSpec((tm, D), lambda i: (i, 0)),
                  pl.BlockSpec((1, D), lambda i: (0, 0)),
                  pl.BlockSpec((D, LANES), lambda i: (0, 0)),
                  pl.BlockSpec((D, LANES), lambda i: (0, 0)),
                  pl.BlockSpec((SUBLANES, LANES), lambda i: (0, 0))],
        out_specs=pl.BlockSpec((tm, LANES), lambda i: (i, 0)),
        out_shape=jax.ShapeDtypeStruct((M, LANES), F32),
        compiler_params=_cparams("parallel"),
        name="small_proj_" + mode,
    )(x, g.reshape(1, D), *w_parts, aux)


def _pad_cols(w):
    return _split_bf16(jnp.pad(w.astype(F32), ((0, 0), (0, LANES - w.shape[1]))))


def _aux_rows(*rows):
    out = [jnp.pad(vals.astype(F32), (start, LANES - start - vals.shape[0])) for start, vals in rows]
    out += [jnp.zeros((LANES,), F32)] * (SUBLANES - len(out))
    return jnp.stack(out)


def _lane_cumsum_kernel(x_ref, o_ref):
    x = x_ref[...]
    n = x.shape[1]
    lane = lax.broadcasted_iota(jnp.int32, x.shape, 1)
    shift = 1
    while shift < n:
        x = x + jnp.where(lane >= shift, pltpu.roll(x, shift, 1), 0.0)
        shift *= 2
    o_ref[...] = x


def lane_cumsum(x):
    R, n = x.shape
    rb = SUBLANES if R % SUBLANES == 0 else R
    return pl.pallas_call(
        _lane_cumsum_kernel,
        grid=(R // rb,),
        in_specs=[pl.BlockSpec((rb, n), lambda i: (i, 0))],
        out_specs=pl.BlockSpec((rb, n), lambda i: (i, 0)),
        out_shape=jax.ShapeDtypeStruct((R, n), F32),
        compiler_params=_cparams("parallel"),
        name="lane_cumsum",
    )(x)


def _fox_prompt_kernel(q_ref, k_ref, v_ref, fk_ref, o_ref, *, tq, scale):
    qi = pl.program_id(2)
    q = q_ref[...]
    hd = q.shape[1]

    def step(ki, carry, diagonal):
        m, l, acc = carry
        rows = pl.ds(pl.multiple_of(ki * tq, tq), tq)
        s = _dot_nt(q, k_ref[rows, :]) * scale - fk_ref[0, pl.ds(ki, 1), :]
        if diagonal:
            r = lax.broadcasted_iota(jnp.int32, s.shape, 0)
            c = lax.broadcasted_iota(jnp.int32, s.shape, 1)
            s = jnp.where(r >= c, s, NEG_INF)
        m_new = jnp.maximum(m, jnp.max(s, axis=1, keepdims=True))
        p = jnp.exp(s - m_new)
        alpha = jnp.exp(m - m_new)
        l = alpha * l + jnp.sum(p, axis=1, keepdims=True)
        acc = alpha * acc + _dot(p.astype(BF16), v_ref[rows, :])
        return m_new, l, acc

    init = (jnp.full((tq, 1), NEG_INF, F32), jnp.zeros((tq, 1), F32), jnp.zeros((tq, hd), F32))
    carry = lax.fori_loop(0, qi, lambda ki, c: step(ki, c, False), init)
    _, l, acc = step(qi, carry, True)
    o_ref[...] = (acc / l).astype(o_ref.dtype)


def fox_attention_prompt(qkv, fk, *, B, L, H, hd, tq):
    nq = L // tq
    return pl.pallas_call(
        functools.partial(_fox_prompt_kernel, tq=tq, scale=hd ** -0.5),
        grid=(B, H, nq),
        in_specs=[pl.BlockSpec((tq, hd), lambda b, h, i: (b * nq + i, h)),
                  pl.BlockSpec((L, hd), lambda b, h, i: (b, H + h)),
                  pl.BlockSpec((L, hd), lambda b, h, i: (b, 2 * H + h)),
                  pl.BlockSpec((1, nq, tq), lambda b, h, i: (b * H + h, 0, 0))],
        out_specs=pl.BlockSpec((tq, hd), lambda b, h, i: (b * nq + i, h)),
        out_shape=jax.ShapeDtypeStruct((B * L, H * hd), BF16),
        compiler_params=_cparams("parallel", "parallel", "arbitrary"),
        name="fox_attention_prompt",
    )(qkv, qkv, qkv, fk)


def _fox_cached_kernel(q_ref, kn_ref, vn_ref, kp_ref, vp_ref, f_ref, o_ref, *, P, scale, precise):
    q = _parts(q_ref[...], precise)
    n = q_ref.shape[0]
    f = f_ref[0]
    s_past = _mm_nt(q, _parts(kp_ref[0], precise)) * scale - f[:, :P]
    s_new = _mm_nt(q, _parts(kn_ref[...], precise)) * scale - f[:, P:P + n]
    r = lax.broadcasted_iota(jnp.int32, s_new.shape, 0)
    c = lax.broadcasted_iota(jnp.int32, s_new.shape, 1)
    s_new = jnp.where(r >= c, s_new, NEG_INF)
    m = jnp.maximum(jnp.max(s_past, axis=1, keepdims=True), jnp.max(s_new, axis=1, keepdims=True))
    p_past = jnp.exp(s_past - m)
    p_new = jnp.exp(s_new - m)
    l = jnp.sum(p_past, axis=1, keepdims=True) + jnp.sum(p_new, axis=1, keepdims=True)
    acc = (_mm(_parts(p_past, precise), _parts(vp_ref[0], precise))
           + _mm(_parts(p_new, precise), _parts(vn_ref[...], precise)))
    o_ref[...] = (acc / l).astype(o_ref.dtype)


def fox_attention_cached(qkv, k_past, v_past, f, *, seq0, k_lane0, v_lane0, B, n, P, H, hd):
    precise = qkv.dtype == F32
    return pl.pallas_call(
        functools.partial(_fox_cached_kernel, P=P, scale=hd ** -0.5, precise=precise),
        grid=(B, H),
        in_specs=[pl.BlockSpec((n, hd), lambda b, h: (b, h)),
                  pl.BlockSpec((n, hd), lambda b, h: (b, H + h)),
                  pl.BlockSpec((n, hd), lambda b, h: (b, 2 * H + h)),
                  pl.BlockSpec((1, P, hd), lambda b, h: (seq0 + b, 0, k_lane0 + h)),
                  pl.BlockSpec((1, P, hd), lambda b, h: (seq0 + b, 0, v_lane0 + h)),
                  pl.BlockSpec((1, 1, f.shape[2]), lambda b, h: (b * H + h, 0, 0))],
        out_specs=pl.BlockSpec((n, hd), lambda b, h: (b, h)),
        out_shape=jax.ShapeDtypeStruct((B * n, H * hd), qkv.dtype),
        compiler_params=_cparams("parallel", "arbitrary"),
        name="fox_attention_cached",
    )(qkv, qkv, qkv, k_past, v_past, f)


def _gelu_tanh(x):
    return 0.5 * x * (1.0 + jnp.tanh(math.sqrt(2.0 / math.pi) * (x + 0.044715 * (x * x * x))))


def _s5_kernel(*refs, nb, c, ns, ncm):
    u_ref, bhi_ref, blo_ref = refs[:3]
    cm_refs = refs[3:3 + ncm]
    ar_ref, ai_ref, d_ref, h0r_ref, h0i_ref, y_ref, hr_out, hi_out, bu_sc, st_sc, hr_sc, hi_sc = refs[3 + ncm:]
    tc = pl.program_id(1)

    @pl.when(tc == 0)
    def _():
        hr_sc[...] = h0r_ref[...]
        hi_sc[...] = h0i_ref[...]

    u = u_ref[...]
    bu_sc[...] = _mm(_parts(u, True), (bhi_ref[0], blo_ref[0]))
    ar = jnp.broadcast_to(ar_ref[0], (nb, ns))
    ai = jnp.broadcast_to(ai_ref[0], (nb, ns))

    def step(t, carry):
        hr, hi = carry
        rows = pl.ds(pl.multiple_of(t * nb, nb), nb)
        nhr = ar * hr - ai * hi + bu_sc[rows, 0:ns]
        nhi = ar * hi + ai * hr + bu_sc[rows, ns:2 * ns]
        st_sc[rows, 0:ns] = nhr
        st_sc[rows, ns:2 * ns] = nhi
        return nhr, nhi

    hr, hi = lax.fori_loop(0, c, step, (hr_sc[...], hi_sc[...]))
    hr_sc[...] = hr
    hi_sc[...] = hi
    y = _mm(_parts(st_sc[...], ncm == 2), tuple(r[0] for r in cm_refs)) + d_ref[...] * u
    y_ref[...] = _gelu_tanh(y).astype(y_ref.dtype)

    @pl.when(tc == pl.num_programs(1) - 1)
    def _():
        hr_out[...] = hr
        hi_out[...] = hi


def s5_scan(u_tm, h0r, h0i, consts, *, nb, c, precise):
    bhi, blo, cm_hi, cm_lo, ar, ai, d = consts
    cms = (cm_hi, cm_lo) if precise else (cm_hi,)
    rows, D = u_tm.shape
    nblk = D // LANES
    ns = ar.shape[2]
    ntc = rows // (c * nb)
    blk = lambda g, t: (g, 0, 0)
    return pl.pallas_call(
        functools.partial(_s5_kernel, nb=nb, c=c, ns=ns, ncm=len(cms)),
        grid=(nblk, ntc),
        in_specs=[pl.BlockSpec((c * nb, LANES), lambda g, t: (t, g)),
                  pl.BlockSpec((1, LANES, 2 * ns), blk),
                  pl.BlockSpec((1, LANES, 2 * ns), blk)]
                 + [pl.BlockSpec((1, 2 * ns, LANES), blk)] * len(cms)
                 + [pl.BlockSpec((1, 1, ns), blk),
                    pl.BlockSpec((1, 1, ns), blk),
                    pl.BlockSpec((1, LANES), lambda g, t: (0, g)),
                    pl.BlockSpec((nb, ns), lambda g, t: (0, g)),
                    pl.BlockSpec((nb, ns), lambda g, t: (0, g))],
        out_specs=[pl.BlockSpec((c * nb, LANES), lambda g, t: (t, g)),
                   pl.BlockSpec((nb, ns), lambda g, t: (0, g)),
                   pl.BlockSpec((nb, ns), lambda g, t: (0, g))],
        out_shape=[jax.ShapeDtypeStruct((rows, D), F32 if precise else BF16),
                   jax.ShapeDtypeStruct(h0r.shape, F32),
                   jax.ShapeDtypeStruct(h0i.shape, F32)],
        scratch_shapes=[pltpu.VMEM((c * nb, 2 * ns), F32),
                        pltpu.VMEM((c * nb, 2 * ns), F32),
                        pltpu.VMEM((nb, ns), F32),
                        pltpu.VMEM((nb, ns), F32)],
        compiler_params=_cparams("parallel", "arbitrary"),
        name="s5_scan",
    )(u_tm, bhi, blo, *cms, ar, ai, d, h0r, h0i)


def s5_constants(a_re, a_im, b_re, b_im, c_re, c_im, d_skip, log_dt):
    G, N = a_re.shape
    gc = b_re.shape[2]
    gpb = LANES // gc
    nblk = G // gpb
    a = lax.complex(a_re.astype(F32), a_im.astype(F32))
    dt = jnp.exp(log_dt.astype(F32))[:, None]
    a_bar = jnp.exp(a * dt)
    b_bar = ((a_bar - 1.0) / a)[..., None] * lax.complex(b_re.astype(F32), b_im.astype(F32))
    eye = jnp.eye(gpb, dtype=F32)

    def in_mat(b):
        b = b.reshape(nblk, gpb, N, gc)
        return jnp.einsum("kgnc,gh->kgchn", b, eye).reshape(nblk, gpb * gc, gpb * N)

    def out_mat(cc):
        cc = cc.reshape(nblk, gpb, gc, N)
        return jnp.einsum("kgcn,gh->kgnhc", cc, eye).reshape(nblk, gpb * N, gpb * gc)

    bhi, blo = _split_bf16(jnp.concatenate([in_mat(b_bar.real), in_mat(b_bar.imag)], axis=2))
    cm_hi, cm_lo = _split_bf16(jnp.concatenate([out_mat(c_re.astype(F32)), -out_mat(c_im.astype(F32))], axis=1))
    ar = a_bar.real.reshape(nblk, 1, gpb * N)
    ai = a_bar.imag.reshape(nblk, 1, gpb * N)
    return bhi, blo, cm_hi, cm_lo, ar, ai, d_skip.astype(F32).reshape(1, G * gc)


def _gdn_conv_kernel(x_ref, halo_ref, buf_ref, w_ref, o_ref, *, tl, n_valid, tiles_per_seq, n_norm_tiles):
    r = pl.program_id(0)
    j = pl.program_id(1)
    x = x_ref[...]
    first = (r % tiles_per_seq) == 0
    prev = jnp.where(first, buf_ref[0], halo_ref[...])
    ext = jnp.concatenate([prev, x], axis=0)
    w = w_ref[...]
    y = x * w[3:4, :]
    for s in (1, 2, 3):
        y = y + pltpu.roll(ext, s, 0)[SUBLANES:SUBLANES + n_valid, :] * w[3 - s:4 - s, :]
    y = _silu(y)
    segs = []
    for a in range(0, y.shape[1], LANES):
        seg = y[:, a:a + LANES]
        segs.append(seg * lax.rsqrt(jnp.sum(seg * seg, axis=1, keepdims=True) + EPS))
    y = jnp.where(j < n_norm_tiles, jnp.concatenate(segs, axis=1), y)
    if n_valid < tl:
        y = jnp.concatenate([y, jnp.zeros((tl - n_valid, y.shape[1]), F32)], axis=0)
    o_ref[...] = y.astype(o_ref.dtype)


def gdn_conv(x, buf8, w8, *, n_seq, seq_len, tl_in, tl_out, n_norm_cols, tc):
    C = x.shape[1]
    tiles_per_seq = seq_len // tl_in
    n_tiles = n_seq * tiles_per_seq
    hpt = tl_in // SUBLANES
    return pl.pallas_call(
        functools.partial(_gdn_conv_kernel, tl=tl_out, n_valid=tl_in, tiles_per_seq=tiles_per_seq,
                          n_norm_tiles=n_norm_cols // tc),
        grid=(n_tiles, C // tc),
        in_specs=[pl.BlockSpec((tl_in, tc), lambda r, j: (r, j)),
                  pl.BlockSpec((SUBLANES, tc), lambda r, j: (jnp.maximum(r * hpt - 1, 0), j)),
                  pl.BlockSpec((1, SUBLANES, tc), lambda r, j: (r // tiles_per_seq, 0, j)),
                  pl.BlockSpec((SUBLANES, tc), lambda r, j: (0, j))],
        out_specs=pl.BlockSpec((tl_out, tc), lambda r, j: (r, j)),
        out_shape=jax.ShapeDtypeStruct((n_tiles * tl_out, C), BF16),
        compiler_params=_cparams("parallel", "arbitrary"),
        name="gdn_conv",
    )(x, x, buf8, w8)


def _gdn_prep_kernel(qk_ref, gt_ref, n_ref, a_ref, gc_ref, *, C, n_qk, rep, hd, scale):
    gates = gt_ref[...]
    nh = n_qk * rep
    lane = lax.broadcasted_iota(jnp.int32, gates.shape, 1)
    g = jnp.where((lane >= nh) & (lane < 2 * nh), gates, 0.0)
    r = lax.broadcasted_iota(jnp.int32, (C, C), 0)
    c = lax.broadcasted_iota(jnp.int32, (C, C), 1)
    tril = (r >= c).astype(BF16)
    g_hi = g.astype(BF16)
    g_r1 = g - g_hi.astype(F32)
    g_mid = g_r1.astype(BF16)
    g_lo = (g_r1 - g_mid.astype(F32)).astype(BF16)
    gcum = _dot(tril, g_hi) + _dot(tril, g_mid) + _dot(tril, g_lo)
    gc_ref[...] = gcum
    gcum_t = gcum.T
    for hq in range(n_qk):
        q = qk_ref[:, hq * hd:(hq + 1) * hd]
        k = qk_ref[:, (n_qk + hq) * hd:(n_qk + hq + 1) * hd]
        kk = _dot_nt(k, k)
        qk = _dot_nt(q, k) * scale
        for rr in range(rep):
            hv = hq * rep + rr
            diff = gcum[:, nh + hv:nh + hv + 1] - gcum_t[nh + hv:nh + hv + 1, :]
            decay = jnp.exp(jnp.where(r >= c, diff, NEG_INF))
            n_ref[0, hv] = jnp.where(r > c, gates[:, hv:hv + 1] * kk * decay, 0.0)
            a_ref[0, hv] = (qk * decay).astype(a_ref.dtype)


def gdn_prep(qkv_c, gates, *, C, n_qk, rep, hd):
    rows = qkv_c.shape[0]
    nch = rows // C
    nh = n_qk * rep
    return pl.pallas_call(
        functools.partial(_gdn_prep_kernel, C=C, n_qk=n_qk, rep=rep, hd=hd, scale=hd ** -0.5),
        grid=(nch,),
        in_specs=[pl.BlockSpec((C, 2 * n_qk * hd), lambda n: (n, 0)),
                  pl.BlockSpec((C, LANES), lambda n: (n, 0))],
        out_specs=[pl.BlockSpec((1, nh, C, C), lambda n: (n, 0, 0, 0)),
                   pl.BlockSpec((1, nh, C, C), lambda n: (n, 0, 0, 0)),
                   pl.BlockSpec((C, LANES), lambda n: (n, 0))],
        out_shape=[jax.ShapeDtypeStruct((nch, nh, C, C), F32),
                   jax.ShapeDtypeStruct((nch, nh, C, C), BF16),
                   jax.ShapeDtypeStruct((rows, LANES), F32)],
        compiler_params=_cparams("parallel"),
        name="gdn_prep",
    )(qkv_c, gates)


def _tri_inv_kernel(n_ref, x_ref, *, C):
    nblk = C // SUBLANES
    lanes = n_ref.shape[1]
    sub = lax.broadcasted_iota(jnp.int32, (SUBLANES, lanes), 0)

    def row(i, _):
        base = i * C

        def inner(j, acc):
            nb = jnp.broadcast_to(n_ref[pl.ds(base + j, 1), :], (SUBLANES, lanes))
            return tuple(acc[cb] - nb * x_ref[pl.ds(pl.multiple_of(j * C + cb * SUBLANES, SUBLANES), SUBLANES), :]
                         for cb in range(nblk))

        init = tuple(jnp.where(sub + cb * SUBLANES == i, 1.0, 0.0).astype(F32) for cb in range(nblk))
        acc = lax.fori_loop(0, i, inner, init)
        for cb in range(nblk):
            x_ref[pl.ds(pl.multiple_of(base + cb * SUBLANES, SUBLANES), SUBLANES), :] = acc[cb]
        return 0

    lax.fori_loop(0, C, row, 0)


def tri_inv(n_t, *, C):
    R, S = n_t.shape
    tl = LANES if S % LANES == 0 else S
    return pl.pallas_call(
        functools.partial(_tri_inv_kernel, C=C),
        grid=(S // tl,),
        in_specs=[pl.BlockSpec((R, tl), lambda i: (0, i))],
        out_specs=pl.BlockSpec((R, tl), lambda i: (0, i)),
        out_shape=jax.ShapeDtypeStruct((R, S), F32),
        compiler_params=_cparams("parallel"),
        name="tri_inv",
    )(n_t)


def _gdn_scan_kernel(q_ref, k_ref, v_ref, z_ref, gt_ref, gc_ref, ti_ref, a_ref, s0_ref, ng_ref,
                     o_ref, s_out, s_sc, *, C, hg, rep, hd, nh, n_valid, scale):
    n = pl.program_id(2)
    hgi = pl.program_id(1)

    @pl.when(n == 0)
    def _():
        s_sc[...] = s0_ref[0]

    gates = gt_ref[...]
    gcum = gc_ref[...]
    lane = lax.broadcasted_iota(jnp.int32, gates.shape, 1)
    outs = []
    for hh in range(hg):
        hv = hgi * hg + hh
        beta = jnp.sum(jnp.where(lane == hv, gates, 0.0), axis=1, keepdims=True)
        gc = jnp.sum(jnp.where(lane == nh + hv, gcum, 0.0), axis=1, keepdims=True)
        eg = jnp.exp(gc)
        g_tot = gc[C - 1:C, :]
        kq = hh // rep
        k = k_ref[:, kq * hd:(kq + 1) * hd].astype(F32)
        q = q_ref[:, kq * hd:(kq + 1) * hd].astype(F32)
        v = v_ref[:, hh * hd:(hh + 1) * hd].astype(F32)
        rhs = jnp.concatenate([v * beta, k * (beta * eg)], axis=1).astype(BF16)
        uw = _dot(ti_ref[0, hh].astype(BF16), rhs)
        s = s_sc[hh]
        s_bf = s.astype(BF16)
        v_new = uw[:, :hd] - _dot(uw[:, hd:].astype(BF16), s_bf)
        v_new_bf = v_new.astype(BF16)
        o = _dot((q * (scale * eg)).astype(BF16), s_bf) + _dot(a_ref[0, hh], v_new_bf)
        k_dec = (k * jnp.exp(g_tot - gc)).astype(BF16)
        s_sc[hh] = s * jnp.exp(g_tot) + _dot_tn(k_dec, v_new_bf)
        o = o[:n_valid]
        z = z_ref[:, hh * hd:(hh + 1) * hd].astype(F32)
        outs.append(_rms_scale(o, ng_ref[...]) * _silu(z))
    o_ref[...] = jnp.concatenate(outs, axis=1).astype(o_ref.dtype)

    @pl.when(n == pl.num_programs(2) - 1)
    def _():
        s_out[0] = s_sc[...]


def gdn_scan(qkv_c, z, gates, gcum, tinv, amat, s0, norm_g, *, n_seq, nc, C, n_valid, n_qk, rep, hd, hg):
    nh = n_qk * rep
    nhg = nh // hg
    qw = (hg // rep) * hd
    k_off = n_qk * hd // qw
    v_off = 2 * n_qk * hd // (hg * hd)
    chunk = lambda s, g, n: s * nc + n
    return pl.pallas_call(
        functools.partial(_gdn_scan_kernel, C=C, hg=hg, rep=rep, hd=hd, nh=nh, n_valid=n_valid,
                          scale=hd ** -0.5),
        grid=(n_seq, nhg, nc),
        in_specs=[pl.BlockSpec((C, qw), lambda s, g, n: (chunk(s, g, n), g)),
                  pl.BlockSpec((C, qw), lambda s, g, n: (chunk(s, g, n), k_off + g)),
                  pl.BlockSpec((C, hg * hd), lambda s, g, n: (chunk(s, g, n), v_off + g)),
                  pl.BlockSpec((n_valid, hg * hd), lambda s, g, n: (chunk(s, g, n), g)),
                  pl.BlockSpec((C, LANES), lambda s, g, n: (chunk(s, g, n), 0)),
                  pl.BlockSpec((C, LANES), lambda s, g, n: (chunk(s, g, n), 0)),
                  pl.BlockSpec((1, hg, C, C), lambda s, g, n: (chunk(s, g, n), g, 0, 0)),
                  pl.BlockSpec((1, hg, C, C), lambda s, g, n: (chunk(s, g, n), g, 0, 0)),
                  pl.BlockSpec((1, hg, hd, hd), lambda s, g, n: (s, g, 0, 0)),
                  pl.BlockSpec((1, hd), lambda s, g, n: (0, 0))],
        out_specs=[pl.BlockSpec((n_valid, hg * hd), lambda s, g, n: (chunk(s, g, n), g)),
                   pl.BlockSpec((1, hg, hd, hd), lambda s, g, n: (s, g, 0, 0))],
        out_shape=[jax.ShapeDtypeStruct((n_seq * nc * n_valid, nh * hd), BF16),
                   jax.ShapeDtypeStruct((n_seq, nh, hd, hd), F32)],
        scratch_shapes=[pltpu.VMEM((hg, hd, hd), F32)],
        compiler_params=_cparams("parallel", "parallel", "arbitrary"),
        name="gdn_scan",
    )(qkv_c, qkv_c, qkv_c, z, gates, gcum, tinv, amat, s0, norm_g.reshape(1, hd))


def _trunk(x, p, fox_hist, s5_h0, gdn_state, W, *, n_seq, seq_len, steps, precise):
    M, D = x.shape
    H, hd = W["fox_heads"]
    n_qk, rep, gd = W["gdn_heads"]
    n_vh = n_qk * rep
    key_dim, val_dim = n_qk * gd, n_vh * gd
    qkv_dim = 2 * key_dim + val_dim
    cw = W["gdn_conv_w"].shape[1]
    tn = min(512, D)
    outs = {}

    def wsel(name, idx, pr):
        hi = W[name][idx]
        return (hi, W[name + "_lo"][idx]) if pr else hi

    def mixer(i, x):
        j = i // 3
        pm = precise
        g_mix = W["norm_mix"][i]
        if i % 3 == 0:
            nt = D // tn
            qkv_dt = F32 if pm else BF16
            qkv, kv = norm_mm(x, g_mix, wsel("fox_w_qkv", j, pm), [(qkv_dt, 0, 3 * nt), (F32, nt, 3 * nt)], tn=tn)
            logf = small_proj(x, g_mix, W["fox_w_f"][j], W["fox_aux"][j], "logf")[:, :H]
            lf = jnp.swapaxes(logf.reshape(n_seq, seq_len, H), 1, 2)
            if j not in fox_hist:
                tq = next(t for t in (256, 384, 128, seq_len) if seq_len % t == 0)
                fk = lane_cumsum(lf.reshape(n_seq * H, seq_len)).reshape(n_seq * H, seq_len // tq, tq)
                o = fox_attention_prompt(qkv, fk, B=n_seq, L=seq_len, H=H, hd=hd, tq=tq)
            else:
                k_past, v_past, k_lane0, v_lane0, seq0, lf_past, P = fox_hist[j]
                fpad = -(-(P + seq_len) // LANES) * LANES
                lf_all = jnp.concatenate([jnp.swapaxes(lf_past.astype(F32), 1, 2), lf], axis=2)
                lf_all = jnp.pad(lf_all, ((0, 0), (0, 0), (0, fpad - (P + seq_len))))
                f = lane_cumsum(lf_all.reshape(n_seq * H, fpad)).reshape(n_seq * H, 1, fpad)
                o = fox_attention_cached(qkv, k_past, v_past, f, seq0=seq0, k_lane0=k_lane0, v_lane0=v_lane0,
                                         B=n_seq, n=seq_len, P=P, H=H, hd=hd)
            outs[("fox", j)] = (kv, logf)
            return mm_res(o, wsel("fox_w_out", j, pm), x, tn=tn, tk=D)
        if i % 3 == 1:
            u = rmsnorm(x, g_mix)
            u_tm = jnp.swapaxes(u.reshape(n_seq, seq_len, D), 0, 1).reshape(M, D)
            y_tm, hr, hi = s5_scan(u_tm, s5_h0[j][0], s5_h0[j][1], W["s5_consts"][j],
                                   nb=n_seq, c=min(CHUNK, seq_len), precise=pm)
            y = jnp.swapaxes(y_tm.reshape(seq_len, n_seq, D), 0, 1).reshape(M, D)
            outs[("s5", j)] = (hr, hi)
            return glu_res(y, wsel("s5_w_glu", j, pm), x, tn=tn)
        s0, conv_rows = gdn_state[j]
        nq = qkv_dim // tn
        nz = val_dim // tn
        qkv_raw, z = norm_mm(x, g_mix, W["gdn_w_qkvz"][j], [(F32, 0, nq), (BF16, nq, nq + nz)], tn=tn)
        gates = small_proj(x, g_mix, W["gdn_w_ba"][j], W["gdn_aux"][j], "gdn", nh=n_vh)
        C = CHUNK
        n_valid = min(C, seq_len)
        nc = seq_len // n_valid
        buf = jnp.pad(conv_rows.astype(F32), ((0, 0), (SUBLANES - (cw - 1), 0), (0, 0)))
        tl_in = min(512, seq_len)
        qkv_c = gdn_conv(qkv_raw, buf, W["gdn_conv_w8"][j], n_seq=n_seq, seq_len=seq_len, tl_in=tl_in,
                         tl_out=max(tl_in, C), n_norm_cols=2 * key_dim, tc=min(512, key_dim))
        gates_c = jnp.pad(gates.reshape(n_seq * nc, n_valid, LANES),
                          ((0, 0), (0, C - n_valid), (0, 0))).reshape(n_seq * nc * C, LANES)
        nmat, amat, gcum = gdn_prep(qkv_c, gates_c, C=C, n_qk=n_qk, rep=rep, hd=gd)
        nch = nmat.shape[0]
        tinv = tri_inv(nmat.reshape(nch * n_vh, C * C).T, C=C).T.reshape(nch, n_vh, C, C)
        o, s_last = gdn_scan(qkv_c, z, gates_c, gcum, tinv, amat, s0.astype(F32), W["gdn_norm"][j],
                             n_seq=n_seq, nc=nc, C=C, n_valid=n_valid, n_qk=n_qk, rep=rep, hd=gd, hg=4)
        rows = jnp.concatenate([conv_rows.astype(F32), qkv_raw.reshape(n_seq, seq_len, qkv_dim)], axis=1)
        outs[("gdn", j)] = (s_last, rows[:, -(cw - 1):])
        return mm_res(o, W["gdn_w_out"][j], x, tn=tn, tk=D)

    def channel(i, x):
        m = i // 2
        g_ffn = W["norm_ffn"][i]
        if i % 2 == 0:
            pf = precise
            h = glu_up(x, g_ffn, wsel("ffn_w_gate", m, pf), wsel("ffn_w_up", m, pf), tn=tn)
            x = mm_res(h, wsel("ffn_w_down", m, pf), x, tn=tn, tk=D)
        else:
            pf = False
            comb = small_proj(x, g_ffn, W["moe_w_router"][m], W["zero_aux"], "router")
            h = glu_up(x, g_ffn, W["moe_w_gate"][m], W["moe_w_up"][m], comb, tn=tn)
            x = mm_res(h, W["moe_w_down"][m], x, tn=tn, tk=D)
        p_i = p[i] if pf else p[i].astype(BF16)
        return ple(x, W["norm_ple"][i], wsel("ple_w_gate", i, pf), p_i, wsel("ple_w_up", i, pf), tn=tn)

    for i, part in steps:
        x = mixer(i, x) if part == "mixer" else channel(i, x)
    return x, outs


def kernel(x_prompt, x_sample, cache_fox_k, cache_fox_v, cache_fox_logf, state_s5_re, state_s5_im, state_gdn, state_gdn_conv, p_prompt, p_sample, norm_mix, norm_ffn, norm_ple, norm_final, fox_w_in, fox_b_f, fox_w_out, s5_a_re, s5_a_im, s5_b_re, s5_b_im, s5_c_re, s5_c_im, s5_d, s5_log_dt, s5_w_glu, gdn_w_in, gdn_conv_w, gdn_a_log, gdn_dt_bias, gdn_norm, gdn_w_out, ffn_w_gate, ffn_w_up, ffn_w_down, moe_w_router, moe_w_gate, moe_w_up, moe_w_down, ple_w_up, ple_w_gate):
    B, L, D = x_prompt.shape
    Bs, Ls, _ = x_sample.shape
    depth = norm_mix.shape[0]
    P = cache_fox_k.shape[2]
    H, hd = cache_fox_k.shape[3], cache_fox_k.shape[4]
    n_vh, gd = state_gdn.shape[2], state_gdn.shape[3]
    qkv_dim = state_gdn_conv.shape[3]
    val_dim = n_vh * gd
    n_qk = (qkv_dim - val_dim) // (2 * gd)
    cw = gdn_conv_w.shape[1]
    G, NS = s5_a_re.shape[1], s5_a_re.shape[2]
    E, _, Fe = moe_w_gate.shape[1:]
    n_fox, n_s5, n_gdn = fox_w_in.shape[0], s5_a_re.shape[0], gdn_w_in.shape[0]

    first_router = 1
    n_lo = {"fox": 1, "ffn": 1, "ple": 1, "s5": 1}

    def hi_lo(w, n):
        hi = w.astype(BF16)
        lo = (w[:n].astype(F32) - hi[:n].astype(F32)).astype(BF16)
        return hi, lo

    W = {"norm_mix": norm_mix, "norm_ffn": norm_ffn, "norm_ple": norm_ple, "norm_final": norm_final,
         "fox_heads": (H, hd), "gdn_heads": (n_qk, n_vh // n_qk, gd), "gdn_conv_w": gdn_conv_w,
         "gdn_norm": gdn_norm, "zero_aux": jnp.zeros((SUBLANES, LANES), F32)}
    W["fox_w_qkv"], W["fox_w_qkv_lo"] = hi_lo(fox_w_in[:, :, :3 * D], n_lo["fox"])
    W["fox_w_out"], W["fox_w_out_lo"] = hi_lo(fox_w_out, n_lo["fox"])
    W["fox_w_f"] = [_pad_cols(fox_w_in[j, :, 3 * D:]) for j in range(n_fox)]
    W["fox_aux"] = [_aux_rows((0, fox_b_f[j])) for j in range(n_fox)]
    W["s5_w_glu"], W["s5_w_glu_lo"] = hi_lo(s5_w_glu, n_lo["s5"])
    W["s5_consts"] = [s5_constants(s5_a_re[j], s5_a_im[j], s5_b_re[j], s5_b_im[j], s5_c_re[j], s5_c_im[j],
                                   s5_d[j], s5_log_dt[j]) for j in range(n_s5)]
    W["gdn_w_qkvz"] = gdn_w_in[:, :, :qkv_dim + val_dim].astype(BF16)
    W["gdn_w_ba"] = [_pad_cols(gdn_w_in[j, :, qkv_dim + val_dim:]) for j in range(n_gdn)]
    W["gdn_aux"] = [_aux_rows((n_vh, gdn_dt_bias[j]), (n_vh, gdn_a_log[j])) for j in range(n_gdn)]
    W["gdn_conv_w8"] = jnp.pad(gdn_conv_w.astype(F32), ((0, 0), (0, SUBLANES - cw), (0, 0)))
    W["gdn_w_out"] = gdn_w_out.astype(BF16)
    W["ffn_w_gate"], W["ffn_w_gate_lo"] = hi_lo(ffn_w_gate, n_lo["ffn"])
    W["ffn_w_up"], W["ffn_w_up_lo"] = hi_lo(ffn_w_up, n_lo["ffn"])
    W["ffn_w_down"], W["ffn_w_down_lo"] = hi_lo(ffn_w_down, n_lo["ffn"])
    W["moe_w_router"] = [_pad_cols(moe_w_router[m]) for m in range(moe_w_router.shape[0])]
    W["moe_w_gate"] = moe_w_gate.astype(BF16)
    W["moe_w_up"] = moe_w_up.astype(BF16)
    W["moe_w_down"] = moe_w_down.astype(BF16).reshape(-1, E * Fe, D)
    W["ple_w_gate"], W["ple_w_gate_lo"] = hi_lo(ple_w_gate, n_lo["ple"])
    W["ple_w_up"], W["ple_w_up_lo"] = hi_lo(ple_w_up, n_lo["ple"])

    steps = [(i, part) for i in range(depth) for part in ("mixer", "channel")]
    steps_a, steps_b = steps[:2 * first_router + 1], steps[2 * first_router + 1:]
    pd = p_prompt.shape[-1]

    Lt = LANES
    Lh = L - Lt
    p_rows = lambda sl: {i: p_prompt[i][:, sl].reshape(-1, pd) for i in range(depth)}
    zeros_s5 = jnp.zeros((B, G * NS), F32)
    xh, out_h = _trunk(x_prompt[:, :Lh].reshape(B * Lh, D), p_rows(slice(0, Lh)), {},
                       {j: (zeros_s5, zeros_s5) for j in range(n_s5)}, {}, W,
                       n_seq=B, seq_len=Lh, steps=steps_a, precise=False)
    hist_t = {}
    for j in range(n_fox):
        if ("fox", j) in out_h:
            kv_h, lf_h = out_h[("fox", j)]
            kv_h = kv_h.reshape(B, Lh, 2 * D)
            hist_t[j] = (kv_h, kv_h, 0, H, 0, lf_h.reshape(B, Lh, H), Lh)
    xt, out_t = _trunk(x_prompt[:, Lh:].reshape(B * Lt, D), p_rows(slice(Lh, L)), hist_t,
                       {j: out_h[("s5", j)] for j in range(n_s5) if ("s5", j) in out_h}, {}, W,
                       n_seq=B, seq_len=Lt, steps=steps_a, precise=True)
    x_p = jnp.concatenate([xh.reshape(B, Lh, D), xt.reshape(B, Lt, D)], axis=1).reshape(B * L, D)
    gdn0 = (jnp.zeros((B, n_vh, gd, gd), F32), jnp.zeros((B, cw - 1, qkv_dim), F32))
    x_p, out_p = _trunk(x_p, p_rows(slice(0, L)), {}, {j: (zeros_s5, zeros_s5) for j in range(n_s5)},
                        {j: gdn0 for j in range(n_gdn)}, W, n_seq=B, seq_len=L, steps=steps_b, precise=False)
    for key, val in out_t.items():
        if key[0] == "fox":
            kv_h, lf_h = out_h[key]
            out_p[key] = (jnp.concatenate([kv_h.reshape(B, Lh, 2 * D), val[0].reshape(B, Lt, 2 * D)], axis=1),
                          jnp.concatenate([lf_h.reshape(B, Lh, H), val[1].reshape(B, Lt, H)], axis=1))
        else:
            out_p[key] = val

    ck = cache_fox_k.reshape(-1, P, H * hd)
    cv = cache_fox_v.reshape(-1, P, H * hd)
    s5_re = state_s5_re.reshape(n_s5, Bs, G * NS).astype(F32)
    s5_im = state_s5_im.reshape(n_s5, Bs, G * NS).astype(F32)
    hist_s = {j: (ck, cv, 0, 0, j * Bs, cache_fox_logf[j], P) for j in range(n_fox)}
    state_s = ({j: (s5_re[j], s5_im[j]) for j in range(n_s5)},
               {j: (state_gdn[j], state_gdn_conv[j]) for j in range(n_gdn)})
    p_s = {i: p_sample[i].reshape(Bs * Ls, pd) for i in range(depth)}
    x_s, out_s = _trunk(x_sample.reshape(Bs * Ls, D), p_s, hist_s, *state_s, W,
                        n_seq=Bs, seq_len=Ls, steps=steps_a, precise=True)
    x_s, out_sb = _trunk(x_s, p_s, hist_s, *state_s, W, n_seq=Bs, seq_len=Ls, steps=steps_b, precise=False)
    out_s.update(out_sb)

    def leaves(x, out, nb, sl):
        y = rmsnorm(x, norm_final).reshape(nb, sl, D)
        kv = [out[("fox", j)][0].reshape(nb, sl, 2, H, hd) for j in range(n_fox)]
        fk = jnp.stack([a[:, :, 0] for a in kv])
        fv = jnp.stack([a[:, :, 1] for a in kv])
        flf = jnp.stack([out[("fox", j)][1].reshape(nb, sl, -1)[..., :H] for j in range(n_fox)])
        sre = jnp.stack([out[("s5", j)][0].reshape(nb, G, NS) for j in range(n_s5)])
        sim = jnp.stack([out[("s5", j)][1].reshape(nb, G, NS) for j in range(n_s5)])
        gs = jnp.stack([out[("gdn", j)][0] for j in range(n_gdn)])
        gc = jnp.stack([out[("gdn", j)][1] for j in range(n_gdn)])
        return y, fk, fv, flf, sre, sim, gs, gc

    (y_p, fk_p, fv_p, flf_p, sre_p, sim_p, gs_p, gc_p) = leaves(x_p, out_p, B, L)
    (y_s, fk_s, fv_s, flf_s, sre_s, sim_s, gs_s, gc_s) = leaves(x_s, out_s, Bs, Ls)
    return (y_p, y_s, fk_p, fv_p, flf_p, fk_s, fv_s, flf_s, sre_p, sim_p, sre_s, sim_s,
            gs_p, gc_p, gs_s, gc_s)
```

```python
import functools
import math

import jax
import jax.numpy as jnp
from jax import lax
from jax.experimental import pallas as pl
from jax.experimental.pallas import tpu as pltpu

F32 = jnp.float32
BF16 = jnp.bfloat16
EPS = 1e-6
LANES = 128
SUBLANES = 8
VMEM_LIMIT = 56 * 1024 * 1024
CHUNK = 64
N_EXPERTS = 8
NEG_INF = float("-inf")


def _cparams(*sem):
    return pltpu.CompilerParams(dimension_semantics=sem, vmem_limit_bytes=VMEM_LIMIT)


def _rms_scale(x, g):
    ms = jnp.mean(x * x, axis=-1, keepdims=True)
    return x * lax.rsqrt(ms + EPS) * g


def _sigmoid(x):
    return 1.0 / (1.0 + jnp.exp(-x))


def _silu(x):
    return x * _sigmoid(x)


def _softplus(x):
    return jnp.maximum(x, 0.0) + jnp.log1p(jnp.exp(-jnp.abs(x)))


def _split_bf16(x):
    hi = x.astype(BF16)
    lo = (x - hi.astype(F32)).astype(BF16)
    return hi, lo


def _dot(a, b):
    return jnp.dot(a, b, preferred_element_type=F32)


def _dot_nt(a, b):
    return lax.dot_general(a, b, (((1,), (1,)), ((), ())), preferred_element_type=F32)


def _dot_tn(a, b):
    return lax.dot_general(a, b, (((0,), (0,)), ((), ())), preferred_element_type=F32)


def _parts(x, precise):
    return _split_bf16(x.astype(F32)) if precise else (x.astype(BF16),)


def _mm(xp, wp):
    if len(wp) == 1:
        return _dot(xp[0], wp[0])
    return _dot(xp[0], wp[0]) + _dot(xp[1], wp[0]) + _dot(xp[0], wp[1])


def _mm_nt(ap, bp):
    if len(ap) == 1:
        return _dot_nt(ap[0], bp[0])
    return _dot_nt(ap[0], bp[0]) + _dot_nt(ap[1], bp[0]) + _dot_nt(ap[0], bp[1])


def _read(refs):
    return tuple(r[...] for r in refs)


def _wparts(w):
    return tuple(w) if isinstance(w, (tuple, list)) else (w,)


def _row_tile(m, n_weight_parts=1):
    cap = 1024 // n_weight_parts
    return next(t for t in (1024, 512, 256, 128) if t <= cap and m % t == 0)


def _store_norm(x_ref, g_ref, xn_refs):
    xn = _rms_scale(x_ref[...], g_ref[...])
    for ref, part in zip(xn_refs, _parts(xn, len(xn_refs) == 2)):
        ref[...] = part


def _norm_mm_kernel(*refs, ranges, nw):
    x_ref, g_ref = refs[:2]
    w_refs = refs[2:2 + nw]
    outs = refs[2 + nw:2 + nw + len(ranges)]
    xn_refs = refs[2 + nw + len(ranges):]
    j = pl.program_id(1)

    @pl.when(j == 0)
    def _():
        _store_norm(x_ref, g_ref, xn_refs)

    acc = _mm(_read(xn_refs), _read(w_refs))
    for o_ref, (a, b) in zip(outs, ranges):
        @pl.when((j >= a) & (j < b))
        def _(o_ref=o_ref):
            o_ref[...] = acc.astype(o_ref.dtype)


def _range_map(i, j, *, a, n):
    return (i, jnp.clip(j - a, 0, n - 1))


def norm_mm(x, g, w, outs, *, tn):
    M, D = x.shape
    wp = _wparts(w)
    N = wp[0].shape[1]
    tm = _row_tile(M, len(wp))
    ranges = tuple((a, b) for _, a, b in outs)
    return pl.pallas_call(
        functools.partial(_norm_mm_kernel, ranges=ranges, nw=len(wp)),
        grid=(M // tm, N // tn),
        in_specs=[pl.BlockSpec((tm, D), lambda i, j: (i, 0)),
                  pl.BlockSpec((1, D), lambda i, j: (0, 0))]
                 + [pl.BlockSpec((D, tn), lambda i, j: (0, j))] * len(wp),
        out_specs=[pl.BlockSpec((tm, tn), functools.partial(_range_map, a=a, n=b - a))
                   for _, a, b in outs],
        out_shape=[jax.ShapeDtypeStruct((M, (b - a) * tn), dt) for dt, a, b in outs],
        scratch_shapes=[pltpu.VMEM((tm, D), BF16)] * len(wp),
        compiler_params=_cparams("parallel", "arbitrary"),
        name="norm_mm",
    )(x, g.reshape(1, D), *wp)


def _glu_up_kernel(*refs, tiles_per_expert, nw):
    x_ref, g_ref = refs[:2]
    wg_refs = refs[2:2 + nw]
    wu_refs = refs[2 + nw:2 + 2 * nw]
    rest = refs[2 + 2 * nw:]
    comb_ref = rest[0] if tiles_per_expert else None
    o_ref = rest[1] if tiles_per_expert else rest[0]
    xn_refs = rest[2:] if tiles_per_expert else rest[1:]
    j = pl.program_id(1)

    @pl.when(j == 0)
    def _():
        _store_norm(x_ref, g_ref, xn_refs)

    xn = _read(xn_refs)
    h = _silu(_mm(xn, _read(wg_refs))) * _mm(xn, _read(wu_refs))
    if tiles_per_expert:
        comb = comb_ref[...]
        lane = lax.broadcasted_iota(jnp.int32, comb.shape, 1)
        col = jnp.sum(jnp.where(lane == j // tiles_per_expert, comb, 0.0), axis=1, keepdims=True)
        h = h * col
    o_ref[...] = h.astype(o_ref.dtype)


def glu_up(x, g, w_gate, w_up, comb=None, *, tn):
    M, D = x.shape
    wg, wu = _wparts(w_gate), _wparts(w_up)
    nw = len(wg)
    tm = _row_tile(M, nw)
    if comb is None:
        n_out, tpe = wg[0].shape[1], 0
        w_spec = pl.BlockSpec((D, tn), lambda i, j: (0, j))
        extra_in, extra_specs = (), []
    else:
        E, _, F = wg[0].shape
        n_out, tpe = E * F, F // tn
        w_spec = pl.BlockSpec((None, D, tn), lambda i, j: (j // tpe, 0, j % tpe))
        extra_in = (comb,)
        extra_specs = [pl.BlockSpec((tm, LANES), lambda i, j: (i, 0))]
    return pl.pallas_call(
        functools.partial(_glu_up_kernel, tiles_per_expert=tpe, nw=nw),
        grid=(M // tm, n_out // tn),
        in_specs=[pl.BlockSpec((tm, D), lambda i, j: (i, 0)),
                  pl.BlockSpec((1, D), lambda i, j: (0, 0))] + [w_spec] * (2 * nw) + extra_specs,
        out_specs=pl.BlockSpec((tm, tn), lambda i, j: (i, j)),
        out_shape=jax.ShapeDtypeStruct((M, n_out), F32 if nw == 2 else BF16),
        scratch_shapes=[pltpu.VMEM((tm, D), BF16)] * nw,
        compiler_params=_cparams("parallel", "arbitrary"),
        name="glu_up",
    )(x, g.reshape(1, D), *wg, *wu, *extra_in)


def _mm_res_kernel(*refs, nk, nw):
    x_ref = refs[0]
    w_refs = refs[1:1 + nw]
    r_ref, o_ref, acc_ref = refs[1 + nw:]
    k = pl.program_id(2)

    @pl.when(k == 0)
    def _():
        acc_ref[...] = jnp.zeros_like(acc_ref)

    acc_ref[...] += _mm(_parts(x_ref[...], nw == 2), _read(w_refs))

    @pl.when(k == nk - 1)
    def _():
        o_ref[...] = r_ref[...] + acc_ref[...]


def mm_res(x, w, res, *, tn, tk):
    M, K = x.shape
    wp = _wparts(w)
    N = wp[0].shape[1]
    tm = _row_tile(M, len(wp))
    nk = K // tk
    return pl.pallas_call(
        functools.partial(_mm_res_kernel, nk=nk, nw=len(wp)),
        grid=(M // tm, N // tn, nk),
        in_specs=[pl.BlockSpec((tm, tk), lambda i, j, k: (i, k))]
                 + [pl.BlockSpec((tk, tn), lambda i, j, k: (k, j))] * len(wp)
                 + [pl.BlockSpec((tm, tn), lambda i, j, k: (i, j))],
        out_specs=pl.BlockSpec((tm, tn), lambda i, j, k: (i, j)),
        out_shape=jax.ShapeDtypeStruct((M, N), F32),
        scratch_shapes=[pltpu.VMEM((tm, tn), F32)],
        compiler_params=_cparams("parallel", "parallel", "arbitrary"),
        name="mm_res",
    )(x, *wp, res)


def _glu_res_kernel(*refs, nw):
    y_ref = refs[0]
    wa_refs = refs[1:1 + nw]
    wb_refs = refs[1 + nw:1 + 2 * nw]
    r_ref, o_ref = refs[1 + 2 * nw:]
    y = _parts(y_ref[...], nw == 2)
    o_ref[...] = r_ref[...] + _mm(y, _read(wa_refs)) * _sigmoid(_mm(y, _read(wb_refs)))


def glu_res(y, w, res, *, tn):
    M, K = y.shape
    wp = _wparts(w)
    N = wp[0].shape[1] // 2
    nj = N // tn
    tm = _row_tile(M, len(wp))
    return pl.pallas_call(
        functools.partial(_glu_res_kernel, nw=len(wp)),
        grid=(M // tm, nj),
        in_specs=[pl.BlockSpec((tm, K), lambda i, j: (i, 0))]
                 + [pl.BlockSpec((K, tn), lambda i, j: (0, j))] * len(wp)
                 + [pl.BlockSpec((K, tn), lambda i, j: (0, j + nj))] * len(wp)
                 + [pl.BlockSpec((tm, tn), lambda i, j: (i, j))],
        out_specs=pl.BlockSpec((tm, tn), lambda i, j: (i, j)),
        out_shape=jax.ShapeDtypeStruct((M, N), F32),
        compiler_params=_cparams("parallel", "arbitrary"),
        name="glu_res",
    )(y, *wp, *wp, res)


def _ple_kernel(*refs, tn, nw):
    x_ref, g_ref = refs[:2]
    wg_refs = refs[2:2 + nw]
    p_ref = refs[2 + nw]
    wu_refs = refs[3 + nw:3 + 2 * nw]
    o_ref = refs[3 + 2 * nw]
    xn_refs = refs[4 + 2 * nw:]
    j = pl.program_id(1)

    @pl.when(j == 0)
    def _():
        _store_norm(x_ref, g_ref, xn_refs)

    gate = _sigmoid(_mm(_read(xn_refs), _read(wg_refs)))
    up = _mm(_parts(p_ref[...], nw == 2), _read(wu_refs))
    x_tile = x_ref[:, pl.ds(pl.multiple_of(j * tn, tn), tn)]
    o_ref[...] = x_tile + up * gate


def ple(x, g, w_gate, p, w_up, *, tn):
    M, D = x.shape
    P = p.shape[1]
    wg, wu = _wparts(w_gate), _wparts(w_up)
    nw = len(wg)
    tm = _row_tile(M, nw)
    return pl.pallas_call(
        functools.partial(_ple_kernel, tn=tn, nw=nw),
        grid=(M // tm, D // tn),
        in_specs=[pl.BlockSpec((tm, D), lambda i, j: (i, 0)),
                  pl.BlockSpec((1, D), lambda i, j: (0, 0))]
                 + [pl.BlockSpec((D, tn), lambda i, j: (0, j))] * nw
                 + [pl.BlockSpec((tm, P), lambda i, j: (i, 0))]
                 + [pl.BlockSpec((P, tn), lambda i, j: (0, j))] * nw,
        out_specs=pl.BlockSpec((tm, tn), lambda i, j: (i, j)),
        out_shape=jax.ShapeDtypeStruct((M, D), F32),
        scratch_shapes=[pltpu.VMEM((tm, D), BF16)] * nw,
        compiler_params=_cparams("parallel", "arbitrary"),
        name="ple",
    )(x, g.reshape(1, D), *wg, p, *wu)


def _rmsnorm_kernel(x_ref, g_ref, o_ref):
    o_ref[...] = _rms_scale(x_ref[...], g_ref[...]).astype(o_ref.dtype)


def rmsnorm(x, g):
    M, D = x.shape
    tm = _row_tile(M)
    return pl.pallas_call(
        _rmsnorm_kernel,
        grid=(M // tm,),
        in_specs=[pl.BlockSpec((tm, D), lambda i: (i, 0)),
                  pl.BlockSpec((1, D), lambda i: (0, 0))],
        out_specs=pl.BlockSpec((tm, D), lambda i: (i, 0)),
        out_shape=jax.ShapeDtypeStruct((M, D), F32),
        compiler_params=_cparams("parallel"),
        name="rmsnorm",
    )(x, g.reshape(1, D))


def _small_proj_kernel(x_ref, g_ref, whi_ref, wlo_ref, aux_ref, o_ref, *, mode, nh):
    xn = _rms_scale(x_ref[...], g_ref[...])
    acc = _mm(_parts(xn, True), (whi_ref[...], wlo_ref[...]))
    lane = lax.broadcasted_iota(jnp.int32, acc.shape, 1)
    if mode == "logf":
        o_ref[...] = -_softplus(-(acc + aux_ref[0:1, :]))
    elif mode == "gdn":
        beta = _sigmoid(acc)
        gdec = -jnp.exp(aux_ref[1:2, :]) * _softplus(acc + aux_ref[0:1, :])
        o_ref[...] = jnp.where(lane < nh, beta, gdec)
    else:
        logits = jnp.where(lane < N_EXPERTS, acc, NEG_INF)
        v1 = jnp.max(logits, axis=1, keepdims=True)
        i1 = jnp.min(jnp.where(logits == v1, lane, LANES), axis=1, keepdims=True)
        rest = jnp.where(lane == i1, NEG_INF, logits)
        v2 = jnp.max(rest, axis=1, keepdims=True)
        i2 = jnp.min(jnp.where(rest == v2, lane, LANES), axis=1, keepdims=True)
        e2 = jnp.exp(v2 - v1)
        g1 = 1.0 / (1.0 + e2)
        o_ref[...] = jnp.where(lane == i1, g1, 0.0) + jnp.where(lane == i2, e2 * g1, 0.0)


def small_proj(x, g, w_parts, aux, mode, *, nh=0):
    M, D = x.shape
    tm = _row_tile(M)
    return pl.pallas_call(
        functools.partial(_small_proj_kernel, mode=mode, nh=nh),
        grid=(M // tm,),
        in_specs=[pl.BlockSpec((tm, D), lambda i: (i, 0)),
                  pl.BlockSpec((1, D), lambda i: (0, 0)),
                  pl.BlockSpec((D, LANES), lambda i: (0, 0)),
                  pl.BlockSpec((D, LANES), lambda i: (0, 0)),
                  pl.BlockSpec((SUBLANES, LANES), lambda i: (0, 0))],
        out_specs=pl.BlockSpec((tm, LANES), lambda i: (i, 0)),
        out_shape=jax.ShapeDtypeStruct((M, LANES), F32),
        compiler_params=_cparams("parallel"),
        name="small_proj_" + mode,
    )(x, g.reshape(1, D), *w_parts, aux)


def _pad_cols(w):
    return _split_bf16(jnp.pad(w.astype(F32), ((0, 0), (0, LANES - w.shape[1]))))


def _aux_rows(*rows):
    out = [jnp.pad(vals.astype(F32), (start, LANES - start - vals.shape[0])) for start, vals in rows]
    out += [jnp.zeros((LANES,), F32)] * (SUBLANES - len(out))
    return jnp.stack(out)


def _lane_cumsum_kernel(x_ref, o_ref):
    x = x_ref[...]
    n = x.shape[1]
    lane = lax.broadcasted_iota(jnp.int32, x.shape, 1)
    shift = 1
    while shift < n:
        x = x + jnp.where(lane >= shift, pltpu.roll(x, shift, 1), 0.0)
        shift *= 2
    o_ref[...] = x


def lane_cumsum(x):
    R, n = x.shape
    rb = SUBLANES if R % SUBLANES == 0 else R
    return pl.pallas_call(
        _lane_cumsum_kernel,
        grid=(R // rb,),
        in_specs=[pl.BlockSpec((rb, n), lambda i: (i, 0))],
        out_specs=pl.BlockSpec((rb, n), lambda i: (i, 0)),
        out_shape=jax.ShapeDtypeStruct((R, n), F32),
        compiler_params=_cparams("parallel"),
        name="lane_cumsum",
    )(x)


def _fox_prompt_kernel(q_ref, k_ref, v_ref, fk_ref, o_ref, *, tq, nhb, hd, scale):
    qi = pl.program_id(2)

    def step(ki, carry, diagonal):
        rows = pl.ds(pl.multiple_of(ki * tq, tq), tq)
        out = []
        for h in range(nhb):
            m, l, acc = carry[h]
            cols = slice(h * hd, (h + 1) * hd)
            s = _dot_nt(q_ref[:, cols], k_ref[rows, cols]) * scale - fk_ref[h, pl.ds(ki, 1), :]
            if diagonal:
                r = lax.broadcasted_iota(jnp.int32, s.shape, 0)
                c = lax.broadcasted_iota(jnp.int32, s.shape, 1)
                s = jnp.where(r >= c, s, NEG_INF)
            m_new = jnp.maximum(m, jnp.max(s, axis=1, keepdims=True))
            p = jnp.exp(s - m_new)
            alpha = jnp.exp(m - m_new)
            l = alpha * l + jnp.sum(p, axis=1, keepdims=True)
            acc = alpha * acc + _dot(p.astype(BF16), v_ref[rows, cols])
            out.append((m_new, l, acc))
        return tuple(out)

    init = tuple((jnp.full((tq, 1), NEG_INF, F32), jnp.zeros((tq, 1), F32), jnp.zeros((tq, hd), F32))
                 for _ in range(nhb))
    carry = lax.fori_loop(0, qi, lambda ki, c: step(ki, c, False), init)
    carry = step(qi, carry, True)
    o_ref[...] = jnp.concatenate([acc / l for _, l, acc in carry], axis=1).astype(o_ref.dtype)


def fox_attention_prompt(qkv, fk, *, B, L, H, hd, tq, nhb=2):
    nq = L // tq
    nhg = H // nhb
    return pl.pallas_call(
        functools.partial(_fox_prompt_kernel, tq=tq, nhb=nhb, hd=hd, scale=hd ** -0.5),
        grid=(B, nhg, nq),
        in_specs=[pl.BlockSpec((tq, nhb * hd), lambda b, g, i: (b * nq + i, g)),
                  pl.BlockSpec((L, nhb * hd), lambda b, g, i: (b, nhg + g)),
                  pl.BlockSpec((L, nhb * hd), lambda b, g, i: (b, 2 * nhg + g)),
                  pl.BlockSpec((nhb, nq, tq), lambda b, g, i: (b * nhg + g, 0, 0))],
        out_specs=pl.BlockSpec((tq, nhb * hd), lambda b, g, i: (b * nq + i, g)),
        out_shape=jax.ShapeDtypeStruct((B * L, H * hd), BF16),
        compiler_params=_cparams("parallel", "parallel", "arbitrary"),
        name="fox_attention_prompt",
    )(qkv, qkv, qkv, fk)


def _fox_cached_kernel(q_ref, kn_ref, vn_ref, kp_ref, vp_ref, f_ref, o_ref, *, P, scale, precise):
    q = _parts(q_ref[...], precise)
    n = q_ref.shape[0]
    f = f_ref[0]
    s_past = _mm_nt(q, _parts(kp_ref[0], precise)) * scale - f[:, :P]
    s_new = _mm_nt(q, _parts(kn_ref[...], precise)) * scale - f[:, P:P + n]
    r = lax.broadcasted_iota(jnp.int32, s_new.shape, 0)
    c = lax.broadcasted_iota(jnp.int32, s_new.shape, 1)
    s_new = jnp.where(r >= c, s_new, NEG_INF)
    m = jnp.maximum(jnp.max(s_past, axis=1, keepdims=True), jnp.max(s_new, axis=1, keepdims=True))
    p_past = jnp.exp(s_past - m)
    p_new = jnp.exp(s_new - m)
    l = jnp.sum(p_past, axis=1, keepdims=True) + jnp.sum(p_new, axis=1, keepdims=True)
    acc = (_mm(_parts(p_past, precise), _parts(vp_ref[0], precise))
           + _mm(_parts(p_new, precise), _parts(vn_ref[...], precise)))
    o_ref[...] = (acc / l).astype(o_ref.dtype)


def fox_attention_cached(qkv, k_past, v_past, f, *, seq0, k_lane0, v_lane0, B, n, P, H, hd):
    precise = qkv.dtype == F32
    return pl.pallas_call(
        functools.partial(_fox_cached_kernel, P=P, scale=hd ** -0.5, precise=precise),
        grid=(B, H),
        in_specs=[pl.BlockSpec((n, hd), lambda b, h: (b, h)),
                  pl.BlockSpec((n, hd), lambda b, h: (b, H + h)),
                  pl.BlockSpec((n, hd), lambda b, h: (b, 2 * H + h)),
                  pl.BlockSpec((1, P, hd), lambda b, h: (seq0 + b, 0, k_lane0 + h)),
                  pl.BlockSpec((1, P, hd), lambda b, h: (seq0 + b, 0, v_lane0 + h)),
                  pl.BlockSpec((1, 1, f.shape[2]), lambda b, h: (b * H + h, 0, 0))],
        out_specs=pl.BlockSpec((n, hd), lambda b, h: (b, h)),
        out_shape=jax.ShapeDtypeStruct((B * n, H * hd), qkv.dtype),
        compiler_params=_cparams("parallel", "arbitrary"),
        name="fox_attention_cached",
    )(qkv, qkv, qkv, k_past, v_past, f)


def _gelu_tanh(x):
    return 0.5 * x * (1.0 + jnp.tanh(math.sqrt(2.0 / math.pi) * (x + 0.044715 * (x * x * x))))


def _s5_kernel(*refs, nb, c, ns, ncm):
    u_ref, bhi_ref, blo_ref = refs[:3]
    cm_refs = refs[3:3 + ncm]
    ar_ref, ai_ref, d_ref, h0r_ref, h0i_ref, y_ref, hr_out, hi_out, bu_sc, st_sc, hr_sc, hi_sc = refs[3 + ncm:]
    tc = pl.program_id(1)

    @pl.when(tc == 0)
    def _():
        hr_sc[...] = h0r_ref[...]
        hi_sc[...] = h0i_ref[...]

    u = u_ref[...]
    precise = ncm == 2
    bu_sc[...] = _mm(_parts(u, precise), (bhi_ref[0], blo_ref[0]) if precise else (bhi_ref[0],))
    ar = jnp.broadcast_to(ar_ref[0], (nb, ns))
    ai = jnp.broadcast_to(ai_ref[0], (nb, ns))

    def step(t, carry):
        hr, hi = carry
        rows = pl.ds(pl.multiple_of(t * nb, nb), nb)
        nhr = ar * hr - ai * hi + bu_sc[rows, 0:ns]
        nhi = ar * hi + ai * hr + bu_sc[rows, ns:2 * ns]
        st_sc[rows, 0:ns] = nhr
        st_sc[rows, ns:2 * ns] = nhi
        return nhr, nhi

    hr, hi = lax.fori_loop(0, c, step, (hr_sc[...], hi_sc[...]))
    hr_sc[...] = hr
    hi_sc[...] = hi
    y = _mm(_parts(st_sc[...], ncm == 2), tuple(r[0] for r in cm_refs)) + d_ref[...] * u
    y_ref[...] = _gelu_tanh(y).astype(y_ref.dtype)

    @pl.when(tc == pl.num_programs(1) - 1)
    def _():
        hr_out[...] = hr
        hi_out[...] = hi


def s5_scan(u_tm, h0r, h0i, consts, *, nb, c, precise):
    bhi, blo, cm_hi, cm_lo, ar, ai, d = consts
    cms = (cm_hi, cm_lo) if precise else (cm_hi,)
    rows, D = u_tm.shape
    nblk = D // LANES
    ns = ar.shape[2]
    ntc = rows // (c * nb)
    blk = lambda g, t: (g, 0, 0)
    return pl.pallas_call(
        functools.partial(_s5_kernel, nb=nb, c=c, ns=ns, ncm=len(cms)),
        grid=(nblk, ntc),
        in_specs=[pl.BlockSpec((c * nb, LANES), lambda g, t: (t, g)),
                  pl.BlockSpec((1, LANES, 2 * ns), blk),
                  pl.BlockSpec((1, LANES, 2 * ns), blk)]
                 + [pl.BlockSpec((1, 2 * ns, LANES), blk)] * len(cms)
                 + [pl.BlockSpec((1, 1, ns), blk),
                    pl.BlockSpec((1, 1, ns), blk),
                    pl.BlockSpec((1, LANES), lambda g, t: (0, g)),
                    pl.BlockSpec((nb, ns), lambda g, t: (0, g)),
                    pl.BlockSpec((nb, ns), lambda g, t: (0, g))],
        out_specs=[pl.BlockSpec((c * nb, LANES), lambda g, t: (t, g)),
                   pl.BlockSpec((nb, ns), lambda g, t: (0, g)),
                   pl.BlockSpec((nb, ns), lambda g, t: (0, g))],
        out_shape=[jax.ShapeDtypeStruct((rows, D), F32 if precise else BF16),
                   jax.ShapeDtypeStruct(h0r.shape, F32),
                   jax.ShapeDtypeStruct(h0i.shape, F32)],
        scratch_shapes=[pltpu.VMEM((c * nb, 2 * ns), F32),
                        pltpu.VMEM((c * nb, 2 * ns), F32),
                        pltpu.VMEM((nb, ns), F32),
                        pltpu.VMEM((nb, ns), F32)],
        compiler_params=_cparams("parallel", "arbitrary"),
        name="s5_scan",
    )(u_tm, bhi, blo, *cms, ar, ai, d, h0r, h0i)


def s5_constants(a_re, a_im, b_re, b_im, c_re, c_im, d_skip, log_dt):
    G, N = a_re.shape
    gc = b_re.shape[2]
    gpb = LANES // gc
    nblk = G // gpb
    a = lax.complex(a_re.astype(F32), a_im.astype(F32))
    dt = jnp.exp(log_dt.astype(F32))[:, None]
    a_bar = jnp.exp(a * dt)
    b_bar = ((a_bar - 1.0) / a)[..., None] * lax.complex(b_re.astype(F32), b_im.astype(F32))
    eye = jnp.eye(gpb, dtype=F32)

    def in_mat(b):
        b = b.reshape(nblk, gpb, N, gc)
        return jnp.einsum("kgnc,gh->kgchn", b, eye).reshape(nblk, gpb * gc, gpb * N)

    def out_mat(cc):
        cc = cc.reshape(nblk, gpb, gc, N)
        return jnp.einsum("kgcn,gh->kgnhc", cc, eye).reshape(nblk, gpb * N, gpb * gc)

    bhi, blo = _split_bf16(jnp.concatenate([in_mat(b_bar.real), in_mat(b_bar.imag)], axis=2))
    cm_hi, cm_lo = _split_bf16(jnp.concatenate([out_mat(c_re.astype(F32)), -out_mat(c_im.astype(F32))], axis=1))
    ar = a_bar.real.reshape(nblk, 1, gpb * N)
    ai = a_bar.imag.reshape(nblk, 1, gpb * N)
    return bhi, blo, cm_hi, cm_lo, ar, ai, d_skip.astype(F32).reshape(1, G * gc)


def _gdn_conv_kernel(x_ref, halo_ref, buf_ref, w_ref, o_ref, ext_sc, *, tl, n_valid, tiles_per_seq, n_norm_tiles):
    r = pl.program_id(0)
    j = pl.program_id(1)
    first = (r % tiles_per_seq) == 0
    ext_sc[0:SUBLANES, :] = jnp.where(first, buf_ref[0], halo_ref[...])
    ext_sc[SUBLANES:SUBLANES + n_valid, :] = x_ref[...]
    w = w_ref[...]
    y = x_ref[...] * w[3:4, :]
    for s in (1, 2, 3):
        y = y + ext_sc[SUBLANES - s:SUBLANES - s + n_valid, :] * w[3 - s:4 - s, :]
    y = _silu(y)

    def store(val):
        if n_valid < tl:
            val = jnp.concatenate([val, jnp.zeros((tl - n_valid, val.shape[1]), F32)], axis=0)
        o_ref[...] = val.astype(o_ref.dtype)

    @pl.when(j < n_norm_tiles)
    def _():
        segs = []
        for a in range(0, y.shape[1], LANES):
            seg = y[:, a:a + LANES]
            segs.append(seg * lax.rsqrt(jnp.sum(seg * seg, axis=1, keepdims=True) + EPS))
        store(jnp.concatenate(segs, axis=1))

    @pl.when(j >= n_norm_tiles)
    def _():
        store(y)


def gdn_conv(x, buf8, w8, *, n_seq, seq_len, tl_in, tl_out, n_norm_cols, tc):
    C = x.shape[1]
    tiles_per_seq = seq_len // tl_in
    n_tiles = n_seq * tiles_per_seq
    hpt = tl_in // SUBLANES
    return pl.pallas_call(
        functools.partial(_gdn_conv_kernel, tl=tl_out, n_valid=tl_in, tiles_per_seq=tiles_per_seq,
                          n_norm_tiles=n_norm_cols // tc),
        grid=(n_tiles, C // tc),
        in_specs=[pl.BlockSpec((tl_in, tc), lambda r, j: (r, j)),
                  pl.BlockSpec((SUBLANES, tc), lambda r, j: (jnp.maximum(r * hpt - 1, 0), j)),
                  pl.BlockSpec((1, SUBLANES, tc), lambda r, j: (r // tiles_per_seq, 0, j)),
                  pl.BlockSpec((SUBLANES, tc), lambda r, j: (0, j))],
        out_specs=pl.BlockSpec((tl_out, tc), lambda r, j: (r, j)),
        out_shape=jax.ShapeDtypeStruct((n_tiles * tl_out, C), BF16),
        scratch_shapes=[pltpu.VMEM((SUBLANES + tl_in, tc), F32)],
        compiler_params=_cparams("parallel", "arbitrary"),
        name="gdn_conv",
    )(x, x, buf8, w8)


def _gdn_prep_kernel(qk_ref, gt_ref, n_ref, a_ref, gc_ref, *, C, n_qk, rep, hd, scale):
    gates = gt_ref[...]
    nh = n_qk * rep
    lane = lax.broadcasted_iota(jnp.int32, gates.shape, 1)
    g = jnp.where((lane >= nh) & (lane < 2 * nh), gates, 0.0)
    r = lax.broadcasted_iota(jnp.int32, (C, C), 0)
    c = lax.broadcasted_iota(jnp.int32, (C, C), 1)
    tril = (r >= c).astype(BF16)
    g_hi = g.astype(BF16)
    g_r1 = g - g_hi.astype(F32)
    g_mid = g_r1.astype(BF16)
    g_lo = (g_r1 - g_mid.astype(F32)).astype(BF16)
    gcum = _dot(tril, g_hi) + _dot(tril, g_mid) + _dot(tril, g_lo)
    gc_ref[...] = gcum
    gcum_t = gcum.T
    for hq in range(n_qk):
        q = qk_ref[:, hq * hd:(hq + 1) * hd]
        k = qk_ref[:, (n_qk + hq) * hd:(n_qk + hq + 1) * hd]
        kk = _dot_nt(k, k)
        qk = _dot_nt(q, k) * scale
        for rr in range(rep):
            hv = hq * rep + rr
            diff = gcum[:, nh + hv:nh + hv + 1] - gcum_t[nh + hv:nh + hv + 1, :]
            decay = jnp.exp(jnp.where(r >= c, diff, NEG_INF))
            n_ref[0, hv] = jnp.where(r > c, gates[:, hv:hv + 1] * kk * decay, 0.0)
            a_ref[0, hv] = (qk * decay).astype(a_ref.dtype)


def gdn_prep(qkv_c, gates, *, C, n_qk, rep, hd):
    rows = qkv_c.shape[0]
    nch = rows // C
    nh = n_qk * rep
    return pl.pallas_call(
        functools.partial(_gdn_prep_kernel, C=C, n_qk=n_qk, rep=rep, hd=hd, scale=hd ** -0.5),
        grid=(nch,),
        in_specs=[pl.BlockSpec((C, 2 * n_qk * hd), lambda n: (n, 0)),
                  pl.BlockSpec((C, LANES), lambda n: (n, 0))],
        out_specs=[pl.BlockSpec((1, nh, C, C), lambda n: (n, 0, 0, 0)),
                   pl.BlockSpec((1, nh, C, C), lambda n: (n, 0, 0, 0)),
                   pl.BlockSpec((C, LANES), lambda n: (n, 0))],
        out_shape=[jax.ShapeDtypeStruct((nch, nh, C, C), F32),
                   jax.ShapeDtypeStruct((nch, nh, C, C), BF16),
                   jax.ShapeDtypeStruct((rows, LANES), F32)],
        compiler_params=_cparams("parallel"),
        name="gdn_prep",
    )(qkv_c, gates)


def _tri_inv_kernel(n_ref, x_ref, *, C):
    nblk = C // SUBLANES
    lanes = n_ref.shape[1]
    sub = lax.broadcasted_iota(jnp.int32, (SUBLANES, lanes), 0)
    x_ref[...] = jnp.zeros(x_ref.shape, F32)
    for ib in range(nblk):
        def row(r, _, ib=ib):
            base = (ib * SUBLANES + r) * C
            acc = [jnp.zeros((SUBLANES, lanes), F32) for _ in range(ib)]
            acc.append(jnp.where(sub == r, 1.0, 0.0).astype(F32))
            for jb in range(ib + 1):
                for j in range(jb * SUBLANES, (jb + 1) * SUBLANES):
                    nb = jnp.broadcast_to(n_ref[pl.ds(base + j, 1), :], (SUBLANES, lanes))
                    for cb in range(jb + 1):
                        acc[cb] = acc[cb] - nb * x_ref[j * C + cb * SUBLANES:j * C + (cb + 1) * SUBLANES, :]
            for cb in range(ib + 1):
                x_ref[pl.ds(pl.multiple_of(base + cb * SUBLANES, SUBLANES), SUBLANES), :] = acc[cb]
            return 0

        lax.fori_loop(0, SUBLANES, row, 0)


def tri_inv(n_t, *, C):
    R, S = n_t.shape
    tl = LANES if S % LANES == 0 else S
    return pl.pallas_call(
        functools.partial(_tri_inv_kernel, C=C),
        grid=(S // tl,),
        in_specs=[pl.BlockSpec((R, tl), lambda i: (0, i))],
        out_specs=pl.BlockSpec((R, tl), lambda i: (0, i)),
        out_shape=jax.ShapeDtypeStruct((R, S), F32),
        compiler_params=_cparams("parallel"),
        name="tri_inv",
    )(n_t)


def _gdn_scan_kernel(q_ref, k_ref, v_ref, z_ref, gt_ref, gc_ref, ti_ref, a_ref, s0_ref, ng_ref,
                     o_ref, s_out, s_sc, *, C, hg, rep, hd, nh, n_valid, scale):
    n = pl.program_id(2)
    hgi = pl.program_id(1)

    @pl.when(n == 0)
    def _():
        s_sc[...] = s0_ref[0]

    shift = (LANES - hgi * hg) % LANES
    gates = pltpu.roll(gt_ref[...], shift, 1)
    gcum = pltpu.roll(gc_ref[...], shift, 1)
    heads = range(hg)
    s_old = [s_sc[hh] for hh in heads]
    s_bf = [s.astype(BF16) for s in s_old]
    beta = [gates[:, hh:hh + 1] for hh in heads]
    gc = [gcum[:, nh + hh:nh + hh + 1] for hh in heads]
    eg = [jnp.exp(g) for g in gc]
    g_tot = [g[C - 1:C, :] for g in gc]
    k = [k_ref[:, (hh // rep) * hd:(hh // rep + 1) * hd].astype(F32) for hh in heads]
    q = [q_ref[:, (hh // rep) * hd:(hh // rep + 1) * hd].astype(F32) for hh in heads]
    v = [v_ref[:, hh * hd:(hh + 1) * hd].astype(F32) for hh in heads]
    uw = [_dot(ti_ref[0, hh].astype(BF16),
               jnp.concatenate([v[hh] * beta[hh], k[hh] * (beta[hh] * eg[hh])], axis=1).astype(BF16))
          for hh in heads]
    q_dec = [(q[hh] * (scale * eg[hh])).astype(BF16) for hh in heads]
    k_dec = [(k[hh] * jnp.exp(g_tot[hh] - gc[hh])).astype(BF16) for hh in heads]
    v_new = [(uw[hh][:, :hd] - _dot(uw[hh][:, hd:].astype(BF16), s_bf[hh])).astype(BF16) for hh in heads]
    o = [_dot(q_dec[hh], s_bf[hh]) + _dot(a_ref[0, hh], v_new[hh]) for hh in heads]
    s_new = [s_old[hh] * jnp.exp(g_tot[hh]) + _dot_tn(k_dec[hh], v_new[hh]) for hh in heads]
    for hh in heads:
        s_sc[hh] = s_new[hh]
    ng = ng_ref[...]
    outs = [_rms_scale(o[hh][:n_valid], ng) * _silu(z_ref[:, hh * hd:(hh + 1) * hd].astype(F32)) for hh in heads]
    o_ref[...] = jnp.concatenate(outs, axis=1).astype(o_ref.dtype)

    @pl.when(n == pl.num_programs(2) - 1)
    def _():
        s_out[0] = s_sc[...]


def gdn_scan(qkv_c, z, gates, gcum, tinv, amat, s0, norm_g, *, n_seq, nc, C, n_valid, n_qk, rep, hd, hg):
    nh = n_qk * rep
    nhg = nh // hg
    qw = (hg // rep) * hd
    k_off = n_qk * hd // qw
    v_off = 2 * n_qk * hd // (hg * hd)
    chunk = lambda s, g, n: s * nc + n
    return pl.pallas_call(
        functools.partial(_gdn_scan_kernel, C=C, hg=hg, rep=rep, hd=hd, nh=nh, n_valid=n_valid,
                          scale=hd ** -0.5),
        grid=(n_seq, nhg, nc),
        in_specs=[pl.BlockSpec((C, qw), lambda s, g, n: (chunk(s, g, n), g)),
                  pl.BlockSpec((C, qw), lambda s, g, n: (chunk(s, g, n), k_off + g)),
                  pl.BlockSpec((C, hg * hd), lambda s, g, n: (chunk(s, g, n), v_off + g)),
                  pl.BlockSpec((n_valid, hg * hd), lambda s, g, n: (chunk(s, g, n), g)),
                  pl.BlockSpec((C, LANES), lambda s, g, n: (chunk(s, g, n), 0)),
                  pl.BlockSpec((C, LANES), lambda s, g, n: (chunk(s, g, n), 0)),
                  pl.BlockSpec((1, hg, C, C), lambda s, g, n: (chunk(s, g, n), g, 0, 0)),
                  pl.BlockSpec((1, hg, C, C), lambda s, g, n: (chunk(s, g, n), g, 0, 0)),
                  pl.BlockSpec((1, hg, hd, hd), lambda s, g, n: (s, g, 0, 0)),
                  pl.BlockSpec((1, hd), lambda s, g, n: (0, 0))],
        out_specs=[pl.BlockSpec((n_valid, hg * hd), lambda s, g, n: (chunk(s, g, n), g)),
                   pl.BlockSpec((1, hg, hd, hd), lambda s, g, n: (s, g, 0, 0))],
        out_shape=[jax.ShapeDtypeStruct((n_seq * nc * n_valid, nh * hd), BF16),
                   jax.ShapeDtypeStruct((n_seq, nh, hd, hd), F32)],
        scratch_shapes=[pltpu.VMEM((hg, hd, hd), F32)],
        compiler_params=_cparams("parallel", "parallel", "arbitrary"),
        name="gdn_scan",
    )(qkv_c, qkv_c, qkv_c, z, gates, gcum, tinv, amat, s0, norm_g.reshape(1, hd))


def _trunk(x, p, fox_hist, s5_h0, gdn_state, W, *, n_seq, seq_len, steps, precise):
    M, D = x.shape
    H, hd = W["fox_heads"]
    n_qk, rep, gd = W["gdn_heads"]
    n_vh = n_qk * rep
    key_dim, val_dim = n_qk * gd, n_vh * gd
    qkv_dim = 2 * key_dim + val_dim
    cw = W["gdn_conv_w"].shape[1]
    tn = min(512, D)
    outs = {}

    def wsel(name, idx, pr):
        hi = W[name][idx]
        return (hi, W[name + "_lo"][idx]) if pr else hi

    def mixer(i, x):
        j = i // 3
        pm = precise
        g_mix = W["norm_mix"][i]
        if i % 3 == 0:
            nt = D // tn
            qkv_dt = F32 if pm else BF16
            qkv, kv = norm_mm(x, g_mix, wsel("fox_w_qkv", j, pm), [(qkv_dt, 0, 3 * nt), (F32, nt, 3 * nt)], tn=tn)
            logf = small_proj(x, g_mix, W["fox_w_f"][j], W["fox_aux"][j], "logf")[:, :H]
            lf = jnp.swapaxes(logf.reshape(n_seq, seq_len, H), 1, 2)
            if j not in fox_hist:
                tq = next(t for t in (512, 384, 256, 128, seq_len) if seq_len % t == 0)
                fk = lane_cumsum(lf.reshape(n_seq * H, seq_len)).reshape(n_seq * H, seq_len // tq, tq)
                o = fox_attention_prompt(qkv, fk, B=n_seq, L=seq_len, H=H, hd=hd, tq=tq)
            else:
                k_past, v_past, k_lane0, v_lane0, seq0, lf_past, P = fox_hist[j]
                fpad = -(-(P + seq_len) // LANES) * LANES
                lf_all = jnp.concatenate([jnp.swapaxes(lf_past.astype(F32), 1, 2), lf], axis=2)
                lf_all = jnp.pad(lf_all, ((0, 0), (0, 0), (0, fpad - (P + seq_len))))
                f = lane_cumsum(lf_all.reshape(n_seq * H, fpad)).reshape(n_seq * H, 1, fpad)
                o = fox_attention_cached(qkv, k_past, v_past, f, seq0=seq0, k_lane0=k_lane0, v_lane0=v_lane0,
                                         B=n_seq, n=seq_len, P=P, H=H, hd=hd)
            outs[("fox", j)] = (kv, logf)
            return mm_res(o, wsel("fox_w_out", j, pm), x, tn=tn, tk=D)
        if i % 3 == 1:
            u = rmsnorm(x, g_mix)
            u_tm = jnp.swapaxes(u.reshape(n_seq, seq_len, D), 0, 1).reshape(M, D)
            y_tm, hr, hi = s5_scan(u_tm, s5_h0[j][0], s5_h0[j][1], W["s5_consts"][j],
                                   nb=n_seq, c=min(CHUNK, seq_len), precise=pm)
            y = jnp.swapaxes(y_tm.reshape(seq_len, n_seq, D), 0, 1).reshape(M, D)
            outs[("s5", j)] = (hr, hi)
            return glu_res(y, wsel("s5_w_glu", j, pm), x, tn=tn)
        s0, conv_rows = gdn_state[j]
        nq = qkv_dim // tn
        nz = val_dim // tn
        qkv_raw, z = norm_mm(x, g_mix, W["gdn_w_qkvz"][j], [(F32, 0, nq), (BF16, nq, nq + nz)], tn=tn)
        gates = small_proj(x, g_mix, W["gdn_w_ba"][j], W["gdn_aux"][j], "gdn", nh=n_vh)
        C = CHUNK
        n_valid = min(C, seq_len)
        nc = seq_len // n_valid
        buf = jnp.pad(conv_rows.astype(F32), ((0, 0), (SUBLANES - (cw - 1), 0), (0, 0)))
        tl_in = min(512, seq_len)
        qkv_c = gdn_conv(qkv_raw, buf, W["gdn_conv_w8"][j], n_seq=n_seq, seq_len=seq_len, tl_in=tl_in,
                         tl_out=max(tl_in, C), n_norm_cols=2 * key_dim, tc=min(512, key_dim))
        gates_c = jnp.pad(gates.reshape(n_seq * nc, n_valid, LANES),
                          ((0, 0), (0, C - n_valid), (0, 0))).reshape(n_seq * nc * C, LANES)
        nmat, amat, gcum = gdn_prep(qkv_c, gates_c, C=C, n_qk=n_qk, rep=rep, hd=gd)
        nch = nmat.shape[0]
        tinv = tri_inv(nmat.reshape(nch * n_vh, C * C).T, C=C).T.reshape(nch, n_vh, C, C)
        o, s_last = gdn_scan(qkv_c, z, gates_c, gcum, tinv, amat, s0.astype(F32), W["gdn_norm"][j],
                             n_seq=n_seq, nc=nc, C=C, n_valid=n_valid, n_qk=n_qk, rep=rep, hd=gd,
                             hg=min(8, n_vh))
        rows = jnp.concatenate([conv_rows.astype(F32), qkv_raw.reshape(n_seq, seq_len, qkv_dim)], axis=1)
        outs[("gdn", j)] = (s_last, rows[:, -(cw - 1):])
        return mm_res(o, W["gdn_w_out"][j], x, tn=tn, tk=D)

    def channel(i, x):
        m = i // 2
        g_ffn = W["norm_ffn"][i]
        if i % 2 == 0:
            pf = precise
            h = glu_up(x, g_ffn, wsel("ffn_w_gate", m, pf), wsel("ffn_w_up", m, pf), tn=tn)
            x = mm_res(h, wsel("ffn_w_down", m, pf), x, tn=tn, tk=D)
        else:
            pf = False
            comb = small_proj(x, g_ffn, W["moe_w_router"][m], W["zero_aux"], "router")
            h = glu_up(x, g_ffn, W["moe_w_gate"][m], W["moe_w_up"][m], comb, tn=tn)
            x = mm_res(h, W["moe_w_down"][m], x, tn=tn, tk=D)
        p_i = p[i] if pf else p[i].astype(BF16)
        return ple(x, W["norm_ple"][i], wsel("ple_w_gate", i, pf), p_i, wsel("ple_w_up", i, pf), tn=tn)

    for i, part in steps:
        x = mixer(i, x) if part == "mixer" else channel(i, x)
    return x, outs


def kernel(x_prompt, x_sample, cache_fox_k, cache_fox_v, cache_fox_logf, state_s5_re, state_s5_im, state_gdn, state_gdn_conv, p_prompt, p_sample, norm_mix, norm_ffn, norm_ple, norm_final, fox_w_in, fox_b_f, fox_w_out, s5_a_re, s5_a_im, s5_b_re, s5_b_im, s5_c_re, s5_c_im, s5_d, s5_log_dt, s5_w_glu, gdn_w_in, gdn_conv_w, gdn_a_log, gdn_dt_bias, gdn_norm, gdn_w_out, ffn_w_gate, ffn_w_up, ffn_w_down, moe_w_router, moe_w_gate, moe_w_up, moe_w_down, ple_w_up, ple_w_gate):
    B, L, D = x_prompt.shape
    Bs, Ls, _ = x_sample.shape
    depth = norm_mix.shape[0]
    P = cache_fox_k.shape[2]
    H, hd = cache_fox_k.shape[3], cache_fox_k.shape[4]
    n_vh, gd = state_gdn.shape[2], state_gdn.shape[3]
    qkv_dim = state_gdn_conv.shape[3]
    val_dim = n_vh * gd
    n_qk = (qkv_dim - val_dim) // (2 * gd)
    cw = gdn_conv_w.shape[1]
    G, NS = s5_a_re.shape[1], s5_a_re.shape[2]
    E, _, Fe = moe_w_gate.shape[1:]
    n_fox, n_s5, n_gdn = fox_w_in.shape[0], s5_a_re.shape[0], gdn_w_in.shape[0]

    first_router = 1
    n_lo = {"fox": 1, "ffn": 1, "ple": 1, "s5": 1}

    def hi_lo(w, n):
        hi = w.astype(BF16)
        lo = (w[:n].astype(F32) - hi[:n].astype(F32)).astype(BF16)
        return hi, lo

    W = {"norm_mix": norm_mix, "norm_ffn": norm_ffn, "norm_ple": norm_ple, "norm_final": norm_final,
         "fox_heads": (H, hd), "gdn_heads": (n_qk, n_vh // n_qk, gd), "gdn_conv_w": gdn_conv_w,
         "gdn_norm": gdn_norm, "zero_aux": jnp.zeros((SUBLANES, LANES), F32)}
    W["fox_w_qkv"], W["fox_w_qkv_lo"] = hi_lo(fox_w_in[:, :, :3 * D], n_lo["fox"])
    W["fox_w_out"], W["fox_w_out_lo"] = hi_lo(fox_w_out, n_lo["fox"])
    W["fox_w_f"] = [_pad_cols(fox_w_in[j, :, 3 * D:]) for j in range(n_fox)]
    W["fox_aux"] = [_aux_rows((0, fox_b_f[j])) for j in range(n_fox)]
    W["s5_w_glu"], W["s5_w_glu_lo"] = hi_lo(s5_w_glu, n_lo["s5"])
    W["s5_consts"] = [s5_constants(s5_a_re[j], s5_a_im[j], s5_b_re[j], s5_b_im[j], s5_c_re[j], s5_c_im[j],
                                   s5_d[j], s5_log_dt[j]) for j in range(n_s5)]
    W["gdn_w_qkvz"] = gdn_w_in[:, :, :qkv_dim + val_dim].astype(BF16)
    W["gdn_w_ba"] = [_pad_cols(gdn_w_in[j, :, qkv_dim + val_dim:]) for j in range(n_gdn)]
    W["gdn_aux"] = [_aux_rows((n_vh, gdn_dt_bias[j]), (n_vh, gdn_a_log[j])) for j in range(n_gdn)]
    W["gdn_conv_w8"] = jnp.pad(gdn_conv_w.astype(F32), ((0, 0), (0, SUBLANES - cw), (0, 0)))
    W["gdn_w_out"] = gdn_w_out.astype(BF16)
    W["ffn_w_gate"], W["ffn_w_gate_lo"] = hi_lo(ffn_w_gate, n_lo["ffn"])
    W["ffn_w_up"], W["ffn_w_up_lo"] = hi_lo(ffn_w_up, n_lo["ffn"])
    W["ffn_w_down"], W["ffn_w_down_lo"] = hi_lo(ffn_w_down, n_lo["ffn"])
    W["moe_w_router"] = [_pad_cols(moe_w_router[m]) for m in range(moe_w_router.shape[0])]
    W["moe_w_gate"] = moe_w_gate.astype(BF16)
    W["moe_w_up"] = moe_w_up.astype(BF16)
    W["moe_w_down"] = moe_w_down.astype(BF16).reshape(-1, E * Fe, D)
    W["ple_w_gate"], W["ple_w_gate_lo"] = hi_lo(ple_w_gate, n_lo["ple"])
    W["ple_w_up"], W["ple_w_up_lo"] = hi_lo(ple_w_up, n_lo["ple"])

    steps = [(i, part) for i in range(depth) for part in ("mixer", "channel")]
    steps_a, steps_b = steps[:2 * first_router + 1], steps[2 * first_router + 1:]
    pd = p_prompt.shape[-1]

    Lt = LANES
    Lh = L - Lt
    p_rows = lambda sl: {i: p_prompt[i][:, sl].reshape(-1, pd) for i in range(depth)}
    zeros_s5 = jnp.zeros((B, G * NS), F32)
    xh, out_h = _trunk(x_prompt[:, :Lh].reshape(B * Lh, D), p_rows(slice(0, Lh)), {},
                       {j: (zeros_s5, zeros_s5) for j in range(n_s5)}, {}, W,
                       n_seq=B, seq_len=Lh, steps=steps_a, precise=False)
    hist_t = {}
    for j in range(n_fox):
        if ("fox", j) in out_h:
            kv_h, lf_h = out_h[("fox", j)]
            kv_h = kv_h.reshape(B, Lh, 2 * D)
            hist_t[j] = (kv_h, kv_h, 0, H, 0, lf_h.reshape(B, Lh, H), Lh)
    xt, out_t = _trunk(x_prompt[:, Lh:].reshape(B * Lt, D), p_rows(slice(Lh, L)), hist_t,
                       {j: out_h[("s5", j)] for j in range(n_s5) if ("s5", j) in out_h}, {}, W,
                       n_seq=B, seq_len=Lt, steps=steps_a, precise=True)
    x_p = jnp.concatenate([xh.reshape(B, Lh, D), xt.reshape(B, Lt, D)], axis=1).reshape(B * L, D)
    gdn0 = (jnp.zeros((B, n_vh, gd, gd), F32), jnp.zeros((B, cw - 1, qkv_dim), F32))
    x_p, out_p = _trunk(x_p, p_rows(slice(0, L)), {}, {j: (zeros_s5, zeros_s5) for j in range(n_s5)},
                        {j: gdn0 for j in range(n_gdn)}, W, n_seq=B, seq_len=L, steps=steps_b, precise=False)
    for key, val in out_t.items():
        if key[0] == "fox":
            kv_h, lf_h = out_h[key]
            out_p[key] = (jnp.concatenate([kv_h.reshape(B, Lh, 2 * D), val[0].reshape(B, Lt, 2 * D)], axis=1),
                          jnp.concatenate([lf_h.reshape(B, Lh, H), val[1].reshape(B, Lt, H)], axis=1))
        else:
            out_p[key] = val

    ck = cache_fox_k.reshape(-1, P, H * hd)
    cv = cache_fox_v.reshape(-1, P, H * hd)
    s5_re = state_s5_re.reshape(n_s5, Bs, G * NS).astype(F32)
    s5_im = state_s5_im.reshape(n_s5, Bs, G * NS).astype(F32)
    hist_s = {j: (ck, cv, 0, 0, j * Bs, cache_fox_logf[j], P) for j in range(n_fox)}
    state_s = ({j: (s5_re[j], s5_im[j]) for j in range(n_s5)},
               {j: (state_gdn[j], state_gdn_conv[j]) for j in range(n_gdn)})
    p_s = {i: p_sample[i].reshape(Bs * Ls, pd) for i in range(depth)}
    x_s, out_s = _trunk(x_sample.reshape(Bs * Ls, D), p_s, hist_s, *state_s, W,
                        n_seq=Bs, seq_len=Ls, steps=steps_a, precise=True)
    x_s, out_sb = _trunk(x_s, p_s, hist_s, *state_s, W, n_seq=Bs, seq_len=Ls, steps=steps_b, precise=False)
    out_s.update(out_sb)

    def leaves(x, out, nb, sl):
        y = rmsnorm(x, norm_final).reshape(nb, sl, D)
        kv = [out[("fox", j)][0].reshape(nb, sl, 2, H, hd) for j in range(n_fox)]
        fk = jnp.stack([a[:, :, 0] for a in kv])
        fv = jnp.stack([a[:, :, 1] for a in kv])
        flf = jnp.stack([out[("fox", j)][1].reshape(nb, sl, -1)[..., :H] for j in range(n_fox)])
        sre = jnp.stack([out[("s5", j)][0].reshape(nb, G, NS) for j in range(n_s5)])
        sim = jnp.stack([out[("s5", j)][1].reshape(nb, G, NS) for j in range(n_s5)])
        gs = jnp.stack([out[("gdn", j)][0] for j in range(n_gdn)])
        gc = jnp.stack([out[("gdn", j)][1] for j in range(n_gdn)])
        return y, fk, fv, flf, sre, sim, gs, gc

    (y_p, fk_p, fv_p, flf_p, sre_p, sim_p, gs_p, gc_p) = leaves(x_p, out_p, B, L)
    (y_s, fk_s, fv_s, flf_s, sre_s, sim_s, gs_s, gc_s) = leaves(x_s, out_s, Bs, Ls)
    return (y_p, y_s, fk_p, fv_p, flf_p, fk_s, fv_s, flf_s, sre_p, sim_p, sre_s, sim_s,
            gs_p, gc_p, gs_s, gc_s)
```

```python
import functools
import math

import jax
import jax.numpy as jnp
from jax import lax
from jax.experimental import pallas as pl
from jax.experimental.pallas import tpu as pltpu

F32 = jnp.float32
BF16 = jnp.bfloat16
EPS = 1e-6
LANES = 128
SUBLANES = 8
VMEM_LIMIT = 56 * 1024 * 1024
CHUNK = 64
N_EXPERTS = 8
NEG_INF = float("-inf")


def _cparams(*sem):
    return pltpu.CompilerParams(dimension_semantics=sem, vmem_limit_bytes=VMEM_LIMIT)


def _rms_scale(x, g):
    ms = jnp.mean(x * x, axis=-1, keepdims=True)
    return x * lax.rsqrt(ms + EPS) * g


def _sigmoid(x):
    return 1.0 / (1.0 + jnp.exp(-x))


def _silu(x):
    return x * _sigmoid(x)


def _softplus(x):
    return jnp.maximum(x, 0.0) + jnp.log1p(jnp.exp(-jnp.abs(x)))


def _split_bf16(x):
    hi = x.astype(BF16)
    lo = (x - hi.astype(F32)).astype(BF16)
    return hi, lo


def _split_weight(w):
    w = w.astype(F32)
    hi = lax.reduce_precision(w, exponent_bits=8, mantissa_bits=7)
    return hi.astype(BF16), (w - hi).astype(BF16)


def _dot(a, b):
    return jnp.dot(a, b, preferred_element_type=F32)


def _dot_nt(a, b):
    return lax.dot_general(a, b, (((1,), (1,)), ((), ())), preferred_element_type=F32)


def _dot_tn(a, b):
    return lax.dot_general(a, b, (((0,), (0,)), ((), ())), preferred_element_type=F32)


def _parts(x, precise):
    return _split_bf16(x.astype(F32)) if precise else (x.astype(BF16),)


def _mm(xp, wp):
    if len(wp) == 1:
        return _dot(xp[0], wp[0])
    return _dot(xp[0], wp[0]) + _dot(xp[1], wp[0]) + _dot(xp[0], wp[1])


def _mm_nt(ap, bp):
    if len(ap) == 1:
        return _dot_nt(ap[0], bp[0])
    return _dot_nt(ap[0], bp[0]) + _dot_nt(ap[1], bp[0]) + _dot_nt(ap[0], bp[1])


def _read(refs):
    return tuple(r[...] for r in refs)


def _wparts(w):
    return tuple(w) if isinstance(w, (tuple, list)) else (w,)


def _row_tile(m, n_weight_parts=1):
    cap = 1024 // n_weight_parts
    return next(t for t in (1024, 512, 256, 128) if t <= cap and m % t == 0)


def _store_norm(x_ref, g_ref, xn_refs):
    xn = _rms_scale(x_ref[...], g_ref[...])
    for ref, part in zip(xn_refs, _parts(xn, len(xn_refs) == 2)):
        ref[...] = part


def _norm_mm_kernel(*refs, ranges, nw):
    x_ref, g_ref = refs[:2]
    w_refs = refs[2:2 + nw]
    outs = refs[2 + nw:2 + nw + len(ranges)]
    xn_refs = refs[2 + nw + len(ranges):]
    j = pl.program_id(1)

    @pl.when(j == 0)
    def _():
        _store_norm(x_ref, g_ref, xn_refs)

    acc = _mm(_read(xn_refs), _read(w_refs))
    for o_ref, (a, b) in zip(outs, ranges):
        @pl.when((j >= a) & (j < b))
        def _(o_ref=o_ref):
            o_ref[...] = acc.astype(o_ref.dtype)


def _range_map(i, j, *, a, n):
    return (i, jnp.clip(j - a, 0, n - 1))


def norm_mm(x, g, w, outs, *, tn):
    M, D = x.shape
    wp = _wparts(w)
    N = wp[0].shape[1]
    tm = _row_tile(M, len(wp))
    ranges = tuple((a, b) for _, a, b in outs)
    return pl.pallas_call(
        functools.partial(_norm_mm_kernel, ranges=ranges, nw=len(wp)),
        grid=(M // tm, N // tn),
        in_specs=[pl.BlockSpec((tm, D), lambda i, j: (i, 0)),
                  pl.BlockSpec((1, D), lambda i, j: (0, 0))]
                 + [pl.BlockSpec((D, tn), lambda i, j: (0, j))] * len(wp),
        out_specs=[pl.BlockSpec((tm, tn), functools.partial(_range_map, a=a, n=b - a))
                   for _, a, b in outs],
        out_shape=[jax.ShapeDtypeStruct((M, (b - a) * tn), dt) for dt, a, b in outs],
        scratch_shapes=[pltpu.VMEM((tm, D), BF16)] * len(wp),
        compiler_params=_cparams("parallel", "arbitrary"),
        name="norm_mm",
    )(x, g.reshape(1, D), *wp)


def _glu_up_kernel(*refs, tiles_per_expert, nw):
    x_ref, g_ref = refs[:2]
    wg_refs = refs[2:2 + nw]
    wu_refs = refs[2 + nw:2 + 2 * nw]
    rest = refs[2 + 2 * nw:]
    comb_ref = rest[0] if tiles_per_expert else None
    o_ref = rest[1] if tiles_per_expert else rest[0]
    xn_refs = rest[2:] if tiles_per_expert else rest[1:]
    j = pl.program_id(1)

    @pl.when(j == 0)
    def _():
        _store_norm(x_ref, g_ref, xn_refs)

    xn = _read(xn_refs)
    h = _silu(_mm(xn, _read(wg_refs))) * _mm(xn, _read(wu_refs))
    if tiles_per_expert:
        comb = comb_ref[...]
        lane = lax.broadcasted_iota(jnp.int32, comb.shape, 1)
        col = jnp.sum(jnp.where(lane == j // tiles_per_expert, comb, 0.0), axis=1, keepdims=True)
        h = h * col
    o_ref[...] = h.astype(o_ref.dtype)


def glu_up(x, g, w_gate, w_up, comb=None, *, tn):
    M, D = x.shape
    wg, wu = _wparts(w_gate), _wparts(w_up)
    nw = len(wg)
    tm = _row_tile(M, nw)
    if comb is None:
        n_out, tpe = wg[0].shape[1], 0
        w_spec = pl.BlockSpec((D, tn), lambda i, j: (0, j))
        extra_in, extra_specs = (), []
    else:
        E, _, F = wg[0].shape
        n_out, tpe = E * F, F // tn
        w_spec = pl.BlockSpec((None, D, tn), lambda i, j: (j // tpe, 0, j % tpe))
        extra_in = (comb,)
        extra_specs = [pl.BlockSpec((tm, LANES), lambda i, j: (i, 0))]
    return pl.pallas_call(
        functools.partial(_glu_up_kernel, tiles_per_expert=tpe, nw=nw),
        grid=(M // tm, n_out // tn),
        in_specs=[pl.BlockSpec((tm, D), lambda i, j: (i, 0)),
                  pl.BlockSpec((1, D), lambda i, j: (0, 0))] + [w_spec] * (2 * nw) + extra_specs,
        out_specs=pl.BlockSpec((tm, tn), lambda i, j: (i, j)),
        out_shape=jax.ShapeDtypeStruct((M, n_out), F32 if nw == 2 else BF16),
        scratch_shapes=[pltpu.VMEM((tm, D), BF16)] * nw,
        compiler_params=_cparams("parallel", "arbitrary"),
        name="glu_up",
    )(x, g.reshape(1, D), *wg, *wu, *extra_in)


def _mm_res_kernel(*refs, nk, nw):
    x_ref = refs[0]
    w_refs = refs[1:1 + nw]
    r_ref, o_ref, acc_ref = refs[1 + nw:]
    k = pl.program_id(2)

    @pl.when(k == 0)
    def _():
        acc_ref[...] = jnp.zeros_like(acc_ref)

    acc_ref[...] += _mm(_parts(x_ref[...], nw == 2), _read(w_refs))

    @pl.when(k == nk - 1)
    def _():
        o_ref[...] = r_ref[...] + acc_ref[...]


def mm_res(x, w, res, *, tn, tk):
    M, K = x.shape
    wp = _wparts(w)
    N = wp[0].shape[1]
    tm = _row_tile(M, len(wp))
    nk = K // tk
    return pl.pallas_call(
        functools.partial(_mm_res_kernel, nk=nk, nw=len(wp)),
        grid=(M // tm, N // tn, nk),
        in_specs=[pl.BlockSpec((tm, tk), lambda i, j, k: (i, k))]
                 + [pl.BlockSpec((tk, tn), lambda i, j, k: (k, j))] * len(wp)
                 + [pl.BlockSpec((tm, tn), lambda i, j, k: (i, j))],
        out_specs=pl.BlockSpec((tm, tn), lambda i, j, k: (i, j)),
        out_shape=jax.ShapeDtypeStruct((M, N), F32),
        scratch_shapes=[pltpu.VMEM((tm, tn), F32)],
        compiler_params=_cparams("parallel", "parallel", "arbitrary"),
        name="mm_res",
    )(x, *wp, res)


def _glu_res_kernel(*refs, nw):
    y_ref = refs[0]
    wa_refs = refs[1:1 + nw]
    wb_refs = refs[1 + nw:1 + 2 * nw]
    r_ref, o_ref = refs[1 + 2 * nw:]
    y = _parts(y_ref[...], nw == 2)
    o_ref[...] = r_ref[...] + _mm(y, _read(wa_refs)) * _sigmoid(_mm(y, _read(wb_refs)))


def glu_res(y, w, res, *, tn, time_major=None):
    M, K = y.shape
    wp = _wparts(w)
    N = wp[0].shape[1] // 2
    nj = N // tn
    tm = _row_tile(M, len(wp))
    y_spec = pl.BlockSpec((tm, K), lambda i, j: (i, 0))
    if time_major is not None:
        n_seq, seq_len = time_major
        tps = seq_len // tm
        y = y.reshape(seq_len, n_seq * K)
        y_spec = pl.BlockSpec((tm, K), lambda i, j: (i % tps, i // tps))
    return pl.pallas_call(
        functools.partial(_glu_res_kernel, nw=len(wp)),
        grid=(M // tm, nj),
        in_specs=[y_spec]
                 + [pl.BlockSpec((K, tn), lambda i, j: (0, j))] * len(wp)
                 + [pl.BlockSpec((K, tn), lambda i, j: (0, j + nj))] * len(wp)
                 + [pl.BlockSpec((tm, tn), lambda i, j: (i, j))],
        out_specs=pl.BlockSpec((tm, tn), lambda i, j: (i, j)),
        out_shape=jax.ShapeDtypeStruct((M, N), F32),
        compiler_params=_cparams("parallel", "arbitrary"),
        name="glu_res",
    )(y, *wp, *wp, res)


def _ple_kernel(*refs, tn, nw):
    x_ref, g_ref = refs[:2]
    wg_refs = refs[2:2 + nw]
    p_ref = refs[2 + nw]
    wu_refs = refs[3 + nw:3 + 2 * nw]
    o_ref = refs[3 + 2 * nw]
    xn_refs = refs[4 + 2 * nw:]
    j = pl.program_id(1)

    @pl.when(j == 0)
    def _():
        _store_norm(x_ref, g_ref, xn_refs)

    gate = _sigmoid(_mm(_read(xn_refs), _read(wg_refs)))
    up = _mm(_parts(p_ref[...], nw == 2), _read(wu_refs))
    x_tile = x_ref[:, pl.ds(pl.multiple_of(j * tn, tn), tn)]
    o_ref[...] = x_tile + up * gate


def ple(x, g, w_gate, p, w_up, *, tn):
    M, D = x.shape
    P = p.shape[1]
    wg, wu = _wparts(w_gate), _wparts(w_up)
    nw = len(wg)
    tm = _row_tile(M, nw)
    return pl.pallas_call(
        functools.partial(_ple_kernel, tn=tn, nw=nw),
        grid=(M // tm, D // tn),
        in_specs=[pl.BlockSpec((tm, D), lambda i, j: (i, 0)),
                  pl.BlockSpec((1, D), lambda i, j: (0, 0))]
                 + [pl.BlockSpec((D, tn), lambda i, j: (0, j))] * nw
                 + [pl.BlockSpec((tm, P), lambda i, j: (i, 0))]
                 + [pl.BlockSpec((P, tn), lambda i, j: (0, j))] * nw,
        out_specs=pl.BlockSpec((tm, tn), lambda i, j: (i, j)),
        out_shape=jax.ShapeDtypeStruct((M, D), F32),
        scratch_shapes=[pltpu.VMEM((tm, D), BF16)] * nw,
        compiler_params=_cparams("parallel", "arbitrary"),
        name="ple",
    )(x, g.reshape(1, D), *wg, p, *wu)


def _rmsnorm_kernel(x_ref, g_ref, o_ref):
    o_ref[...] = _rms_scale(x_ref[...], g_ref[...]).astype(o_ref.dtype)


def _time_major_tiles(m, n_seq, seq_len):
    tm = _row_tile(m)
    return (tm, seq_len // tm) if seq_len % tm == 0 else None


def rmsnorm(x, g, *, time_major=None):
    M, D = x.shape
    tm = _row_tile(M)
    out_spec = pl.BlockSpec((tm, D), lambda i: (i, 0))
    out_shape = (M, D)
    if time_major is not None:
        n_seq, seq_len = time_major
        tm, tps = _time_major_tiles(M, n_seq, seq_len)
        out_spec = pl.BlockSpec((tm, D), lambda i: (i % tps, i // tps))
        out_shape = (seq_len, n_seq * D)
    out = pl.pallas_call(
        _rmsnorm_kernel,
        grid=(M // tm,),
        in_specs=[pl.BlockSpec((tm, D), lambda i: (i, 0)),
                  pl.BlockSpec((1, D), lambda i: (0, 0))],
        out_specs=out_spec,
        out_shape=jax.ShapeDtypeStruct(out_shape, F32),
        compiler_params=_cparams("parallel"),
        name="rmsnorm",
    )(x, g.reshape(1, D))
    return out.reshape(M, D)


def _small_proj_kernel(x_ref, g_ref, whi_ref, wlo_ref, aux_ref, o_ref, *, mode, nh):
    xn = _rms_scale(x_ref[...], g_ref[...])
    acc = _mm(_parts(xn, True), (whi_ref[...], wlo_ref[...]))
    lane = lax.broadcasted_iota(jnp.int32, acc.shape, 1)
    if mode == "logf":
        o_ref[...] = -_softplus(-(acc + aux_ref[0:1, :]))
    elif mode == "gdn":
        beta = _sigmoid(acc)
        gdec = -jnp.exp(aux_ref[1:2, :]) * _softplus(acc + aux_ref[0:1, :])
        o_ref[...] = jnp.where(lane < nh, beta, gdec)
    else:
        logits = jnp.where(lane < N_EXPERTS, acc, NEG_INF)
        v1 = jnp.max(logits, axis=1, keepdims=True)
        i1 = jnp.min(jnp.where(logits == v1, lane, LANES), axis=1, keepdims=True)
        rest = jnp.where(lane == i1, NEG_INF, logits)
        v2 = jnp.max(rest, axis=1, keepdims=True)
        i2 = jnp.min(jnp.where(rest == v2, lane, LANES), axis=1, keepdims=True)
        e2 = jnp.exp(v2 - v1)
        g1 = 1.0 / (1.0 + e2)
        o_ref[...] = jnp.where(lane == i1, g1, 0.0) + jnp.where(lane == i2, e2 * g1, 0.0)


def small_proj(x, g, w_parts, aux, mode, *, nh=0):
    M, D = x.shape
    tm = _row_tile(M)
    return pl.pallas_call(
        functools.partial(_small_proj_kernel, mode=mode, nh=nh),
        grid=(M // tm,),
        in_specs=[pl.BlockSpec((tm, D), lambda i: (i, 0)),
                  pl.BlockSpec((1, D), lambda i: (0, 0)),
                  pl.BlockSpec((D, LANES), lambda i: (0, 0)),
                  pl.BlockSpec((D, LANES), lambda i: (0, 0)),
                  pl.BlockSpec((SUBLANES, LANES), lambda i: (0, 0))],
        out_specs=pl.BlockSpec((tm, LANES), lambda i: (i, 0)),
        out_shape=jax.ShapeDtypeStruct((M, LANES), F32),
        compiler_params=_cparams("parallel"),
        name="small_proj_" + mode,
    )(x, g.reshape(1, D), *w_parts, aux)


def _pad_cols(w):
    return _split_weight(jnp.pad(w.astype(F32), ((0, 0), (0, LANES - w.shape[1]))))


def _aux_rows(*rows):
    out = [jnp.pad(vals.astype(F32), (start, LANES - start - vals.shape[0])) for start, vals in rows]
    out += [jnp.zeros((LANES,), F32)] * (SUBLANES - len(out))
    return jnp.stack(out)


def _lane_cumsum_kernel(x_ref, o_ref):
    x = x_ref[...]
    n = x.shape[1]
    lane = lax.broadcasted_iota(jnp.int32, x.shape, 1)
    shift = 1
    while shift < n:
        x = x + jnp.where(lane >= shift, pltpu.roll(x, shift, 1), 0.0)
        shift *= 2
    o_ref[...] = x


def lane_cumsum(x):
    R, n = x.shape
    rb = SUBLANES if R % SUBLANES == 0 else R
    return pl.pallas_call(
        _lane_cumsum_kernel,
        grid=(R // rb,),
        in_specs=[pl.BlockSpec((rb, n), lambda i: (i, 0))],
        out_specs=pl.BlockSpec((rb, n), lambda i: (i, 0)),
        out_shape=jax.ShapeDtypeStruct((R, n), F32),
        compiler_params=_cparams("parallel"),
        name="lane_cumsum",
    )(x)


def _fox_prompt_kernel(q_ref, k_ref, v_ref, fk_ref, o_ref, *, tq, nhb, hd, scale):
    qi = pl.program_id(2)

    def step(ki, carry, diagonal):
        rows = pl.ds(pl.multiple_of(ki * tq, tq), tq)
        out = []
        for h in range(nhb):
            m, l, acc = carry[h]
            cols = slice(h * hd, (h + 1) * hd)
            s = _dot_nt(q_ref[:, cols], k_ref[rows, cols]) * scale - fk_ref[h, pl.ds(ki, 1), :]
            if diagonal:
                r = lax.broadcasted_iota(jnp.int32, s.shape, 0)
                c = lax.broadcasted_iota(jnp.int32, s.shape, 1)
                s = jnp.where(r >= c, s, NEG_INF)
            m_new = jnp.maximum(m, jnp.max(s, axis=1, keepdims=True))
            p = jnp.exp(s - m_new)
            alpha = jnp.exp(m - m_new)
            l = alpha * l + jnp.sum(p, axis=1, keepdims=True)
            acc = alpha * acc + _dot(p.astype(BF16), v_ref[rows, cols])
            out.append((m_new, l, acc))
        return tuple(out)

    init = tuple((jnp.full((tq, 1), NEG_INF, F32), jnp.zeros((tq, 1), F32), jnp.zeros((tq, hd), F32))
                 for _ in range(nhb))
    carry = lax.fori_loop(0, qi, lambda ki, c: step(ki, c, False), init)
    carry = step(qi, carry, True)
    o_ref[...] = jnp.concatenate([acc / l for _, l, acc in carry], axis=1).astype(o_ref.dtype)


def fox_attention_prompt(qkv, fk, *, B, L, H, hd, tq, nhb=2):
    nq = L // tq
    nhg = H // nhb
    return pl.pallas_call(
        functools.partial(_fox_prompt_kernel, tq=tq, nhb=nhb, hd=hd, scale=hd ** -0.5),
        grid=(B, nhg, nq),
        in_specs=[pl.BlockSpec((tq, nhb * hd), lambda b, g, i: (b * nq + i, g)),
                  pl.BlockSpec((L, nhb * hd), lambda b, g, i: (b, nhg + g)),
                  pl.BlockSpec((L, nhb * hd), lambda b, g, i: (b, 2 * nhg + g)),
                  pl.BlockSpec((nhb, nq, tq), lambda b, g, i: (b * nhg + g, 0, 0))],
        out_specs=pl.BlockSpec((tq, nhb * hd), lambda b, g, i: (b * nq + i, g)),
        out_shape=jax.ShapeDtypeStruct((B * L, H * hd), BF16),
        compiler_params=_cparams("parallel", "parallel", "arbitrary"),
        name="fox_attention_prompt",
    )(qkv, qkv, qkv, fk)


def _fox_cached_kernel(q_ref, kn_ref, vn_ref, kp_ref, vp_ref, f_ref, o_ref, *, P, nhb, hd, scale, precise):
    n = q_ref.shape[0]
    r = lax.broadcasted_iota(jnp.int32, (n, n), 0)
    c = lax.broadcasted_iota(jnp.int32, (n, n), 1)
    outs = []
    for h in range(nhb):
        cols = slice(h * hd, (h + 1) * hd)
        q = _parts(q_ref[:, cols], precise)
        f = f_ref[h]
        s_past = _mm_nt(q, _parts(kp_ref[0, :, cols], precise)) * scale - f[:, :P]
        s_new = _mm_nt(q, _parts(kn_ref[:, cols], precise)) * scale - f[:, P:P + n]
        s_new = jnp.where(r >= c, s_new, NEG_INF)
        m = jnp.maximum(jnp.max(s_past, axis=1, keepdims=True), jnp.max(s_new, axis=1, keepdims=True))
        p_past = jnp.exp(s_past - m)
        p_new = jnp.exp(s_new - m)
        l = jnp.sum(p_past, axis=1, keepdims=True) + jnp.sum(p_new, axis=1, keepdims=True)
        acc = (_mm(_parts(p_past, precise), _parts(vp_ref[0, :, cols], precise))
               + _mm(_parts(p_new, precise), _parts(vn_ref[:, cols], precise)))
        outs.append(acc / l)
    o_ref[...] = jnp.concatenate(outs, axis=1).astype(o_ref.dtype)


def fox_attention_cached(qkv, k_past, v_past, f, *, seq0, k_lane0, v_lane0, B, n, P, H, hd):
    precise = qkv.dtype == F32
    nhb = 4 if H % 4 == 0 else 1
    nhg = H // nhb
    w = nhb * hd
    return pl.pallas_call(
        functools.partial(_fox_cached_kernel, P=P, nhb=nhb, hd=hd, scale=hd ** -0.5, precise=precise),
        grid=(B, nhg),
        in_specs=[pl.BlockSpec((n, w), lambda b, g: (b, g)),
                  pl.BlockSpec((n, w), lambda b, g: (b, nhg + g)),
                  pl.BlockSpec((n, w), lambda b, g: (b, 2 * nhg + g)),
                  pl.BlockSpec((1, P, w), lambda b, g: (seq0 + b, 0, k_lane0 // nhb + g)),
                  pl.BlockSpec((1, P, w), lambda b, g: (seq0 + b, 0, v_lane0 // nhb + g)),
                  pl.BlockSpec((nhb, 1, f.shape[2]), lambda b, g: (b * nhg + g, 0, 0))],
        out_specs=pl.BlockSpec((n, w), lambda b, g: (b, g)),
        out_shape=jax.ShapeDtypeStruct((B * n, H * hd), qkv.dtype),
        compiler_params=_cparams("parallel", "arbitrary"),
        name="fox_attention_cached",
    )(qkv, qkv, qkv, k_past, v_past, f)


def _gelu_tanh(x):
    return 0.5 * x * (1.0 + jnp.tanh(math.sqrt(2.0 / math.pi) * (x + 0.044715 * (x * x * x))))


def _s5_kernel(*refs, nb, c, ns, ncm, snap_chunk):
    u_ref, bhi_ref, blo_ref = refs[:3]
    cm_refs = refs[3:3 + ncm]
    (ar_ref, ai_ref, d_ref, h0r_ref, h0i_ref, y_ref, hr_out, hi_out, hr_snap, hi_snap,
     bu_sc, st_sc, hr_sc, hi_sc) = refs[3 + ncm:]
    tc = pl.program_id(1)

    @pl.when(tc == 0)
    def _():
        hr_sc[...] = h0r_ref[...]
        hi_sc[...] = h0i_ref[...]

    u = u_ref[...]
    bu_sc[...] = _mm(_parts(u, True), (bhi_ref[0], blo_ref[0]))
    ar = jnp.broadcast_to(ar_ref[0], (nb, ns))
    ai = jnp.broadcast_to(ai_ref[0], (nb, ns))

    def step(t, carry):
        hr, hi = carry
        rows = pl.ds(pl.multiple_of(t * nb, nb), nb)
        nhr = ar * hr - ai * hi + bu_sc[rows, 0:ns]
        nhi = ar * hi + ai * hr + bu_sc[rows, ns:2 * ns]
        st_sc[rows, 0:ns] = nhr
        st_sc[rows, ns:2 * ns] = nhi
        return nhr, nhi

    hr, hi = lax.fori_loop(0, c, step, (hr_sc[...], hi_sc[...]))
    hr_sc[...] = hr
    hi_sc[...] = hi
    y = _mm(_parts(st_sc[...], ncm == 2), tuple(r[0] for r in cm_refs)) + d_ref[...] * u
    y_ref[...] = _gelu_tanh(y).astype(y_ref.dtype)

    @pl.when(tc == snap_chunk)
    def _():
        hr_snap[...] = hr
        hi_snap[...] = hi

    @pl.when(tc == pl.num_programs(1) - 1)
    def _():
        hr_out[...] = hr
        hi_out[...] = hi


def s5_scan(u_tm, h0r, h0i, consts, *, nb, c, precise, snap_steps=None):
    bhi, blo, cm_hi, cm_lo, ar, ai, d = consts
    cms = (cm_hi, cm_lo) if precise else (cm_hi,)
    rows, D = u_tm.shape
    nblk = D // LANES
    ns = ar.shape[2]
    ntc = rows // (c * nb)
    snap_chunk = ntc - 1 if snap_steps is None else snap_steps // c - 1
    blk = lambda g, t: (g, 0, 0)
    state_spec = pl.BlockSpec((nb, ns), lambda g, t: (0, g))
    state_shape = jax.ShapeDtypeStruct(h0r.shape, F32)
    return pl.pallas_call(
        functools.partial(_s5_kernel, nb=nb, c=c, ns=ns, ncm=len(cms), snap_chunk=snap_chunk),
        grid=(nblk, ntc),
        in_specs=[pl.BlockSpec((c * nb, LANES), lambda g, t: (t, g)),
                  pl.BlockSpec((1, LANES, 2 * ns), blk),
                  pl.BlockSpec((1, LANES, 2 * ns), blk)]
                 + [pl.BlockSpec((1, 2 * ns, LANES), blk)] * len(cms)
                 + [pl.BlockSpec((1, 1, ns), blk),
                    pl.BlockSpec((1, 1, ns), blk),
                    pl.BlockSpec((1, LANES), lambda g, t: (0, g)),
                    state_spec, state_spec],
        out_specs=[pl.BlockSpec((c * nb, LANES), lambda g, t: (t, g))] + [state_spec] * 4,
        out_shape=[jax.ShapeDtypeStruct((rows, D), F32 if precise else BF16)] + [state_shape] * 4,
        scratch_shapes=[pltpu.VMEM((c * nb, 2 * ns), F32),
                        pltpu.VMEM((c * nb, 2 * ns), F32),
                        pltpu.VMEM((nb, ns), F32),
                        pltpu.VMEM((nb, ns), F32)],
        compiler_params=_cparams("parallel", "arbitrary"),
        name="s5_scan",
    )(u_tm, bhi, blo, *cms, ar, ai, d, h0r, h0i)


def s5_constants(a_re, a_im, b_re, b_im, c_re, c_im, d_skip, log_dt):
    G, N = a_re.shape
    gc = b_re.shape[2]
    gpb = LANES // gc
    nblk = G // gpb
    a = lax.complex(a_re.astype(F32), a_im.astype(F32))
    dt = jnp.exp(log_dt.astype(F32))[:, None]
    a_bar = jnp.exp(a * dt)
    b_bar = ((a_bar - 1.0) / a)[..., None] * lax.complex(b_re.astype(F32), b_im.astype(F32))
    eye = jnp.eye(gpb, dtype=F32)

    def in_mat(b):
        b = b.reshape(nblk, gpb, N, gc)
        return jnp.einsum("kgnc,gh->kgchn", b, eye).reshape(nblk, gpb * gc, gpb * N)

    def out_mat(cc):
        cc = cc.reshape(nblk, gpb, gc, N)
        return jnp.einsum("kgcn,gh->kgnhc", cc, eye).reshape(nblk, gpb * N, gpb * gc)

    bhi, blo = _split_weight(jnp.concatenate([in_mat(b_bar.real), in_mat(b_bar.imag)], axis=2))
    cm_hi, cm_lo = _split_weight(jnp.concatenate([out_mat(c_re.astype(F32)), -out_mat(c_im.astype(F32))], axis=1))
    ar = a_bar.real.reshape(nblk, 1, gpb * N)
    ai = a_bar.imag.reshape(nblk, 1, gpb * N)
    return bhi, blo, cm_hi, cm_lo, ar, ai, d_skip.astype(F32).reshape(1, G * gc)


def _gdn_conv_kernel(x_ref, halo_ref, buf_ref, w_ref, o_ref, ext_sc, *, tl, n_valid, tiles_per_seq, n_norm_tiles):
    r = pl.program_id(0)
    j = pl.program_id(1)
    first = (r % tiles_per_seq) == 0
    ext_sc[0:SUBLANES, :] = jnp.where(first, buf_ref[0], halo_ref[...])
    ext_sc[SUBLANES:SUBLANES + n_valid, :] = x_ref[...]
    w = w_ref[...]
    y = x_ref[...] * w[3:4, :]
    for s in (1, 2, 3):
        y = y + ext_sc[SUBLANES - s:SUBLANES - s + n_valid, :] * w[3 - s:4 - s, :]
    y = _silu(y)

    def store(val):
        if n_valid < tl:
            val = jnp.concatenate([val, jnp.zeros((tl - n_valid, val.shape[1]), F32)], axis=0)
        o_ref[...] = val.astype(o_ref.dtype)

    @pl.when(j < n_norm_tiles)
    def _():
        segs = []
        for a in range(0, y.shape[1], LANES):
            seg = y[:, a:a + LANES]
            segs.append(seg * lax.rsqrt(jnp.sum(seg * seg, axis=1, keepdims=True) + EPS))
        store(jnp.concatenate(segs, axis=1))

    @pl.when(j >= n_norm_tiles)
    def _():
        store(y)


def gdn_conv(x, buf8, w8, *, n_seq, seq_len, tl_in, tl_out, n_norm_cols, tc):
    C = x.shape[1]
    tiles_per_seq = seq_len // tl_in
    n_tiles = n_seq * tiles_per_seq
    hpt = tl_in // SUBLANES
    return pl.pallas_call(
        functools.partial(_gdn_conv_kernel, tl=tl_out, n_valid=tl_in, tiles_per_seq=tiles_per_seq,
                          n_norm_tiles=n_norm_cols // tc),
        grid=(n_tiles, C // tc),
        in_specs=[pl.BlockSpec((tl_in, tc), lambda r, j: (r, j)),
                  pl.BlockSpec((SUBLANES, tc), lambda r, j: (jnp.maximum(r * hpt - 1, 0), j)),
                  pl.BlockSpec((1, SUBLANES, tc), lambda r, j: (r // tiles_per_seq, 0, j)),
                  pl.BlockSpec((SUBLANES, tc), lambda r, j: (0, j))],
        out_specs=pl.BlockSpec((tl_out, tc), lambda r, j: (r, j)),
        out_shape=jax.ShapeDtypeStruct((n_tiles * tl_out, C), BF16),
        scratch_shapes=[pltpu.VMEM((SUBLANES + tl_in, tc), F32)],
        compiler_params=_cparams("parallel", "arbitrary"),
        name="gdn_conv",
    )(x, x, buf8, w8)


def _gdn_prep_kernel(qk_ref, gt_ref, ti_ref, a_ref, gc_ref, n_sc, nt_sc, x_sc, *, C, cps, n_qk, rep, hd, scale):
    nh = n_qk * rep
    n_sys = cps * nh
    r = lax.broadcasted_iota(jnp.int32, (C, C), 0)
    c = lax.broadcasted_iota(jnp.int32, (C, C), 1)
    tril = (r >= c).astype(BF16)
    lane_pad = jnp.zeros((C, LANES - C), F32)
    for cc in range(cps):
        rows = slice(cc * C, (cc + 1) * C)
        gates = gt_ref[rows, :]
        lane = lax.broadcasted_iota(jnp.int32, gates.shape, 1)
        g = jnp.where((lane >= nh) & (lane < 2 * nh), gates, 0.0)
        g_hi = g.astype(BF16)
        g_r1 = g - g_hi.astype(F32)
        g_mid = g_r1.astype(BF16)
        g_lo = (g_r1 - g_mid.astype(F32)).astype(BF16)
        gcum = _dot(tril, g_hi) + _dot(tril, g_mid) + _dot(tril, g_lo)
        gc_ref[rows, :] = gcum
        gcum_t = gcum.T
        for hq in range(n_qk):
            q = qk_ref[rows, hq * hd:(hq + 1) * hd]
            k = qk_ref[rows, (n_qk + hq) * hd:(n_qk + hq + 1) * hd]
            kk = _dot_nt(k, k)
            qk = _dot_nt(q, k) * scale
            for rr in range(rep):
                hv = hq * rep + rr
                sys = cc * nh + hv
                diff = gcum[:, nh + hv:nh + hv + 1] - gcum_t[nh + hv:nh + hv + 1, :]
                decay = jnp.exp(jnp.where(r >= c, diff, NEG_INF))
                n_val = jnp.where(r > c, gates[:, hv:hv + 1] * kk * decay, 0.0)
                n_sc[sys * C:(sys + 1) * C, :] = jnp.concatenate([n_val, lane_pad], axis=1)
                a_ref[cc, hv] = (qk * decay).astype(a_ref.dtype)
    for i in range(C):
        slab = n_sc[pl.ds(i, n_sys, stride=C), :]
        nt_sc[i * C:(i + 1) * C, :] = slab.T[:C, :]
    _tri_solve(nt_sc, x_sc, C)
    row_pad = jnp.zeros((LANES - C, n_sys), F32)
    for i in range(C):
        cols = jnp.concatenate([x_sc[i * C:(i + 1) * C, :], row_pad], axis=0)
        n_sc[pl.ds(i, n_sys, stride=C), :] = cols.T
    for cc in range(cps):
        for hv in range(nh):
            sys = cc * nh + hv
            ti_ref[cc, hv] = n_sc[sys * C:(sys + 1) * C, :C].astype(ti_ref.dtype)


def gdn_prep(qkv_c, gates, *, C, n_qk, rep, hd):
    rows = qkv_c.shape[0]
    nch = rows // C
    nh = n_qk * rep
    cps = LANES // nh
    assert nch % cps == 0 and cps * nh == LANES
    return pl.pallas_call(
        functools.partial(_gdn_prep_kernel, C=C, cps=cps, n_qk=n_qk, rep=rep, hd=hd, scale=hd ** -0.5),
        grid=(nch // cps,),
        in_specs=[pl.BlockSpec((cps * C, 2 * n_qk * hd), lambda n: (n, 0)),
                  pl.BlockSpec((cps * C, LANES), lambda n: (n, 0))],
        out_specs=[pl.BlockSpec((cps, nh, C, C), lambda n: (n, 0, 0, 0)),
                   pl.BlockSpec((cps, nh, C, C), lambda n: (n, 0, 0, 0)),
                   pl.BlockSpec((cps * C, LANES), lambda n: (n, 0))],
        out_shape=[jax.ShapeDtypeStruct((nch, nh, C, C), BF16),
                   jax.ShapeDtypeStruct((nch, nh, C, C), BF16),
                   jax.ShapeDtypeStruct((rows, LANES), F32)],
        scratch_shapes=[pltpu.VMEM((LANES * C, LANES), F32),
                        pltpu.VMEM((C * C, LANES), F32),
                        pltpu.VMEM((C * C, LANES), F32)],
        compiler_params=_cparams("parallel"),
        name="gdn_prep",
    )(qkv_c, gates)


def _tri_solve(n_ref, x_ref, C):
    nblk = C // SUBLANES
    lanes = n_ref.shape[1]
    sub = lax.broadcasted_iota(jnp.int32, (SUBLANES, lanes), 0)
    x_ref[...] = jnp.zeros(x_ref.shape, F32)
    for ib in range(nblk):
        def row(r, _, ib=ib):
            base = (ib * SUBLANES + r) * C
            acc = [jnp.zeros((SUBLANES, lanes), F32) for _ in range(ib)]
            acc.append(jnp.where(sub == r, 1.0, 0.0).astype(F32))
            for jb in range(ib + 1):
                for j in range(jb * SUBLANES, (jb + 1) * SUBLANES):
                    nb = jnp.broadcast_to(n_ref[pl.ds(base + j, 1), :], (SUBLANES, lanes))
                    for cb in range(jb + 1):
                        acc[cb] = acc[cb] - nb * x_ref[j * C + cb * SUBLANES:j * C + (cb + 1) * SUBLANES, :]
            for cb in range(ib + 1):
                x_ref[pl.ds(pl.multiple_of(base + cb * SUBLANES, SUBLANES), SUBLANES), :] = acc[cb]
            return 0

        lax.fori_loop(0, SUBLANES, row, 0)


def _gdn_scan_kernel(q_ref, k_ref, v_ref, z_ref, gt_ref, gc_ref, ti_ref, a_ref, s0_ref, ng_ref,
                     o_ref, s_out, s_sc, *, C, hg, rep, hd, nh, n_valid, scale):
    n = pl.program_id(2)
    hgi = pl.program_id(1)

    @pl.when(n == 0)
    def _():
        s_sc[...] = s0_ref[0]

    shift = (LANES - hgi * hg) % LANES
    gates = pltpu.roll(gt_ref[...], shift, 1)
    gcum = pltpu.roll(gc_ref[...], shift, 1)
    heads = range(hg)
    s_old = [s_sc[hh] for hh in heads]
    s_bf = [s.astype(BF16) for s in s_old]
    beta = [gates[:, hh:hh + 1] for hh in heads]
    gc = [gcum[:, nh + hh:nh + hh + 1] for hh in heads]
    eg = [jnp.exp(g) for g in gc]
    g_tot = [g[C - 1:C, :] for g in gc]
    k = [k_ref[:, (hh // rep) * hd:(hh // rep + 1) * hd].astype(F32) for hh in heads]
    q = [q_ref[:, (hh // rep) * hd:(hh // rep + 1) * hd].astype(F32) for hh in heads]
    v = [v_ref[:, hh * hd:(hh + 1) * hd].astype(F32) for hh in heads]
    uw = [_dot(ti_ref[0, hh].astype(BF16),
               jnp.concatenate([v[hh] * beta[hh], k[hh] * (beta[hh] * eg[hh])], axis=1).astype(BF16))
          for hh in heads]
    q_dec = [(q[hh] * (scale * eg[hh])).astype(BF16) for hh in heads]
    k_dec = [(k[hh] * jnp.exp(g_tot[hh] - gc[hh])).astype(BF16) for hh in heads]
    v_new = [(uw[hh][:, :hd] - _dot(uw[hh][:, hd:].astype(BF16), s_bf[hh])).astype(BF16) for hh in heads]
    o = [_dot(q_dec[hh], s_bf[hh]) + _dot(a_ref[0, hh], v_new[hh]) for hh in heads]
    s_new = [s_old[hh] * jnp.exp(g_tot[hh]) + _dot_tn(k_dec[hh], v_new[hh]) for hh in heads]
    for hh in heads:
        s_sc[hh] = s_new[hh]
    ng = ng_ref[...]
    outs = [_rms_scale(o[hh][:n_valid], ng) * _silu(z_ref[:, hh * hd:(hh + 1) * hd].astype(F32)) for hh in heads]
    o_ref[...] = jnp.concatenate(outs, axis=1).astype(o_ref.dtype)

    @pl.when(n == pl.num_programs(2) - 1)
    def _():
        s_out[0] = s_sc[...]


def gdn_scan(qkv_c, z, gates, gcum, tinv, amat, s0, norm_g, *, n_seq, nc, C, n_valid, n_qk, rep, hd, hg):
    nh = n_qk * rep
    nhg = nh // hg
    qw = (hg // rep) * hd
    k_off = n_qk * hd // qw
    v_off = 2 * n_qk * hd // (hg * hd)
    chunk = lambda s, g, n: s * nc + n
    return pl.pallas_call(
        functools.partial(_gdn_scan_kernel, C=C, hg=hg, rep=rep, hd=hd, nh=nh, n_valid=n_valid,
                          scale=hd ** -0.5),
        grid=(n_seq, nhg, nc),
        in_specs=[pl.BlockSpec((C, qw), lambda s, g, n: (chunk(s, g, n), g)),
                  pl.BlockSpec((C, qw), lambda s, g, n: (chunk(s, g, n), k_off + g)),
                  pl.BlockSpec((C, hg * hd), lambda s, g, n: (chunk(s, g, n), v_off + g)),
                  pl.BlockSpec((n_valid, hg * hd), lambda s, g, n: (chunk(s, g, n), g)),
                  pl.BlockSpec((C, LANES), lambda s, g, n: (chunk(s, g, n), 0)),
                  pl.BlockSpec((C, LANES), lambda s, g, n: (chunk(s, g, n), 0)),
                  pl.BlockSpec((1, hg, C, C), lambda s, g, n: (chunk(s, g, n), g, 0, 0)),
                  pl.BlockSpec((1, hg, C, C), lambda s, g, n: (chunk(s, g, n), g, 0, 0)),
                  pl.BlockSpec((1, hg, hd, hd), lambda s, g, n: (s, g, 0, 0)),
                  pl.BlockSpec((1, hd), lambda s, g, n: (0, 0))],
        out_specs=[pl.BlockSpec((n_valid, hg * hd), lambda s, g, n: (chunk(s, g, n), g)),
                   pl.BlockSpec((1, hg, hd, hd), lambda s, g, n: (s, g, 0, 0))],
        out_shape=[jax.ShapeDtypeStruct((n_seq * nc * n_valid, nh * hd), BF16),
                   jax.ShapeDtypeStruct((n_seq, nh, hd, hd), F32)],
        scratch_shapes=[pltpu.VMEM((hg, hd, hd), F32)],
        compiler_params=_cparams("parallel", "parallel", "arbitrary"),
        name="gdn_scan",
    )(qkv_c, qkv_c, qkv_c, z, gates, gcum, tinv, amat, s0, norm_g.reshape(1, hd))


def _trunk(x, p, fox_hist, s5_h0, gdn_state, W, *, n_seq, seq_len, steps, precise, snap_steps=None):
    M, D = x.shape
    H, hd = W["fox_heads"]
    n_qk, rep, gd = W["gdn_heads"]
    n_vh = n_qk * rep
    key_dim, val_dim = n_qk * gd, n_vh * gd
    qkv_dim = 2 * key_dim + val_dim
    cw = W["gdn_conv_w"].shape[1]
    tn = min(512, D)
    outs = {}

    def wsel(name, idx, pr):
        hi = W[name][idx]
        return (hi, W[name + "_lo"][idx]) if pr else hi

    def mixer(i, x):
        j = i // 3
        pm = precise
        g_mix = W["norm_mix"][i]
        if i % 3 == 0:
            nt = D // tn
            qkv_dt = F32 if pm else BF16
            qkv, kv = norm_mm(x, g_mix, wsel("fox_w_qkv", j, pm), [(qkv_dt, 0, 3 * nt), (F32, nt, 3 * nt)], tn=tn)
            logf = small_proj(x, g_mix, W["fox_w_f"][j], W["fox_aux"][j], "logf")[:, :H]
            lf = jnp.swapaxes(logf.reshape(n_seq, seq_len, H), 1, 2)
            if j not in fox_hist:
                tq = next(t for t in (512, 384, 256, 128, seq_len) if seq_len % t == 0)
                fk = lane_cumsum(lf.reshape(n_seq * H, seq_len)).reshape(n_seq * H, seq_len // tq, tq)
                o = fox_attention_prompt(qkv, fk, B=n_seq, L=seq_len, H=H, hd=hd, tq=tq)
            else:
                k_past, v_past, k_lane0, v_lane0, seq0, lf_past, P = fox_hist[j]
                fpad = -(-(P + seq_len) // LANES) * LANES
                lf_all = jnp.concatenate([jnp.swapaxes(lf_past.astype(F32), 1, 2), lf], axis=2)
                lf_all = jnp.pad(lf_all, ((0, 0), (0, 0), (0, fpad - (P + seq_len))))
                f = lane_cumsum(lf_all.reshape(n_seq * H, fpad)).reshape(n_seq * H, 1, fpad)
                o = fox_attention_cached(qkv, k_past, v_past, f, seq0=seq0, k_lane0=k_lane0, v_lane0=v_lane0,
                                         B=n_seq, n=seq_len, P=P, H=H, hd=hd)
            outs[("fox", j)] = (kv, logf)
            return mm_res(o, wsel("fox_w_out", j, pm), x, tn=tn, tk=D)
        if i % 3 == 1:
            tmaj = (n_seq, seq_len) if not pm and _time_major_tiles(M, n_seq, seq_len) else None
            if tmaj:
                u_tm = rmsnorm(x, g_mix, time_major=tmaj)
            else:
                u_tm = jnp.swapaxes(rmsnorm(x, g_mix).reshape(n_seq, seq_len, D), 0, 1).reshape(M, D)
            y_tm, hr, hi, hr_snap, hi_snap = s5_scan(u_tm, s5_h0[j][0], s5_h0[j][1], W["s5_consts"][j],
                                                     nb=n_seq, c=min(CHUNK, seq_len), precise=pm,
                                                     snap_steps=snap_steps)
            outs[("s5", j)] = (hr, hi)
            outs[("s5_snap", j)] = (hr_snap, hi_snap)
            if tmaj:
                return glu_res(y_tm, wsel("s5_w_glu", j, pm), x, tn=tn, time_major=tmaj)
            y = jnp.swapaxes(y_tm.reshape(seq_len, n_seq, D), 0, 1).reshape(M, D)
            return glu_res(y, wsel("s5_w_glu", j, pm), x, tn=tn)
        s0, conv_rows = gdn_state[j]
        nq = qkv_dim // tn
        nz = val_dim // tn
        qkv_raw, z = norm_mm(x, g_mix, W["gdn_w_qkvz"][j], [(F32, 0, nq), (BF16, nq, nq + nz)], tn=tn)
        gates = small_proj(x, g_mix, W["gdn_w_ba"][j], W["gdn_aux"][j], "gdn", nh=n_vh)
        C = CHUNK
        n_valid = min(C, seq_len)
        nc = seq_len // n_valid
        buf = jnp.pad(conv_rows.astype(F32), ((0, 0), (SUBLANES - (cw - 1), 0), (0, 0)))
        tl_in = min(512, seq_len)
        qkv_c = gdn_conv(qkv_raw, buf, W["gdn_conv_w8"][j], n_seq=n_seq, seq_len=seq_len, tl_in=tl_in,
                         tl_out=max(tl_in, C), n_norm_cols=2 * key_dim,
                         tc=key_dim if tl_in < C else min(512, key_dim))
        gates_c = jnp.pad(gates.reshape(n_seq * nc, n_valid, LANES),
                          ((0, 0), (0, C - n_valid), (0, 0))).reshape(n_seq * nc * C, LANES)
        tinv, amat, gcum = gdn_prep(qkv_c, gates_c, C=C, n_qk=n_qk, rep=rep, hd=gd)
        o, s_last = gdn_scan(qkv_c, z, gates_c, gcum, tinv, amat, s0.astype(F32), W["gdn_norm"][j],
                             n_seq=n_seq, nc=nc, C=C, n_valid=n_valid, n_qk=n_qk, rep=rep, hd=gd,
                             hg=min(8, n_vh))
        rows = jnp.concatenate([conv_rows.astype(F32), qkv_raw.reshape(n_seq, seq_len, qkv_dim)], axis=1)
        outs[("gdn", j)] = (s_last, rows[:, -(cw - 1):])
        return mm_res(o, W["gdn_w_out"][j], x, tn=tn, tk=D)

    def channel(i, x):
        m = i // 2
        g_ffn = W["norm_ffn"][i]
        if i % 2 == 0:
            pf = precise
            h = glu_up(x, g_ffn, wsel("ffn_w_gate", m, pf), wsel("ffn_w_up", m, pf), tn=tn)
            x = mm_res(h, wsel("ffn_w_down", m, pf), x, tn=tn, tk=D)
        else:
            pf = False
            comb = small_proj(x, g_ffn, W["moe_w_router"][m], W["zero_aux"], "router")
            h = glu_up(x, g_ffn, W["moe_w_gate"][m], W["moe_w_up"][m], comb, tn=tn)
            x = mm_res(h, W["moe_w_down"][m], x, tn=tn, tk=D)
        p_i = p[i] if pf else p[i].astype(BF16)
        return ple(x, W["norm_ple"][i], wsel("ple_w_gate", i, pf), p_i, wsel("ple_w_up", i, pf), tn=tn)

    for i, part in steps:
        x = mixer(i, x) if part == "mixer" else channel(i, x)
    return x, outs


def kernel(x_prompt, x_sample, cache_fox_k, cache_fox_v, cache_fox_logf, state_s5_re, state_s5_im, state_gdn, state_gdn_conv, p_prompt, p_sample, norm_mix, norm_ffn, norm_ple, norm_final, fox_w_in, fox_b_f, fox_w_out, s5_a_re, s5_a_im, s5_b_re, s5_b_im, s5_c_re, s5_c_im, s5_d, s5_log_dt, s5_w_glu, gdn_w_in, gdn_conv_w, gdn_a_log, gdn_dt_bias, gdn_norm, gdn_w_out, ffn_w_gate, ffn_w_up, ffn_w_down, moe_w_router, moe_w_gate, moe_w_up, moe_w_down, ple_w_up, ple_w_gate):
    B, L, D = x_prompt.shape
    Bs, Ls, _ = x_sample.shape
    depth = norm_mix.shape[0]
    P = cache_fox_k.shape[2]
    H, hd = cache_fox_k.shape[3], cache_fox_k.shape[4]
    n_vh, gd = state_gdn.shape[2], state_gdn.shape[3]
    qkv_dim = state_gdn_conv.shape[3]
    val_dim = n_vh * gd
    n_qk = (qkv_dim - val_dim) // (2 * gd)
    cw = gdn_conv_w.shape[1]
    G, NS = s5_a_re.shape[1], s5_a_re.shape[2]
    E, _, Fe = moe_w_gate.shape[1:]
    n_fox, n_s5, n_gdn = fox_w_in.shape[0], s5_a_re.shape[0], gdn_w_in.shape[0]

    first_router = 1
    n_lo = {"fox": 1, "ffn": 1, "ple": 1, "s5": 1}

    def hi_lo(w, n):
        return w.astype(BF16), _split_weight(w[:n])[1]

    W = {"norm_mix": norm_mix, "norm_ffn": norm_ffn, "norm_ple": norm_ple, "norm_final": norm_final,
         "fox_heads": (H, hd), "gdn_heads": (n_qk, n_vh // n_qk, gd), "gdn_conv_w": gdn_conv_w,
         "gdn_norm": gdn_norm, "zero_aux": jnp.zeros((SUBLANES, LANES), F32)}
    W["fox_w_qkv"], W["fox_w_qkv_lo"] = hi_lo(fox_w_in[:, :, :3 * D], n_lo["fox"])
    W["fox_w_out"], W["fox_w_out_lo"] = hi_lo(fox_w_out, n_lo["fox"])
    W["fox_w_f"] = [_pad_cols(fox_w_in[j, :, 3 * D:]) for j in range(n_fox)]
    W["fox_aux"] = [_aux_rows((0, fox_b_f[j])) for j in range(n_fox)]
    W["s5_w_glu"], W["s5_w_glu_lo"] = hi_lo(s5_w_glu, n_lo["s5"])
    W["s5_consts"] = [s5_constants(s5_a_re[j], s5_a_im[j], s5_b_re[j], s5_b_im[j], s5_c_re[j], s5_c_im[j],
                                   s5_d[j], s5_log_dt[j]) for j in range(n_s5)]
    W["gdn_w_qkvz"] = gdn_w_in[:, :, :qkv_dim + val_dim].astype(BF16)
    W["gdn_w_ba"] = [_pad_cols(gdn_w_in[j, :, qkv_dim + val_dim:]) for j in range(n_gdn)]
    W["gdn_aux"] = [_aux_rows((n_vh, gdn_dt_bias[j]), (n_vh, gdn_a_log[j])) for j in range(n_gdn)]
    W["gdn_conv_w8"] = jnp.pad(gdn_conv_w.astype(F32), ((0, 0), (0, SUBLANES - cw), (0, 0)))
    W["gdn_w_out"] = gdn_w_out.astype(BF16)
    W["ffn_w_gate"], W["ffn_w_gate_lo"] = hi_lo(ffn_w_gate, n_lo["ffn"])
    W["ffn_w_up"], W["ffn_w_up_lo"] = hi_lo(ffn_w_up, n_lo["ffn"])
    W["ffn_w_down"], W["ffn_w_down_lo"] = hi_lo(ffn_w_down, n_lo["ffn"])
    W["moe_w_router"] = [_pad_cols(moe_w_router[m]) for m in range(moe_w_router.shape[0])]
    W["moe_w_gate"] = moe_w_gate.astype(BF16)
    W["moe_w_up"] = moe_w_up.astype(BF16)
    W["moe_w_down"] = moe_w_down.astype(BF16).reshape(-1, E * Fe, D)
    W["ple_w_gate"], W["ple_w_gate_lo"] = hi_lo(ple_w_gate, n_lo["ple"])
    W["ple_w_up"], W["ple_w_up_lo"] = hi_lo(ple_w_up, n_lo["ple"])

    steps = [(i, part) for i in range(depth) for part in ("mixer", "channel")]
    steps_a, steps_b = steps[:2 * first_router + 1], steps[2 * first_router + 1:]
    pd = p_prompt.shape[-1]

    Lt = LANES
    Lh = L - Lt
    p_all = {i: p_prompt[i].reshape(B * L, pd) for i in range(depth)}
    p_tail = {i: p_prompt[i][:, Lh:].reshape(B * Lt, pd) for i in range(depth)}
    zeros_s5 = {j: (jnp.zeros((B, G * NS), F32),) * 2 for j in range(n_s5)}
    x_f, out_p = _trunk(x_prompt.reshape(B * L, D), p_all, {}, zeros_s5, {}, W,
                        n_seq=B, seq_len=L, steps=steps_a, precise=False, snap_steps=Lh)
    hist_t = {}
    for j in range(n_fox):
        if ("fox", j) in out_p:
            kv_f, lf_f = out_p[("fox", j)]
            kv_f = kv_f.reshape(B, L, 2 * D)
            hist_t[j] = (kv_f, kv_f, 0, H, 0, lf_f.reshape(B, L, H)[:, :Lh], Lh)
    xt, out_t = _trunk(x_prompt[:, Lh:].reshape(B * Lt, D), p_tail, hist_t,
                       {j: out_p[("s5_snap", j)] for j in range(n_s5) if ("s5_snap", j) in out_p}, {}, W,
                       n_seq=B, seq_len=Lt, steps=steps_a, precise=True)
    x_p = lax.dynamic_update_slice(x_f.reshape(B, L, D), xt.reshape(B, Lt, D), (0, Lh, 0)).reshape(B * L, D)
    out_p.update({key: val for key, val in out_t.items() if key[0] == "s5"})
    gdn0 = (jnp.zeros((B, n_vh, gd, gd), F32), jnp.zeros((B, cw - 1, qkv_dim), F32))
    x_p, out_b = _trunk(x_p, p_all, {}, zeros_s5, {j: gdn0 for j in range(n_gdn)}, W,
                        n_seq=B, seq_len=L, steps=steps_b, precise=False)
    out_p.update(out_b)

    ck = cache_fox_k.reshape(-1, P, H * hd)
    cv = cache_fox_v.reshape(-1, P, H * hd)
    s5_re = state_s5_re.reshape(n_s5, Bs, G * NS).astype(F32)
    s5_im = state_s5_im.reshape(n_s5, Bs, G * NS).astype(F32)
    hist_s = {j: (ck, cv, 0, 0, j * Bs, cache_fox_logf[j], P) for j in range(n_fox)}
    state_s = ({j: (s5_re[j], s5_im[j]) for j in range(n_s5)},
               {j: (state_gdn[j], state_gdn_conv[j]) for j in range(n_gdn)})
    p_s = {i: p_sample[i].reshape(Bs * Ls, pd) for i in range(depth)}
    x_s, out_s = _trunk(x_sample.reshape(Bs * Ls, D), p_s, hist_s, *state_s, W,
                        n_seq=Bs, seq_len=Ls, steps=steps_a, precise=True)
    x_s, out_sb = _trunk(x_s, p_s, hist_s, *state_s, W, n_seq=Bs, seq_len=Ls, steps=steps_b, precise=False)
    out_s.update(out_sb)

    def leaves(x, out, nb, sl):
        y = rmsnorm(x, norm_final).reshape(nb, sl, D)
        kv = [out[("fox", j)][0].reshape(nb, sl, 2, H, hd) for j in range(n_fox)]
        fk = jnp.stack([a[:, :, 0] for a in kv])
        fv = jnp.stack([a[:, :, 1] for a in kv])
        flf = jnp.stack([out[("fox", j)][1].reshape(nb, sl, -1)[..., :H] for j in range(n_fox)])
        sre = jnp.stack([out[("s5", j)][0].reshape(nb, G, NS) for j in range(n_s5)])
        sim = jnp.stack([out[("s5", j)][1].reshape(nb, G, NS) for j in range(n_s5)])
        gs = jnp.stack([out[("gdn", j)][0] for j in range(n_gdn)])
        gc = jnp.stack([out[("gdn", j)][1] for j in range(n_gdn)])
        return y, fk, fv, flf, sre, sim, gs, gc

    (y_p, fk_p, fv_p, flf_p, sre_p, sim_p, gs_p, gc_p) = leaves(x_p, out_p, B, L)
    (y_s, fk_s, fv_s, flf_s, sre_s, sim_s, gs_s, gc_s) = leaves(x_s, out_s, Bs, Ls)
    return (y_p, y_s, fk_p, fv_p, flf_p, fk_s, fv_s, flf_s, sre_p, sim_p, sre_s, sim_s,
            gs_p, gc_p, gs_s, gc_s)
```

```python
import functools
import math

import jax
import jax.numpy as jnp
from jax import lax
from jax.experimental import pallas as pl
from jax.experimental.pallas import tpu as pltpu

F32 = jnp.float32
BF16 = jnp.bfloat16
EPS = 1e-6
LANES = 128
SUBLANES = 8
VMEM_LIMIT = 56 * 1024 * 1024
CHUNK = 64
N_EXPERTS = 8
NEG_INF = float("-inf")
LOG2E = math.log2(math.e)


def _cparams(*sem):
    return pltpu.CompilerParams(dimension_semantics=sem, vmem_limit_bytes=VMEM_LIMIT)


def _rms_scale(x, g):
    ms = jnp.mean(x * x, axis=-1, keepdims=True)
    return x * lax.rsqrt(ms + EPS) * g


def _sigmoid(x):
    return 0.5 * jnp.tanh(0.5 * x) + 0.5


def _silu(x):
    return x * _sigmoid(x)


def _softplus(x):
    return jnp.maximum(x, 0.0) + jnp.log1p(jnp.exp(-jnp.abs(x)))


def _split_bf16(x):
    hi = x.astype(BF16)
    lo = (x - hi.astype(F32)).astype(BF16)
    return hi, lo


def _split_weight(w):
    w = w.astype(F32)
    hi = lax.reduce_precision(w, exponent_bits=8, mantissa_bits=7)
    return hi.astype(BF16), (w - hi).astype(BF16)


def _dot(a, b):
    return jnp.dot(a, b, preferred_element_type=F32)


def _dot_nt(a, b):
    return lax.dot_general(a, b, (((1,), (1,)), ((), ())), preferred_element_type=F32)


def _dot_tn(a, b):
    return lax.dot_general(a, b, (((0,), (0,)), ((), ())), preferred_element_type=F32)


def _parts(x, precise):
    return _split_bf16(x.astype(F32)) if precise else (x.astype(BF16),)


def _mm(xp, wp):
    if len(wp) == 1:
        return _dot(xp[0], wp[0])
    return _dot(xp[0], wp[0]) + _dot(xp[1], wp[0]) + _dot(xp[0], wp[1])


def _mm_nt(ap, bp):
    if len(ap) == 1:
        return _dot_nt(ap[0], bp[0])
    return _dot_nt(ap[0], bp[0]) + _dot_nt(ap[1], bp[0]) + _dot_nt(ap[0], bp[1])


def _read(refs):
    return tuple(r[...] for r in refs)


def _wparts(w):
    return tuple(w) if isinstance(w, (tuple, list)) else (w,)


def _row_tile(m, n_weight_parts=1):
    cap = 1024 // n_weight_parts
    return next(t for t in (1024, 512, 256, 128) if t <= cap and m % t == 0)


def _store_norm(x_ref, g_ref, xn_refs):
    xn = _rms_scale(x_ref[...], g_ref[...])
    for ref, part in zip(xn_refs, _parts(xn, len(xn_refs) == 2)):
        ref[...] = part


def _norm_mm_kernel(*refs, ranges, nw):
    x_ref, g_ref = refs[:2]
    w_refs = refs[2:2 + nw]
    outs = refs[2 + nw:2 + nw + len(ranges)]
    xn_refs = refs[2 + nw + len(ranges):]
    j = pl.program_id(1)

    @pl.when(j == 0)
    def _():
        _store_norm(x_ref, g_ref, xn_refs)

    acc = _mm(_read(xn_refs), _read(w_refs))
    for o_ref, (a, b) in zip(outs, ranges):
        @pl.when((j >= a) & (j < b))
        def _(o_ref=o_ref):
            o_ref[...] = acc.astype(o_ref.dtype)


def _range_map(i, j, *, a, n):
    return (i, jnp.clip(j - a, 0, n - 1))


def norm_mm(x, g, w, outs, *, tn):
    M, D = x.shape
    wp = _wparts(w)
    N = wp[0].shape[1]
    tm = _row_tile(M, len(wp))
    ranges = tuple((a, b) for _, a, b in outs)
    return pl.pallas_call(
        functools.partial(_norm_mm_kernel, ranges=ranges, nw=len(wp)),
        grid=(M // tm, N // tn),
        in_specs=[pl.BlockSpec((tm, D), lambda i, j: (i, 0)),
                  pl.BlockSpec((1, D), lambda i, j: (0, 0))]
                 + [pl.BlockSpec((D, tn), lambda i, j: (0, j))] * len(wp),
        out_specs=[pl.BlockSpec((tm, tn), functools.partial(_range_map, a=a, n=b - a))
                   for _, a, b in outs],
        out_shape=[jax.ShapeDtypeStruct((M, (b - a) * tn), dt) for dt, a, b in outs],
        scratch_shapes=[pltpu.VMEM((tm, D), BF16)] * len(wp),
        compiler_params=_cparams("parallel", "arbitrary"),
        name="norm_mm",
    )(x, g.reshape(1, D), *wp)


def _fox_in_kernel(*refs, nw, nt, n_heads, hd, tm, n_bufs):
    x_ref, g_ref = refs[:2]
    w_refs = refs[2:2 + nw]
    qkv_ref, k_ref, v_ref = refs[2 + nw + n_bufs:5 + nw + n_bufs]
    xn_refs = refs[5 + nw + n_bufs:]
    j = pl.program_id(1)

    @pl.when(j == 0)
    def _():
        _store_norm(x_ref, g_ref, xn_refs)

    acc = _mm(_read(xn_refs), _read(w_refs))
    qkv_ref[...] = acc.astype(qkv_ref.dtype)
    hpt = acc.shape[1] // hd
    for dst, base in ((k_ref, nt), (v_ref, 2 * nt)):
        for jj in range(nt):
            @pl.when(j == base + jj)
            def _(dst=dst, jj=jj):
                for hh in range(hpt):
                    dst[pl.ds(jj * hpt + hh, tm, stride=n_heads), :] = acc[:, hh * hd:(hh + 1) * hd]


def fox_in_proj(x, g, w, *, n_heads, hd, tn, bufs=None, slot=0, n_slots=1):
    M, D = x.shape
    wp = _wparts(w)
    nw = len(wp)
    tm = min(512, M)
    nt = D // tn
    n_bufs = 0 if bufs is None else 2
    blk0 = slot * (M // tm)
    head_spec = pl.BlockSpec((tm * n_heads, hd), lambda i, j: (blk0 + i, 0))
    head_shape = jax.ShapeDtypeStruct((n_slots * M * n_heads, hd), F32)
    return pl.pallas_call(
        functools.partial(_fox_in_kernel, nw=nw, nt=nt, n_heads=n_heads, hd=hd, tm=tm, n_bufs=n_bufs),
        grid=(M // tm, 3 * nt),
        in_specs=[pl.BlockSpec((tm, D), lambda i, j: (i, 0)),
                  pl.BlockSpec((1, D), lambda i, j: (0, 0))]
                 + [pl.BlockSpec((D, tn), lambda i, j: (0, j))] * nw
                 + [pl.BlockSpec(memory_space=pl.ANY)] * n_bufs,
        out_specs=[pl.BlockSpec((tm, tn), lambda i, j: (i, j)), head_spec, head_spec],
        out_shape=[jax.ShapeDtypeStruct((M, 3 * D), F32 if nw == 2 else BF16), head_shape, head_shape],
        scratch_shapes=[pltpu.VMEM((tm, D), BF16)] * nw,
        input_output_aliases={} if bufs is None else {2 + nw: 1, 3 + nw: 2},
        compiler_params=_cparams("parallel", "arbitrary"),
        name="fox_in_proj",
    )(x, g.reshape(1, D), *wp, *(bufs or ()))


def _glu_up_kernel(*refs, tiles_per_expert, nw):
    x_ref, g_ref = refs[:2]
    wg_refs = refs[2:2 + nw]
    wu_refs = refs[2 + nw:2 + 2 * nw]
    rest = refs[2 + 2 * nw:]
    comb_ref = rest[0] if tiles_per_expert else None
    o_ref = rest[1] if tiles_per_expert else rest[0]
    xn_refs = rest[2:] if tiles_per_expert else rest[1:]
    j = pl.program_id(1)

    @pl.when(j == 0)
    def _():
        _store_norm(x_ref, g_ref, xn_refs)

    xn = _read(xn_refs)
    h = _silu(_mm(xn, _read(wg_refs))) * _mm(xn, _read(wu_refs))
    if tiles_per_expert:
        comb = comb_ref[...]
        lane = lax.broadcasted_iota(jnp.int32, comb.shape, 1)
        col = jnp.sum(jnp.where(lane == j // tiles_per_expert, comb, 0.0), axis=1, keepdims=True)
        h = h * col
    o_ref[...] = h.astype(o_ref.dtype)


def glu_up(x, g, w_gate, w_up, comb=None, *, tn):
    M, D = x.shape
    wg, wu = _wparts(w_gate), _wparts(w_up)
    nw = len(wg)
    tm = _row_tile(M, nw)
    if comb is None:
        n_out, tpe = wg[0].shape[1], 0
        w_spec = pl.BlockSpec((D, tn), lambda i, j: (0, j))
        extra_in, extra_specs = (), []
    else:
        E, _, F = wg[0].shape
        n_out, tpe = E * F, F // tn
        w_spec = pl.BlockSpec((None, D, tn), lambda i, j: (j // tpe, 0, j % tpe))
        extra_in = (comb,)
        extra_specs = [pl.BlockSpec((tm, LANES), lambda i, j: (i, 0))]
    return pl.pallas_call(
        functools.partial(_glu_up_kernel, tiles_per_expert=tpe, nw=nw),
        grid=(M // tm, n_out // tn),
        in_specs=[pl.BlockSpec((tm, D), lambda i, j: (i, 0)),
                  pl.BlockSpec((1, D), lambda i, j: (0, 0))] + [w_spec] * (2 * nw) + extra_specs,
        out_specs=pl.BlockSpec((tm, tn), lambda i, j: (i, j)),
        out_shape=jax.ShapeDtypeStruct((M, n_out), F32 if nw == 2 else BF16),
        scratch_shapes=[pltpu.VMEM((tm, D), BF16)] * nw,
        compiler_params=_cparams("parallel", "arbitrary"),
        name="glu_up",
    )(x, g.reshape(1, D), *wg, *wu, *extra_in)


def _mm_res_kernel(*refs, nk, nw):
    x_ref = refs[0]
    w_refs = refs[1:1 + nw]
    r_ref, o_ref, acc_ref = refs[1 + nw:]
    k = pl.program_id(2)

    @pl.when(k == 0)
    def _():
        acc_ref[...] = jnp.zeros_like(acc_ref)

    acc_ref[...] += _mm(_parts(x_ref[...], nw == 2), _read(w_refs))

    @pl.when(k == nk - 1)
    def _():
        o_ref[...] = r_ref[...] + acc_ref[...]


def mm_res(x, w, res, *, tn, tk):
    M, K = x.shape
    wp = _wparts(w)
    N = wp[0].shape[1]
    tm = _row_tile(M, len(wp))
    nk = K // tk
    return pl.pallas_call(
        functools.partial(_mm_res_kernel, nk=nk, nw=len(wp)),
        grid=(M // tm, N // tn, nk),
        in_specs=[pl.BlockSpec((tm, tk), lambda i, j, k: (i, k))]
                 + [pl.BlockSpec((tk, tn), lambda i, j, k: (k, j))] * len(wp)
                 + [pl.BlockSpec((tm, tn), lambda i, j, k: (i, j))],
        out_specs=pl.BlockSpec((tm, tn), lambda i, j, k: (i, j)),
        out_shape=jax.ShapeDtypeStruct((M, N), F32),
        scratch_shapes=[pltpu.VMEM((tm, tn), F32)],
        compiler_params=_cparams("parallel", "parallel", "arbitrary"),
        name="mm_res",
    )(x, *wp, res)


def _glu_res_kernel(*refs, nw):
    y_ref = refs[0]
    wa_refs = refs[1:1 + nw]
    wb_refs = refs[1 + nw:1 + 2 * nw]
    r_ref, o_ref = refs[1 + 2 * nw:]
    y = _parts(y_ref[...], nw == 2)
    o_ref[...] = r_ref[...] + _mm(y, _read(wa_refs)) * _sigmoid(_mm(y, _read(wb_refs)))


def glu_res(y, w, res, *, tn, time_major=None):
    M, K = y.shape
    wp = _wparts(w)
    N = wp[0].shape[1] // 2
    nj = N // tn
    tm = _row_tile(M, len(wp))
    y_spec = pl.BlockSpec((tm, K), lambda i, j: (i, 0))
    if time_major is not None:
        n_seq, seq_len = time_major
        tps = seq_len // tm
        y = y.reshape(seq_len, n_seq * K)
        y_spec = pl.BlockSpec((tm, K), lambda i, j: (i % tps, i // tps))
    return pl.pallas_call(
        functools.partial(_glu_res_kernel, nw=len(wp)),
        grid=(M // tm, nj),
        in_specs=[y_spec]
                 + [pl.BlockSpec((K, tn), lambda i, j: (0, j))] * len(wp)
                 + [pl.BlockSpec((K, tn), lambda i, j: (0, j + nj))] * len(wp)
                 + [pl.BlockSpec((tm, tn), lambda i, j: (i, j))],
        out_specs=pl.BlockSpec((tm, tn), lambda i, j: (i, j)),
        out_shape=jax.ShapeDtypeStruct((M, N), F32),
        compiler_params=_cparams("parallel", "arbitrary"),
        name="glu_res",
    )(y, *wp, *wp, res)


def _ple_kernel(*refs, tn, nw):
    x_ref, g_ref = refs[:2]
    wg_refs = refs[2:2 + nw]
    p_ref = refs[2 + nw]
    wu_refs = refs[3 + nw:3 + 2 * nw]
    o_ref = refs[3 + 2 * nw]
    xn_refs = refs[4 + 2 * nw:]
    j = pl.program_id(1)

    @pl.when(j == 0)
    def _():
        _store_norm(x_ref, g_ref, xn_refs)

    gate = _sigmoid(_mm(_read(xn_refs), _read(wg_refs)))
    up = _mm(_parts(p_ref[...], nw == 2), _read(wu_refs))
    x_tile = x_ref[:, pl.ds(pl.multiple_of(j * tn, tn), tn)]
    o_ref[...] = x_tile + up * gate


def ple(x, g, w_gate, p, w_up, *, tn):
    M, D = x.shape
    P = p.shape[1]
    wg, wu = _wparts(w_gate), _wparts(w_up)
    nw = len(wg)
    tm = _row_tile(M, nw)
    return pl.pallas_call(
        functools.partial(_ple_kernel, tn=tn, nw=nw),
        grid=(M // tm, D // tn),
        in_specs=[pl.BlockSpec((tm, D), lambda i, j: (i, 0)),
                  pl.BlockSpec((1, D), lambda i, j: (0, 0))]
                 + [pl.BlockSpec((D, tn), lambda i, j: (0, j))] * nw
                 + [pl.BlockSpec((tm, P), lambda i, j: (i, 0))]
                 + [pl.BlockSpec((P, tn), lambda i, j: (0, j))] * nw,
        out_specs=pl.BlockSpec((tm, tn), lambda i, j: (i, j)),
        out_shape=jax.ShapeDtypeStruct((M, D), F32),
        scratch_shapes=[pltpu.VMEM((tm, D), BF16)] * nw,
        compiler_params=_cparams("parallel", "arbitrary"),
        name="ple",
    )(x, g.reshape(1, D), *wg, p, *wu)


def _rmsnorm_kernel(x_ref, g_ref, o_ref):
    o_ref[...] = _rms_scale(x_ref[...], g_ref[...]).astype(o_ref.dtype)


def _time_major_tiles(m, n_seq, seq_len):
    tm = _row_tile(m)
    return (tm, seq_len // tm) if seq_len % tm == 0 else None


def rmsnorm(x, g, *, time_major=None):
    M, D = x.shape
    tm = _row_tile(M)
    out_spec = pl.BlockSpec((tm, D), lambda i: (i, 0))
    out_shape = (M, D)
    if time_major is not None:
        n_seq, seq_len = time_major
        tm, tps = _time_major_tiles(M, n_seq, seq_len)
        out_spec = pl.BlockSpec((tm, D), lambda i: (i % tps, i // tps))
        out_shape = (seq_len, n_seq * D)
    out = pl.pallas_call(
        _rmsnorm_kernel,
        grid=(M // tm,),
        in_specs=[pl.BlockSpec((tm, D), lambda i: (i, 0)),
                  pl.BlockSpec((1, D), lambda i: (0, 0))],
        out_specs=out_spec,
        out_shape=jax.ShapeDtypeStruct(out_shape, F32),
        compiler_params=_cparams("parallel"),
        name="rmsnorm",
    )(x, g.reshape(1, D))
    return out.reshape(M, D)


def _small_proj_kernel(x_ref, g_ref, whi_ref, wlo_ref, aux_ref, o_ref, *, mode, nh):
    xn = _rms_scale(x_ref[...], g_ref[...])
    acc = _mm(_parts(xn, True), (whi_ref[...], wlo_ref[...]))
    lane = lax.broadcasted_iota(jnp.int32, acc.shape, 1)
    if mode == "logf":
        o_ref[...] = -_softplus(-(acc + aux_ref[0:1, :]))
    elif mode == "gdn":
        beta = _sigmoid(acc)
        gdec = -jnp.exp(aux_ref[1:2, :]) * _softplus(acc + aux_ref[0:1, :])
        o_ref[...] = jnp.where(lane < nh, beta, gdec)
    else:
        logits = jnp.where(lane < N_EXPERTS, acc, NEG_INF)
        v1 = jnp.max(logits, axis=1, keepdims=True)
        i1 = jnp.min(jnp.where(logits == v1, lane, LANES), axis=1, keepdims=True)
        rest = jnp.where(lane == i1, NEG_INF, logits)
        v2 = jnp.max(rest, axis=1, keepdims=True)
        i2 = jnp.min(jnp.where(rest == v2, lane, LANES), axis=1, keepdims=True)
        e2 = jnp.exp(v2 - v1)
        g1 = 1.0 / (1.0 + e2)
        o_ref[...] = jnp.where(lane == i1, g1, 0.0) + jnp.where(lane == i2, e2 * g1, 0.0)


def small_proj(x, g, w_parts, aux, mode, *, nh=0):
    M, D = x.shape
    tm = _row_tile(M)
    return pl.pallas_call(
        functools.partial(_small_proj_kernel, mode=mode, nh=nh),
        grid=(M // tm,),
        in_specs=[pl.BlockSpec((tm, D), lambda i: (i, 0)),
                  pl.BlockSpec((1, D), lambda i: (0, 0)),
                  pl.BlockSpec((D, LANES), lambda i: (0, 0)),
                  pl.BlockSpec((D, LANES), lambda i: (0, 0)),
                  pl.BlockSpec((SUBLANES, LANES), lambda i: (0, 0))],
        out_specs=pl.BlockSpec((tm, LANES), lambda i: (i, 0)),
        out_shape=jax.ShapeDtypeStruct((M, LANES), F32),
        compiler_params=_cparams("parallel"),
        name="small_proj_" + mode,
    )(x, g.reshape(1, D), *w_parts, aux)


def _pad_cols(w):
    return _split_weight(jnp.pad(w.astype(F32), ((0, 0), (0, LANES - w.shape[1]))))


def _aux_rows(*rows):
    out = [jnp.pad(vals.astype(F32), (start, LANES - start - vals.shape[0])) for start, vals in rows]
    out += [jnp.zeros((LANES,), F32)] * (SUBLANES - len(out))
    return jnp.stack(out)


def _lane_cumsum_kernel(x_ref, o_ref):
    x = x_ref[...]
    n = x.shape[1]
    lane = lax.broadcasted_iota(jnp.int32, x.shape, 1)
    shift = 1
    while shift < n:
        x = x + jnp.where(lane >= shift, pltpu.roll(x, shift, 1), 0.0)
        shift *= 2
    o_ref[...] = x


def lane_cumsum(x):
    R, n = x.shape
    rb = SUBLANES if R % SUBLANES == 0 else R
    return pl.pallas_call(
        _lane_cumsum_kernel,
        grid=(R // rb,),
        in_specs=[pl.BlockSpec((rb, n), lambda i: (i, 0))],
        out_specs=pl.BlockSpec((rb, n), lambda i: (i, 0)),
        out_shape=jax.ShapeDtypeStruct((R, n), F32),
        compiler_params=_cparams("parallel"),
        name="lane_cumsum",
    )(x)


def _fox_prompt_kernel(q_ref, k_ref, v_ref, fk_ref, o_ref, *, tq, nhb, hd, scale):
    qi = pl.program_id(2)

    def step(ki, carry, diagonal):
        rows = pl.ds(pl.multiple_of(ki * tq, tq), tq)
        out = []
        for h in range(nhb):
            m, l, acc = carry[h]
            cols = slice(h * hd, (h + 1) * hd)
            s = (_dot_nt(q_ref[:, cols], k_ref[rows, cols]) * (scale * LOG2E)
                 - fk_ref[h, pl.ds(ki, 1), :] * LOG2E)
            if diagonal:
                r = lax.broadcasted_iota(jnp.int32, s.shape, 0)
                c = lax.broadcasted_iota(jnp.int32, s.shape, 1)
                s = jnp.where(r >= c, s, NEG_INF)
            m_new = jnp.maximum(m, jnp.max(s, axis=1, keepdims=True))
            p = jnp.exp2(s - m_new)
            alpha = jnp.exp2(m - m_new)
            l = alpha * l + jnp.sum(p, axis=1, keepdims=True)
            acc = alpha * acc + _dot(p.astype(BF16), v_ref[rows, cols])
            out.append((m_new, l, acc))
        return tuple(out)

    init = tuple((jnp.full((tq, 1), NEG_INF, F32), jnp.zeros((tq, 1), F32), jnp.zeros((tq, hd), F32))
                 for _ in range(nhb))
    carry = lax.fori_loop(0, qi, lambda ki, c: step(ki, c, False), init)
    carry = step(qi, carry, True)
    o_ref[...] = jnp.concatenate([acc / l for _, l, acc in carry], axis=1).astype(o_ref.dtype)


def fox_attention_prompt(qkv, fk, *, B, L, H, hd, tq, nhb=2):
    nq = L // tq
    nhg = H // nhb
    return pl.pallas_call(
        functools.partial(_fox_prompt_kernel, tq=tq, nhb=nhb, hd=hd, scale=hd ** -0.5),
        grid=(B, nhg, nq),
        in_specs=[pl.BlockSpec((tq, nhb * hd), lambda b, g, i: (b * nq + i, g)),
                  pl.BlockSpec((L, nhb * hd), lambda b, g, i: (b, nhg + g)),
                  pl.BlockSpec((L, nhb * hd), lambda b, g, i: (b, 2 * nhg + g)),
                  pl.BlockSpec((nhb, nq, tq), lambda b, g, i: (b * nhg + g, 0, 0))],
        out_specs=pl.BlockSpec((tq, nhb * hd), lambda b, g, i: (b * nq + i, g)),
        out_shape=jax.ShapeDtypeStruct((B * L, H * hd), BF16),
        compiler_params=_cparams("parallel", "parallel", "arbitrary"),
        name="fox_attention_prompt",
    )(qkv, qkv, qkv, fk)


def _fox_cached_kernel(q_ref, kn_ref, vn_ref, kp_ref, vp_ref, f_ref, o_ref, *, P, nhb, hd, scale, precise):
    n = q_ref.shape[0]
    r = lax.broadcasted_iota(jnp.int32, (n, n), 0)
    c = lax.broadcasted_iota(jnp.int32, (n, n), 1)
    outs = []
    for h in range(nhb):
        cols = slice(h * hd, (h + 1) * hd)
        q = _parts(q_ref[:, cols], precise)
        f = f_ref[h]
        s_past = _mm_nt(q, _parts(kp_ref[0, :, cols], precise)) * scale - f[:, :P]
        s_new = _mm_nt(q, _parts(kn_ref[:, cols], precise)) * scale - f[:, P:P + n]
        s_new = jnp.where(r >= c, s_new, NEG_INF)
        m = jnp.maximum(jnp.max(s_past, axis=1, keepdims=True), jnp.max(s_new, axis=1, keepdims=True))
        p_past = jnp.exp(s_past - m)
        p_new = jnp.exp(s_new - m)
        l = jnp.sum(p_past, axis=1, keepdims=True) + jnp.sum(p_new, axis=1, keepdims=True)
        acc = (_mm(_parts(p_past, precise), _parts(vp_ref[0, :, cols], precise))
               + _mm(_parts(p_new, precise), _parts(vn_ref[:, cols], precise)))
        outs.append(acc / l)
    o_ref[...] = jnp.concatenate(outs, axis=1).astype(o_ref.dtype)


def fox_attention_cached(qkv, k_past, v_past, f, *, seq0, k_lane0, v_lane0, B, n, P, H, hd):
    precise = qkv.dtype == F32
    nhb = 4 if H % 4 == 0 else 1
    nhg = H // nhb
    w = nhb * hd
    return pl.pallas_call(
        functools.partial(_fox_cached_kernel, P=P, nhb=nhb, hd=hd, scale=hd ** -0.5, precise=precise),
        grid=(B, nhg),
        in_specs=[pl.BlockSpec((n, w), lambda b, g: (b, g)),
                  pl.BlockSpec((n, w), lambda b, g: (b, nhg + g)),
                  pl.BlockSpec((n, w), lambda b, g: (b, 2 * nhg + g)),
                  pl.BlockSpec((1, P, w), lambda b, g: (seq0 + b, 0, k_lane0 // nhb + g)),
                  pl.BlockSpec((1, P, w), lambda b, g: (seq0 + b, 0, v_lane0 // nhb + g)),
                  pl.BlockSpec((nhb, 1, f.shape[2]), lambda b, g: (b * nhg + g, 0, 0))],
        out_specs=pl.BlockSpec((n, w), lambda b, g: (b, g)),
        out_shape=jax.ShapeDtypeStruct((B * n, H * hd), qkv.dtype),
        compiler_params=_cparams("parallel", "arbitrary"),
        name="fox_attention_cached",
    )(qkv, qkv, qkv, k_past, v_past, f)


def _gelu_tanh(x):
    return 0.5 * x * (1.0 + jnp.tanh(math.sqrt(2.0 / math.pi) * (x + 0.044715 * (x * x * x))))


def _s5_kernel(*refs, nb, c, ns, ncm, snap_chunk):
    u_ref, bhi_ref, blo_ref = refs[:3]
    cm_refs = refs[3:3 + ncm]
    (ar_ref, ai_ref, d_ref, h0r_ref, h0i_ref, y_ref, hr_out, hi_out, hr_snap, hi_snap,
     bu_sc, st_sc, hr_sc, hi_sc) = refs[3 + ncm:]
    tc = pl.program_id(1)

    @pl.when(tc == 0)
    def _():
        hr_sc[...] = h0r_ref[...]
        hi_sc[...] = h0i_ref[...]

    u = u_ref[...]
    bu_sc[...] = _mm(_parts(u, True), (bhi_ref[0], blo_ref[0]))
    ar = jnp.broadcast_to(ar_ref[0], (nb, ns))
    ai = jnp.broadcast_to(ai_ref[0], (nb, ns))

    def step(t, carry):
        hr, hi = carry
        rows = pl.ds(pl.multiple_of(t * nb, nb), nb)
        nhr = ar * hr - ai * hi + bu_sc[rows, 0:ns]
        nhi = ar * hi + ai * hr + bu_sc[rows, ns:2 * ns]
        st_sc[rows, 0:ns] = nhr
        st_sc[rows, ns:2 * ns] = nhi
        return nhr, nhi

    hr, hi = lax.fori_loop(0, c, step, (hr_sc[...], hi_sc[...]))
    hr_sc[...] = hr
    hi_sc[...] = hi
    y = _mm(_parts(st_sc[...], ncm == 2), tuple(r[0] for r in cm_refs)) + d_ref[...] * u
    y_ref[...] = _gelu_tanh(y).astype(y_ref.dtype)

    @pl.when(tc == snap_chunk)
    def _():
        hr_snap[...] = hr
        hi_snap[...] = hi

    @pl.when(tc == pl.num_programs(1) - 1)
    def _():
        hr_out[...] = hr
        hi_out[...] = hi


def s5_scan(u_tm, h0r, h0i, consts, *, nb, c, precise, snap_steps=None):
    bhi, blo, cm_hi, cm_lo, ar, ai, d = consts
    cms = (cm_hi, cm_lo) if precise else (cm_hi,)
    rows, D = u_tm.shape
    nblk = D // LANES
    ns = ar.shape[2]
    ntc = rows // (c * nb)
    snap_chunk = ntc - 1 if snap_steps is None else snap_steps // c - 1
    blk = lambda g, t: (g, 0, 0)
    state_spec = pl.BlockSpec((nb, ns), lambda g, t: (0, g))
    state_shape = jax.ShapeDtypeStruct(h0r.shape, F32)
    return pl.pallas_call(
        functools.partial(_s5_kernel, nb=nb, c=c, ns=ns, ncm=len(cms), snap_chunk=snap_chunk),
        grid=(nblk, ntc),
        in_specs=[pl.BlockSpec((c * nb, LANES), lambda g, t: (t, g)),
                  pl.BlockSpec((1, LANES, 2 * ns), blk),
                  pl.BlockSpec((1, LANES, 2 * ns), blk)]
                 + [pl.BlockSpec((1, 2 * ns, LANES), blk)] * len(cms)
                 + [pl.BlockSpec((1, 1, ns), blk),
                    pl.BlockSpec((1, 1, ns), blk),
                    pl.BlockSpec((1, LANES), lambda g, t: (0, g)),
                    state_spec, state_spec],
        out_specs=[pl.BlockSpec((c * nb, LANES), lambda g, t: (t, g))] + [state_spec] * 4,
        out_shape=[jax.ShapeDtypeStruct((rows, D), F32 if precise else BF16)] + [state_shape] * 4,
        scratch_shapes=[pltpu.VMEM((c * nb, 2 * ns), F32),
                        pltpu.VMEM((c * nb, 2 * ns), F32),
                        pltpu.VMEM((nb, ns), F32),
                        pltpu.VMEM((nb, ns), F32)],
        compiler_params=_cparams("parallel", "arbitrary"),
        name="s5_scan",
    )(u_tm, bhi, blo, *cms, ar, ai, d, h0r, h0i)


def s5_constants(a_re, a_im, b_re, b_im, c_re, c_im, d_skip, log_dt):
    G, N = a_re.shape
    gc = b_re.shape[2]
    gpb = LANES // gc
    nblk = G // gpb
    a = lax.complex(a_re.astype(F32), a_im.astype(F32))
    dt = jnp.exp(log_dt.astype(F32))[:, None]
    a_bar = jnp.exp(a * dt)
    b_bar = ((a_bar - 1.0) / a)[..., None] * lax.complex(b_re.astype(F32), b_im.astype(F32))
    eye = jnp.eye(gpb, dtype=F32)

    def in_mat(b):
        b = b.reshape(nblk, gpb, N, gc)
        return jnp.einsum("kgnc,gh->kgchn", b, eye).reshape(nblk, gpb * gc, gpb * N)

    def out_mat(cc):
        cc = cc.reshape(nblk, gpb, gc, N)
        return jnp.einsum("kgcn,gh->kgnhc", cc, eye).reshape(nblk, gpb * N, gpb * gc)

    bhi, blo = _split_weight(jnp.concatenate([in_mat(b_bar.real), in_mat(b_bar.imag)], axis=2))
    cm_hi, cm_lo = _split_weight(jnp.concatenate([out_mat(c_re.astype(F32)), -out_mat(c_im.astype(F32))], axis=1))
    ar = a_bar.real.reshape(nblk, 1, gpb * N)
    ai = a_bar.imag.reshape(nblk, 1, gpb * N)
    return bhi, blo, cm_hi, cm_lo, ar, ai, d_skip.astype(F32).reshape(1, G * gc)


def _gdn_conv_kernel(x_ref, halo_ref, buf_ref, w_ref, o_ref, ext_sc, *, tl, n_valid, tiles_per_seq, n_norm_tiles):
    r = pl.program_id(0)
    j = pl.program_id(1)
    first = (r % tiles_per_seq) == 0
    ext_sc[0:SUBLANES, :] = jnp.where(first, buf_ref[0], halo_ref[...])
    ext_sc[SUBLANES:SUBLANES + n_valid, :] = x_ref[...]
    w = w_ref[...]
    y = x_ref[...] * w[3:4, :]
    for s in (1, 2, 3):
        y = y + ext_sc[SUBLANES - s:SUBLANES - s + n_valid, :] * w[3 - s:4 - s, :]
    y = _silu(y)

    def store(val):
        if n_valid < tl:
            val = jnp.concatenate([val, jnp.zeros((tl - n_valid, val.shape[1]), F32)], axis=0)
        o_ref[...] = val.astype(o_ref.dtype)

    @pl.when(j < n_norm_tiles)
    def _():
        segs = []
        for a in range(0, y.shape[1], LANES):
            seg = y[:, a:a + LANES]
            segs.append(seg * lax.rsqrt(jnp.sum(seg * seg, axis=1, keepdims=True) + EPS))
        store(jnp.concatenate(segs, axis=1))

    @pl.when(j >= n_norm_tiles)
    def _():
        store(y)


def gdn_conv(x, buf8, w8, *, n_seq, seq_len, tl_in, tl_out, n_norm_cols, tc):
    C = x.shape[1]
    tiles_per_seq = seq_len // tl_in
    n_tiles = n_seq * tiles_per_seq
    hpt = tl_in // SUBLANES
    return pl.pallas_call(
        functools.partial(_gdn_conv_kernel, tl=tl_out, n_valid=tl_in, tiles_per_seq=tiles_per_seq,
                          n_norm_tiles=n_norm_cols // tc),
        grid=(n_tiles, C // tc),
        in_specs=[pl.BlockSpec((tl_in, tc), lambda r, j: (r, j)),
                  pl.BlockSpec((SUBLANES, tc), lambda r, j: (jnp.maximum(r * hpt - 1, 0), j)),
                  pl.BlockSpec((1, SUBLANES, tc), lambda r, j: (r // tiles_per_seq, 0, j)),
                  pl.BlockSpec((SUBLANES, tc), lambda r, j: (0, j))],
        out_specs=pl.BlockSpec((tl_out, tc), lambda r, j: (r, j)),
        out_shape=jax.ShapeDtypeStruct((n_tiles * tl_out, C), BF16),
        scratch_shapes=[pltpu.VMEM((SUBLANES + tl_in, tc), F32)],
        compiler_params=_cparams("parallel", "arbitrary"),
        name="gdn_conv",
    )(x, x, buf8, w8)


def _gdn_prep_kernel(qk_ref, gt_ref, ti_ref, a_ref, gc_ref, n_sc, nt_sc, x_sc, *, C, cps, n_qk, rep, hd, scale):
    nh = n_qk * rep
    n_sys = cps * nh
    r = lax.broadcasted_iota(jnp.int32, (C, C), 0)
    c = lax.broadcasted_iota(jnp.int32, (C, C), 1)
    tril = (r >= c).astype(BF16)
    lane_pad = jnp.zeros((C, LANES - C), F32)
    for cc in range(cps):
        rows = slice(cc * C, (cc + 1) * C)
        gates = gt_ref[rows, :]
        lane = lax.broadcasted_iota(jnp.int32, gates.shape, 1)
        g = jnp.where((lane >= nh) & (lane < 2 * nh), gates, 0.0)
        g_hi = g.astype(BF16)
        g_r1 = g - g_hi.astype(F32)
        g_mid = g_r1.astype(BF16)
        g_lo = (g_r1 - g_mid.astype(F32)).astype(BF16)
        gcum = _dot(tril, g_hi) + _dot(tril, g_mid) + _dot(tril, g_lo)
        gc_ref[rows, :] = gcum
        gcum_t = gcum.T
        for hq in range(n_qk):
            q = qk_ref[rows, hq * hd:(hq + 1) * hd]
            k = qk_ref[rows, (n_qk + hq) * hd:(n_qk + hq + 1) * hd]
            kk = _dot_nt(k, k)
            qk = _dot_nt(q, k) * scale
            for rr in range(rep):
                hv = hq * rep + rr
                sys = cc * nh + hv
                diff = gcum[:, nh + hv:nh + hv + 1] - gcum_t[nh + hv:nh + hv + 1, :]
                decay = jnp.exp(jnp.where(r >= c, diff, NEG_INF))
                n_val = jnp.where(r > c, gates[:, hv:hv + 1] * kk * decay, 0.0)
                n_sc[sys * C:(sys + 1) * C, :] = jnp.concatenate([n_val, lane_pad], axis=1)
                a_ref[cc, hv] = (qk * decay).astype(a_ref.dtype)
    for i in range(C):
        slab = n_sc[pl.ds(i, n_sys, stride=C), :]
        nt_sc[i * C:(i + 1) * C, :] = slab.T[:C, :]
    _tri_solve(nt_sc, x_sc, C)
    row_pad = jnp.zeros((LANES - C, n_sys), F32)
    for i in range(C):
        cols = jnp.concatenate([x_sc[i * C:(i + 1) * C, :], row_pad], axis=0)
        n_sc[pl.ds(i, n_sys, stride=C), :] = cols.T
    for cc in range(cps):
        for hv in range(nh):
            sys = cc * nh + hv
            ti_ref[cc, hv] = n_sc[sys * C:(sys + 1) * C, :C].astype(ti_ref.dtype)


def gdn_prep(qkv_c, gates, *, C, n_qk, rep, hd):
    rows = qkv_c.shape[0]
    nch = rows // C
    nh = n_qk * rep
    cps = LANES // nh
    assert nch % cps == 0 and cps * nh == LANES
    return pl.pallas_call(
        functools.partial(_gdn_prep_kernel, C=C, cps=cps, n_qk=n_qk, rep=rep, hd=hd, scale=hd ** -0.5),
        grid=(nch // cps,),
        in_specs=[pl.BlockSpec((cps * C, 2 * n_qk * hd), lambda n: (n, 0)),
                  pl.BlockSpec((cps * C, LANES), lambda n: (n, 0))],
        out_specs=[pl.BlockSpec((cps, nh, C, C), lambda n: (n, 0, 0, 0)),
                   pl.BlockSpec((cps, nh, C, C), lambda n: (n, 0, 0, 0)),
                   pl.BlockSpec((cps * C, LANES), lambda n: (n, 0))],
        out_shape=[jax.ShapeDtypeStruct((nch, nh, C, C), BF16),
                   jax.ShapeDtypeStruct((nch, nh, C, C), BF16),
                   jax.ShapeDtypeStruct((rows, LANES), F32)],
        scratch_shapes=[pltpu.VMEM((LANES * C, LANES), F32),
                        pltpu.VMEM((C * C, LANES), F32),
                        pltpu.VMEM((C * C, LANES), F32)],
        compiler_params=_cparams("parallel"),
        name="gdn_prep",
    )(qkv_c, gates)


def _tri_solve(n_ref, x_ref, C):
    nblk = C // SUBLANES
    lanes = n_ref.shape[1]
    sub = lax.broadcasted_iota(jnp.int32, (SUBLANES, lanes), 0)
    x_ref[...] = jnp.zeros(x_ref.shape, F32)
    for ib in range(nblk):
        def row(r, _, ib=ib):
            base = (ib * SUBLANES + r) * C
            acc = [jnp.zeros((SUBLANES, lanes), F32) for _ in range(ib)]
            acc.append(jnp.where(sub == r, 1.0, 0.0).astype(F32))
            for jb in range(ib + 1):
                for j in range(jb * SUBLANES, (jb + 1) * SUBLANES):
                    nb = jnp.broadcast_to(n_ref[pl.ds(base + j, 1), :], (SUBLANES, lanes))
                    for cb in range(jb + 1):
                        acc[cb] = acc[cb] - nb * x_ref[j * C + cb * SUBLANES:j * C + (cb + 1) * SUBLANES, :]
            for cb in range(ib + 1):
                x_ref[pl.ds(pl.multiple_of(base + cb * SUBLANES, SUBLANES), SUBLANES), :] = acc[cb]
            return 0

        lax.fori_loop(0, SUBLANES, row, 0)


def _gdn_scan_kernel(q_ref, k_ref, v_ref, z_ref, gt_ref, gc_ref, ti_ref, a_ref, s0_ref, ng_ref,
                     o_ref, s_out, s_sc, *, C, hg, rep, hd, nh, n_valid, scale):
    n = pl.program_id(2)
    hgi = pl.program_id(1)

    @pl.when(n == 0)
    def _():
        s_sc[...] = s0_ref[0]

    shift = (LANES - hgi * hg) % LANES
    gates = pltpu.roll(gt_ref[...], shift, 1)
    gcum = pltpu.roll(gc_ref[...], shift, 1)
    heads = range(hg)
    s_old = [s_sc[hh] for hh in heads]
    s_bf = [s.astype(BF16) for s in s_old]
    beta = [gates[:, hh:hh + 1] for hh in heads]
    gc = [gcum[:, nh + hh:nh + hh + 1] for hh in heads]
    eg = [jnp.exp(g) for g in gc]
    g_tot = [g[C - 1:C, :] for g in gc]
    k = [k_ref[:, (hh // rep) * hd:(hh // rep + 1) * hd].astype(F32) for hh in heads]
    q = [q_ref[:, (hh // rep) * hd:(hh // rep + 1) * hd].astype(F32) for hh in heads]
    v = [v_ref[:, hh * hd:(hh + 1) * hd].astype(F32) for hh in heads]
    uw = [_dot(ti_ref[0, hh].astype(BF16),
               jnp.concatenate([v[hh] * beta[hh], k[hh] * (beta[hh] * eg[hh])], axis=1).astype(BF16))
          for hh in heads]
    q_dec = [(q[hh] * (scale * eg[hh])).astype(BF16) for hh in heads]
    k_dec = [(k[hh] * jnp.exp(g_tot[hh] - gc[hh])).astype(BF16) for hh in heads]
    v_new = [(uw[hh][:, :hd] - _dot(uw[hh][:, hd:].astype(BF16), s_bf[hh])).astype(BF16) for hh in heads]
    o = [_dot(q_dec[hh], s_bf[hh]) + _dot(a_ref[0, hh], v_new[hh]) for hh in heads]
    s_new = [s_old[hh] * jnp.exp(g_tot[hh]) + _dot_tn(k_dec[hh], v_new[hh]) for hh in heads]
    for hh in heads:
        s_sc[hh] = s_new[hh]
    ng = ng_ref[...]
    outs = [_rms_scale(o[hh][:n_valid], ng) * _silu(z_ref[:, hh * hd:(hh + 1) * hd].astype(F32)) for hh in heads]
    o_ref[...] = jnp.concatenate(outs, axis=1).astype(o_ref.dtype)

    @pl.when(n == pl.num_programs(2) - 1)
    def _():
        s_out[0] = s_sc[...]


def gdn_scan(qkv_c, z, gates, gcum, tinv, amat, s0, norm_g, *, n_seq, nc, C, n_valid, n_qk, rep, hd, hg):
    nh = n_qk * rep
    nhg = nh // hg
    qw = (hg // rep) * hd
    k_off = n_qk * hd // qw
    v_off = 2 * n_qk * hd // (hg * hd)
    chunk = lambda s, g, n: s * nc + n
    return pl.pallas_call(
        functools.partial(_gdn_scan_kernel, C=C, hg=hg, rep=rep, hd=hd, nh=nh, n_valid=n_valid,
                          scale=hd ** -0.5),
        grid=(n_seq, nhg, nc),
        in_specs=[pl.BlockSpec((C, qw), lambda s, g, n: (chunk(s, g, n), g)),
                  pl.BlockSpec((C, qw), lambda s, g, n: (chunk(s, g, n), k_off + g)),
                  pl.BlockSpec((C, hg * hd), lambda s, g, n: (chunk(s, g, n), v_off + g)),
                  pl.BlockSpec((n_valid, hg * hd), lambda s, g, n: (chunk(s, g, n), g)),
                  pl.BlockSpec((C, LANES), lambda s, g, n: (chunk(s, g, n), 0)),
                  pl.BlockSpec((C, LANES), lambda s, g, n: (chunk(s, g, n), 0)),
                  pl.BlockSpec((1, hg, C, C), lambda s, g, n: (chunk(s, g, n), g, 0, 0)),
                  pl.BlockSpec((1, hg, C, C), lambda s, g, n: (chunk(s, g, n), g, 0, 0)),
                  pl.BlockSpec((1, hg, hd, hd), lambda s, g, n: (s, g, 0, 0)),
                  pl.BlockSpec((1, hd), lambda s, g, n: (0, 0))],
        out_specs=[pl.BlockSpec((n_valid, hg * hd), lambda s, g, n: (chunk(s, g, n), g)),
                   pl.BlockSpec((1, hg, hd, hd), lambda s, g, n: (s, g, 0, 0))],
        out_shape=[jax.ShapeDtypeStruct((n_seq * nc * n_valid, nh * hd), BF16),
                   jax.ShapeDtypeStruct((n_seq, nh, hd, hd), F32)],
        scratch_shapes=[pltpu.VMEM((hg, hd, hd), F32)],
        compiler_params=_cparams("parallel", "parallel", "arbitrary"),
        name="gdn_scan",
    )(qkv_c, qkv_c, qkv_c, z, gates, gcum, tinv, amat, s0, norm_g.reshape(1, hd))


def _trunk(x, p, fox_hist, s5_h0, gdn_state, W, *, n_seq, seq_len, steps, precise, snap_steps=None,
           fox_kv=None):
    M, D = x.shape
    H, hd = W["fox_heads"]
    n_qk, rep, gd = W["gdn_heads"]
    n_vh = n_qk * rep
    key_dim, val_dim = n_qk * gd, n_vh * gd
    qkv_dim = 2 * key_dim + val_dim
    cw = W["gdn_conv_w"].shape[1]
    tn = min(512, D)
    outs = {} if fox_kv is None else {"fox_kv": fox_kv}

    def wsel(name, idx, pr):
        hi = W[name][idx]
        return (hi, W[name + "_lo"][idx]) if pr else hi

    def mixer(i, x):
        j = i // 3
        pm = precise
        g_mix = W["norm_mix"][i]
        if i % 3 == 0:
            qkv, k_heads, v_heads = fox_in_proj(x, g_mix, wsel("fox_w_qkv", j, pm), n_heads=H, hd=hd, tn=tn,
                                                bufs=outs.get("fox_kv"), slot=j, n_slots=W["n_fox"])
            outs["fox_kv"] = (k_heads, v_heads)
            outs[("fox_qkv", j)] = qkv
            logf = small_proj(x, g_mix, W["fox_w_f"][j], W["fox_aux"][j], "logf")[:, :H]
            lf = jnp.swapaxes(logf.reshape(n_seq, seq_len, H), 1, 2)
            if j not in fox_hist:
                tq = next(t for t in (512, 384, 256, 128, seq_len) if seq_len % t == 0)
                fk = lane_cumsum(lf.reshape(n_seq * H, seq_len)).reshape(n_seq * H, seq_len // tq, tq)
                o = fox_attention_prompt(qkv, fk, B=n_seq, L=seq_len, H=H, hd=hd, tq=tq)
            else:
                k_past, v_past, k_lane0, v_lane0, seq0, lf_past, P = fox_hist[j]
                fpad = -(-(P + seq_len) // LANES) * LANES
                lf_all = jnp.concatenate([jnp.swapaxes(lf_past.astype(F32), 1, 2), lf], axis=2)
                lf_all = jnp.pad(lf_all, ((0, 0), (0, 0), (0, fpad - (P + seq_len))))
                f = lane_cumsum(lf_all.reshape(n_seq * H, fpad)).reshape(n_seq * H, 1, fpad)
                o = fox_attention_cached(qkv, k_past, v_past, f, seq0=seq0, k_lane0=k_lane0, v_lane0=v_lane0,
                                         B=n_seq, n=seq_len, P=P, H=H, hd=hd)
            outs[("fox", j)] = logf
            return mm_res(o, wsel("fox_w_out", j, pm), x, tn=tn, tk=D)
        if i % 3 == 1:
            tmaj = (n_seq, seq_len) if not pm and _time_major_tiles(M, n_seq, seq_len) else None
            if tmaj:
                u_tm = rmsnorm(x, g_mix, time_major=tmaj)
            else:
                u_tm = jnp.swapaxes(rmsnorm(x, g_mix).reshape(n_seq, seq_len, D), 0, 1).reshape(M, D)
            y_tm, hr, hi, hr_snap, hi_snap = s5_scan(u_tm, s5_h0[j][0], s5_h0[j][1], W["s5_consts"][j],
                                                     nb=n_seq, c=min(CHUNK, seq_len), precise=pm,
                                                     snap_steps=snap_steps)
            outs[("s5", j)] = (hr, hi)
            outs[("s5_snap", j)] = (hr_snap, hi_snap)
            if tmaj:
                return glu_res(y_tm, wsel("s5_w_glu", j, pm), x, tn=tn, time_major=tmaj)
            y = jnp.swapaxes(y_tm.reshape(seq_len, n_seq, D), 0, 1).reshape(M, D)
            return glu_res(y, wsel("s5_w_glu", j, pm), x, tn=tn)
        s0, conv_rows = gdn_state[j]
        nq = qkv_dim // tn
        nz = val_dim // tn
        qkv_raw, z = norm_mm(x, g_mix, W["gdn_w_qkvz"][j], [(F32, 0, nq), (BF16, nq, nq + nz)], tn=tn)
        gates = small_proj(x, g_mix, W["gdn_w_ba"][j], W["gdn_aux"][j], "gdn", nh=n_vh)
        C = CHUNK
        n_valid = min(C, seq_len)
        nc = seq_len // n_valid
        buf = jnp.pad(conv_rows.astype(F32), ((0, 0), (SUBLANES - (cw - 1), 0), (0, 0)))
        tl_in = min(512, seq_len)
        qkv_c = gdn_conv(qkv_raw, buf, W["gdn_conv_w8"][j], n_seq=n_seq, seq_len=seq_len, tl_in=tl_in,
                         tl_out=max(tl_in, C), n_norm_cols=2 * key_dim,
                         tc=key_dim if tl_in < C else min(512, key_dim))
        gates_c = jnp.pad(gates.reshape(n_seq * nc, n_valid, LANES),
                          ((0, 0), (0, C - n_valid), (0, 0))).reshape(n_seq * nc * C, LANES)
        tinv, amat, gcum = gdn_prep(qkv_c, gates_c, C=C, n_qk=n_qk, rep=rep, hd=gd)
        o, s_last = gdn_scan(qkv_c, z, gates_c, gcum, tinv, amat, s0.astype(F32), W["gdn_norm"][j],
                             n_seq=n_seq, nc=nc, C=C, n_valid=n_valid, n_qk=n_qk, rep=rep, hd=gd,
                             hg=min(8, n_vh))
        rows = jnp.concatenate([conv_rows.astype(F32), qkv_raw.reshape(n_seq, seq_len, qkv_dim)], axis=1)
        outs[("gdn", j)] = (s_last, rows[:, -(cw - 1):])
        return mm_res(o, W["gdn_w_out"][j], x, tn=tn, tk=D)

    def channel(i, x):
        m = i // 2
        g_ffn = W["norm_ffn"][i]
        if i % 2 == 0:
            pf = precise
            h = glu_up(x, g_ffn, wsel("ffn_w_gate", m, pf), wsel("ffn_w_up", m, pf), tn=tn)
            x = mm_res(h, wsel("ffn_w_down", m, pf), x, tn=tn, tk=D)
        else:
            pf = False
            comb = small_proj(x, g_ffn, W["moe_w_router"][m], W["zero_aux"], "router")
            h = glu_up(x, g_ffn, W["moe_w_gate"][m], W["moe_w_up"][m], comb, tn=tn)
            x = mm_res(h, W["moe_w_down"][m], x, tn=tn, tk=D)
        p_i = p[i] if pf else p[i].astype(BF16)
        return ple(x, W["norm_ple"][i], wsel("ple_w_gate", i, pf), p_i, wsel("ple_w_up", i, pf), tn=tn)

    for i, part in steps:
        x = mixer(i, x) if part == "mixer" else channel(i, x)
    return x, outs


def kernel(x_prompt, x_sample, cache_fox_k, cache_fox_v, cache_fox_logf, state_s5_re, state_s5_im, state_gdn, state_gdn_conv, p_prompt, p_sample, norm_mix, norm_ffn, norm_ple, norm_final, fox_w_in, fox_b_f, fox_w_out, s5_a_re, s5_a_im, s5_b_re, s5_b_im, s5_c_re, s5_c_im, s5_d, s5_log_dt, s5_w_glu, gdn_w_in, gdn_conv_w, gdn_a_log, gdn_dt_bias, gdn_norm, gdn_w_out, ffn_w_gate, ffn_w_up, ffn_w_down, moe_w_router, moe_w_gate, moe_w_up, moe_w_down, ple_w_up, ple_w_gate):
    B, L, D = x_prompt.shape
    Bs, Ls, _ = x_sample.shape
    depth = norm_mix.shape[0]
    P = cache_fox_k.shape[2]
    H, hd = cache_fox_k.shape[3], cache_fox_k.shape[4]
    n_vh, gd = state_gdn.shape[2], state_gdn.shape[3]
    qkv_dim = state_gdn_conv.shape[3]
    val_dim = n_vh * gd
    n_qk = (qkv_dim - val_dim) // (2 * gd)
    cw = gdn_conv_w.shape[1]
    G, NS = s5_a_re.shape[1], s5_a_re.shape[2]
    E, _, Fe = moe_w_gate.shape[1:]
    n_fox, n_s5, n_gdn = fox_w_in.shape[0], s5_a_re.shape[0], gdn_w_in.shape[0]

    first_router = 1
    n_lo = {"fox": 1, "ffn": 1, "ple": 1, "s5": 1}

    def hi_lo(w, n):
        return w.astype(BF16), _split_weight(w[:n])[1]

    W = {"norm_mix": norm_mix, "norm_ffn": norm_ffn, "norm_ple": norm_ple, "norm_final": norm_final,
         "fox_heads": (H, hd), "n_fox": n_fox, "gdn_heads": (n_qk, n_vh // n_qk, gd), "gdn_conv_w": gdn_conv_w,
         "gdn_norm": gdn_norm, "zero_aux": jnp.zeros((SUBLANES, LANES), F32)}
    W["fox_w_qkv"], W["fox_w_qkv_lo"] = hi_lo(fox_w_in[:, :, :3 * D], n_lo["fox"])
    W["fox_w_out"], W["fox_w_out_lo"] = hi_lo(fox_w_out, n_lo["fox"])
    W["fox_w_f"] = [_pad_cols(fox_w_in[j, :, 3 * D:]) for j in range(n_fox)]
    W["fox_aux"] = [_aux_rows((0, fox_b_f[j])) for j in range(n_fox)]
    W["s5_w_glu"], W["s5_w_glu_lo"] = hi_lo(s5_w_glu, n_lo["s5"])
    W["s5_consts"] = [s5_constants(s5_a_re[j], s5_a_im[j], s5_b_re[j], s5_b_im[j], s5_c_re[j], s5_c_im[j],
                                   s5_d[j], s5_log_dt[j]) for j in range(n_s5)]
    W["gdn_w_qkvz"] = gdn_w_in[:, :, :qkv_dim + val_dim].astype(BF16)
    W["gdn_w_ba"] = [_pad_cols(gdn_w_in[j, :, qkv_dim + val_dim:]) for j in range(n_gdn)]
    W["gdn_aux"] = [_aux_rows((n_vh, gdn_dt_bias[j]), (n_vh, gdn_a_log[j])) for j in range(n_gdn)]
    W["gdn_conv_w8"] = jnp.pad(gdn_conv_w.astype(F32), ((0, 0), (0, SUBLANES - cw), (0, 0)))
    W["gdn_w_out"] = gdn_w_out.astype(BF16)
    W["ffn_w_gate"], W["ffn_w_gate_lo"] = hi_lo(ffn_w_gate, n_lo["ffn"])
    W["ffn_w_up"], W["ffn_w_up_lo"] = hi_lo(ffn_w_up, n_lo["ffn"])
    W["ffn_w_down"], W["ffn_w_down_lo"] = hi_lo(ffn_w_down, n_lo["ffn"])
    W["moe_w_router"] = [_pad_cols(moe_w_router[m]) for m in range(moe_w_router.shape[0])]
    W["moe_w_gate"] = moe_w_gate.astype(BF16)
    W["moe_w_up"] = moe_w_up.astype(BF16)
    W["moe_w_down"] = moe_w_down.astype(BF16).reshape(-1, E * Fe, D)
    W["ple_w_gate"], W["ple_w_gate_lo"] = hi_lo(ple_w_gate, n_lo["ple"])
    W["ple_w_up"], W["ple_w_up_lo"] = hi_lo(ple_w_up, n_lo["ple"])

    steps = [(i, part) for i in range(depth) for part in ("mixer", "channel")]
    steps_a, steps_b = steps[:2 * first_router + 1], steps[2 * first_router + 1:]
    pd = p_prompt.shape[-1]

    Lt = LANES
    Lh = L - Lt
    p_all = {i: p_prompt[i].reshape(B * L, pd) for i in range(depth)}
    p_tail = {i: p_prompt[i][:, Lh:].reshape(B * Lt, pd) for i in range(depth)}
    zeros_s5 = {j: (jnp.zeros((B, G * NS), F32),) * 2 for j in range(n_s5)}
    x_f, out_p = _trunk(x_prompt.reshape(B * L, D), p_all, {}, zeros_s5, {}, W,
                        n_seq=B, seq_len=L, steps=steps_a, precise=False, snap_steps=Lh)
    hist_t = {}
    for j in range(n_fox):
        if ("fox", j) in out_p:
            qkv_f = out_p[("fox_qkv", j)].reshape(B, L, 3 * D)
            hist_t[j] = (qkv_f, qkv_f, H, 2 * H, 0, out_p[("fox", j)].reshape(B, L, H)[:, :Lh], Lh)
    xt, out_t = _trunk(x_prompt[:, Lh:].reshape(B * Lt, D), p_tail, hist_t,
                       {j: out_p[("s5_snap", j)] for j in range(n_s5) if ("s5_snap", j) in out_p}, {}, W,
                       n_seq=B, seq_len=Lt, steps=steps_a, precise=True)
    x_p = lax.dynamic_update_slice(x_f.reshape(B, L, D), xt.reshape(B, Lt, D), (0, Lh, 0)).reshape(B * L, D)
    out_p.update({key: val for key, val in out_t.items() if key[0] == "s5"})
    gdn0 = (jnp.zeros((B, n_vh, gd, gd), F32), jnp.zeros((B, cw - 1, qkv_dim), F32))
    x_p, out_b = _trunk(x_p, p_all, {}, zeros_s5, {j: gdn0 for j in range(n_gdn)}, W,
                        n_seq=B, seq_len=L, steps=steps_b, precise=False, fox_kv=out_p.get("fox_kv"))
    out_p.update(out_b)

    ck = cache_fox_k.reshape(-1, P, H * hd)
    cv = cache_fox_v.reshape(-1, P, H * hd)
    s5_re = state_s5_re.reshape(n_s5, Bs, G * NS).astype(F32)
    s5_im = state_s5_im.reshape(n_s5, Bs, G * NS).astype(F32)
    hist_s = {j: (ck, cv, 0, 0, j * Bs, cache_fox_logf[j], P) for j in range(n_fox)}
    state_s = ({j: (s5_re[j], s5_im[j]) for j in range(n_s5)},
               {j: (state_gdn[j], state_gdn_conv[j]) for j in range(n_gdn)})
    p_s = {i: p_sample[i].reshape(Bs * Ls, pd) for i in range(depth)}
    x_s, out_s = _trunk(x_sample.reshape(Bs * Ls, D), p_s, hist_s, *state_s, W,
                        n_seq=Bs, seq_len=Ls, steps=steps_a, precise=True)
    x_s, out_sb = _trunk(x_s, p_s, hist_s, *state_s, W, n_seq=Bs, seq_len=Ls, steps=steps_b, precise=False,
                         fox_kv=out_s.get("fox_kv"))
    out_s.update(out_sb)

    def leaves(x, out, nb, sl):
        y = rmsnorm(x, norm_final).reshape(nb, sl, D)
        fk, fv = (a.reshape(n_fox, nb, sl, H, hd) for a in out["fox_kv"])
        flf = jnp.stack([out[("fox", j)].reshape(nb, sl, H) for j in range(n_fox)])
        sre = jnp.stack([out[("s5", j)][0].reshape(nb, G, NS) for j in range(n_s5)])
        sim = jnp.stack([out[("s5", j)][1].reshape(nb, G, NS) for j in range(n_s5)])
        gs = jnp.stack([out[("gdn", j)][0] for j in range(n_gdn)])
        gc = jnp.stack([out[("gdn", j)][1] for j in range(n_gdn)])
        return y, fk, fv, flf, sre, sim, gs, gc

    (y_p, fk_p, fv_p, flf_p, sre_p, sim_p, gs_p, gc_p) = leaves(x_p, out_p, B, L)
    (y_s, fk_s, fv_s, flf_s, sre_s, sim_s, gs_s, gc_s) = leaves(x_s, out_s, Bs, Ls)
    return (y_p, y_s, fk_p, fv_p, flf_p, fk_s, fv_s, flf_s, sre_p, sim_p, sre_s, sim_s,
            gs_p, gc_p, gs_s, gc_s)
```

```python
import functools
import math

import jax
import jax.numpy as jnp
from jax import lax
from jax.experimental import pallas as pl
from jax.experimental.pallas import tpu as pltpu

F32 = jnp.float32
BF16 = jnp.bfloat16
EPS = 1e-6
LANES = 128
SUBLANES = 8
VMEM_LIMIT = 56 * 1024 * 1024
CHUNK = 64
N_EXPERTS = 8
NEG_INF = float("-inf")
LOG2E = math.log2(math.e)


def _cparams(*sem):
    return pltpu.CompilerParams(dimension_semantics=sem, vmem_limit_bytes=VMEM_LIMIT)


def _rms_scale(x, g):
    ms = jnp.mean(x * x, axis=-1, keepdims=True)
    return x * lax.rsqrt(ms + EPS) * g


def _sigmoid(x):
    return 0.5 * jnp.tanh(0.5 * x) + 0.5


def _silu(x):
    return x * _sigmoid(x)


def _softplus(x):
    return jnp.maximum(x, 0.0) + jnp.log1p(jnp.exp(-jnp.abs(x)))


def _split_bf16(x):
    hi = x.astype(BF16)
    lo = (x - hi.astype(F32)).astype(BF16)
    return hi, lo


def _split_weight(w):
    w = w.astype(F32)
    hi = lax.reduce_precision(w, exponent_bits=8, mantissa_bits=7)
    return hi.astype(BF16), (w - hi).astype(BF16)


def _dot(a, b):
    return jnp.dot(a, b, preferred_element_type=F32)


def _dot_nt(a, b):
    return lax.dot_general(a, b, (((1,), (1,)), ((), ())), preferred_element_type=F32)


def _dot_tn(a, b):
    return lax.dot_general(a, b, (((0,), (0,)), ((), ())), preferred_element_type=F32)


def _parts(x, precise):
    return _split_bf16(x.astype(F32)) if precise else (x.astype(BF16),)


def _mm(xp, wp):
    if len(wp) == 1:
        return _dot(xp[0], wp[0])
    return _dot(xp[0], wp[0]) + _dot(xp[1], wp[0]) + _dot(xp[0], wp[1])


def _mm_nt(ap, bp):
    if len(ap) == 1:
        return _dot_nt(ap[0], bp[0])
    return _dot_nt(ap[0], bp[0]) + _dot_nt(ap[1], bp[0]) + _dot_nt(ap[0], bp[1])


def _read(refs):
    return tuple(r[...] for r in refs)


def _wparts(w):
    return tuple(w) if isinstance(w, (tuple, list)) else (w,)


def _row_tile(m, n_weight_parts=1):
    cap = 1024 // n_weight_parts
    return next(t for t in (1024, 512, 256, 128) if t <= cap and m % t == 0)


def _store_norm(x_ref, g_ref, xn_refs):
    xn = _rms_scale(x_ref[...], g_ref[...])
    for ref, part in zip(xn_refs, _parts(xn, len(xn_refs) == 2)):
        ref[...] = part


def _norm_mm_kernel(*refs, ranges, nw):
    x_ref, g_ref = refs[:2]
    w_refs = refs[2:2 + nw]
    outs = refs[2 + nw:2 + nw + len(ranges)]
    xn_refs = refs[2 + nw + len(ranges):]
    j = pl.program_id(1)

    @pl.when(j == 0)
    def _():
        _store_norm(x_ref, g_ref, xn_refs)

    acc = _mm(_read(xn_refs), _read(w_refs))
    for o_ref, (a, b) in zip(outs, ranges):
        @pl.when((j >= a) & (j < b))
        def _(o_ref=o_ref):
            o_ref[...] = acc.astype(o_ref.dtype)


def _range_map(i, j, *, a, n):
    return (i, jnp.clip(j - a, 0, n - 1))


def norm_mm(x, g, w, outs, *, tn):
    M, D = x.shape
    wp = _wparts(w)
    N = wp[0].shape[1]
    tm = _row_tile(M, len(wp))
    ranges = tuple((a, b) for _, a, b in outs)
    return pl.pallas_call(
        functools.partial(_norm_mm_kernel, ranges=ranges, nw=len(wp)),
        grid=(M // tm, N // tn),
        in_specs=[pl.BlockSpec((tm, D), lambda i, j: (i, 0)),
                  pl.BlockSpec((1, D), lambda i, j: (0, 0))]
                 + [pl.BlockSpec((D, tn), lambda i, j: (0, j))] * len(wp),
        out_specs=[pl.BlockSpec((tm, tn), functools.partial(_range_map, a=a, n=b - a))
                   for _, a, b in outs],
        out_shape=[jax.ShapeDtypeStruct((M, (b - a) * tn), dt) for dt, a, b in outs],
        scratch_shapes=[pltpu.VMEM((tm, D), BF16)] * len(wp),
        compiler_params=_cparams("parallel", "arbitrary"),
        name="norm_mm",
    )(x, g.reshape(1, D), *wp)


def _fox_in_kernel(*refs, nw, nt, n_heads, hd, tm, n_bufs):
    x_ref, g_ref = refs[:2]
    w_refs = refs[2:2 + nw]
    qkv_ref, k_ref, v_ref = refs[2 + nw + n_bufs:5 + nw + n_bufs]
    xn_refs = refs[5 + nw + n_bufs:]
    j = pl.program_id(1)

    @pl.when(j == 0)
    def _():
        _store_norm(x_ref, g_ref, xn_refs)

    acc = _mm(_read(xn_refs), _read(w_refs))
    qkv_ref[...] = acc.astype(qkv_ref.dtype)
    hpt = acc.shape[1] // hd
    for dst, base in ((k_ref, nt), (v_ref, 2 * nt)):
        for jj in range(nt):
            @pl.when(j == base + jj)
            def _(dst=dst, jj=jj):
                for hh in range(hpt):
                    dst[pl.ds(jj * hpt + hh, tm, stride=n_heads), :] = acc[:, hh * hd:(hh + 1) * hd]


def fox_in_proj(x, g, w, *, n_heads, hd, tn, bufs=None, slot=0, n_slots=1):
    M, D = x.shape
    wp = _wparts(w)
    nw = len(wp)
    tm = min(512, M)
    nt = D // tn
    n_bufs = 0 if bufs is None else 2
    blk0 = slot * (M // tm)
    head_spec = pl.BlockSpec((tm * n_heads, hd), lambda i, j: (blk0 + i, 0))
    head_shape = jax.ShapeDtypeStruct((n_slots * M * n_heads, hd), F32)
    return pl.pallas_call(
        functools.partial(_fox_in_kernel, nw=nw, nt=nt, n_heads=n_heads, hd=hd, tm=tm, n_bufs=n_bufs),
        grid=(M // tm, 3 * nt),
        in_specs=[pl.BlockSpec((tm, D), lambda i, j: (i, 0)),
                  pl.BlockSpec((1, D), lambda i, j: (0, 0))]
                 + [pl.BlockSpec((D, tn), lambda i, j: (0, j))] * nw
                 + [pl.BlockSpec(memory_space=pl.ANY)] * n_bufs,
        out_specs=[pl.BlockSpec((tm, tn), lambda i, j: (i, j)), head_spec, head_spec],
        out_shape=[jax.ShapeDtypeStruct((M, 3 * D), F32 if nw == 2 else BF16), head_shape, head_shape],
        scratch_shapes=[pltpu.VMEM((tm, D), BF16)] * nw,
        input_output_aliases={} if bufs is None else {2 + nw: 1, 3 + nw: 2},
        compiler_params=_cparams("parallel", "arbitrary"),
        name="fox_in_proj",
    )(x, g.reshape(1, D), *wp, *(bufs or ()))


def _glu_up_kernel(*refs, tiles_per_expert, nw):
    x_ref, g_ref = refs[:2]
    wg_refs = refs[2:2 + nw]
    wu_refs = refs[2 + nw:2 + 2 * nw]
    rest = refs[2 + 2 * nw:]
    comb_ref = rest[0] if tiles_per_expert else None
    o_ref = rest[1] if tiles_per_expert else rest[0]
    xn_refs = rest[2:] if tiles_per_expert else rest[1:]
    j = pl.program_id(1)

    @pl.when(j == 0)
    def _():
        _store_norm(x_ref, g_ref, xn_refs)

    xn = _read(xn_refs)
    h = _silu(_mm(xn, _read(wg_refs))) * _mm(xn, _read(wu_refs))
    if tiles_per_expert:
        comb = comb_ref[...]
        lane = lax.broadcasted_iota(jnp.int32, comb.shape, 1)
        col = jnp.sum(jnp.where(lane == j // tiles_per_expert, comb, 0.0), axis=1, keepdims=True)
        h = h * col
    o_ref[...] = h.astype(o_ref.dtype)


def glu_up(x, g, w_gate, w_up, comb=None, *, tn):
    M, D = x.shape
    wg, wu = _wparts(w_gate), _wparts(w_up)
    nw = len(wg)
    tm = _row_tile(M, nw)
    if comb is None:
        n_out, tpe = wg[0].shape[1], 0
        w_spec = pl.BlockSpec((D, tn), lambda i, j: (0, j))
        extra_in, extra_specs = (), []
    else:
        E, _, F = wg[0].shape
        n_out, tpe = E * F, F // tn
        w_spec = pl.BlockSpec((None, D, tn), lambda i, j: (j // tpe, 0, j % tpe))
        extra_in = (comb,)
        extra_specs = [pl.BlockSpec((tm, LANES), lambda i, j: (i, 0))]
    return pl.pallas_call(
        functools.partial(_glu_up_kernel, tiles_per_expert=tpe, nw=nw),
        grid=(M // tm, n_out // tn),
        in_specs=[pl.BlockSpec((tm, D), lambda i, j: (i, 0)),
                  pl.BlockSpec((1, D), lambda i, j: (0, 0))] + [w_spec] * (2 * nw) + extra_specs,
        out_specs=pl.BlockSpec((tm, tn), lambda i, j: (i, j)),
        out_shape=jax.ShapeDtypeStruct((M, n_out), F32 if nw == 2 else BF16),
        scratch_shapes=[pltpu.VMEM((tm, D), BF16)] * nw,
        compiler_params=_cparams("parallel", "arbitrary"),
        name="glu_up",
    )(x, g.reshape(1, D), *wg, *wu, *extra_in)


def _mm_res_kernel(*refs, nk, nw):
    x_ref = refs[0]
    w_refs = refs[1:1 + nw]
    r_ref, o_ref, acc_ref = refs[1 + nw:]
    k = pl.program_id(2)

    @pl.when(k == 0)
    def _():
        acc_ref[...] = jnp.zeros_like(acc_ref)

    acc_ref[...] += _mm(_parts(x_ref[...], nw == 2), _read(w_refs))

    @pl.when(k == nk - 1)
    def _():
        o_ref[...] = r_ref[...] + acc_ref[...]


def mm_res(x, w, res, *, tn, tk):
    M, K = x.shape
    wp = _wparts(w)
    N = wp[0].shape[1]
    tm = _row_tile(M, len(wp))
    nk = K // tk
    return pl.pallas_call(
        functools.partial(_mm_res_kernel, nk=nk, nw=len(wp)),
        grid=(M // tm, N // tn, nk),
        in_specs=[pl.BlockSpec((tm, tk), lambda i, j, k: (i, k))]
                 + [pl.BlockSpec((tk, tn), lambda i, j, k: (k, j))] * len(wp)
                 + [pl.BlockSpec((tm, tn), lambda i, j, k: (i, j))],
        out_specs=pl.BlockSpec((tm, tn), lambda i, j, k: (i, j)),
        out_shape=jax.ShapeDtypeStruct((M, N), F32),
        scratch_shapes=[pltpu.VMEM((tm, tn), F32)],
        compiler_params=_cparams("parallel", "parallel", "arbitrary"),
        name="mm_res",
    )(x, *wp, res)


def _glu_res_kernel(*refs, nw):
    y_ref = refs[0]
    wa_refs = refs[1:1 + nw]
    wb_refs = refs[1 + nw:1 + 2 * nw]
    r_ref, o_ref = refs[1 + 2 * nw:]
    y = _parts(y_ref[...], nw == 2)
    o_ref[...] = r_ref[...] + _mm(y, _read(wa_refs)) * _sigmoid(_mm(y, _read(wb_refs)))


def glu_res(y, w, res, *, tn):
    M, K = y.shape
    wp = _wparts(w)
    N = wp[0].shape[1] // 2
    nj = N // tn
    tm = _row_tile(M, len(wp))
    return pl.pallas_call(
        functools.partial(_glu_res_kernel, nw=len(wp)),
        grid=(M // tm, nj),
        in_specs=[pl.BlockSpec((tm, K), lambda i, j: (i, 0))]
                 + [pl.BlockSpec((K, tn), lambda i, j: (0, j))] * len(wp)
                 + [pl.BlockSpec((K, tn), lambda i, j: (0, j + nj))] * len(wp)
                 + [pl.BlockSpec((tm, tn), lambda i, j: (i, j))],
        out_specs=pl.BlockSpec((tm, tn), lambda i, j: (i, j)),
        out_shape=jax.ShapeDtypeStruct((M, N), F32),
        compiler_params=_cparams("parallel", "arbitrary"),
        name="glu_res",
    )(y, *wp, *wp, res)


def _ple_kernel(*refs, tn, nw):
    x_ref, g_ref = refs[:2]
    wg_refs = refs[2:2 + nw]
    p_ref = refs[2 + nw]
    wu_refs = refs[3 + nw:3 + 2 * nw]
    o_ref = refs[3 + 2 * nw]
    xn_refs = refs[4 + 2 * nw:]
    j = pl.program_id(1)

    @pl.when(j == 0)
    def _():
        _store_norm(x_ref, g_ref, xn_refs)

    gate = _sigmoid(_mm(_read(xn_refs), _read(wg_refs)))
    up = _mm(_parts(p_ref[...], nw == 2), _read(wu_refs))
    x_tile = x_ref[:, pl.ds(pl.multiple_of(j * tn, tn), tn)]
    o_ref[...] = x_tile + up * gate


def ple(x, g, w_gate, p, w_up, *, tn):
    M, D = x.shape
    P = p.shape[1]
    wg, wu = _wparts(w_gate), _wparts(w_up)
    nw = len(wg)
    tm = _row_tile(M, nw)
    return pl.pallas_call(
        functools.partial(_ple_kernel, tn=tn, nw=nw),
        grid=(M // tm, D // tn),
        in_specs=[pl.BlockSpec((tm, D), lambda i, j: (i, 0)),
                  pl.BlockSpec((1, D), lambda i, j: (0, 0))]
                 + [pl.BlockSpec((D, tn), lambda i, j: (0, j))] * nw
                 + [pl.BlockSpec((tm, P), lambda i, j: (i, 0))]
                 + [pl.BlockSpec((P, tn), lambda i, j: (0, j))] * nw,
        out_specs=pl.BlockSpec((tm, tn), lambda i, j: (i, j)),
        out_shape=jax.ShapeDtypeStruct((M, D), F32),
        scratch_shapes=[pltpu.VMEM((tm, D), BF16)] * nw,
        compiler_params=_cparams("parallel", "arbitrary"),
        name="ple",
    )(x, g.reshape(1, D), *wg, p, *wu)


def _rmsnorm_kernel(x_ref, g_ref, o_ref):
    o_ref[...] = _rms_scale(x_ref[...], g_ref[...]).astype(o_ref.dtype)


def rmsnorm(x, g):
    M, D = x.shape
    tm = _row_tile(M)
    return pl.pallas_call(
        _rmsnorm_kernel,
        grid=(M // tm,),
        in_specs=[pl.BlockSpec((tm, D), lambda i: (i, 0)),
                  pl.BlockSpec((1, D), lambda i: (0, 0))],
        out_specs=pl.BlockSpec((tm, D), lambda i: (i, 0)),
        out_shape=jax.ShapeDtypeStruct((M, D), F32),
        compiler_params=_cparams("parallel"),
        name="rmsnorm",
    )(x, g.reshape(1, D))


def _row_rstd_kernel(x_ref, o_ref):
    x = x_ref[...]
    rstd = lax.rsqrt(jnp.mean(x * x, axis=-1, keepdims=True) + EPS)
    o_ref[...] = jnp.broadcast_to(rstd, o_ref.shape)


def row_rstd(x):
    M, D = x.shape
    tm = _row_tile(M)
    return pl.pallas_call(
        _row_rstd_kernel,
        grid=(M // tm,),
        in_specs=[pl.BlockSpec((tm, D), lambda i: (i, 0))],
        out_specs=pl.BlockSpec((tm, LANES), lambda i: (i, 0)),
        out_shape=jax.ShapeDtypeStruct((M, LANES), F32),
        compiler_params=_cparams("parallel"),
        name="row_rstd",
    )(x)


def _small_proj_kernel(x_ref, g_ref, whi_ref, wlo_ref, aux_ref, o_ref, *, mode, nh):
    xn = _rms_scale(x_ref[...], g_ref[...])
    acc = _mm(_parts(xn, True), (whi_ref[...], wlo_ref[...]))
    lane = lax.broadcasted_iota(jnp.int32, acc.shape, 1)
    if mode == "logf":
        o_ref[...] = -_softplus(-(acc + aux_ref[0:1, :]))
    elif mode == "gdn":
        beta = _sigmoid(acc)
        gdec = -jnp.exp(aux_ref[1:2, :]) * _softplus(acc + aux_ref[0:1, :])
        o_ref[...] = jnp.where(lane < nh, beta, gdec)
    else:
        logits = jnp.where(lane < N_EXPERTS, acc, NEG_INF)
        v1 = jnp.max(logits, axis=1, keepdims=True)
        i1 = jnp.min(jnp.where(logits == v1, lane, LANES), axis=1, keepdims=True)
        rest = jnp.where(lane == i1, NEG_INF, logits)
        v2 = jnp.max(rest, axis=1, keepdims=True)
        i2 = jnp.min(jnp.where(rest == v2, lane, LANES), axis=1, keepdims=True)
        e2 = jnp.exp(v2 - v1)
        g1 = 1.0 / (1.0 + e2)
        o_ref[...] = jnp.where(lane == i1, g1, 0.0) + jnp.where(lane == i2, e2 * g1, 0.0)


def small_proj(x, g, w_parts, aux, mode, *, nh=0):
    M, D = x.shape
    tm = _row_tile(M)
    return pl.pallas_call(
        functools.partial(_small_proj_kernel, mode=mode, nh=nh),
        grid=(M // tm,),
        in_specs=[pl.BlockSpec((tm, D), lambda i: (i, 0)),
                  pl.BlockSpec((1, D), lambda i: (0, 0)),
                  pl.BlockSpec((D, LANES), lambda i: (0, 0)),
                  pl.BlockSpec((D, LANES), lambda i: (0, 0)),
                  pl.BlockSpec((SUBLANES, LANES), lambda i: (0, 0))],
        out_specs=pl.BlockSpec((tm, LANES), lambda i: (i, 0)),
        out_shape=jax.ShapeDtypeStruct((M, LANES), F32),
        compiler_params=_cparams("parallel"),
        name="small_proj_" + mode,
    )(x, g.reshape(1, D), *w_parts, aux)


def _pad_cols(w):
    return _split_weight(jnp.pad(w.astype(F32), ((0, 0), (0, LANES - w.shape[1]))))


def _aux_rows(*rows):
    out = [jnp.pad(vals.astype(F32), (start, LANES - start - vals.shape[0])) for start, vals in rows]
    out += [jnp.zeros((LANES,), F32)] * (SUBLANES - len(out))
    return jnp.stack(out)


def _lane_cumsum_kernel(x_ref, o_ref):
    x = x_ref[...]
    n = x.shape[1]
    lane = lax.broadcasted_iota(jnp.int32, x.shape, 1)
    shift = 1
    while shift < n:
        x = x + jnp.where(lane >= shift, pltpu.roll(x, shift, 1), 0.0)
        shift *= 2
    o_ref[...] = x


def lane_cumsum(x):
    R, n = x.shape
    rb = SUBLANES if R % SUBLANES == 0 else R
    return pl.pallas_call(
        _lane_cumsum_kernel,
        grid=(R // rb,),
        in_specs=[pl.BlockSpec((rb, n), lambda i: (i, 0))],
        out_specs=pl.BlockSpec((rb, n), lambda i: (i, 0)),
        out_shape=jax.ShapeDtypeStruct((R, n), F32),
        compiler_params=_cparams("parallel"),
        name="lane_cumsum",
    )(x)


def _fox_prompt_kernel(q_ref, k_ref, v_ref, fk_ref, o_ref, *, tq, nhb, hd, scale):
    qi = pl.program_id(2)

    def step(ki, carry, diagonal):
        rows = pl.ds(pl.multiple_of(ki * tq, tq), tq)
        out = []
        for h in range(nhb):
            m, l, acc = carry[h]
            cols = slice(h * hd, (h + 1) * hd)
            s = (_dot_nt(q_ref[:, cols], k_ref[rows, cols]) * (scale * LOG2E)
                 - fk_ref[h, pl.ds(ki, 1), :] * LOG2E)
            if diagonal:
                r = lax.broadcasted_iota(jnp.int32, s.shape, 0)
                c = lax.broadcasted_iota(jnp.int32, s.shape, 1)
                s = jnp.where(r >= c, s, NEG_INF)
            m_new = jnp.maximum(m, jnp.max(s, axis=1, keepdims=True))
            p = jnp.exp2(s - m_new)
            alpha = jnp.exp2(m - m_new)
            l = alpha * l + jnp.sum(p, axis=1, keepdims=True)
            acc = alpha * acc + _dot(p.astype(BF16), v_ref[rows, cols])
            out.append((m_new, l, acc))
        return tuple(out)

    init = tuple((jnp.full((tq, 1), NEG_INF, F32), jnp.zeros((tq, 1), F32), jnp.zeros((tq, hd), F32))
                 for _ in range(nhb))
    carry = lax.fori_loop(0, qi, lambda ki, c: step(ki, c, False), init)
    carry = step(qi, carry, True)
    o_ref[...] = jnp.concatenate([acc / l for _, l, acc in carry], axis=1).astype(o_ref.dtype)


def fox_attention_prompt(qkv, fk, *, B, L, H, hd, tq, nhb=2):
    nq = L // tq
    nhg = H // nhb
    return pl.pallas_call(
        functools.partial(_fox_prompt_kernel, tq=tq, nhb=nhb, hd=hd, scale=hd ** -0.5),
        grid=(B, nhg, nq),
        in_specs=[pl.BlockSpec((tq, nhb * hd), lambda b, g, i: (b * nq + i, g)),
                  pl.BlockSpec((L, nhb * hd), lambda b, g, i: (b, nhg + g)),
                  pl.BlockSpec((L, nhb * hd), lambda b, g, i: (b, 2 * nhg + g)),
                  pl.BlockSpec((nhb, nq, tq), lambda b, g, i: (b * nhg + g, 0, 0))],
        out_specs=pl.BlockSpec((tq, nhb * hd), lambda b, g, i: (b * nq + i, g)),
        out_shape=jax.ShapeDtypeStruct((B * L, H * hd), BF16),
        compiler_params=_cparams("parallel", "parallel", "arbitrary"),
        name="fox_attention_prompt",
    )(qkv, qkv, qkv, fk)


def _fox_cached_kernel(q_ref, kn_ref, vn_ref, kp_ref, vp_ref, f_ref, o_ref, *, P, nhb, hd, scale, precise):
    n = q_ref.shape[0]
    r = lax.broadcasted_iota(jnp.int32, (n, n), 0)
    c = lax.broadcasted_iota(jnp.int32, (n, n), 1)
    outs = []
    for h in range(nhb):
        cols = slice(h * hd, (h + 1) * hd)
        q = _parts(q_ref[:, cols], precise)
        f = f_ref[h]
        s_past = _mm_nt(q, _parts(kp_ref[0, :, cols], precise)) * scale - f[:, :P]
        s_new = _mm_nt(q, _parts(kn_ref[:, cols], precise)) * scale - f[:, P:P + n]
        s_new = jnp.where(r >= c, s_new, NEG_INF)
        m = jnp.maximum(jnp.max(s_past, axis=1, keepdims=True), jnp.max(s_new, axis=1, keepdims=True))
        p_past = jnp.exp(s_past - m)
        p_new = jnp.exp(s_new - m)
        l = jnp.sum(p_past, axis=1, keepdims=True) + jnp.sum(p_new, axis=1, keepdims=True)
        acc = (_mm(_parts(p_past, precise), _parts(vp_ref[0, :, cols], precise))
               + _mm(_parts(p_new, precise), _parts(vn_ref[:, cols], precise)))
        outs.append(acc / l)
    o_ref[...] = jnp.concatenate(outs, axis=1).astype(o_ref.dtype)


def fox_attention_cached(qkv, k_past, v_past, f, *, seq0, k_lane0, v_lane0, B, n, P, H, hd):
    precise = qkv.dtype == F32
    nhb = 4 if H % 4 == 0 else 1
    nhg = H // nhb
    w = nhb * hd
    return pl.pallas_call(
        functools.partial(_fox_cached_kernel, P=P, nhb=nhb, hd=hd, scale=hd ** -0.5, precise=precise),
        grid=(B, nhg),
        in_specs=[pl.BlockSpec((n, w), lambda b, g: (b, g)),
                  pl.BlockSpec((n, w), lambda b, g: (b, nhg + g)),
                  pl.BlockSpec((n, w), lambda b, g: (b, 2 * nhg + g)),
                  pl.BlockSpec((1, P, w), lambda b, g: (seq0 + b, 0, k_lane0 // nhb + g)),
                  pl.BlockSpec((1, P, w), lambda b, g: (seq0 + b, 0, v_lane0 // nhb + g)),
                  pl.BlockSpec((nhb, 1, f.shape[2]), lambda b, g: (b * nhg + g, 0, 0))],
        out_specs=pl.BlockSpec((n, w), lambda b, g: (b, g)),
        out_shape=jax.ShapeDtypeStruct((B * n, H * hd), qkv.dtype),
        compiler_params=_cparams("parallel", "arbitrary"),
        name="fox_attention_cached",
    )(qkv, qkv, qkv, k_past, v_past, f)


def _gelu_tanh(x):
    return 0.5 * x * (1.0 + jnp.tanh(math.sqrt(2.0 / math.pi) * (x + 0.044715 * (x * x * x))))


def _s5_kernel(*refs, nb, c, ns, ncm, snap_chunk):
    x_ref, rstd_ref, g_ref, bhi_ref, blo_ref = refs[:5]
    cm_refs = refs[5:5 + ncm]
    (ar_ref, ai_ref, d_ref, h0r_ref, h0i_ref, y_ref, hr_out, hi_out, hr_snap, hi_snap,
     u_sc, y_sc, bu_sc, st_sc, hr_sc, hi_sc) = refs[5 + ncm:]
    tc = pl.program_id(1)

    @pl.when(tc == 0)
    def _():
        hr_sc[...] = h0r_ref[...]
        hi_sc[...] = h0i_ref[...]

    g = g_ref[...]
    for b in range(nb):
        u_sc[pl.ds(b, c, stride=nb), :] = x_ref[b] * rstd_ref[b] * g
    u = u_sc[...]
    bu_sc[...] = _mm(_parts(u, True), (bhi_ref[0], blo_ref[0]))
    ar = jnp.broadcast_to(ar_ref[0], (nb, ns))
    ai = jnp.broadcast_to(ai_ref[0], (nb, ns))

    def step(t, carry):
        hr, hi = carry
        rows = pl.ds(pl.multiple_of(t * nb, nb), nb)
        nhr = ar * hr - ai * hi + bu_sc[rows, 0:ns]
        nhi = ar * hi + ai * hr + bu_sc[rows, ns:2 * ns]
        st_sc[rows, 0:ns] = nhr
        st_sc[rows, ns:2 * ns] = nhi
        return nhr, nhi

    hr, hi = lax.fori_loop(0, c, step, (hr_sc[...], hi_sc[...]))
    hr_sc[...] = hr
    hi_sc[...] = hi
    y = _mm(_parts(st_sc[...], ncm == 2), tuple(r[0] for r in cm_refs)) + d_ref[...] * u
    y_sc[...] = _gelu_tanh(y)
    for b in range(nb):
        y_ref[b] = y_sc[pl.ds(b, c, stride=nb), :].astype(y_ref.dtype)

    @pl.when(tc == snap_chunk)
    def _():
        hr_snap[...] = hr
        hi_snap[...] = hi

    @pl.when(tc == pl.num_programs(1) - 1)
    def _():
        hr_out[...] = hr
        hi_out[...] = hi


def s5_scan(x, rstd, g, h0r, h0i, consts, *, c, precise, snap_steps=None):
    bhi, blo, cm_hi, cm_lo, ar, ai, d = consts
    cms = (cm_hi, cm_lo) if precise else (cm_hi,)
    nb, L, D = x.shape
    nblk = D // LANES
    ns = ar.shape[2]
    ntc = L // c
    snap_chunk = ntc - 1 if snap_steps is None else snap_steps // c - 1
    blk = lambda g, t: (g, 0, 0)
    state_spec = pl.BlockSpec((nb, ns), lambda g, t: (0, g))
    state_shape = jax.ShapeDtypeStruct(h0r.shape, F32)
    return pl.pallas_call(
        functools.partial(_s5_kernel, nb=nb, c=c, ns=ns, ncm=len(cms), snap_chunk=snap_chunk),
        grid=(nblk, ntc),
        in_specs=[pl.BlockSpec((nb, c, LANES), lambda g, t: (0, t, g)),
                  pl.BlockSpec((nb, c, LANES), lambda g, t: (0, t, 0)),
                  pl.BlockSpec((1, LANES), lambda g, t: (0, g)),
                  pl.BlockSpec((1, LANES, 2 * ns), blk),
                  pl.BlockSpec((1, LANES, 2 * ns), blk)]
                 + [pl.BlockSpec((1, 2 * ns, LANES), blk)] * len(cms)
                 + [pl.BlockSpec((1, 1, ns), blk),
                    pl.BlockSpec((1, 1, ns), blk),
                    pl.BlockSpec((1, LANES), lambda g, t: (0, g)),
                    state_spec, state_spec],
        out_specs=[pl.BlockSpec((nb, c, LANES), lambda g, t: (0, t, g))] + [state_spec] * 4,
        out_shape=[jax.ShapeDtypeStruct((nb, L, D), F32 if precise else BF16)] + [state_shape] * 4,
        scratch_shapes=[pltpu.VMEM((c * nb, LANES), F32),
                        pltpu.VMEM((c * nb, LANES), F32),
                        pltpu.VMEM((c * nb, 2 * ns), F32),
                        pltpu.VMEM((c * nb, 2 * ns), F32),
                        pltpu.VMEM((nb, ns), F32),
                        pltpu.VMEM((nb, ns), F32)],
        compiler_params=_cparams("parallel", "arbitrary"),
        name="s5_scan",
    )(x, rstd, g.reshape(1, D), bhi, blo, *cms, ar, ai, d, h0r, h0i)


def s5_constants(a_re, a_im, b_re, b_im, c_re, c_im, d_skip, log_dt):
    G, N = a_re.shape
    gc = b_re.shape[2]
    gpb = LANES // gc
    nblk = G // gpb
    a = lax.complex(a_re.astype(F32), a_im.astype(F32))
    dt = jnp.exp(log_dt.astype(F32))[:, None]
    a_bar = jnp.exp(a * dt)
    b_bar = ((a_bar - 1.0) / a)[..., None] * lax.complex(b_re.astype(F32), b_im.astype(F32))
    eye = jnp.eye(gpb, dtype=F32)

    def in_mat(b):
        b = b.reshape(nblk, gpb, N, gc)
        return jnp.einsum("kgnc,gh->kgchn", b, eye).reshape(nblk, gpb * gc, gpb * N)

    def out_mat(cc):
        cc = cc.reshape(nblk, gpb, gc, N)
        return jnp.einsum("kgcn,gh->kgnhc", cc, eye).reshape(nblk, gpb * N, gpb * gc)

    bhi, blo = _split_weight(jnp.concatenate([in_mat(b_bar.real), in_mat(b_bar.imag)], axis=2))
    cm_hi, cm_lo = _split_weight(jnp.concatenate([out_mat(c_re.astype(F32)), -out_mat(c_im.astype(F32))], axis=1))
    ar = a_bar.real.reshape(nblk, 1, gpb * N)
    ai = a_bar.imag.reshape(nblk, 1, gpb * N)
    return bhi, blo, cm_hi, cm_lo, ar, ai, d_skip.astype(F32).reshape(1, G * gc)


def _gdn_conv_kernel(x_ref, halo_ref, buf_ref, w_ref, o_ref, ext_sc, *, tl, n_valid, tiles_per_seq, n_norm_tiles):
    r = pl.program_id(0)
    j = pl.program_id(1)
    first = (r % tiles_per_seq) == 0
    ext_sc[0:SUBLANES, :] = jnp.where(first, buf_ref[0], halo_ref[...])
    ext_sc[SUBLANES:SUBLANES + n_valid, :] = x_ref[...]
    w = w_ref[...]
    y = x_ref[...] * w[3:4, :]
    for s in (1, 2, 3):
        y = y + ext_sc[SUBLANES - s:SUBLANES - s + n_valid, :] * w[3 - s:4 - s, :]
    y = _silu(y)

    def store(val):
        if n_valid < tl:
            val = jnp.concatenate([val, jnp.zeros((tl - n_valid, val.shape[1]), F32)], axis=0)
        o_ref[...] = val.astype(o_ref.dtype)

    @pl.when(j < n_norm_tiles)
    def _():
        segs = []
        for a in range(0, y.shape[1], LANES):
            seg = y[:, a:a + LANES]
            segs.append(seg * lax.rsqrt(jnp.sum(seg * seg, axis=1, keepdims=True) + EPS))
        store(jnp.concatenate(segs, axis=1))

    @pl.when(j >= n_norm_tiles)
    def _():
        store(y)


def gdn_conv(x, buf8, w8, *, n_seq, seq_len, tl_in, tl_out, n_norm_cols, tc):
    C = x.shape[1]
    tiles_per_seq = seq_len // tl_in
    n_tiles = n_seq * tiles_per_seq
    hpt = tl_in // SUBLANES
    return pl.pallas_call(
        functools.partial(_gdn_conv_kernel, tl=tl_out, n_valid=tl_in, tiles_per_seq=tiles_per_seq,
                          n_norm_tiles=n_norm_cols // tc),
        grid=(n_tiles, C // tc),
        in_specs=[pl.BlockSpec((tl_in, tc), lambda r, j: (r, j)),
                  pl.BlockSpec((SUBLANES, tc), lambda r, j: (jnp.maximum(r * hpt - 1, 0), j)),
                  pl.BlockSpec((1, SUBLANES, tc), lambda r, j: (r // tiles_per_seq, 0, j)),
                  pl.BlockSpec((SUBLANES, tc), lambda r, j: (0, j))],
        out_specs=pl.BlockSpec((tl_out, tc), lambda r, j: (r, j)),
        out_shape=jax.ShapeDtypeStruct((n_tiles * tl_out, C), BF16),
        scratch_shapes=[pltpu.VMEM((SUBLANES + tl_in, tc), F32)],
        compiler_params=_cparams("parallel", "arbitrary"),
        name="gdn_conv",
    )(x, x, buf8, w8)


def _gdn_prep_kernel(qk_ref, gt_ref, ti_ref, a_ref, gc_ref, n_sc, nt_sc, x_sc, *, C, cps, n_qk, rep, hd, scale):
    nh = n_qk * rep
    n_sys = cps * nh
    pitch = C + SUBLANES
    r = lax.broadcasted_iota(jnp.int32, (C, C), 0)
    c = lax.broadcasted_iota(jnp.int32, (C, C), 1)
    tril = (r >= c).astype(BF16)
    lane_pad = jnp.zeros((C, LANES - C), F32)
    for cc in range(cps):
        rows = slice(cc * C, (cc + 1) * C)
        gates = gt_ref[rows, :]
        lane = lax.broadcasted_iota(jnp.int32, gates.shape, 1)
        g = jnp.where((lane >= nh) & (lane < 2 * nh), gates, 0.0)
        g_hi = g.astype(BF16)
        g_r1 = g - g_hi.astype(F32)
        g_mid = g_r1.astype(BF16)
        g_lo = (g_r1 - g_mid.astype(F32)).astype(BF16)
        gcum = _dot(tril, g_hi) + _dot(tril, g_mid) + _dot(tril, g_lo)
        gc_ref[rows, :] = gcum
        gcum_t = gcum.T
        for hq in range(n_qk):
            q = qk_ref[rows, hq * hd:(hq + 1) * hd]
            k = qk_ref[rows, (n_qk + hq) * hd:(n_qk + hq + 1) * hd]
            kk = _dot_nt(k, k)
            qk = _dot_nt(q, k) * scale
            for rr in range(rep):
                hv = hq * rep + rr
                sys = cc * nh + hv
                diff = gcum[:, nh + hv:nh + hv + 1] - gcum_t[nh + hv:nh + hv + 1, :]
                decay = jnp.exp(jnp.where(r >= c, diff, NEG_INF))
                n_val = jnp.where(r > c, gates[:, hv:hv + 1] * kk * decay, 0.0)
                n_sc[sys * pitch:sys * pitch + C, :] = jnp.concatenate([n_val, lane_pad], axis=1)
                a_ref[cc, hv] = (qk * decay).astype(a_ref.dtype)
    for i in range(C):
        slab = n_sc[pl.ds(i, n_sys, stride=pitch), :]
        nt_sc[i * C:(i + 1) * C, :] = slab.T[:C, :]
    _tri_solve(nt_sc, x_sc, C)
    row_pad = jnp.zeros((LANES - C, n_sys), F32)
    for i in range(C):
        cols = jnp.concatenate([x_sc[i * C:(i + 1) * C, :], row_pad], axis=0)
        n_sc[pl.ds(i, n_sys, stride=pitch), :] = cols.T
    for cc in range(cps):
        for hv in range(nh):
            sys = cc * nh + hv
            ti_ref[cc, hv] = n_sc[sys * pitch:sys * pitch + C, :C].astype(ti_ref.dtype)


def gdn_prep(qkv_c, gates, *, C, n_qk, rep, hd):
    rows = qkv_c.shape[0]
    nch = rows // C
    nh = n_qk * rep
    cps = LANES // nh
    assert nch % cps == 0 and cps * nh == LANES
    return pl.pallas_call(
        functools.partial(_gdn_prep_kernel, C=C, cps=cps, n_qk=n_qk, rep=rep, hd=hd, scale=hd ** -0.5),
        grid=(nch // cps,),
        in_specs=[pl.BlockSpec((cps * C, 2 * n_qk * hd), lambda n: (n, 0)),
                  pl.BlockSpec((cps * C, LANES), lambda n: (n, 0))],
        out_specs=[pl.BlockSpec((cps, nh, C, C), lambda n: (n, 0, 0, 0)),
                   pl.BlockSpec((cps, nh, C, C), lambda n: (n, 0, 0, 0)),
                   pl.BlockSpec((cps * C, LANES), lambda n: (n, 0))],
        out_shape=[jax.ShapeDtypeStruct((nch, nh, C, C), BF16),
                   jax.ShapeDtypeStruct((nch, nh, C, C), BF16),
                   jax.ShapeDtypeStruct((rows, LANES), F32)],
        scratch_shapes=[pltpu.VMEM((LANES * (C + SUBLANES), LANES), F32),
                        pltpu.VMEM((C * C, LANES), F32),
                        pltpu.VMEM((C * C, LANES), F32)],
        compiler_params=_cparams("parallel"),
        name="gdn_prep",
    )(qkv_c, gates)


def _tri_solve(n_ref, x_ref, C):
    nblk = C // SUBLANES
    lanes = n_ref.shape[1]
    sub = lax.broadcasted_iota(jnp.int32, (SUBLANES, lanes), 0)
    x_ref[...] = jnp.zeros(x_ref.shape, F32)
    for ib in range(nblk):
        def row(r, _, ib=ib):
            base = (ib * SUBLANES + r) * C
            acc = [jnp.zeros((SUBLANES, lanes), F32) for _ in range(ib)]
            acc.append(jnp.where(sub == r, 1.0, 0.0).astype(F32))
            for jb in range(ib + 1):
                for j in range(jb * SUBLANES, (jb + 1) * SUBLANES):
                    nb = jnp.broadcast_to(n_ref[pl.ds(base + j, 1), :], (SUBLANES, lanes))
                    for cb in range(jb + 1):
                        acc[cb] = acc[cb] - nb * x_ref[j * C + cb * SUBLANES:j * C + (cb + 1) * SUBLANES, :]
            for cb in range(ib + 1):
                x_ref[pl.ds(pl.multiple_of(base + cb * SUBLANES, SUBLANES), SUBLANES), :] = acc[cb]
            return 0

        lax.fori_loop(0, SUBLANES, row, 0)


def _gdn_scan_kernel(q_ref, k_ref, v_ref, z_ref, gt_ref, gc_ref, ti_ref, a_ref, s0_ref, ng_ref,
                     o_ref, s_out, s_sc, *, C, hg, rep, hd, nh, n_valid, scale):
    n = pl.program_id(2)
    hgi = pl.program_id(1)

    @pl.when(n == 0)
    def _():
        s_sc[...] = s0_ref[0]

    shift = (LANES - hgi * hg) % LANES
    gates = pltpu.roll(gt_ref[...], shift, 1)
    gcum = pltpu.roll(gc_ref[...], shift, 1)
    heads = range(hg)
    s_old = [s_sc[hh] for hh in heads]
    s_bf = [s.astype(BF16) for s in s_old]
    beta = [gates[:, hh:hh + 1] for hh in heads]
    gc = [gcum[:, nh + hh:nh + hh + 1] for hh in heads]
    eg = [jnp.exp(g) for g in gc]
    g_tot = [g[C - 1:C, :] for g in gc]
    k = [k_ref[:, (hh // rep) * hd:(hh // rep + 1) * hd].astype(F32) for hh in heads]
    q = [q_ref[:, (hh // rep) * hd:(hh // rep + 1) * hd].astype(F32) for hh in heads]
    v = [v_ref[:, hh * hd:(hh + 1) * hd].astype(F32) for hh in heads]
    uw = [_dot(ti_ref[0, hh].astype(BF16),
               jnp.concatenate([v[hh] * beta[hh], k[hh] * (beta[hh] * eg[hh])], axis=1).astype(BF16))
          for hh in heads]
    q_dec = [(q[hh] * (scale * eg[hh])).astype(BF16) for hh in heads]
    k_dec = [(k[hh] * jnp.exp(g_tot[hh] - gc[hh])).astype(BF16) for hh in heads]
    v_new = [(uw[hh][:, :hd] - _dot(uw[hh][:, hd:].astype(BF16), s_bf[hh])).astype(BF16) for hh in heads]
    o = [_dot(q_dec[hh], s_bf[hh]) + _dot(a_ref[0, hh], v_new[hh]) for hh in heads]
    s_new = [s_old[hh] * jnp.exp(g_tot[hh]) + _dot_tn(k_dec[hh], v_new[hh]) for hh in heads]
    for hh in heads:
        s_sc[hh] = s_new[hh]
    ng = ng_ref[...]
    outs = [_rms_scale(o[hh][:n_valid], ng) * _silu(z_ref[:, hh * hd:(hh + 1) * hd].astype(F32)) for hh in heads]
    o_ref[...] = jnp.concatenate(outs, axis=1).astype(o_ref.dtype)

    @pl.when(n == pl.num_programs(2) - 1)
    def _():
        s_out[0] = s_sc[...]


def gdn_scan(qkv_c, z, gates, gcum, tinv, amat, s0, norm_g, *, n_seq, nc, C, n_valid, n_qk, rep, hd, hg):
    nh = n_qk * rep
    nhg = nh // hg
    qw = (hg // rep) * hd
    k_off = n_qk * hd // qw
    v_off = 2 * n_qk * hd // (hg * hd)
    chunk = lambda s, g, n: s * nc + n
    return pl.pallas_call(
        functools.partial(_gdn_scan_kernel, C=C, hg=hg, rep=rep, hd=hd, nh=nh, n_valid=n_valid,
                          scale=hd ** -0.5),
        grid=(n_seq, nhg, nc),
        in_specs=[pl.BlockSpec((C, qw), lambda s, g, n: (chunk(s, g, n), g)),
                  pl.BlockSpec((C, qw), lambda s, g, n: (chunk(s, g, n), k_off + g)),
                  pl.BlockSpec((C, hg * hd), lambda s, g, n: (chunk(s, g, n), v_off + g)),
                  pl.BlockSpec((n_valid, hg * hd), lambda s, g, n: (chunk(s, g, n), g)),
                  pl.BlockSpec((C, LANES), lambda s, g, n: (chunk(s, g, n), 0)),
                  pl.BlockSpec((C, LANES), lambda s, g, n: (chunk(s, g, n), 0)),
                  pl.BlockSpec((1, hg, C, C), lambda s, g, n: (chunk(s, g, n), g, 0, 0)),
                  pl.BlockSpec((1, hg, C, C), lambda s, g, n: (chunk(s, g, n), g, 0, 0)),
                  pl.BlockSpec((1, hg, hd, hd), lambda s, g, n: (s, g, 0, 0)),
                  pl.BlockSpec((1, hd), lambda s, g, n: (0, 0))],
        out_specs=[pl.BlockSpec((n_valid, hg * hd), lambda s, g, n: (chunk(s, g, n), g)),
                   pl.BlockSpec((1, hg, hd, hd), lambda s, g, n: (s, g, 0, 0))],
        out_shape=[jax.ShapeDtypeStruct((n_seq * nc * n_valid, nh * hd), BF16),
                   jax.ShapeDtypeStruct((n_seq, nh, hd, hd), F32)],
        scratch_shapes=[pltpu.VMEM((hg, hd, hd), F32)],
        compiler_params=_cparams("parallel", "parallel", "arbitrary"),
        name="gdn_scan",
    )(qkv_c, qkv_c, qkv_c, z, gates, gcum, tinv, amat, s0, norm_g.reshape(1, hd))


def _trunk(x, p, fox_hist, s5_h0, gdn_state, W, *, n_seq, seq_len, steps, precise, snap_steps=None,
           fox_kv=None):
    M, D = x.shape
    H, hd = W["fox_heads"]
    n_qk, rep, gd = W["gdn_heads"]
    n_vh = n_qk * rep
    key_dim, val_dim = n_qk * gd, n_vh * gd
    qkv_dim = 2 * key_dim + val_dim
    cw = W["gdn_conv_w"].shape[1]
    tn = min(512, D)
    outs = {} if fox_kv is None else {"fox_kv": fox_kv}

    def wsel(name, idx, pr):
        hi = W[name][idx]
        return (hi, W[name + "_lo"][idx]) if pr else hi

    def mixer(i, x):
        j = i // 3
        pm = precise
        g_mix = W["norm_mix"][i]
        if i % 3 == 0:
            qkv, k_heads, v_heads = fox_in_proj(x, g_mix, wsel("fox_w_qkv", j, pm), n_heads=H, hd=hd, tn=tn,
                                                bufs=outs.get("fox_kv"), slot=j, n_slots=W["n_fox"])
            outs["fox_kv"] = (k_heads, v_heads)
            outs[("fox_qkv", j)] = qkv
            logf = small_proj(x, g_mix, W["fox_w_f"][j], W["fox_aux"][j], "logf")[:, :H]
            lf = jnp.swapaxes(logf.reshape(n_seq, seq_len, H), 1, 2)
            if j not in fox_hist:
                tq = next(t for t in (512, 384, 256, 128, seq_len) if seq_len % t == 0)
                fk = lane_cumsum(lf.reshape(n_seq * H, seq_len)).reshape(n_seq * H, seq_len // tq, tq)
                o = fox_attention_prompt(qkv, fk, B=n_seq, L=seq_len, H=H, hd=hd, tq=tq)
            else:
                k_past, v_past, k_lane0, v_lane0, seq0, lf_past, P = fox_hist[j]
                fpad = -(-(P + seq_len) // LANES) * LANES
                lf_all = jnp.concatenate([jnp.swapaxes(lf_past.astype(F32), 1, 2), lf], axis=2)
                lf_all = jnp.pad(lf_all, ((0, 0), (0, 0), (0, fpad - (P + seq_len))))
                f = lane_cumsum(lf_all.reshape(n_seq * H, fpad)).reshape(n_seq * H, 1, fpad)
                o = fox_attention_cached(qkv, k_past, v_past, f, seq0=seq0, k_lane0=k_lane0, v_lane0=v_lane0,
                                         B=n_seq, n=seq_len, P=P, H=H, hd=hd)
            outs[("fox", j)] = logf
            return mm_res(o, wsel("fox_w_out", j, pm), x, tn=tn, tk=D)
        if i % 3 == 1:
            rstd = row_rstd(x).reshape(n_seq, seq_len, LANES)
            y, hr, hi, hr_snap, hi_snap = s5_scan(x.reshape(n_seq, seq_len, D), rstd, g_mix,
                                                  s5_h0[j][0], s5_h0[j][1], W["s5_consts"][j],
                                                  c=min(CHUNK, seq_len), precise=pm, snap_steps=snap_steps)
            outs[("s5", j)] = (hr, hi)
            outs[("s5_snap", j)] = (hr_snap, hi_snap)
            return glu_res(y.reshape(M, D), wsel("s5_w_glu", j, pm), x, tn=tn)
        s0, conv_rows = gdn_state[j]
        nq = qkv_dim // tn
        nz = val_dim // tn
        qkv_raw, z = norm_mm(x, g_mix, W["gdn_w_qkvz"][j], [(F32, 0, nq), (BF16, nq, nq + nz)], tn=tn)
        gates = small_proj(x, g_mix, W["gdn_w_ba"][j], W["gdn_aux"][j], "gdn", nh=n_vh)
        C = CHUNK
        n_valid = min(C, seq_len)
        nc = seq_len // n_valid
        buf = jnp.pad(conv_rows.astype(F32), ((0, 0), (SUBLANES - (cw - 1), 0), (0, 0)))
        tl_in = min(512, seq_len)
        qkv_c = gdn_conv(qkv_raw, buf, W["gdn_conv_w8"][j], n_seq=n_seq, seq_len=seq_len, tl_in=tl_in,
                         tl_out=max(tl_in, C), n_norm_cols=2 * key_dim,
                         tc=key_dim if tl_in < C else min(512, key_dim))
        gates_c = jnp.pad(gates.reshape(n_seq * nc, n_valid, LANES),
                          ((0, 0), (0, C - n_valid), (0, 0))).reshape(n_seq * nc * C, LANES)
        tinv, amat, gcum = gdn_prep(qkv_c, gates_c, C=C, n_qk=n_qk, rep=rep, hd=gd)
        o, s_last = gdn_scan(qkv_c, z, gates_c, gcum, tinv, amat, s0.astype(F32), W["gdn_norm"][j],
                             n_seq=n_seq, nc=nc, C=C, n_valid=n_valid, n_qk=n_qk, rep=rep, hd=gd,
                             hg=min(8, n_vh))
        rows = jnp.concatenate([conv_rows.astype(F32), qkv_raw.reshape(n_seq, seq_len, qkv_dim)], axis=1)
        outs[("gdn", j)] = (s_last, rows[:, -(cw - 1):])
        return mm_res(o, W["gdn_w_out"][j], x, tn=tn, tk=D)

    def channel(i, x):
        m = i // 2
        g_ffn = W["norm_ffn"][i]
        if i % 2 == 0:
            pf = precise
            h = glu_up(x, g_ffn, wsel("ffn_w_gate", m, pf), wsel("ffn_w_up", m, pf), tn=tn)
            x = mm_res(h, wsel("ffn_w_down", m, pf), x, tn=tn, tk=D)
        else:
            pf = False
            comb = small_proj(x, g_ffn, W["moe_w_router"][m], W["zero_aux"], "router")
            h = glu_up(x, g_ffn, W["moe_w_gate"][m], W["moe_w_up"][m], comb, tn=tn)
            x = mm_res(h, W["moe_w_down"][m], x, tn=tn, tk=D)
        p_i = p[i] if pf else p[i].astype(BF16)
        return ple(x, W["norm_ple"][i], wsel("ple_w_gate", i, pf), p_i, wsel("ple_w_up", i, pf), tn=tn)

    for i, part in steps:
        x = mixer(i, x) if part == "mixer" else channel(i, x)
    return x, outs


def kernel(x_prompt, x_sample, cache_fox_k, cache_fox_v, cache_fox_logf, state_s5_re, state_s5_im, state_gdn, state_gdn_conv, p_prompt, p_sample, norm_mix, norm_ffn, norm_ple, norm_final, fox_w_in, fox_b_f, fox_w_out, s5_a_re, s5_a_im, s5_b_re, s5_b_im, s5_c_re, s5_c_im, s5_d, s5_log_dt, s5_w_glu, gdn_w_in, gdn_conv_w, gdn_a_log, gdn_dt_bias, gdn_norm, gdn_w_out, ffn_w_gate, ffn_w_up, ffn_w_down, moe_w_router, moe_w_gate, moe_w_up, moe_w_down, ple_w_up, ple_w_gate):
    B, L, D = x_prompt.shape
    Bs, Ls, _ = x_sample.shape
    depth = norm_mix.shape[0]
    P = cache_fox_k.shape[2]
    H, hd = cache_fox_k.shape[3], cache_fox_k.shape[4]
    n_vh, gd = state_gdn.shape[2], state_gdn.shape[3]
    qkv_dim = state_gdn_conv.shape[3]
    val_dim = n_vh * gd
    n_qk = (qkv_dim - val_dim) // (2 * gd)
    cw = gdn_conv_w.shape[1]
    G, NS = s5_a_re.shape[1], s5_a_re.shape[2]
    E, _, Fe = moe_w_gate.shape[1:]
    n_fox, n_s5, n_gdn = fox_w_in.shape[0], s5_a_re.shape[0], gdn_w_in.shape[0]

    first_router = 1
    n_lo = {"fox": 1, "ffn": 1, "ple": 1, "s5": 1}

    def hi_lo(w, n):
        return w.astype(BF16), _split_weight(w[:n])[1]

    W = {"norm_mix": norm_mix, "norm_ffn": norm_ffn, "norm_ple": norm_ple, "norm_final": norm_final,
         "fox_heads": (H, hd), "n_fox": n_fox, "gdn_heads": (n_qk, n_vh // n_qk, gd), "gdn_conv_w": gdn_conv_w,
         "gdn_norm": gdn_norm, "zero_aux": jnp.zeros((SUBLANES, LANES), F32)}
    W["fox_w_qkv"], W["fox_w_qkv_lo"] = hi_lo(fox_w_in[:, :, :3 * D], n_lo["fox"])
    W["fox_w_out"], W["fox_w_out_lo"] = hi_lo(fox_w_out, n_lo["fox"])
    W["fox_w_f"] = [_pad_cols(fox_w_in[j, :, 3 * D:]) for j in range(n_fox)]
    W["fox_aux"] = [_aux_rows((0, fox_b_f[j])) for j in range(n_fox)]
    W["s5_w_glu"], W["s5_w_glu_lo"] = hi_lo(s5_w_glu, n_lo["s5"])
    W["s5_consts"] = [s5_constants(s5_a_re[j], s5_a_im[j], s5_b_re[j], s5_b_im[j], s5_c_re[j], s5_c_im[j],
                                   s5_d[j], s5_log_dt[j]) for j in range(n_s5)]
    W["gdn_w_qkvz"] = gdn_w_in[:, :, :qkv_dim + val_dim].astype(BF16)
    W["gdn_w_ba"] = [_pad_cols(gdn_w_in[j, :, qkv_dim + val_dim:]) for j in range(n_gdn)]
    W["gdn_aux"] = [_aux_rows((n_vh, gdn_dt_bias[j]), (n_vh, gdn_a_log[j])) for j in range(n_gdn)]
    W["gdn_conv_w8"] = jnp.pad(gdn_conv_w.astype(F32), ((0, 0), (0, SUBLANES - cw), (0, 0)))
    W["gdn_w_out"] = gdn_w_out.astype(BF16)
    W["ffn_w_gate"], W["ffn_w_gate_lo"] = hi_lo(ffn_w_gate, n_lo["ffn"])
    W["ffn_w_up"], W["ffn_w_up_lo"] = hi_lo(ffn_w_up, n_lo["ffn"])
    W["ffn_w_down"], W["ffn_w_down_lo"] = hi_lo(ffn_w_down, n_lo["ffn"])
    W["moe_w_router"] = [_pad_cols(moe_w_router[m]) for m in range(moe_w_router.shape[0])]
    W["moe_w_gate"] = moe_w_gate.astype(BF16)
    W["moe_w_up"] = moe_w_up.astype(BF16)
    W["moe_w_down"] = moe_w_down.astype(BF16).reshape(-1, E * Fe, D)
    W["ple_w_gate"], W["ple_w_gate_lo"] = hi_lo(ple_w_gate, n_lo["ple"])
    W["ple_w_up"], W["ple_w_up_lo"] = hi_lo(ple_w_up, n_lo["ple"])

    steps = [(i, part) for i in range(depth) for part in ("mixer", "channel")]
    steps_a, steps_b = steps[:2 * first_router + 1], steps[2 * first_router + 1:]
    pd = p_prompt.shape[-1]

    Lt = LANES
    Lh = L - Lt
    p_all = {i: p_prompt[i].reshape(B * L, pd) for i in range(depth)}
    p_tail = {i: p_prompt[i][:, Lh:].reshape(B * Lt, pd) for i in range(depth)}
    zeros_s5 = {j: (jnp.zeros((B, G * NS), F32),) * 2 for j in range(n_s5)}
    x_f, out_p = _trunk(x_prompt.reshape(B * L, D), p_all, {}, zeros_s5, {}, W,
                        n_seq=B, seq_len=L, steps=steps_a, precise=False, snap_steps=Lh)
    hist_t = {}
    for j in range(n_fox):
        if ("fox", j) in out_p:
            qkv_f = out_p[("fox_qkv", j)].reshape(B, L, 3 * D)
            hist_t[j] = (qkv_f, qkv_f, H, 2 * H, 0, out_p[("fox", j)].reshape(B, L, H)[:, :Lh], Lh)
    xt, out_t = _trunk(x_prompt[:, Lh:].reshape(B * Lt, D), p_tail, hist_t,
                       {j: out_p[("s5_snap", j)] for j in range(n_s5) if ("s5_snap", j) in out_p}, {}, W,
                       n_seq=B, seq_len=Lt, steps=steps_a, precise=True)
    x_p = lax.dynamic_update_slice(x_f.reshape(B, L, D), xt.reshape(B, Lt, D), (0, Lh, 0)).reshape(B * L, D)
    out_p.update({key: val for key, val in out_t.items() if key[0] == "s5"})
    gdn0 = (jnp.zeros((B, n_vh, gd, gd), F32), jnp.zeros((B, cw - 1, qkv_dim), F32))
    x_p, out_b = _trunk(x_p, p_all, {}, zeros_s5, {j: gdn0 for j in range(n_gdn)}, W,
                        n_seq=B, seq_len=L, steps=steps_b, precise=False, fox_kv=out_p.get("fox_kv"))
    out_p.update(out_b)

    ck = cache_fox_k.reshape(-1, P, H * hd)
    cv = cache_fox_v.reshape(-1, P, H * hd)
    s5_re = state_s5_re.reshape(n_s5, Bs, G * NS).astype(F32)
    s5_im = state_s5_im.reshape(n_s5, Bs, G * NS).astype(F32)
    hist_s = {j: (ck, cv, 0, 0, j * Bs, cache_fox_logf[j], P) for j in range(n_fox)}
    state_s = ({j: (s5_re[j], s5_im[j]) for j in range(n_s5)},
               {j: (state_gdn[j], state_gdn_conv[j]) for j in range(n_gdn)})
    p_s = {i: p_sample[i].reshape(Bs * Ls, pd) for i in range(depth)}
    x_s, out_s = _trunk(x_sample.reshape(Bs * Ls, D), p_s, hist_s, *state_s, W,
                        n_seq=Bs, seq_len=Ls, steps=steps_a, precise=True)
    x_s, out_sb = _trunk(x_s, p_s, hist_s, *state_s, W, n_seq=Bs, seq_len=Ls, steps=steps_b, precise=False,
                         fox_kv=out_s.get("fox_kv"))
    out_s.update(out_sb)

    def leaves(x, out, nb, sl):
        y = rmsnorm(x, norm_final).reshape(nb, sl, D)
        fk, fv = (a.reshape(n_fox, nb, sl, H, hd) for a in out["fox_kv"])
        flf = jnp.stack([out[("fox", j)].reshape(nb, sl, H) for j in range(n_fox)])
        sre = jnp.stack([out[("s5", j)][0].reshape(nb, G, NS) for j in range(n_s5)])
        sim = jnp.stack([out[("s5", j)][1].reshape(nb, G, NS) for j in range(n_s5)])
        gs = jnp.stack([out[("gdn", j)][0] for j in range(n_gdn)])
        gc = jnp.stack([out[("gdn", j)][1] for j in range(n_gdn)])
        return y, fk, fv, flf, sre, sim, gs, gc

    (y_p, fk_p, fv_p, flf_p, sre_p, sim_p, gs_p, gc_p) = leaves(x_p, out_p, B, L)
    (y_s, fk_s, fv_s, flf_s, sre_s, sim_s, gs_s, gc_s) = leaves(x_s, out_s, Bs, Ls)
    return (y_p, y_s, fk_p, fv_p, flf_p, fk_s, fv_s, flf_s, sre_p, sim_p, sre_s, sim_s,
            gs_p, gc_p, gs_s, gc_s)
```

```python
import functools
import math

import jax
import jax.numpy as jnp
from jax import lax
from jax.experimental import pallas as pl
from jax.experimental.pallas import tpu as pltpu

F32 = jnp.float32
BF16 = jnp.bfloat16
EPS = 1e-6
LANES = 128
SUBLANES = 8
VMEM_LIMIT = 56 * 1024 * 1024
CHUNK = 64
N_EXPERTS = 8
NEG_INF = float("-inf")
LOG2E = math.log2(math.e)


def _cparams(*sem):
    return pltpu.CompilerParams(dimension_semantics=sem, vmem_limit_bytes=VMEM_LIMIT)


def _rms_scale(x, g):
    ms = jnp.mean(x * x, axis=-1, keepdims=True)
    return x * lax.rsqrt(ms + EPS) * g


def _sigmoid(x):
    return 0.5 * jnp.tanh(0.5 * x) + 0.5


def _silu(x):
    return x * _sigmoid(x)


def _softplus(x):
    return jnp.maximum(x, 0.0) + jnp.log1p(jnp.exp(-jnp.abs(x)))


def _split_bf16(x):
    hi = x.astype(BF16)
    lo = (x - hi.astype(F32)).astype(BF16)
    return hi, lo


def _split_weight(w):
    w = w.astype(F32)
    hi = lax.reduce_precision(w, exponent_bits=8, mantissa_bits=7)
    return hi.astype(BF16), (w - hi).astype(BF16)


def _dot(a, b):
    return jnp.dot(a, b, preferred_element_type=F32)


def _dot_nt(a, b):
    return lax.dot_general(a, b, (((1,), (1,)), ((), ())), preferred_element_type=F32)


def _dot_tn(a, b):
    return lax.dot_general(a, b, (((0,), (0,)), ((), ())), preferred_element_type=F32)


def _parts(x, precise):
    return _split_bf16(x.astype(F32)) if precise else (x.astype(BF16),)


def _mm(xp, wp):
    if len(wp) == 1:
        return _dot(xp[0], wp[0])
    return _dot(xp[0], wp[0]) + _dot(xp[1], wp[0]) + _dot(xp[0], wp[1])


def _mm_nt(ap, bp):
    if len(ap) == 1:
        return _dot_nt(ap[0], bp[0])
    return _dot_nt(ap[0], bp[0]) + _dot_nt(ap[1], bp[0]) + _dot_nt(ap[0], bp[1])


def _read(refs):
    return tuple(r[...] for r in refs)


def _wparts(w):
    return tuple(w) if isinstance(w, (tuple, list)) else (w,)


def _row_tile(m, n_weight_parts=1):
    cap = 1024 // n_weight_parts
    return next(t for t in (1024, 512, 256, 128) if t <= cap and m % t == 0)


def _store_norm(x_ref, g_ref, xn_refs):
    xn = _rms_scale(x_ref[...], g_ref[...])
    for ref, part in zip(xn_refs, _parts(xn, len(xn_refs) == 2)):
        ref[...] = part


def _norm_mm_kernel(*refs, ranges, nw):
    x_ref, g_ref = refs[:2]
    w_refs = refs[2:2 + nw]
    outs = refs[2 + nw:2 + nw + len(ranges)]
    xn_refs = refs[2 + nw + len(ranges):]
    j = pl.program_id(1)

    @pl.when(j == 0)
    def _():
        _store_norm(x_ref, g_ref, xn_refs)

    acc = _mm(_read(xn_refs), _read(w_refs))
    for o_ref, (a, b) in zip(outs, ranges):
        @pl.when((j >= a) & (j < b))
        def _(o_ref=o_ref):
            o_ref[...] = acc.astype(o_ref.dtype)


def _range_map(i, j, *, a, n):
    return (i, jnp.clip(j - a, 0, n - 1))


def norm_mm(x, g, w, outs, *, tn):
    M, D = x.shape
    wp = _wparts(w)
    N = wp[0].shape[1]
    tm = _row_tile(M, len(wp))
    ranges = tuple((a, b) for _, a, b in outs)
    return pl.pallas_call(
        functools.partial(_norm_mm_kernel, ranges=ranges, nw=len(wp)),
        grid=(M // tm, N // tn),
        in_specs=[pl.BlockSpec((tm, D), lambda i, j: (i, 0)),
                  pl.BlockSpec((1, D), lambda i, j: (0, 0))]
                 + [pl.BlockSpec((D, tn), lambda i, j: (0, j))] * len(wp),
        out_specs=[pl.BlockSpec((tm, tn), functools.partial(_range_map, a=a, n=b - a))
                   for _, a, b in outs],
        out_shape=[jax.ShapeDtypeStruct((M, (b - a) * tn), dt) for dt, a, b in outs],
        scratch_shapes=[pltpu.VMEM((tm, D), BF16)] * len(wp),
        compiler_params=_cparams("parallel", "arbitrary"),
        name="norm_mm",
    )(x, g.reshape(1, D), *wp)


def _fox_in_kernel(*refs, nw, nt, n_heads, hd, tm, n_bufs):
    x_ref, g_ref = refs[:2]
    w_refs = refs[2:2 + nw]
    qkv_ref, k_ref, v_ref = refs[2 + nw + n_bufs:5 + nw + n_bufs]
    xn_refs = refs[5 + nw + n_bufs:]
    j = pl.program_id(1)

    @pl.when(j == 0)
    def _():
        _store_norm(x_ref, g_ref, xn_refs)

    acc = _mm(_read(xn_refs), _read(w_refs))
    qkv_ref[...] = acc.astype(qkv_ref.dtype)
    hpt = acc.shape[1] // hd
    for dst, base in ((k_ref, nt), (v_ref, 2 * nt)):
        for jj in range(nt):
            @pl.when(j == base + jj)
            def _(dst=dst, jj=jj):
                for hh in range(hpt):
                    dst[pl.ds(jj * hpt + hh, tm, stride=n_heads), :] = acc[:, hh * hd:(hh + 1) * hd]


def fox_in_proj(x, g, w, *, n_heads, hd, tn, bufs=None, slot=0, n_slots=1):
    M, D = x.shape
    wp = _wparts(w)
    nw = len(wp)
    tm = min(512, M)
    nt = D // tn
    n_bufs = 0 if bufs is None else 2
    blk0 = slot * (M // tm)
    head_spec = pl.BlockSpec((tm * n_heads, hd), lambda i, j: (blk0 + i, 0))
    head_shape = jax.ShapeDtypeStruct((n_slots * M * n_heads, hd), F32)
    return pl.pallas_call(
        functools.partial(_fox_in_kernel, nw=nw, nt=nt, n_heads=n_heads, hd=hd, tm=tm, n_bufs=n_bufs),
        grid=(M // tm, 3 * nt),
        in_specs=[pl.BlockSpec((tm, D), lambda i, j: (i, 0)),
                  pl.BlockSpec((1, D), lambda i, j: (0, 0))]
                 + [pl.BlockSpec((D, tn), lambda i, j: (0, j))] * nw
                 + [pl.BlockSpec(memory_space=pl.ANY)] * n_bufs,
        out_specs=[pl.BlockSpec((tm, tn), lambda i, j: (i, j)), head_spec, head_spec],
        out_shape=[jax.ShapeDtypeStruct((M, 3 * D), F32 if nw == 2 else BF16), head_shape, head_shape],
        scratch_shapes=[pltpu.VMEM((tm, D), BF16)] * nw,
        input_output_aliases={} if bufs is None else {2 + nw: 1, 3 + nw: 2},
        compiler_params=_cparams("parallel", "arbitrary"),
        name="fox_in_proj",
    )(x, g.reshape(1, D), *wp, *(bufs or ()))


def _glu_up_kernel(*refs, tiles_per_expert, nw):
    x_ref, g_ref = refs[:2]
    wg_refs = refs[2:2 + nw]
    wu_refs = refs[2 + nw:2 + 2 * nw]
    rest = refs[2 + 2 * nw:]
    comb_ref = rest[0] if tiles_per_expert else None
    o_ref = rest[1] if tiles_per_expert else rest[0]
    xn_refs = rest[2:] if tiles_per_expert else rest[1:]
    j = pl.program_id(1)

    @pl.when(j == 0)
    def _():
        _store_norm(x_ref, g_ref, xn_refs)

    xn = _read(xn_refs)
    h = _silu(_mm(xn, _read(wg_refs))) * _mm(xn, _read(wu_refs))
    if tiles_per_expert:
        comb = comb_ref[...]
        lane = lax.broadcasted_iota(jnp.int32, comb.shape, 1)
        col = jnp.sum(jnp.where(lane == j // tiles_per_expert, comb, 0.0), axis=1, keepdims=True)
        h = h * col
    o_ref[...] = h.astype(o_ref.dtype)


def glu_up(x, g, w_gate, w_up, comb=None, *, tn):
    M, D = x.shape
    wg, wu = _wparts(w_gate), _wparts(w_up)
    nw = len(wg)
    tm = _row_tile(M, nw)
    if comb is None:
        n_out, tpe = wg[0].shape[1], 0
        w_spec = pl.BlockSpec((D, tn), lambda i, j: (0, j))
        extra_in, extra_specs = (), []
    else:
        E, _, F = wg[0].shape
        n_out, tpe = E * F, F // tn
        w_spec = pl.BlockSpec((None, D, tn), lambda i, j: (j // tpe, 0, j % tpe))
        extra_in = (comb,)
        extra_specs = [pl.BlockSpec((tm, LANES), lambda i, j: (i, 0))]
    return pl.pallas_call(
        functools.partial(_glu_up_kernel, tiles_per_expert=tpe, nw=nw),
        grid=(M // tm, n_out // tn),
        in_specs=[pl.BlockSpec((tm, D), lambda i, j: (i, 0)),
                  pl.BlockSpec((1, D), lambda i, j: (0, 0))] + [w_spec] * (2 * nw) + extra_specs,
        out_specs=pl.BlockSpec((tm, tn), lambda i, j: (i, j)),
        out_shape=jax.ShapeDtypeStruct((M, n_out), F32 if nw == 2 else BF16),
        scratch_shapes=[pltpu.VMEM((tm, D), BF16)] * nw,
        compiler_params=_cparams("parallel", "arbitrary"),
        name="glu_up",
    )(x, g.reshape(1, D), *wg, *wu, *extra_in)


def _mm_res_kernel(*refs, nk, nw):
    x_ref = refs[0]
    w_refs = refs[1:1 + nw]
    r_ref, o_ref, acc_ref = refs[1 + nw:]
    k = pl.program_id(2)

    @pl.when(k == 0)
    def _():
        acc_ref[...] = jnp.zeros_like(acc_ref)

    acc_ref[...] += _mm(_parts(x_ref[...], nw == 2), _read(w_refs))

    @pl.when(k == nk - 1)
    def _():
        o_ref[...] = r_ref[...] + acc_ref[...]


def mm_res(x, w, res, *, tn, tk):
    M, K = x.shape
    wp = _wparts(w)
    N = wp[0].shape[1]
    tm = _row_tile(M, len(wp))
    nk = K // tk
    return pl.pallas_call(
        functools.partial(_mm_res_kernel, nk=nk, nw=len(wp)),
        grid=(M // tm, N // tn, nk),
        in_specs=[pl.BlockSpec((tm, tk), lambda i, j, k: (i, k))]
                 + [pl.BlockSpec((tk, tn), lambda i, j, k: (k, j))] * len(wp)
                 + [pl.BlockSpec((tm, tn), lambda i, j, k: (i, j))],
        out_specs=pl.BlockSpec((tm, tn), lambda i, j, k: (i, j)),
        out_shape=jax.ShapeDtypeStruct((M, N), F32),
        scratch_shapes=[pltpu.VMEM((tm, tn), F32)],
        compiler_params=_cparams("parallel", "parallel", "arbitrary"),
        name="mm_res",
    )(x, *wp, res)


def _glu_res_kernel(*refs, nw):
    y_ref = refs[0]
    wa_refs = refs[1:1 + nw]
    wb_refs = refs[1 + nw:1 + 2 * nw]
    r_ref, o_ref = refs[1 + 2 * nw:]
    y = _parts(y_ref[...], nw == 2)
    o_ref[...] = r_ref[...] + _mm(y, _read(wa_refs)) * _sigmoid(_mm(y, _read(wb_refs)))


def glu_res(y, w, res, *, tn):
    M, K = y.shape
    wp = _wparts(w)
    N = wp[0].shape[1] // 2
    nj = N // tn
    tm = _row_tile(M, len(wp))
    return pl.pallas_call(
        functools.partial(_glu_res_kernel, nw=len(wp)),
        grid=(M // tm, nj),
        in_specs=[pl.BlockSpec((tm, K), lambda i, j: (i, 0))]
                 + [pl.BlockSpec((K, tn), lambda i, j: (0, j))] * len(wp)
                 + [pl.BlockSpec((K, tn), lambda i, j: (0, j + nj))] * len(wp)
                 + [pl.BlockSpec((tm, tn), lambda i, j: (i, j))],
        out_specs=pl.BlockSpec((tm, tn), lambda i, j: (i, j)),
        out_shape=jax.ShapeDtypeStruct((M, N), F32),
        compiler_params=_cparams("parallel", "arbitrary"),
        name="glu_res",
    )(y, *wp, *wp, res)


def _ple_kernel(*refs, tn, nw):
    x_ref, g_ref = refs[:2]
    wg_refs = refs[2:2 + nw]
    p_ref = refs[2 + nw]
    wu_refs = refs[3 + nw:3 + 2 * nw]
    o_ref = refs[3 + 2 * nw]
    xn_refs = refs[4 + 2 * nw:]
    j = pl.program_id(1)

    @pl.when(j == 0)
    def _():
        _store_norm(x_ref, g_ref, xn_refs)

    gate = _sigmoid(_mm(_read(xn_refs), _read(wg_refs)))
    up = _mm(_parts(p_ref[...], nw == 2), _read(wu_refs))
    x_tile = x_ref[:, pl.ds(pl.multiple_of(j * tn, tn), tn)]
    o_ref[...] = x_tile + up * gate


def ple(x, g, w_gate, p, w_up, *, tn):
    M, D = x.shape
    P = p.shape[1]
    wg, wu = _wparts(w_gate), _wparts(w_up)
    nw = len(wg)
    tm = _row_tile(M, nw)
    return pl.pallas_call(
        functools.partial(_ple_kernel, tn=tn, nw=nw),
        grid=(M // tm, D // tn),
        in_specs=[pl.BlockSpec((tm, D), lambda i, j: (i, 0)),
                  pl.BlockSpec((1, D), lambda i, j: (0, 0))]
                 + [pl.BlockSpec((D, tn), lambda i, j: (0, j))] * nw
                 + [pl.BlockSpec((tm, P), lambda i, j: (i, 0))]
                 + [pl.BlockSpec((P, tn), lambda i, j: (0, j))] * nw,
        out_specs=pl.BlockSpec((tm, tn), lambda i, j: (i, j)),
        out_shape=jax.ShapeDtypeStruct((M, D), F32),
        scratch_shapes=[pltpu.VMEM((tm, D), BF16)] * nw,
        compiler_params=_cparams("parallel", "arbitrary"),
        name="ple",
    )(x, g.reshape(1, D), *wg, p, *wu)


def _rmsnorm_kernel(x_ref, g_ref, o_ref):
    o_ref[...] = _rms_scale(x_ref[...], g_ref[...]).astype(o_ref.dtype)


def rmsnorm(x, g):
    M, D = x.shape
    tm = _row_tile(M)
    return pl.pallas_call(
        _rmsnorm_kernel,
        grid=(M // tm,),
        in_specs=[pl.BlockSpec((tm, D), lambda i: (i, 0)),
                  pl.BlockSpec((1, D), lambda i: (0, 0))],
        out_specs=pl.BlockSpec((tm, D), lambda i: (i, 0)),
        out_shape=jax.ShapeDtypeStruct((M, D), F32),
        compiler_params=_cparams("parallel"),
        name="rmsnorm",
    )(x, g.reshape(1, D))


def _row_rstd_kernel(x_ref, o_ref):
    x = x_ref[...]
    rstd = lax.rsqrt(jnp.mean(x * x, axis=-1, keepdims=True) + EPS)
    o_ref[...] = jnp.broadcast_to(rstd, o_ref.shape)


def row_rstd(x):
    M, D = x.shape
    tm = _row_tile(M)
    return pl.pallas_call(
        _row_rstd_kernel,
        grid=(M // tm,),
        in_specs=[pl.BlockSpec((tm, D), lambda i: (i, 0))],
        out_specs=pl.BlockSpec((tm, LANES), lambda i: (i, 0)),
        out_shape=jax.ShapeDtypeStruct((M, LANES), F32),
        compiler_params=_cparams("parallel"),
        name="row_rstd",
    )(x)


def _small_proj_kernel(x_ref, g_ref, whi_ref, wlo_ref, aux_ref, o_ref, *, mode, nh):
    xn = _rms_scale(x_ref[...], g_ref[...])
    acc = _mm(_parts(xn, True), (whi_ref[...], wlo_ref[...]))
    lane = lax.broadcasted_iota(jnp.int32, acc.shape, 1)
    if mode == "logf":
        o_ref[...] = -_softplus(-(acc + aux_ref[0:1, :]))
    elif mode == "gdn":
        beta = _sigmoid(acc)
        gdec = -jnp.exp(aux_ref[1:2, :]) * _softplus(acc + aux_ref[0:1, :])
        o_ref[...] = jnp.where(lane < nh, beta, gdec)
    else:
        logits = jnp.where(lane < N_EXPERTS, acc, NEG_INF)
        v1 = jnp.max(logits, axis=1, keepdims=True)
        i1 = jnp.min(jnp.where(logits == v1, lane, LANES), axis=1, keepdims=True)
        rest = jnp.where(lane == i1, NEG_INF, logits)
        v2 = jnp.max(rest, axis=1, keepdims=True)
        i2 = jnp.min(jnp.where(rest == v2, lane, LANES), axis=1, keepdims=True)
        e2 = jnp.exp(v2 - v1)
        g1 = 1.0 / (1.0 + e2)
        o_ref[...] = jnp.where(lane == i1, g1, 0.0) + jnp.where(lane == i2, e2 * g1, 0.0)


def small_proj(x, g, w_parts, aux, mode, *, nh=0):
    M, D = x.shape
    tm = _row_tile(M)
    return pl.pallas_call(
        functools.partial(_small_proj_kernel, mode=mode, nh=nh),
        grid=(M // tm,),
        in_specs=[pl.BlockSpec((tm, D), lambda i: (i, 0)),
                  pl.BlockSpec((1, D), lambda i: (0, 0)),
                  pl.BlockSpec((D, LANES), lambda i: (0, 0)),
                  pl.BlockSpec((D, LANES), lambda i: (0, 0)),
                  pl.BlockSpec((SUBLANES, LANES), lambda i: (0, 0))],
        out_specs=pl.BlockSpec((tm, LANES), lambda i: (i, 0)),
        out_shape=jax.ShapeDtypeStruct((M, LANES), F32),
        compiler_params=_cparams("parallel"),
        name="small_proj_" + mode,
    )(x, g.reshape(1, D), *w_parts, aux)


def _pad_cols(w):
    return _split_weight(jnp.pad(w.astype(F32), ((0, 0), (0, LANES - w.shape[1]))))


def _aux_rows(*rows):
    out = [jnp.pad(vals.astype(F32), (start, LANES - start - vals.shape[0])) for start, vals in rows]
    out += [jnp.zeros((LANES,), F32)] * (SUBLANES - len(out))
    return jnp.stack(out)


def _lane_cumsum_kernel(x_ref, o_ref):
    x = x_ref[...]
    n = x.shape[1]
    lane = lax.broadcasted_iota(jnp.int32, x.shape, 1)
    shift = 1
    while shift < n:
        x = x + jnp.where(lane >= shift, pltpu.roll(x, shift, 1), 0.0)
        shift *= 2
    o_ref[...] = x


def lane_cumsum(x):
    R, n = x.shape
    rb = SUBLANES if R % SUBLANES == 0 else R
    return pl.pallas_call(
        _lane_cumsum_kernel,
        grid=(R // rb,),
        in_specs=[pl.BlockSpec((rb, n), lambda i: (i, 0))],
        out_specs=pl.BlockSpec((rb, n), lambda i: (i, 0)),
        out_shape=jax.ShapeDtypeStruct((R, n), F32),
        compiler_params=_cparams("parallel"),
        name="lane_cumsum",
    )(x)


def _fox_prompt_kernel(q_ref, k_ref, v_ref, fk_ref, o_ref, *, tq, nhb, hd, scale):
    qi = pl.program_id(2)

    def step(ki, carry, diagonal):
        rows = pl.ds(pl.multiple_of(ki * tq, tq), tq)
        out = []
        for h in range(nhb):
            m, l, acc = carry[h]
            cols = slice(h * hd, (h + 1) * hd)
            s = (_dot_nt(q_ref[:, cols], k_ref[rows, cols]) * (scale * LOG2E)
                 - fk_ref[h, pl.ds(ki, 1), :] * LOG2E)
            if diagonal:
                r = lax.broadcasted_iota(jnp.int32, s.shape, 0)
                c = lax.broadcasted_iota(jnp.int32, s.shape, 1)
                s = jnp.where(r >= c, s, NEG_INF)
            m_new = jnp.maximum(m, jnp.max(s, axis=1, keepdims=True))
            p = jnp.exp2(s - m_new)
            alpha = jnp.exp2(m - m_new)
            l = alpha * l + jnp.sum(p, axis=1, keepdims=True)
            acc = alpha * acc + _dot(p.astype(BF16), v_ref[rows, cols])
            out.append((m_new, l, acc))
        return tuple(out)

    init = tuple((jnp.full((tq, 1), NEG_INF, F32), jnp.zeros((tq, 1), F32), jnp.zeros((tq, hd), F32))
                 for _ in range(nhb))
    carry = lax.fori_loop(0, qi, lambda ki, c: step(ki, c, False), init)
    carry = step(qi, carry, True)
    o_ref[...] = jnp.concatenate([acc / l for _, l, acc in carry], axis=1).astype(o_ref.dtype)


def fox_attention_prompt(qkv, fk, *, B, L, H, hd, tq, nhb=2):
    nq = L // tq
    nhg = H // nhb
    return pl.pallas_call(
        functools.partial(_fox_prompt_kernel, tq=tq, nhb=nhb, hd=hd, scale=hd ** -0.5),
        grid=(B, nhg, nq),
        in_specs=[pl.BlockSpec((tq, nhb * hd), lambda b, g, i: (b * nq + i, g)),
                  pl.BlockSpec((L, nhb * hd), lambda b, g, i: (b, nhg + g)),
                  pl.BlockSpec((L, nhb * hd), lambda b, g, i: (b, 2 * nhg + g)),
                  pl.BlockSpec((nhb, nq, tq), lambda b, g, i: (b * nhg + g, 0, 0))],
        out_specs=pl.BlockSpec((tq, nhb * hd), lambda b, g, i: (b * nq + i, g)),
        out_shape=jax.ShapeDtypeStruct((B * L, H * hd), BF16),
        compiler_params=_cparams("parallel", "parallel", "arbitrary"),
        name="fox_attention_prompt",
    )(qkv, qkv, qkv, fk)


def _fox_cached_kernel(q_ref, kn_ref, vn_ref, kp_ref, vp_ref, f_ref, o_ref, *, P, nhb, hd, scale, precise,
                       head_rows):
    n = q_ref.shape[0]
    r = lax.broadcasted_iota(jnp.int32, (n, n), 0)
    c = lax.broadcasted_iota(jnp.int32, (n, n), 1)
    g = pl.program_id(1)

    def past(ref, h):
        if head_rows:
            return ref[pl.ds(g * nhb + h, P, stride=head_rows), :]
        return ref[:, h * hd:(h + 1) * hd]

    outs = []
    for h in range(nhb):
        cols = slice(h * hd, (h + 1) * hd)
        q = _parts(q_ref[:, cols], precise)
        f = f_ref[h]
        s_past = _mm_nt(q, _parts(past(kp_ref, h), precise)) * scale - f[:, :P]
        s_new = _mm_nt(q, _parts(kn_ref[:, cols], precise)) * scale - f[:, P:P + n]
        s_new = jnp.where(r >= c, s_new, NEG_INF)
        m = jnp.maximum(jnp.max(s_past, axis=1, keepdims=True), jnp.max(s_new, axis=1, keepdims=True))
        p_past = jnp.exp(s_past - m)
        p_new = jnp.exp(s_new - m)
        l = jnp.sum(p_past, axis=1, keepdims=True) + jnp.sum(p_new, axis=1, keepdims=True)
        acc = (_mm(_parts(p_past, precise), _parts(past(vp_ref, h), precise))
               + _mm(_parts(p_new, precise), _parts(vn_ref[:, cols], precise)))
        outs.append(acc / l)
    o_ref[...] = jnp.concatenate(outs, axis=1).astype(o_ref.dtype)


def fox_attention_cached(qkv, k_past, v_past, f, *, seq0, k_lane0, v_lane0, B, n, P, H, hd):
    precise = qkv.dtype == F32
    nhb = 4 if H % 4 == 0 else 1
    nhg = H // nhb
    w = nhb * hd
    head_rows = H if k_lane0 is None else 0
    if head_rows:
        past_specs = [pl.BlockSpec((None, P * H, hd), lambda b, g: (seq0 + b, 0, 0))] * 2
    else:
        past_specs = [pl.BlockSpec((None, P, w), lambda b, g: (seq0 + b, 0, k_lane0 // nhb + g)),
                      pl.BlockSpec((None, P, w), lambda b, g: (seq0 + b, 0, v_lane0 // nhb + g))]
    return pl.pallas_call(
        functools.partial(_fox_cached_kernel, P=P, nhb=nhb, hd=hd, scale=hd ** -0.5, precise=precise,
                          head_rows=head_rows),
        grid=(B, nhg),
        in_specs=[pl.BlockSpec((n, w), lambda b, g: (b, g)),
                  pl.BlockSpec((n, w), lambda b, g: (b, nhg + g)),
                  pl.BlockSpec((n, w), lambda b, g: (b, 2 * nhg + g))]
                 + past_specs
                 + [pl.BlockSpec((nhb, 1, f.shape[2]), lambda b, g: (b * nhg + g, 0, 0))],
        out_specs=pl.BlockSpec((n, w), lambda b, g: (b, g)),
        out_shape=jax.ShapeDtypeStruct((B * n, H * hd), qkv.dtype),
        compiler_params=_cparams("parallel", "arbitrary"),
        name="fox_attention_cached",
    )(qkv, qkv, qkv, k_past, v_past, f)


def _gelu_tanh(x):
    return 0.5 * x * (1.0 + jnp.tanh(math.sqrt(2.0 / math.pi) * (x + 0.044715 * (x * x * x))))


def _s5_kernel(*refs, nb, c, ns, ncm, snap_chunk):
    x_ref, rstd_ref, g_ref, bhi_ref, blo_ref = refs[:5]
    cm_refs = refs[5:5 + ncm]
    (ar_ref, ai_ref, d_ref, h0r_ref, h0i_ref, y_ref, hr_out, hi_out, hr_snap, hi_snap,
     u_sc, y_sc, bu_sc, st_sc, hr_sc, hi_sc) = refs[5 + ncm:]
    tc = pl.program_id(1)

    @pl.when(tc == 0)
    def _():
        hr_sc[...] = h0r_ref[...]
        hi_sc[...] = h0i_ref[...]

    g = g_ref[...]
    for b in range(nb):
        u_sc[pl.ds(b, c, stride=nb), :] = x_ref[b] * rstd_ref[b] * g
    u = u_sc[...]
    u_hi, u_lo = _split_bf16(u)
    bu_sc[...] = _dot(jnp.concatenate([u_hi, u_lo], axis=1), bhi_ref[0]) + _dot(u_hi, blo_ref[0])
    ar = jnp.broadcast_to(ar_ref[0], (nb, ns))
    ai = jnp.broadcast_to(ai_ref[0], (nb, ns))

    def step(t, carry):
        hr, hi = carry
        rows = pl.ds(pl.multiple_of(t * nb, nb), nb)
        nhr = ar * hr - ai * hi + bu_sc[rows, 0:ns]
        nhi = ar * hi + ai * hr + bu_sc[rows, ns:2 * ns]
        st_sc[rows, 0:ns] = nhr
        st_sc[rows, ns:2 * ns] = nhi
        return nhr, nhi

    hr, hi = lax.fori_loop(0, c, step, (hr_sc[...], hi_sc[...]))
    hr_sc[...] = hr
    hi_sc[...] = hi
    y = _mm(_parts(st_sc[...], ncm == 2), tuple(r[0] for r in cm_refs)) + d_ref[...] * u
    y_sc[...] = _gelu_tanh(y)
    for b in range(nb):
        y_ref[b] = y_sc[pl.ds(b, c, stride=nb), :].astype(y_ref.dtype)

    @pl.when(tc == snap_chunk)
    def _():
        hr_snap[...] = hr
        hi_snap[...] = hi

    @pl.when(tc == pl.num_programs(1) - 1)
    def _():
        hr_out[...] = hr
        hi_out[...] = hi


def s5_scan(x, rstd, g, h0r, h0i, consts, *, c, precise, snap_steps=None):
    bhi, blo, cm_hi, cm_lo, ar, ai, d = consts
    cms = (cm_hi, cm_lo) if precise else (cm_hi,)
    nb, L, D = x.shape
    nblk = D // LANES
    ns = ar.shape[2]
    ntc = L // c
    snap_chunk = ntc - 1 if snap_steps is None else snap_steps // c - 1
    blk = lambda g, t: (g, 0, 0)
    state_spec = pl.BlockSpec((nb, ns), lambda g, t: (0, g))
    state_shape = jax.ShapeDtypeStruct(h0r.shape, F32)
    return pl.pallas_call(
        functools.partial(_s5_kernel, nb=nb, c=c, ns=ns, ncm=len(cms), snap_chunk=snap_chunk),
        grid=(nblk, ntc),
        in_specs=[pl.BlockSpec((nb, c, LANES), lambda g, t: (0, t, g)),
                  pl.BlockSpec((nb, c, LANES), lambda g, t: (0, t, 0)),
                  pl.BlockSpec((1, LANES), lambda g, t: (0, g)),
                  pl.BlockSpec((1, 2 * LANES, 2 * ns), blk),
                  pl.BlockSpec((1, LANES, 2 * ns), blk)]
                 + [pl.BlockSpec((1, 2 * ns, LANES), blk)] * len(cms)
                 + [pl.BlockSpec((1, 1, ns), blk),
                    pl.BlockSpec((1, 1, ns), blk),
                    pl.BlockSpec((1, LANES), lambda g, t: (0, g)),
                    state_spec, state_spec],
        out_specs=[pl.BlockSpec((nb, c, LANES), lambda g, t: (0, t, g))] + [state_spec] * 4,
        out_shape=[jax.ShapeDtypeStruct((nb, L, D), F32 if precise else BF16)] + [state_shape] * 4,
        scratch_shapes=[pltpu.VMEM((c * nb, LANES), F32),
                        pltpu.VMEM((c * nb, LANES), F32),
                        pltpu.VMEM((c * nb, 2 * ns), F32),
                        pltpu.VMEM((c * nb, 2 * ns), F32),
                        pltpu.VMEM((nb, ns), F32),
                        pltpu.VMEM((nb, ns), F32)],
        compiler_params=_cparams("parallel", "arbitrary"),
        name="s5_scan",
    )(x, rstd, g.reshape(1, D), bhi, blo, *cms, ar, ai, d, h0r, h0i)


def s5_constants(a_re, a_im, b_re, b_im, c_re, c_im, d_skip, log_dt):
    G, N = a_re.shape
    gc = b_re.shape[2]
    gpb = LANES // gc
    nblk = G // gpb
    a = lax.complex(a_re.astype(F32), a_im.astype(F32))
    dt = jnp.exp(log_dt.astype(F32))[:, None]
    a_bar = jnp.exp(a * dt)
    b_bar = ((a_bar - 1.0) / a)[..., None] * lax.complex(b_re.astype(F32), b_im.astype(F32))
    eye = jnp.eye(gpb, dtype=F32)

    def in_mat(b):
        b = b.reshape(nblk, gpb, N, gc)
        return jnp.einsum("kgnc,gh->kgchn", b, eye).reshape(nblk, gpb * gc, gpb * N)

    def out_mat(cc):
        cc = cc.reshape(nblk, gpb, gc, N)
        return jnp.einsum("kgcn,gh->kgnhc", cc, eye).reshape(nblk, gpb * N, gpb * gc)

    bhi, blo = _split_weight(jnp.concatenate([in_mat(b_bar.real), in_mat(b_bar.imag)], axis=2))
    bhi = jnp.concatenate([bhi, bhi], axis=1)
    cm_hi, cm_lo = _split_weight(jnp.concatenate([out_mat(c_re.astype(F32)), -out_mat(c_im.astype(F32))], axis=1))
    ar = a_bar.real.reshape(nblk, 1, gpb * N)
    ai = a_bar.imag.reshape(nblk, 1, gpb * N)
    return bhi, blo, cm_hi, cm_lo, ar, ai, d_skip.astype(F32).reshape(1, G * gc)


def _gdn_conv_kernel(x_ref, halo_ref, buf_ref, w_ref, o_ref, ext_sc, *, tl, n_valid, tiles_per_seq, n_norm_tiles):
    r = pl.program_id(0)
    j = pl.program_id(1)
    first = (r % tiles_per_seq) == 0
    ext_sc[0:SUBLANES, :] = jnp.where(first, buf_ref[0], halo_ref[...])
    ext_sc[SUBLANES:SUBLANES + n_valid, :] = x_ref[...]
    w = w_ref[...]
    y = x_ref[...] * w[3:4, :]
    for s in (1, 2, 3):
        y = y + ext_sc[SUBLANES - s:SUBLANES - s + n_valid, :] * w[3 - s:4 - s, :]
    y = _silu(y)

    def store(val):
        if n_valid < tl:
            val = jnp.concatenate([val, jnp.zeros((tl - n_valid, val.shape[1]), F32)], axis=0)
        o_ref[...] = val.astype(o_ref.dtype)

    @pl.when(j < n_norm_tiles)
    def _():
        segs = []
        for a in range(0, y.shape[1], LANES):
            seg = y[:, a:a + LANES]
            segs.append(seg * lax.rsqrt(jnp.sum(seg * seg, axis=1, keepdims=True) + EPS))
        store(jnp.concatenate(segs, axis=1))

    @pl.when(j >= n_norm_tiles)
    def _():
        store(y)


def gdn_conv(x, buf8, w8, *, n_seq, seq_len, tl_in, tl_out, n_norm_cols, tc):
    C = x.shape[1]
    tiles_per_seq = seq_len // tl_in
    n_tiles = n_seq * tiles_per_seq
    hpt = tl_in // SUBLANES
    return pl.pallas_call(
        functools.partial(_gdn_conv_kernel, tl=tl_out, n_valid=tl_in, tiles_per_seq=tiles_per_seq,
                          n_norm_tiles=n_norm_cols // tc),
        grid=(n_tiles, C // tc),
        in_specs=[pl.BlockSpec((tl_in, tc), lambda r, j: (r, j)),
                  pl.BlockSpec((SUBLANES, tc), lambda r, j: (jnp.maximum(r * hpt - 1, 0), j)),
                  pl.BlockSpec((1, SUBLANES, tc), lambda r, j: (r // tiles_per_seq, 0, j)),
                  pl.BlockSpec((SUBLANES, tc), lambda r, j: (0, j))],
        out_specs=pl.BlockSpec((tl_out, tc), lambda r, j: (r, j)),
        out_shape=jax.ShapeDtypeStruct((n_tiles * tl_out, C), BF16),
        scratch_shapes=[pltpu.VMEM((SUBLANES + tl_in, tc), F32)],
        compiler_params=_cparams("parallel", "arbitrary"),
        name="gdn_conv",
    )(x, x, buf8, w8)


def _gdn_prep_kernel(qk_ref, gt_ref, ti_ref, a_ref, gc_ref, n_sc, nt_sc, x_sc, *, C, cps, n_qk, rep, hd, scale):
    nh = n_qk * rep
    n_sys = cps * nh
    pitch = C + SUBLANES
    r = lax.broadcasted_iota(jnp.int32, (C, C), 0)
    c = lax.broadcasted_iota(jnp.int32, (C, C), 1)
    tril = (r >= c).astype(BF16)
    lane_pad = jnp.zeros((C, LANES - C), F32)
    for cc in range(cps):
        rows = slice(cc * C, (cc + 1) * C)
        gates = gt_ref[rows, :]
        lane = lax.broadcasted_iota(jnp.int32, gates.shape, 1)
        g = jnp.where((lane >= nh) & (lane < 2 * nh), gates, 0.0)
        g_hi = g.astype(BF16)
        g_r1 = g - g_hi.astype(F32)
        g_mid = g_r1.astype(BF16)
        g_lo = (g_r1 - g_mid.astype(F32)).astype(BF16)
        gcum = _dot(tril, g_hi) + _dot(tril, g_mid) + _dot(tril, g_lo)
        gc_ref[rows, :] = gcum
        gcum_t = gcum.T
        for hq in range(n_qk):
            q = qk_ref[rows, hq * hd:(hq + 1) * hd]
            k = qk_ref[rows, (n_qk + hq) * hd:(n_qk + hq + 1) * hd]
            kk = _dot_nt(k, k)
            qk = _dot_nt(q, k) * scale
            for rr in range(rep):
                hv = hq * rep + rr
                sys = cc * nh + hv
                diff = gcum[:, nh + hv:nh + hv + 1] - gcum_t[nh + hv:nh + hv + 1, :]
                decay = jnp.exp(jnp.where(r >= c, diff, NEG_INF))
                n_val = jnp.where(r > c, gates[:, hv:hv + 1] * kk * decay, 0.0)
                n_sc[sys * pitch:sys * pitch + C, :] = jnp.concatenate([n_val, lane_pad], axis=1)
                a_ref[cc, hv] = (qk * decay).astype(a_ref.dtype)
    for i in range(C):
        slab = n_sc[pl.ds(i, n_sys, stride=pitch), :]
        nt_sc[i * C:(i + 1) * C, :] = slab.T[:C, :]
    _tri_solve(nt_sc, x_sc, C)
    row_pad = jnp.zeros((LANES - C, n_sys), F32)
    for i in range(C):
        cols = jnp.concatenate([x_sc[i * C:(i + 1) * C, :], row_pad], axis=0)
        n_sc[pl.ds(i, n_sys, stride=pitch), :] = cols.T
    for cc in range(cps):
        for hv in range(nh):
            sys = cc * nh + hv
            ti_ref[cc, hv] = n_sc[sys * pitch:sys * pitch + C, :C].astype(ti_ref.dtype)


def gdn_prep(qkv_c, gates, *, C, n_qk, rep, hd):
    rows = qkv_c.shape[0]
    nch = rows // C
    nh = n_qk * rep
    cps = LANES // nh
    assert nch % cps == 0 and cps * nh == LANES
    return pl.pallas_call(
        functools.partial(_gdn_prep_kernel, C=C, cps=cps, n_qk=n_qk, rep=rep, hd=hd, scale=hd ** -0.5),
        grid=(nch // cps,),
        in_specs=[pl.BlockSpec((cps * C, 2 * n_qk * hd), lambda n: (n, 0)),
                  pl.BlockSpec((cps * C, LANES), lambda n: (n, 0))],
        out_specs=[pl.BlockSpec((cps, nh, C, C), lambda n: (n, 0, 0, 0)),
                   pl.BlockSpec((cps, nh, C, C), lambda n: (n, 0, 0, 0)),
                   pl.BlockSpec((cps * C, LANES), lambda n: (n, 0))],
        out_shape=[jax.ShapeDtypeStruct((nch, nh, C, C), BF16),
                   jax.ShapeDtypeStruct((nch, nh, C, C), BF16),
                   jax.ShapeDtypeStruct((rows, LANES), F32)],
        scratch_shapes=[pltpu.VMEM((LANES * (C + SUBLANES), LANES), F32),
                        pltpu.VMEM((C * C, LANES), F32),
                        pltpu.VMEM((C * C, LANES), F32)],
        compiler_params=_cparams("parallel"),
        name="gdn_prep",
    )(qkv_c, gates)


def _tri_solve(n_ref, x_ref, C):
    nblk = C // SUBLANES
    lanes = n_ref.shape[1]
    sub = lax.broadcasted_iota(jnp.int32, (SUBLANES, lanes), 0)
    x_ref[...] = jnp.zeros(x_ref.shape, F32)
    for ib in range(nblk):
        def row(r, _, ib=ib):
            base = (ib * SUBLANES + r) * C
            acc = [jnp.zeros((SUBLANES, lanes), F32) for _ in range(ib)]
            acc.append(jnp.where(sub == r, 1.0, 0.0).astype(F32))
            for jb in range(ib + 1):
                for j in range(jb * SUBLANES, (jb + 1) * SUBLANES):
                    nb = jnp.broadcast_to(n_ref[pl.ds(base + j, 1), :], (SUBLANES, lanes))
                    for cb in range(jb + 1):
                        acc[cb] = acc[cb] - nb * x_ref[j * C + cb * SUBLANES:j * C + (cb + 1) * SUBLANES, :]
            for cb in range(ib + 1):
                x_ref[pl.ds(pl.multiple_of(base + cb * SUBLANES, SUBLANES), SUBLANES), :] = acc[cb]
            return 0

        lax.fori_loop(0, SUBLANES, row, 0)


def _gdn_scan_kernel(q_ref, k_ref, v_ref, z_ref, gt_ref, gc_ref, ti_ref, a_ref, s0_ref, ng_ref,
                     o_ref, s_out, s_sc, *, C, hg, rep, hd, nh, n_valid, scale):
    n = pl.program_id(2)
    hgi = pl.program_id(1)

    @pl.when(n == 0)
    def _():
        s_sc[...] = s0_ref[0]

    shift = (LANES - hgi * hg) % LANES
    gates = pltpu.roll(gt_ref[...], shift, 1)
    gcum = pltpu.roll(gc_ref[...], shift, 1)
    heads = range(hg)
    s_old = [s_sc[hh] for hh in heads]
    s_bf = [s.astype(BF16) for s in s_old]
    beta = [gates[:, hh:hh + 1] for hh in heads]
    gc = [gcum[:, nh + hh:nh + hh + 1] for hh in heads]
    eg = [jnp.exp(g) for g in gc]
    g_tot = [g[C - 1:C, :] for g in gc]
    k = [k_ref[:, (hh // rep) * hd:(hh // rep + 1) * hd].astype(F32) for hh in heads]
    q = [q_ref[:, (hh // rep) * hd:(hh // rep + 1) * hd].astype(F32) for hh in heads]
    v = [v_ref[:, hh * hd:(hh + 1) * hd].astype(F32) for hh in heads]
    uw = [_dot(ti_ref[0, hh].astype(BF16),
               jnp.concatenate([v[hh] * beta[hh], k[hh] * (beta[hh] * eg[hh])], axis=1).astype(BF16))
          for hh in heads]
    q_dec = [(q[hh] * (scale * eg[hh])).astype(BF16) for hh in heads]
    k_dec = [(k[hh] * jnp.exp(g_tot[hh] - gc[hh])).astype(BF16) for hh in heads]
    v_new = [(uw[hh][:, :hd] - _dot(uw[hh][:, hd:].astype(BF16), s_bf[hh])).astype(BF16) for hh in heads]
    o = [_dot(q_dec[hh], s_bf[hh]) + _dot(a_ref[0, hh], v_new[hh]) for hh in heads]
    s_new = [s_old[hh] * jnp.exp(g_tot[hh]) + _dot_tn(k_dec[hh], v_new[hh]) for hh in heads]
    for hh in heads:
        s_sc[hh] = s_new[hh]
    ng = ng_ref[...]
    outs = [_rms_scale(o[hh][:n_valid], ng) * _silu(z_ref[:, hh * hd:(hh + 1) * hd].astype(F32)) for hh in heads]
    o_ref[...] = jnp.concatenate(outs, axis=1).astype(o_ref.dtype)

    @pl.when(n == pl.num_programs(2) - 1)
    def _():
        s_out[0] = s_sc[...]


def gdn_scan(qkv_c, z, gates, gcum, tinv, amat, s0, norm_g, *, n_seq, nc, C, n_valid, n_qk, rep, hd, hg):
    nh = n_qk * rep
    nhg = nh // hg
    qw = (hg // rep) * hd
    k_off = n_qk * hd // qw
    v_off = 2 * n_qk * hd // (hg * hd)
    chunk = lambda s, g, n: s * nc + n
    return pl.pallas_call(
        functools.partial(_gdn_scan_kernel, C=C, hg=hg, rep=rep, hd=hd, nh=nh, n_valid=n_valid,
                          scale=hd ** -0.5),
        grid=(n_seq, nhg, nc),
        in_specs=[pl.BlockSpec((C, qw), lambda s, g, n: (chunk(s, g, n), g)),
                  pl.BlockSpec((C, qw), lambda s, g, n: (chunk(s, g, n), k_off + g)),
                  pl.BlockSpec((C, hg * hd), lambda s, g, n: (chunk(s, g, n), v_off + g)),
                  pl.BlockSpec((n_valid, hg * hd), lambda s, g, n: (chunk(s, g, n), g)),
                  pl.BlockSpec((C, LANES), lambda s, g, n: (chunk(s, g, n), 0)),
                  pl.BlockSpec((C, LANES), lambda s, g, n: (chunk(s, g, n), 0)),
                  pl.BlockSpec((1, hg, C, C), lambda s, g, n: (chunk(s, g, n), g, 0, 0)),
                  pl.BlockSpec((1, hg, C, C), lambda s, g, n: (chunk(s, g, n), g, 0, 0)),
                  pl.BlockSpec((1, hg, hd, hd), lambda s, g, n: (s, g, 0, 0)),
                  pl.BlockSpec((1, hd), lambda s, g, n: (0, 0))],
        out_specs=[pl.BlockSpec((n_valid, hg * hd), lambda s, g, n: (chunk(s, g, n), g)),
                   pl.BlockSpec((1, hg, hd, hd), lambda s, g, n: (s, g, 0, 0))],
        out_shape=[jax.ShapeDtypeStruct((n_seq * nc * n_valid, nh * hd), BF16),
                   jax.ShapeDtypeStruct((n_seq, nh, hd, hd), F32)],
        scratch_shapes=[pltpu.VMEM((hg, hd, hd), F32)],
        compiler_params=_cparams("parallel", "parallel", "arbitrary"),
        name="gdn_scan",
    )(qkv_c, qkv_c, qkv_c, z, gates, gcum, tinv, amat, s0, norm_g.reshape(1, hd))


def _trunk(x, p, fox_hist, s5_h0, gdn_state, W, *, n_seq, seq_len, steps, precise, snap_steps=None,
           fox_kv=None):
    M, D = x.shape
    H, hd = W["fox_heads"]
    n_qk, rep, gd = W["gdn_heads"]
    n_vh = n_qk * rep
    key_dim, val_dim = n_qk * gd, n_vh * gd
    qkv_dim = 2 * key_dim + val_dim
    cw = W["gdn_conv_w"].shape[1]
    tn = min(512, D)
    outs = {} if fox_kv is None else {"fox_kv": fox_kv}

    def wsel(name, idx, pr):
        hi = W[name][idx]
        return (hi, W[name + "_lo"][idx]) if pr else hi

    def mixer(i, x):
        j = i // 3
        pm = precise
        g_mix = W["norm_mix"][i]
        if i % 3 == 0:
            qkv, k_heads, v_heads = fox_in_proj(x, g_mix, wsel("fox_w_qkv", j, pm), n_heads=H, hd=hd, tn=tn,
                                                bufs=outs.get("fox_kv"), slot=j, n_slots=W["n_fox"])
            outs["fox_kv"] = (k_heads, v_heads)
            outs[("fox_qkv", j)] = qkv
            logf = small_proj(x, g_mix, W["fox_w_f"][j], W["fox_aux"][j], "logf")[:, :H]
            lf = jnp.swapaxes(logf.reshape(n_seq, seq_len, H), 1, 2)
            if j not in fox_hist:
                tq = next(t for t in (512, 384, 256, 128, seq_len) if seq_len % t == 0)
                fk = lane_cumsum(lf.reshape(n_seq * H, seq_len)).reshape(n_seq * H, seq_len // tq, tq)
                o = fox_attention_prompt(qkv, fk, B=n_seq, L=seq_len, H=H, hd=hd, tq=tq)
            else:
                k_past, v_past, k_lane0, v_lane0, seq0, lf_past, P = fox_hist[j]
                fpad = -(-(P + seq_len) // LANES) * LANES
                lf_all = jnp.concatenate([jnp.swapaxes(lf_past.astype(F32), 1, 2), lf], axis=2)
                lf_all = jnp.pad(lf_all, ((0, 0), (0, 0), (0, fpad - (P + seq_len))))
                f = lane_cumsum(lf_all.reshape(n_seq * H, fpad)).reshape(n_seq * H, 1, fpad)
                o = fox_attention_cached(qkv, k_past, v_past, f, seq0=seq0, k_lane0=k_lane0, v_lane0=v_lane0,
                                         B=n_seq, n=seq_len, P=P, H=H, hd=hd)
            outs[("fox", j)] = logf
            return mm_res(o, wsel("fox_w_out", j, pm), x, tn=tn, tk=D)
        if i % 3 == 1:
            rstd = row_rstd(x).reshape(n_seq, seq_len, LANES)
            y, hr, hi, hr_snap, hi_snap = s5_scan(x.reshape(n_seq, seq_len, D), rstd, g_mix,
                                                  s5_h0[j][0], s5_h0[j][1], W["s5_consts"][j],
                                                  c=min(CHUNK, seq_len), precise=pm, snap_steps=snap_steps)
            outs[("s5", j)] = (hr, hi)
            outs[("s5_snap", j)] = (hr_snap, hi_snap)
            return glu_res(y.reshape(M, D), wsel("s5_w_glu", j, pm), x, tn=tn)
        s0, conv_rows = gdn_state[j]
        nq = qkv_dim // tn
        nz = val_dim // tn
        qkv_raw, z = norm_mm(x, g_mix, W["gdn_w_qkvz"][j], [(F32, 0, nq), (BF16, nq, nq + nz)], tn=tn)
        gates = small_proj(x, g_mix, W["gdn_w_ba"][j], W["gdn_aux"][j], "gdn", nh=n_vh)
        C = CHUNK
        n_valid = min(C, seq_len)
        nc = seq_len // n_valid
        buf = jnp.pad(conv_rows.astype(F32), ((0, 0), (SUBLANES - (cw - 1), 0), (0, 0)))
        tl_in = min(512, seq_len)
        qkv_c = gdn_conv(qkv_raw, buf, W["gdn_conv_w8"][j], n_seq=n_seq, seq_len=seq_len, tl_in=tl_in,
                         tl_out=max(tl_in, C), n_norm_cols=2 * key_dim,
                         tc=key_dim if tl_in < C else min(512, key_dim))
        gates_c = jnp.pad(gates.reshape(n_seq * nc, n_valid, LANES),
                          ((0, 0), (0, C - n_valid), (0, 0))).reshape(n_seq * nc * C, LANES)
        tinv, amat, gcum = gdn_prep(qkv_c, gates_c, C=C, n_qk=n_qk, rep=rep, hd=gd)
        o, s_last = gdn_scan(qkv_c, z, gates_c, gcum, tinv, amat, s0.astype(F32), W["gdn_norm"][j],
                             n_seq=n_seq, nc=nc, C=C, n_valid=n_valid, n_qk=n_qk, rep=rep, hd=gd,
                             hg=min(8, n_vh))
        rows = jnp.concatenate([conv_rows.astype(F32), qkv_raw.reshape(n_seq, seq_len, qkv_dim)], axis=1)
        outs[("gdn", j)] = (s_last, rows[:, -(cw - 1):])
        return mm_res(o, W["gdn_w_out"][j], x, tn=tn, tk=D)

    def channel(i, x):
        m = i // 2
        g_ffn = W["norm_ffn"][i]
        if i % 2 == 0:
            pf = precise
            h = glu_up(x, g_ffn, wsel("ffn_w_gate", m, pf), wsel("ffn_w_up", m, pf), tn=tn)
            x = mm_res(h, wsel("ffn_w_down", m, pf), x, tn=tn, tk=D)
        else:
            pf = False
            comb = small_proj(x, g_ffn, W["moe_w_router"][m], W["zero_aux"], "router")
            h = glu_up(x, g_ffn, W["moe_w_gate"][m], W["moe_w_up"][m], comb, tn=tn)
            x = mm_res(h, W["moe_w_down"][m], x, tn=tn, tk=D)
        p_i = p[i] if pf else p[i].astype(BF16)
        return ple(x, W["norm_ple"][i], wsel("ple_w_gate", i, pf), p_i, wsel("ple_w_up", i, pf), tn=tn)

    for i, part in steps:
        x = mixer(i, x) if part == "mixer" else channel(i, x)
    return x, outs


def kernel(x_prompt, x_sample, cache_fox_k, cache_fox_v, cache_fox_logf, state_s5_re, state_s5_im, state_gdn, state_gdn_conv, p_prompt, p_sample, norm_mix, norm_ffn, norm_ple, norm_final, fox_w_in, fox_b_f, fox_w_out, s5_a_re, s5_a_im, s5_b_re, s5_b_im, s5_c_re, s5_c_im, s5_d, s5_log_dt, s5_w_glu, gdn_w_in, gdn_conv_w, gdn_a_log, gdn_dt_bias, gdn_norm, gdn_w_out, ffn_w_gate, ffn_w_up, ffn_w_down, moe_w_router, moe_w_gate, moe_w_up, moe_w_down, ple_w_up, ple_w_gate):
    B, L, D = x_prompt.shape
    Bs, Ls, _ = x_sample.shape
    depth = norm_mix.shape[0]
    P = cache_fox_k.shape[2]
    H, hd = cache_fox_k.shape[3], cache_fox_k.shape[4]
    n_vh, gd = state_gdn.shape[2], state_gdn.shape[3]
    qkv_dim = state_gdn_conv.shape[3]
    val_dim = n_vh * gd
    n_qk = (qkv_dim - val_dim) // (2 * gd)
    cw = gdn_conv_w.shape[1]
    G, NS = s5_a_re.shape[1], s5_a_re.shape[2]
    E, _, Fe = moe_w_gate.shape[1:]
    n_fox, n_s5, n_gdn = fox_w_in.shape[0], s5_a_re.shape[0], gdn_w_in.shape[0]

    first_router = 1
    n_lo = {"fox": 1, "ffn": 1, "ple": 1, "s5": 1}

    def hi_lo(w, n):
        return w.astype(BF16), _split_weight(w[:n])[1]

    W = {"norm_mix": norm_mix, "norm_ffn": norm_ffn, "norm_ple": norm_ple, "norm_final": norm_final,
         "fox_heads": (H, hd), "n_fox": n_fox, "gdn_heads": (n_qk, n_vh // n_qk, gd), "gdn_conv_w": gdn_conv_w,
         "gdn_norm": gdn_norm, "zero_aux": jnp.zeros((SUBLANES, LANES), F32)}
    W["fox_w_qkv"], W["fox_w_qkv_lo"] = hi_lo(fox_w_in[:, :, :3 * D], n_lo["fox"])
    W["fox_w_out"], W["fox_w_out_lo"] = hi_lo(fox_w_out, n_lo["fox"])
    W["fox_w_f"] = [_pad_cols(fox_w_in[j, :, 3 * D:]) for j in range(n_fox)]
    W["fox_aux"] = [_aux_rows((0, fox_b_f[j])) for j in range(n_fox)]
    W["s5_w_glu"], W["s5_w_glu_lo"] = hi_lo(s5_w_glu, n_lo["s5"])
    W["s5_consts"] = [s5_constants(s5_a_re[j], s5_a_im[j], s5_b_re[j], s5_b_im[j], s5_c_re[j], s5_c_im[j],
                                   s5_d[j], s5_log_dt[j]) for j in range(n_s5)]
    W["gdn_w_qkvz"] = gdn_w_in[:, :, :qkv_dim + val_dim].astype(BF16)
    W["gdn_w_ba"] = [_pad_cols(gdn_w_in[j, :, qkv_dim + val_dim:]) for j in range(n_gdn)]
    W["gdn_aux"] = [_aux_rows((n_vh, gdn_dt_bias[j]), (n_vh, gdn_a_log[j])) for j in range(n_gdn)]
    W["gdn_conv_w8"] = jnp.pad(gdn_conv_w.astype(F32), ((0, 0), (0, SUBLANES - cw), (0, 0)))
    W["gdn_w_out"] = gdn_w_out.astype(BF16)
    W["ffn_w_gate"], W["ffn_w_gate_lo"] = hi_lo(ffn_w_gate, n_lo["ffn"])
    W["ffn_w_up"], W["ffn_w_up_lo"] = hi_lo(ffn_w_up, n_lo["ffn"])
    W["ffn_w_down"], W["ffn_w_down_lo"] = hi_lo(ffn_w_down, n_lo["ffn"])
    W["moe_w_router"] = [_pad_cols(moe_w_router[m]) for m in range(moe_w_router.shape[0])]
    W["moe_w_gate"] = moe_w_gate.astype(BF16)
    W["moe_w_up"] = moe_w_up.astype(BF16)
    W["moe_w_down"] = moe_w_down.astype(BF16).reshape(-1, E * Fe, D)
    W["ple_w_gate"], W["ple_w_gate_lo"] = hi_lo(ple_w_gate, n_lo["ple"])
    W["ple_w_up"], W["ple_w_up_lo"] = hi_lo(ple_w_up, n_lo["ple"])

    steps = [(i, part) for i in range(depth) for part in ("mixer", "channel")]
    steps_a, steps_b = steps[:2 * first_router + 1], steps[2 * first_router + 1:]
    pd = p_prompt.shape[-1]

    Lt = LANES
    Lh = L - Lt
    p_all = {i: p_prompt[i].reshape(B * L, pd) for i in range(depth)}
    p_tail = {i: p_prompt[i][:, Lh:].reshape(B * Lt, pd) for i in range(depth)}
    zeros_s5 = {j: (jnp.zeros((B, G * NS), F32),) * 2 for j in range(n_s5)}
    x_f, out_p = _trunk(x_prompt.reshape(B * L, D), p_all, {}, zeros_s5, {}, W,
                        n_seq=B, seq_len=L, steps=steps_a, precise=False, snap_steps=Lh)
    hist_t = {}
    for j in range(n_fox):
        if ("fox", j) in out_p:
            qkv_f = out_p[("fox_qkv", j)].reshape(B, L, 3 * D)
            hist_t[j] = (qkv_f, qkv_f, H, 2 * H, 0, out_p[("fox", j)].reshape(B, L, H)[:, :Lh], Lh)
    xt, out_t = _trunk(x_prompt[:, Lh:].reshape(B * Lt, D), p_tail, hist_t,
                       {j: out_p[("s5_snap", j)] for j in range(n_s5) if ("s5_snap", j) in out_p}, {}, W,
                       n_seq=B, seq_len=Lt, steps=steps_a, precise=True)
    x_p = lax.dynamic_update_slice(x_f.reshape(B, L, D), xt.reshape(B, Lt, D), (0, Lh, 0)).reshape(B * L, D)
    out_p.update({key: val for key, val in out_t.items() if key[0] == "s5"})
    gdn0 = (jnp.zeros((B, n_vh, gd, gd), F32), jnp.zeros((B, cw - 1, qkv_dim), F32))
    x_p, out_b = _trunk(x_p, p_all, {}, zeros_s5, {j: gdn0 for j in range(n_gdn)}, W,
                        n_seq=B, seq_len=L, steps=steps_b, precise=False, fox_kv=out_p.get("fox_kv"))
    out_p.update(out_b)

    ck = cache_fox_k.reshape(-1, P * H, hd)
    cv = cache_fox_v.reshape(-1, P * H, hd)
    s5_re = state_s5_re.reshape(n_s5, Bs, G * NS).astype(F32)
    s5_im = state_s5_im.reshape(n_s5, Bs, G * NS).astype(F32)
    hist_s = {j: (ck, cv, None, None, j * Bs, cache_fox_logf[j], P) for j in range(n_fox)}
    state_s = ({j: (s5_re[j], s5_im[j]) for j in range(n_s5)},
               {j: (state_gdn[j], state_gdn_conv[j]) for j in range(n_gdn)})
    p_s = {i: p_sample[i].reshape(Bs * Ls, pd) for i in range(depth)}
    x_s, out_s = _trunk(x_sample.reshape(Bs * Ls, D), p_s, hist_s, *state_s, W,
                        n_seq=Bs, seq_len=Ls, steps=steps_a, precise=True)
    x_s, out_sb = _trunk(x_s, p_s, hist_s, *state_s, W, n_seq=Bs, seq_len=Ls, steps=steps_b, precise=False,
                         fox_kv=out_s.get("fox_kv"))
    out_s.update(out_sb)

    def leaves(x, out, nb, sl):
        y = rmsnorm(x, norm_final).reshape(nb, sl, D)
        fk, fv = (a.reshape(n_fox, nb, sl, H, hd) for a in out["fox_kv"])
        flf = jnp.stack([out[("fox", j)].reshape(nb, sl, H) for j in range(n_fox)])
        sre = jnp.stack([out[("s5", j)][0].reshape(nb, G, NS) for j in range(n_s5)])
        sim = jnp.stack([out[("s5", j)][1].reshape(nb, G, NS) for j in range(n_s5)])
        gs = jnp.stack([out[("gdn", j)][0] for j in range(n_gdn)])
        gc = jnp.stack([out[("gdn", j)][1] for j in range(n_gdn)])
        return y, fk, fv, flf, sre, sim, gs, gc

    (y_p, fk_p, fv_p, flf_p, sre_p, sim_p, gs_p, gc_p) = leaves(x_p, out_p, B, L)
    (y_s, fk_s, fv_s, flf_s, sre_s, sim_s, gs_s, gc_s) = leaves(x_s, out_s, Bs, Ls)
    return (y_p, y_s, fk_p, fv_p, flf_p, fk_s, fv_s, flf_s, sre_p, sim_p, sre_s, sim_s,
            gs_p, gc_p, gs_s, gc_s)
```

```python
import functools
import math

import jax
import jax.numpy as jnp
from jax import lax
from jax.experimental import pallas as pl
from jax.experimental.pallas import tpu as pltpu

F32 = jnp.float32
BF16 = jnp.bfloat16
EPS = 1e-6
LANES = 128
SUBLANES = 8
VMEM_LIMIT = 56 * 1024 * 1024
CHUNK = 64
N_EXPERTS = 8
NEG_INF = float("-inf")
LOG2E = math.log2(math.e)


def _cparams(*sem):
    return pltpu.CompilerParams(dimension_semantics=sem, vmem_limit_bytes=VMEM_LIMIT)


def _rms_scale(x, g):
    ms = jnp.mean(x * x, axis=-1, keepdims=True)
    return x * lax.rsqrt(ms + EPS) * g


def _sigmoid(x):
    return 0.5 * jnp.tanh(0.5 * x) + 0.5


def _silu(x):
    return x * _sigmoid(x)


def _softplus(x):
    return jnp.maximum(x, 0.0) + jnp.log1p(jnp.exp(-jnp.abs(x)))


def _split_bf16(x):
    hi = x.astype(BF16)
    lo = (x - hi.astype(F32)).astype(BF16)
    return hi, lo


def _split_weight(w):
    w = w.astype(F32)
    hi = lax.reduce_precision(w, exponent_bits=8, mantissa_bits=7)
    return hi.astype(BF16), (w - hi).astype(BF16)


def _dot(a, b):
    return jnp.dot(a, b, preferred_element_type=F32)


def _dot_nt(a, b):
    return lax.dot_general(a, b, (((1,), (1,)), ((), ())), preferred_element_type=F32)


def _dot_tn(a, b):
    return lax.dot_general(a, b, (((0,), (0,)), ((), ())), preferred_element_type=F32)


def _parts(x, precise):
    return _split_bf16(x.astype(F32)) if precise else (x.astype(BF16),)


def _mm(xp, wp):
    if len(wp) == 1:
        return _dot(xp[0], wp[0])
    return _dot(xp[0], wp[0]) + _dot(xp[1], wp[0]) + _dot(xp[0], wp[1])


def _mm_nt(ap, bp):
    if len(ap) == 1:
        return _dot_nt(ap[0], bp[0])
    return _dot_nt(ap[0], bp[0]) + _dot_nt(ap[1], bp[0]) + _dot_nt(ap[0], bp[1])


def _read(refs):
    return tuple(r[...] for r in refs)


def _wparts(w):
    return tuple(w) if isinstance(w, (tuple, list)) else (w,)


def _row_tile(m, n_weight_parts=1):
    cap = 1024 // n_weight_parts
    return next(t for t in (1024, 512, 256, 128) if t <= cap and m % t == 0)


def _store_norm(x_ref, g_ref, xn_refs):
    xn = _rms_scale(x_ref[...], g_ref[...])
    for ref, part in zip(xn_refs, _parts(xn, len(xn_refs) == 2)):
        ref[...] = part


def _norm_mm_kernel(*refs, ranges, nw):
    x_ref, g_ref = refs[:2]
    w_refs = refs[2:2 + nw]
    outs = refs[2 + nw:2 + nw + len(ranges)]
    xn_refs = refs[2 + nw + len(ranges):]
    j = pl.program_id(1)

    @pl.when(j == 0)
    def _():
        _store_norm(x_ref, g_ref, xn_refs)

    acc = _mm(_read(xn_refs), _read(w_refs))
    for o_ref, (a, b) in zip(outs, ranges):
        @pl.when((j >= a) & (j < b))
        def _(o_ref=o_ref):
            o_ref[...] = acc.astype(o_ref.dtype)


def _range_map(i, j, *, a, n):
    return (i, jnp.clip(j - a, 0, n - 1))


def norm_mm(x, g, w, outs, *, tn):
    M, D = x.shape
    wp = _wparts(w)
    N = wp[0].shape[1]
    tm = _row_tile(M, len(wp))
    ranges = tuple((a, b) for _, a, b in outs)
    return pl.pallas_call(
        functools.partial(_norm_mm_kernel, ranges=ranges, nw=len(wp)),
        grid=(M // tm, N // tn),
        in_specs=[pl.BlockSpec((tm, D), lambda i, j: (i, 0)),
                  pl.BlockSpec((1, D), lambda i, j: (0, 0))]
                 + [pl.BlockSpec((D, tn), lambda i, j: (0, j))] * len(wp),
        out_specs=[pl.BlockSpec((tm, tn), functools.partial(_range_map, a=a, n=b - a))
                   for _, a, b in outs],
        out_shape=[jax.ShapeDtypeStruct((M, (b - a) * tn), dt) for dt, a, b in outs],
        scratch_shapes=[pltpu.VMEM((tm, D), BF16)] * len(wp),
        compiler_params=_cparams("parallel", "arbitrary"),
        name="norm_mm",
    )(x, g.reshape(1, D), *wp)


def _fox_in_kernel(*refs, nw, nt, n_heads, hd, tm, n_bufs):
    x_ref, g_ref = refs[:2]
    w_refs = refs[2:2 + nw]
    qkv_ref, k_ref, v_ref = refs[2 + nw + n_bufs:5 + nw + n_bufs]
    xn_refs = refs[5 + nw + n_bufs:]
    j = pl.program_id(1)

    @pl.when(j == 0)
    def _():
        _store_norm(x_ref, g_ref, xn_refs)

    acc = _mm(_read(xn_refs), _read(w_refs))
    qkv_ref[...] = acc.astype(qkv_ref.dtype)
    hpt = acc.shape[1] // hd
    for dst, base in ((k_ref, nt), (v_ref, 2 * nt)):
        for jj in range(nt):
            @pl.when(j == base + jj)
            def _(dst=dst, jj=jj):
                for hh in range(hpt):
                    dst[pl.ds(jj * hpt + hh, tm, stride=n_heads), :] = acc[:, hh * hd:(hh + 1) * hd]


def fox_in_proj(x, g, w, *, n_heads, hd, tn, bufs=None, slot=0, n_slots=1):
    M, D = x.shape
    wp = _wparts(w)
    nw = len(wp)
    tm = min(512, M)
    nt = D // tn
    n_bufs = 0 if bufs is None else 2
    blk0 = slot * (M // tm)
    head_spec = pl.BlockSpec((tm * n_heads, hd), lambda i, j: (blk0 + i, 0))
    head_shape = jax.ShapeDtypeStruct((n_slots * M * n_heads, hd), F32)
    return pl.pallas_call(
        functools.partial(_fox_in_kernel, nw=nw, nt=nt, n_heads=n_heads, hd=hd, tm=tm, n_bufs=n_bufs),
        grid=(M // tm, 3 * nt),
        in_specs=[pl.BlockSpec((tm, D), lambda i, j: (i, 0)),
                  pl.BlockSpec((1, D), lambda i, j: (0, 0))]
                 + [pl.BlockSpec((D, tn), lambda i, j: (0, j))] * nw
                 + [pl.BlockSpec(memory_space=pl.ANY)] * n_bufs,
        out_specs=[pl.BlockSpec((tm, tn), lambda i, j: (i, j)), head_spec, head_spec],
        out_shape=[jax.ShapeDtypeStruct((M, 3 * D), F32 if nw == 2 else BF16), head_shape, head_shape],
        scratch_shapes=[pltpu.VMEM((tm, D), BF16)] * nw,
        input_output_aliases={} if bufs is None else {2 + nw: 1, 3 + nw: 2},
        compiler_params=_cparams("parallel", "arbitrary"),
        name="fox_in_proj",
    )(x, g.reshape(1, D), *wp, *(bufs or ()))


def _glu_up_kernel(*refs, tiles_per_expert, nw):
    x_ref, g_ref = refs[:2]
    wg_refs = refs[2:2 + nw]
    wu_refs = refs[2 + nw:2 + 2 * nw]
    rest = refs[2 + 2 * nw:]
    comb_ref = rest[0] if tiles_per_expert else None
    o_ref = rest[1] if tiles_per_expert else rest[0]
    xn_refs = rest[2:] if tiles_per_expert else rest[1:]
    j = pl.program_id(1)

    @pl.when(j == 0)
    def _():
        _store_norm(x_ref, g_ref, xn_refs)

    xn = _read(xn_refs)
    h = _silu(_mm(xn, _read(wg_refs))) * _mm(xn, _read(wu_refs))
    if tiles_per_expert:
        comb = comb_ref[...]
        lane = lax.broadcasted_iota(jnp.int32, comb.shape, 1)
        col = jnp.sum(jnp.where(lane == j // tiles_per_expert, comb, 0.0), axis=1, keepdims=True)
        h = h * col
    o_ref[...] = h.astype(o_ref.dtype)


def glu_up(x, g, w_gate, w_up, comb=None, *, tn):
    M, D = x.shape
    wg, wu = _wparts(w_gate), _wparts(w_up)
    nw = len(wg)
    tm = _row_tile(M, nw)
    if comb is None:
        n_out, tpe = wg[0].shape[1], 0
        w_spec = pl.BlockSpec((D, tn), lambda i, j: (0, j))
        extra_in, extra_specs = (), []
    else:
        E, _, F = wg[0].shape
        n_out, tpe = E * F, F // tn
        w_spec = pl.BlockSpec((None, D, tn), lambda i, j: (j // tpe, 0, j % tpe))
        extra_in = (comb,)
        extra_specs = [pl.BlockSpec((tm, LANES), lambda i, j: (i, 0))]
    return pl.pallas_call(
        functools.partial(_glu_up_kernel, tiles_per_expert=tpe, nw=nw),
        grid=(M // tm, n_out // tn),
        in_specs=[pl.BlockSpec((tm, D), lambda i, j: (i, 0)),
                  pl.BlockSpec((1, D), lambda i, j: (0, 0))] + [w_spec] * (2 * nw) + extra_specs,
        out_specs=pl.BlockSpec((tm, tn), lambda i, j: (i, j)),
        out_shape=jax.ShapeDtypeStruct((M, n_out), F32 if nw == 2 else BF16),
        scratch_shapes=[pltpu.VMEM((tm, D), BF16)] * nw,
        compiler_params=_cparams("parallel", "arbitrary"),
        name="glu_up",
    )(x, g.reshape(1, D), *wg, *wu, *extra_in)


def _mm_res_kernel(*refs, nk, nw):
    x_ref = refs[0]
    w_refs = refs[1:1 + nw]
    r_ref, o_ref = refs[1 + nw:]
    k = pl.program_id(2)
    prod = _mm(_parts(x_ref[...], nw == 2), _read(w_refs))

    @pl.when(k == 0)
    def _():
        o_ref[...] = r_ref[...] + prod

    if nk > 1:
        @pl.when(k > 0)
        def _():
            o_ref[...] += prod


def mm_res(x, w, res, *, tn):
    M, K = x.shape
    wp = _wparts(w)
    N = wp[0].shape[1]
    tm = _row_tile(M, len(wp))
    tk = min(K, 4096 // len(wp))
    nk = K // tk
    return pl.pallas_call(
        functools.partial(_mm_res_kernel, nk=nk, nw=len(wp)),
        grid=(M // tm, N // tn, nk),
        in_specs=[pl.BlockSpec((tm, tk), lambda i, j, k: (i, k))]
                 + [pl.BlockSpec((tk, tn), lambda i, j, k: (k, j))] * len(wp)
                 + [pl.BlockSpec((tm, tn), lambda i, j, k: (i, j))],
        out_specs=pl.BlockSpec((tm, tn), lambda i, j, k: (i, j)),
        out_shape=jax.ShapeDtypeStruct((M, N), F32),
        compiler_params=_cparams("parallel", "parallel", "arbitrary"),
        name="mm_res",
    )(x, *wp, res)


def _glu_res_kernel(*refs, nw):
    y_ref = refs[0]
    wa_refs = refs[1:1 + nw]
    wb_refs = refs[1 + nw:1 + 2 * nw]
    r_ref, o_ref = refs[1 + 2 * nw:]
    y = _parts(y_ref[...], nw == 2)
    o_ref[...] = r_ref[...] + _mm(y, _read(wa_refs)) * _sigmoid(_mm(y, _read(wb_refs)))


def glu_res(y, w, res, *, tn):
    M, K = y.shape
    wp = _wparts(w)
    N = wp[0].shape[1] // 2
    nj = N // tn
    tm = _row_tile(M, len(wp))
    return pl.pallas_call(
        functools.partial(_glu_res_kernel, nw=len(wp)),
        grid=(M // tm, nj),
        in_specs=[pl.BlockSpec((tm, K), lambda i, j: (i, 0))]
                 + [pl.BlockSpec((K, tn), lambda i, j: (0, j))] * len(wp)
                 + [pl.BlockSpec((K, tn), lambda i, j: (0, j + nj))] * len(wp)
                 + [pl.BlockSpec((tm, tn), lambda i, j: (i, j))],
        out_specs=pl.BlockSpec((tm, tn), lambda i, j: (i, j)),
        out_shape=jax.ShapeDtypeStruct((M, N), F32),
        compiler_params=_cparams("parallel", "arbitrary"),
        name="glu_res",
    )(y, *wp, *wp, res)


def _ple_kernel(*refs, tn, nw):
    x_ref, g_ref = refs[:2]
    wg_refs = refs[2:2 + nw]
    p_ref = refs[2 + nw]
    wu_refs = refs[3 + nw:3 + 2 * nw]
    o_ref = refs[3 + 2 * nw]
    xn_refs = refs[4 + 2 * nw:]
    j = pl.program_id(1)

    @pl.when(j == 0)
    def _():
        _store_norm(x_ref, g_ref, xn_refs)

    gate = _sigmoid(_mm(_read(xn_refs), _read(wg_refs)))
    up = _mm(_parts(p_ref[...], nw == 2), _read(wu_refs))
    x_tile = x_ref[:, pl.ds(pl.multiple_of(j * tn, tn), tn)]
    o_ref[...] = x_tile + up * gate


def ple(x, g, w_gate, p, w_up, *, tn):
    M, D = x.shape
    P = p.shape[1]
    wg, wu = _wparts(w_gate), _wparts(w_up)
    nw = len(wg)
    tm = _row_tile(M, nw)
    return pl.pallas_call(
        functools.partial(_ple_kernel, tn=tn, nw=nw),
        grid=(M // tm, D // tn),
        in_specs=[pl.BlockSpec((tm, D), lambda i, j: (i, 0)),
                  pl.BlockSpec((1, D), lambda i, j: (0, 0))]
                 + [pl.BlockSpec((D, tn), lambda i, j: (0, j))] * nw
                 + [pl.BlockSpec((tm, P), lambda i, j: (i, 0))]
                 + [pl.BlockSpec((P, tn), lambda i, j: (0, j))] * nw,
        out_specs=pl.BlockSpec((tm, tn), lambda i, j: (i, j)),
        out_shape=jax.ShapeDtypeStruct((M, D), F32),
        scratch_shapes=[pltpu.VMEM((tm, D), BF16)] * nw,
        compiler_params=_cparams("parallel", "arbitrary"),
        name="ple",
    )(x, g.reshape(1, D), *wg, p, *wu)


def _rmsnorm_kernel(x_ref, g_ref, o_ref):
    o_ref[...] = _rms_scale(x_ref[...], g_ref[...]).astype(o_ref.dtype)


def rmsnorm(x, g):
    M, D = x.shape
    tm = _row_tile(M)
    return pl.pallas_call(
        _rmsnorm_kernel,
        grid=(M // tm,),
        in_specs=[pl.BlockSpec((tm, D), lambda i: (i, 0)),
                  pl.BlockSpec((1, D), lambda i: (0, 0))],
        out_specs=pl.BlockSpec((tm, D), lambda i: (i, 0)),
        out_shape=jax.ShapeDtypeStruct((M, D), F32),
        compiler_params=_cparams("parallel"),
        name="rmsnorm",
    )(x, g.reshape(1, D))


def _row_rstd_kernel(x_ref, o_ref):
    x = x_ref[...]
    rstd = lax.rsqrt(jnp.mean(x * x, axis=-1, keepdims=True) + EPS)
    o_ref[...] = jnp.broadcast_to(rstd, o_ref.shape)


def row_rstd(x):
    M, D = x.shape
    tm = _row_tile(M)
    return pl.pallas_call(
        _row_rstd_kernel,
        grid=(M // tm,),
        in_specs=[pl.BlockSpec((tm, D), lambda i: (i, 0))],
        out_specs=pl.BlockSpec((tm, LANES), lambda i: (i, 0)),
        out_shape=jax.ShapeDtypeStruct((M, LANES), F32),
        compiler_params=_cparams("parallel"),
        name="row_rstd",
    )(x)


def _small_proj_kernel(x_ref, g_ref, whi_ref, wlo_ref, aux_ref, o_ref, *, mode, nh):
    xn = _rms_scale(x_ref[...], g_ref[...])
    acc = _mm(_parts(xn, True), (whi_ref[...], wlo_ref[...]))
    lane = lax.broadcasted_iota(jnp.int32, acc.shape, 1)
    if mode == "logf":
        o_ref[...] = -_softplus(-(acc + aux_ref[0:1, :]))
    elif mode == "gdn":
        beta = _sigmoid(acc)
        gdec = -jnp.exp(aux_ref[1:2, :]) * _softplus(acc + aux_ref[0:1, :])
        o_ref[...] = jnp.where(lane < nh, beta, gdec)
    else:
        logits = jnp.where(lane < N_EXPERTS, acc, NEG_INF)
        v1 = jnp.max(logits, axis=1, keepdims=True)
        i1 = jnp.min(jnp.where(logits == v1, lane, LANES), axis=1, keepdims=True)
        rest = jnp.where(lane == i1, NEG_INF, logits)
        v2 = jnp.max(rest, axis=1, keepdims=True)
        i2 = jnp.min(jnp.where(rest == v2, lane, LANES), axis=1, keepdims=True)
        e2 = jnp.exp(v2 - v1)
        g1 = 1.0 / (1.0 + e2)
        o_ref[...] = jnp.where(lane == i1, g1, 0.0) + jnp.where(lane == i2, e2 * g1, 0.0)


def small_proj(x, g, w_parts, aux, mode, *, nh=0):
    M, D = x.shape
    tm = _row_tile(M)
    return pl.pallas_call(
        functools.partial(_small_proj_kernel, mode=mode, nh=nh),
        grid=(M // tm,),
        in_specs=[pl.BlockSpec((tm, D), lambda i: (i, 0)),
                  pl.BlockSpec((1, D), lambda i: (0, 0)),
                  pl.BlockSpec((D, LANES), lambda i: (0, 0)),
                  pl.BlockSpec((D, LANES), lambda i: (0, 0)),
                  pl.BlockSpec((SUBLANES, LANES), lambda i: (0, 0))],
        out_specs=pl.BlockSpec((tm, LANES), lambda i: (i, 0)),
        out_shape=jax.ShapeDtypeStruct((M, LANES), F32),
        compiler_params=_cparams("parallel"),
        name="small_proj_" + mode,
    )(x, g.reshape(1, D), *w_parts, aux)


def _pad_cols(w):
    return _split_weight(jnp.pad(w.astype(F32), ((0, 0), (0, LANES - w.shape[1]))))


def _aux_rows(*rows):
    out = [jnp.pad(vals.astype(F32), (start, LANES - start - vals.shape[0])) for start, vals in rows]
    out += [jnp.zeros((LANES,), F32)] * (SUBLANES - len(out))
    return jnp.stack(out)


def _lane_cumsum_kernel(x_ref, o_ref):
    x = x_ref[...]
    n = x.shape[1]
    lane = lax.broadcasted_iota(jnp.int32, x.shape, 1)
    shift = 1
    while shift < n:
        x = x + jnp.where(lane >= shift, pltpu.roll(x, shift, 1), 0.0)
        shift *= 2
    o_ref[...] = x


def lane_cumsum(x):
    R, n = x.shape
    rb = SUBLANES if R % SUBLANES == 0 else R
    return pl.pallas_call(
        _lane_cumsum_kernel,
        grid=(R // rb,),
        in_specs=[pl.BlockSpec((rb, n), lambda i: (i, 0))],
        out_specs=pl.BlockSpec((rb, n), lambda i: (i, 0)),
        out_shape=jax.ShapeDtypeStruct((R, n), F32),
        compiler_params=_cparams("parallel"),
        name="lane_cumsum",
    )(x)


def _fox_prompt_kernel(q_ref, k_ref, v_ref, fk_ref, o_ref, *, tq, nhb, hd, scale):
    qi = pl.program_id(2)

    def step(ki, carry, diagonal):
        rows = pl.ds(pl.multiple_of(ki * tq, tq), tq)
        out = []
        for h in range(nhb):
            m, l, acc = carry[h]
            cols = slice(h * hd, (h + 1) * hd)
            s = (_dot_nt(q_ref[:, cols], k_ref[rows, cols]) * (scale * LOG2E)
                 - fk_ref[h, pl.ds(ki, 1), :] * LOG2E)
            if diagonal:
                r = lax.broadcasted_iota(jnp.int32, s.shape, 0)
                c = lax.broadcasted_iota(jnp.int32, s.shape, 1)
                s = jnp.where(r >= c, s, NEG_INF)
            m_new = jnp.maximum(m, jnp.max(s, axis=1, keepdims=True))
            p = jnp.exp2(s - m_new)
            alpha = jnp.exp2(m - m_new)
            l = alpha * l + jnp.sum(p, axis=1, keepdims=True)
            acc = alpha * acc + _dot(p.astype(BF16), v_ref[rows, cols])
            out.append((m_new, l, acc))
        return tuple(out)

    init = tuple((jnp.full((tq, 1), NEG_INF, F32), jnp.zeros((tq, 1), F32), jnp.zeros((tq, hd), F32))
                 for _ in range(nhb))
    carry = lax.fori_loop(0, qi, lambda ki, c: step(ki, c, False), init)
    carry = step(qi, carry, True)
    o_ref[...] = jnp.concatenate([acc / l for _, l, acc in carry], axis=1).astype(o_ref.dtype)


def fox_attention_prompt(qkv, fk, *, B, L, H, hd, tq, nhb=2):
    nq = L // tq
    nhg = H // nhb
    return pl.pallas_call(
        functools.partial(_fox_prompt_kernel, tq=tq, nhb=nhb, hd=hd, scale=hd ** -0.5),
        grid=(B, nhg, nq),
        in_specs=[pl.BlockSpec((tq, nhb * hd), lambda b, g, i: (b * nq + i, g)),
                  pl.BlockSpec((L, nhb * hd), lambda b, g, i: (b, nhg + g)),
                  pl.BlockSpec((L, nhb * hd), lambda b, g, i: (b, 2 * nhg + g)),
                  pl.BlockSpec((nhb, nq, tq), lambda b, g, i: (b * nhg + g, 0, 0))],
        out_specs=pl.BlockSpec((tq, nhb * hd), lambda b, g, i: (b * nq + i, g)),
        out_shape=jax.ShapeDtypeStruct((B * L, H * hd), BF16),
        compiler_params=_cparams("parallel", "parallel", "arbitrary"),
        name="fox_attention_prompt",
    )(qkv, qkv, qkv, fk)


def _fox_cached_kernel(q_ref, kn_ref, vn_ref, kp_ref, vp_ref, f_ref, o_ref, *, P, nhb, hd, scale, precise,
                       head_rows):
    n = q_ref.shape[0]
    r = lax.broadcasted_iota(jnp.int32, (n, n), 0)
    c = lax.broadcasted_iota(jnp.int32, (n, n), 1)
    g = pl.program_id(1)

    def past(ref, h):
        if head_rows:
            return ref[pl.ds(g * nhb + h, P, stride=head_rows), :]
        return ref[:, h * hd:(h + 1) * hd]

    outs = []
    for h in range(nhb):
        cols = slice(h * hd, (h + 1) * hd)
        q = _parts(q_ref[:, cols], precise)
        f = f_ref[h]
        s_past = _mm_nt(q, _parts(past(kp_ref, h), precise)) * scale - f[:, :P]
        s_new = _mm_nt(q, _parts(kn_ref[:, cols], precise)) * scale - f[:, P:P + n]
        s_new = jnp.where(r >= c, s_new, NEG_INF)
        m = jnp.maximum(jnp.max(s_past, axis=1, keepdims=True), jnp.max(s_new, axis=1, keepdims=True))
        p_past = jnp.exp(s_past - m)
        p_new = jnp.exp(s_new - m)
        l = jnp.sum(p_past, axis=1, keepdims=True) + jnp.sum(p_new, axis=1, keepdims=True)
        acc = (_mm(_parts(p_past, precise), _parts(past(vp_ref, h), precise))
               + _mm(_parts(p_new, precise), _parts(vn_ref[:, cols], precise)))
        outs.append(acc / l)
    o_ref[...] = jnp.concatenate(outs, axis=1).astype(o_ref.dtype)


def fox_attention_cached(qkv, k_past, v_past, f, *, seq0, k_lane0, v_lane0, B, n, P, H, hd):
    precise = qkv.dtype == F32
    nhb = 4 if H % 4 == 0 else 1
    nhg = H // nhb
    w = nhb * hd
    head_rows = H if k_lane0 is None else 0
    if head_rows:
        past_specs = [pl.BlockSpec((None, P * H, hd), lambda b, g: (seq0 + b, 0, 0))] * 2
    else:
        past_specs = [pl.BlockSpec((None, P, w), lambda b, g: (seq0 + b, 0, k_lane0 // nhb + g)),
                      pl.BlockSpec((None, P, w), lambda b, g: (seq0 + b, 0, v_lane0 // nhb + g))]
    return pl.pallas_call(
        functools.partial(_fox_cached_kernel, P=P, nhb=nhb, hd=hd, scale=hd ** -0.5, precise=precise,
                          head_rows=head_rows),
        grid=(B, nhg),
        in_specs=[pl.BlockSpec((n, w), lambda b, g: (b, g)),
                  pl.BlockSpec((n, w), lambda b, g: (b, nhg + g)),
                  pl.BlockSpec((n, w), lambda b, g: (b, 2 * nhg + g))]
                 + past_specs
                 + [pl.BlockSpec((nhb, 1, f.shape[2]), lambda b, g: (b * nhg + g, 0, 0))],
        out_specs=pl.BlockSpec((n, w), lambda b, g: (b, g)),
        out_shape=jax.ShapeDtypeStruct((B * n, H * hd), qkv.dtype),
        compiler_params=_cparams("parallel", "arbitrary"),
        name="fox_attention_cached",
    )(qkv, qkv, qkv, k_past, v_past, f)


def _gelu_tanh(x):
    return 0.5 * x * (1.0 + jnp.tanh(math.sqrt(2.0 / math.pi) * (x + 0.044715 * (x * x * x))))


def _s5_kernel(*refs, nb, c, ns, ncm, snap_chunk):
    x_ref, rstd_ref, g_ref, bhi_ref, blo_ref = refs[:5]
    cm_refs = refs[5:5 + ncm]
    (ar_ref, ai_ref, d_ref, h0r_ref, h0i_ref, y_ref, hr_out, hi_out, hr_snap, hi_snap,
     u_sc, y_sc, bu_sc, st_sc, hr_sc, hi_sc) = refs[5 + ncm:]
    tc = pl.program_id(1)

    @pl.when(tc == 0)
    def _():
        hr_sc[...] = h0r_ref[...]
        hi_sc[...] = h0i_ref[...]

    g = g_ref[...]
    for b in range(nb):
        u_sc[pl.ds(b, c, stride=nb), :] = x_ref[b] * rstd_ref[b] * g
    u = u_sc[...]
    u_hi, u_lo = _split_bf16(u)
    bu_sc[...] = _dot(jnp.concatenate([u_hi, u_lo], axis=1), bhi_ref[0]) + _dot(u_hi, blo_ref[0])
    ar = jnp.broadcast_to(ar_ref[0], (nb, ns))
    ai = jnp.broadcast_to(ai_ref[0], (nb, ns))

    def step(t, carry):
        hr, hi = carry
        rows = pl.ds(pl.multiple_of(t * nb, nb), nb)
        nhr = ar * hr - ai * hi + bu_sc[rows, 0:ns]
        nhi = ar * hi + ai * hr + bu_sc[rows, ns:2 * ns]
        st_sc[rows, 0:ns] = nhr
        st_sc[rows, ns:2 * ns] = nhi
        return nhr, nhi

    hr, hi = lax.fori_loop(0, c, step, (hr_sc[...], hi_sc[...]))
    hr_sc[...] = hr
    hi_sc[...] = hi
    y = _mm(_parts(st_sc[...], ncm == 2), tuple(r[0] for r in cm_refs)) + d_ref[...] * u
    y_sc[...] = _gelu_tanh(y)
    for b in range(nb):
        y_ref[b] = y_sc[pl.ds(b, c, stride=nb), :].astype(y_ref.dtype)

    @pl.when(tc == snap_chunk)
    def _():
        hr_snap[...] = hr
        hi_snap[...] = hi

    @pl.when(tc == pl.num_programs(1) - 1)
    def _():
        hr_out[...] = hr
        hi_out[...] = hi


def s5_scan(x, rstd, g, h0r, h0i, consts, *, c, precise, snap_steps=None):
    bhi, blo, cm_hi, cm_lo, ar, ai, d = consts
    cms = (cm_hi, cm_lo) if precise else (cm_hi,)
    nb, L, D = x.shape
    nblk = D // LANES
    ns = ar.shape[2]
    ntc = L // c
    snap_chunk = ntc - 1 if snap_steps is None else snap_steps // c - 1
    blk = lambda g, t: (g, 0, 0)
    state_spec = pl.BlockSpec((nb, ns), lambda g, t: (0, g))
    state_shape = jax.ShapeDtypeStruct(h0r.shape, F32)
    return pl.pallas_call(
        functools.partial(_s5_kernel, nb=nb, c=c, ns=ns, ncm=len(cms), snap_chunk=snap_chunk),
        grid=(nblk, ntc),
        in_specs=[pl.BlockSpec((nb, c, LANES), lambda g, t: (0, t, g)),
                  pl.BlockSpec((nb, c, LANES), lambda g, t: (0, t, 0)),
                  pl.BlockSpec((1, LANES), lambda g, t: (0, g)),
                  pl.BlockSpec((1, 2 * LANES, 2 * ns), blk),
                  pl.BlockSpec((1, LANES, 2 * ns), blk)]
                 + [pl.BlockSpec((1, 2 * ns, LANES), blk)] * len(cms)
                 + [pl.BlockSpec((1, 1, ns), blk),
                    pl.BlockSpec((1, 1, ns), blk),
                    pl.BlockSpec((1, LANES), lambda g, t: (0, g)),
                    state_spec, state_spec],
        out_specs=[pl.BlockSpec((nb, c, LANES), lambda g, t: (0, t, g))] + [state_spec] * 4,
        out_shape=[jax.ShapeDtypeStruct((nb, L, D), F32 if precise else BF16)] + [state_shape] * 4,
        scratch_shapes=[pltpu.VMEM((c * nb, LANES), F32),
                        pltpu.VMEM((c * nb, LANES), F32),
                        pltpu.VMEM((c * nb, 2 * ns), F32),
                        pltpu.VMEM((c * nb, 2 * ns), F32),
                        pltpu.VMEM((nb, ns), F32),
                        pltpu.VMEM((nb, ns), F32)],
        compiler_params=_cparams("parallel", "arbitrary"),
        name="s5_scan",
    )(x, rstd, g.reshape(1, D), bhi, blo, *cms, ar, ai, d, h0r, h0i)


def s5_constants(a_re, a_im, b_re, b_im, c_re, c_im, d_skip, log_dt):
    G, N = a_re.shape
    gc = b_re.shape[2]
    gpb = LANES // gc
    nblk = G // gpb
    a = lax.complex(a_re.astype(F32), a_im.astype(F32))
    dt = jnp.exp(log_dt.astype(F32))[:, None]
    a_bar = jnp.exp(a * dt)
    b_bar = ((a_bar - 1.0) / a)[..., None] * lax.complex(b_re.astype(F32), b_im.astype(F32))
    eye = jnp.eye(gpb, dtype=F32)

    def in_mat(b):
        b = b.reshape(nblk, gpb, N, gc)
        return jnp.einsum("kgnc,gh->kgchn", b, eye).reshape(nblk, gpb * gc, gpb * N)

    def out_mat(cc):
        cc = cc.reshape(nblk, gpb, gc, N)
        return jnp.einsum("kgcn,gh->kgnhc", cc, eye).reshape(nblk, gpb * N, gpb * gc)

    bhi, blo = _split_weight(jnp.concatenate([in_mat(b_bar.real), in_mat(b_bar.imag)], axis=2))
    bhi = jnp.concatenate([bhi, bhi], axis=1)
    cm_hi, cm_lo = _split_weight(jnp.concatenate([out_mat(c_re.astype(F32)), -out_mat(c_im.astype(F32))], axis=1))
    ar = a_bar.real.reshape(nblk, 1, gpb * N)
    ai = a_bar.imag.reshape(nblk, 1, gpb * N)
    return bhi, blo, cm_hi, cm_lo, ar, ai, d_skip.astype(F32).reshape(1, G * gc)


def _gdn_conv_kernel(x_ref, halo_ref, buf_ref, w_ref, o_ref, ext_sc, *, tl, n_valid, tiles_per_seq, n_norm_tiles):
    r = pl.program_id(0)
    j = pl.program_id(1)
    first = (r % tiles_per_seq) == 0
    ext_sc[0:SUBLANES, :] = jnp.where(first, buf_ref[0], halo_ref[...])
    ext_sc[SUBLANES:SUBLANES + n_valid, :] = x_ref[...]
    w = w_ref[...]
    y = x_ref[...] * w[3:4, :]
    for s in (1, 2, 3):
        y = y + ext_sc[SUBLANES - s:SUBLANES - s + n_valid, :] * w[3 - s:4 - s, :]
    y = _silu(y)

    def store(val):
        if n_valid < tl:
            val = jnp.concatenate([val, jnp.zeros((tl - n_valid, val.shape[1]), F32)], axis=0)
        o_ref[...] = val.astype(o_ref.dtype)

    @pl.when(j < n_norm_tiles)
    def _():
        segs = []
        for a in range(0, y.shape[1], LANES):
            seg = y[:, a:a + LANES]
            segs.append(seg * lax.rsqrt(jnp.sum(seg * seg, axis=1, keepdims=True) + EPS))
        store(jnp.concatenate(segs, axis=1))

    @pl.when(j >= n_norm_tiles)
    def _():
        store(y)


def gdn_conv(x, buf8, w8, *, n_seq, seq_len, tl_in, tl_out, n_norm_cols, tc):
    C = x.shape[1]
    tiles_per_seq = seq_len // tl_in
    n_tiles = n_seq * tiles_per_seq
    hpt = tl_in // SUBLANES
    return pl.pallas_call(
        functools.partial(_gdn_conv_kernel, tl=tl_out, n_valid=tl_in, tiles_per_seq=tiles_per_seq,
                          n_norm_tiles=n_norm_cols // tc),
        grid=(n_tiles, C // tc),
        in_specs=[pl.BlockSpec((tl_in, tc), lambda r, j: (r, j)),
                  pl.BlockSpec((SUBLANES, tc), lambda r, j: (jnp.maximum(r * hpt - 1, 0), j)),
                  pl.BlockSpec((1, SUBLANES, tc), lambda r, j: (r // tiles_per_seq, 0, j)),
                  pl.BlockSpec((SUBLANES, tc), lambda r, j: (0, j))],
        out_specs=pl.BlockSpec((tl_out, tc), lambda r, j: (r, j)),
        out_shape=jax.ShapeDtypeStruct((n_tiles * tl_out, C), BF16),
        scratch_shapes=[pltpu.VMEM((SUBLANES + tl_in, tc), F32)],
        compiler_params=_cparams("parallel", "arbitrary"),
        name="gdn_conv",
    )(x, x, buf8, w8)


def _gdn_prep_kernel(qk_ref, gt_ref, ti_ref, a_ref, gc_ref, n_sc, nt_sc, x_sc, *, C, cps, n_qk, rep, hd, scale):
    nh = n_qk * rep
    n_sys = cps * nh
    pitch = C + SUBLANES
    r = lax.broadcasted_iota(jnp.int32, (C, C), 0)
    c = lax.broadcasted_iota(jnp.int32, (C, C), 1)
    tril = (r >= c).astype(BF16)
    lane_pad = jnp.zeros((C, LANES - C), F32)
    for cc in range(cps):
        rows = slice(cc * C, (cc + 1) * C)
        gates = gt_ref[rows, :]
        lane = lax.broadcasted_iota(jnp.int32, gates.shape, 1)
        g = jnp.where((lane >= nh) & (lane < 2 * nh), gates, 0.0)
        g_hi = g.astype(BF16)
        g_r1 = g - g_hi.astype(F32)
        g_mid = g_r1.astype(BF16)
        g_lo = (g_r1 - g_mid.astype(F32)).astype(BF16)
        gcum = _dot(tril, g_hi) + _dot(tril, g_mid) + _dot(tril, g_lo)
        gc_ref[rows, :] = gcum
        gcum_t = gcum.T
        for hq in range(n_qk):
            q = qk_ref[rows, hq * hd:(hq + 1) * hd]
            k = qk_ref[rows, (n_qk + hq) * hd:(n_qk + hq + 1) * hd]
            kk = _dot_nt(k, k)
            qk = _dot_nt(q, k) * scale
            for rr in range(rep):
                hv = hq * rep + rr
                sys = cc * nh + hv
                diff = gcum[:, nh + hv:nh + hv + 1] - gcum_t[nh + hv:nh + hv + 1, :]
                decay = jnp.exp(jnp.where(r >= c, diff, NEG_INF))
                n_val = jnp.where(r > c, gates[:, hv:hv + 1] * kk * decay, 0.0)
                n_sc[sys * pitch:sys * pitch + C, :] = jnp.concatenate([n_val, lane_pad], axis=1)
                a_ref[cc, hv] = (qk * decay).astype(a_ref.dtype)
    for i in range(C):
        slab = n_sc[pl.ds(i, n_sys, stride=pitch), :]
        nt_sc[i * C:(i + 1) * C, :] = slab.T[:C, :]
    _tri_solve(nt_sc, x_sc, C)
    row_pad = jnp.zeros((LANES - C, n_sys), F32)
    for i in range(C):
        cols = jnp.concatenate([x_sc[i * C:(i + 1) * C, :], row_pad], axis=0)
        n_sc[pl.ds(i, n_sys, stride=pitch), :] = cols.T
    for cc in range(cps):
        for hv in range(nh):
            sys = cc * nh + hv
            ti_ref[cc, hv] = n_sc[sys * pitch:sys * pitch + C, :C].astype(ti_ref.dtype)


def gdn_prep(qkv_c, gates, *, C, n_qk, rep, hd):
    rows = qkv_c.shape[0]
    nch = rows // C
    nh = n_qk * rep
    cps = LANES // nh
    assert nch % cps == 0 and cps * nh == LANES
    return pl.pallas_call(
        functools.partial(_gdn_prep_kernel, C=C, cps=cps, n_qk=n_qk, rep=rep, hd=hd, scale=hd ** -0.5),
        grid=(nch // cps,),
        in_specs=[pl.BlockSpec((cps * C, 2 * n_qk * hd), lambda n: (n, 0)),
                  pl.BlockSpec((cps * C, LANES), lambda n: (n, 0))],
        out_specs=[pl.BlockSpec((cps, nh, C, C), lambda n: (n, 0, 0, 0)),
                   pl.BlockSpec((cps, nh, C, C), lambda n: (n, 0, 0, 0)),
                   pl.BlockSpec((cps * C, LANES), lambda n: (n, 0))],
        out_shape=[jax.ShapeDtypeStruct((nch, nh, C, C), BF16),
                   jax.ShapeDtypeStruct((nch, nh, C, C), BF16),
                   jax.ShapeDtypeStruct((rows, LANES), F32)],
        scratch_shapes=[pltpu.VMEM((LANES * (C + SUBLANES), LANES), F32),
                        pltpu.VMEM((C * C, LANES), F32),
                        pltpu.VMEM((C * C, LANES), F32)],
        compiler_params=_cparams("parallel"),
        name="gdn_prep",
    )(qkv_c, gates)


def _tri_solve(n_ref, x_ref, C):
    nblk = C // SUBLANES
    lanes = n_ref.shape[1]
    sub = lax.broadcasted_iota(jnp.int32, (SUBLANES, lanes), 0)
    x_ref[...] = jnp.zeros(x_ref.shape, F32)
    for ib in range(nblk):
        def row(r, _, ib=ib):
            base = (ib * SUBLANES + r) * C
            acc = [jnp.zeros((SUBLANES, lanes), F32) for _ in range(ib)]
            acc.append(jnp.where(sub == r, 1.0, 0.0).astype(F32))
            for jb in range(ib + 1):
                for j in range(jb * SUBLANES, (jb + 1) * SUBLANES):
                    nb = jnp.broadcast_to(n_ref[pl.ds(base + j, 1), :], (SUBLANES, lanes))
                    for cb in range(jb + 1):
                        acc[cb] = acc[cb] - nb * x_ref[j * C + cb * SUBLANES:j * C + (cb + 1) * SUBLANES, :]
            for cb in range(ib + 1):
                x_ref[pl.ds(pl.multiple_of(base + cb * SUBLANES, SUBLANES), SUBLANES), :] = acc[cb]
            return 0

        lax.fori_loop(0, SUBLANES, row, 0)


def _gdn_scan_kernel(q_ref, k_ref, v_ref, z_ref, gt_ref, gc_ref, ti_ref, a_ref, s0_ref, ng_ref,
                     o_ref, s_out, s_sc, *, C, hg, rep, hd, nh, n_valid, scale):
    n = pl.program_id(2)
    hgi = pl.program_id(1)

    @pl.when(n == 0)
    def _():
        s_sc[...] = s0_ref[0]

    shift = (LANES - hgi * hg) % LANES
    gates = pltpu.roll(gt_ref[...], shift, 1)
    gcum = pltpu.roll(gc_ref[...], shift, 1)
    heads = range(hg)
    s_old = [s_sc[hh] for hh in heads]
    s_bf = [s.astype(BF16) for s in s_old]
    beta = [gates[:, hh:hh + 1] for hh in heads]
    gc = [gcum[:, nh + hh:nh + hh + 1] for hh in heads]
    eg = [jnp.exp(g) for g in gc]
    g_tot = [g[C - 1:C, :] for g in gc]
    k = [k_ref[:, (hh // rep) * hd:(hh // rep + 1) * hd].astype(F32) for hh in heads]
    q = [q_ref[:, (hh // rep) * hd:(hh // rep + 1) * hd].astype(F32) for hh in heads]
    v = [v_ref[:, hh * hd:(hh + 1) * hd].astype(F32) for hh in heads]
    uw = [_dot(ti_ref[0, hh].astype(BF16),
               jnp.concatenate([v[hh] * beta[hh], k[hh] * (beta[hh] * eg[hh])], axis=1).astype(BF16))
          for hh in heads]
    q_dec = [(q[hh] * (scale * eg[hh])).astype(BF16) for hh in heads]
    k_dec = [(k[hh] * jnp.exp(g_tot[hh] - gc[hh])).astype(BF16) for hh in heads]
    v_new = [(uw[hh][:, :hd] - _dot(uw[hh][:, hd:].astype(BF16), s_bf[hh])).astype(BF16) for hh in heads]
    o = [_dot(q_dec[hh], s_bf[hh]) + _dot(a_ref[0, hh], v_new[hh]) for hh in heads]
    s_new = [s_old[hh] * jnp.exp(g_tot[hh]) + _dot_tn(k_dec[hh], v_new[hh]) for hh in heads]
    for hh in heads:
        s_sc[hh] = s_new[hh]
    ng = ng_ref[...]
    outs = [_rms_scale(o[hh][:n_valid], ng) * _silu(z_ref[:, hh * hd:(hh + 1) * hd].astype(F32)) for hh in heads]
    o_ref[...] = jnp.concatenate(outs, axis=1).astype(o_ref.dtype)

    @pl.when(n == pl.num_programs(2) - 1)
    def _():
        s_out[0] = s_sc[...]


def gdn_scan(qkv_c, z, gates, gcum, tinv, amat, s0, norm_g, *, n_seq, nc, C, n_valid, n_qk, rep, hd, hg):
    nh = n_qk * rep
    nhg = nh // hg
    qw = (hg // rep) * hd
    k_off = n_qk * hd // qw
    v_off = 2 * n_qk * hd // (hg * hd)
    chunk = lambda s, g, n: s * nc + n
    return pl.pallas_call(
        functools.partial(_gdn_scan_kernel, C=C, hg=hg, rep=rep, hd=hd, nh=nh, n_valid=n_valid,
                          scale=hd ** -0.5),
        grid=(n_seq, nhg, nc),
        in_specs=[pl.BlockSpec((C, qw), lambda s, g, n: (chunk(s, g, n), g)),
                  pl.BlockSpec((C, qw), lambda s, g, n: (chunk(s, g, n), k_off + g)),
                  pl.BlockSpec((C, hg * hd), lambda s, g, n: (chunk(s, g, n), v_off + g)),
                  pl.BlockSpec((n_valid, hg * hd), lambda s, g, n: (chunk(s, g, n), g)),
                  pl.BlockSpec((C, LANES), lambda s, g, n: (chunk(s, g, n), 0)),
                  pl.BlockSpec((C, LANES), lambda s, g, n: (chunk(s, g, n), 0)),
                  pl.BlockSpec((1, hg, C, C), lambda s, g, n: (chunk(s, g, n), g, 0, 0)),
                  pl.BlockSpec((1, hg, C, C), lambda s, g, n: (chunk(s, g, n), g, 0, 0)),
                  pl.BlockSpec((1, hg, hd, hd), lambda s, g, n: (s, g, 0, 0)),
                  pl.BlockSpec((1, hd), lambda s, g, n: (0, 0))],
        out_specs=[pl.BlockSpec((n_valid, hg * hd), lambda s, g, n: (chunk(s, g, n), g)),
                   pl.BlockSpec((1, hg, hd, hd), lambda s, g, n: (s, g, 0, 0))],
        out_shape=[jax.ShapeDtypeStruct((n_seq * nc * n_valid, nh * hd), BF16),
                   jax.ShapeDtypeStruct((n_seq, nh, hd, hd), F32)],
        scratch_shapes=[pltpu.VMEM((hg, hd, hd), F32)],
        compiler_params=_cparams("parallel", "parallel", "arbitrary"),
        name="gdn_scan",
    )(qkv_c, qkv_c, qkv_c, z, gates, gcum, tinv, amat, s0, norm_g.reshape(1, hd))


def _trunk(x, p, fox_hist, s5_h0, gdn_state, W, *, n_seq, seq_len, steps, precise, snap_steps=None,
           fox_kv=None):
    M, D = x.shape
    H, hd = W["fox_heads"]
    n_qk, rep, gd = W["gdn_heads"]
    n_vh = n_qk * rep
    key_dim, val_dim = n_qk * gd, n_vh * gd
    qkv_dim = 2 * key_dim + val_dim
    cw = W["gdn_conv_w"].shape[1]
    tn = min(512, D)
    outs = {} if fox_kv is None else {"fox_kv": fox_kv}

    def wsel(name, idx, pr):
        hi = W[name][idx]
        return (hi, W[name + "_lo"][idx]) if pr else hi

    def mixer(i, x):
        j = i // 3
        pm = precise
        g_mix = W["norm_mix"][i]
        if i % 3 == 0:
            qkv, k_heads, v_heads = fox_in_proj(x, g_mix, wsel("fox_w_qkv", j, pm), n_heads=H, hd=hd, tn=tn,
                                                bufs=outs.get("fox_kv"), slot=j, n_slots=W["n_fox"])
            outs["fox_kv"] = (k_heads, v_heads)
            outs[("fox_qkv", j)] = qkv
            logf = small_proj(x, g_mix, W["fox_w_f"][j], W["fox_aux"][j], "logf")[:, :H]
            lf = jnp.swapaxes(logf.reshape(n_seq, seq_len, H), 1, 2)
            if j not in fox_hist:
                tq = next(t for t in (512, 384, 256, 128, seq_len) if seq_len % t == 0)
                fk = lane_cumsum(lf.reshape(n_seq * H, seq_len)).reshape(n_seq * H, seq_len // tq, tq)
                o = fox_attention_prompt(qkv, fk, B=n_seq, L=seq_len, H=H, hd=hd, tq=tq)
            else:
                k_past, v_past, k_lane0, v_lane0, seq0, lf_past, P = fox_hist[j]
                fpad = -(-(P + seq_len) // LANES) * LANES
                lf_all = jnp.concatenate([jnp.swapaxes(lf_past.astype(F32), 1, 2), lf], axis=2)
                lf_all = jnp.pad(lf_all, ((0, 0), (0, 0), (0, fpad - (P + seq_len))))
                f = lane_cumsum(lf_all.reshape(n_seq * H, fpad)).reshape(n_seq * H, 1, fpad)
                o = fox_attention_cached(qkv, k_past, v_past, f, seq0=seq0, k_lane0=k_lane0, v_lane0=v_lane0,
                                         B=n_seq, n=seq_len, P=P, H=H, hd=hd)
            outs[("fox", j)] = logf
            return mm_res(o, wsel("fox_w_out", j, pm), x, tn=tn)
        if i % 3 == 1:
            rstd = row_rstd(x).reshape(n_seq, seq_len, LANES)
            y, hr, hi, hr_snap, hi_snap = s5_scan(x.reshape(n_seq, seq_len, D), rstd, g_mix,
                                                  s5_h0[j][0], s5_h0[j][1], W["s5_consts"][j],
                                                  c=min(CHUNK, seq_len), precise=pm, snap_steps=snap_steps)
            outs[("s5", j)] = (hr, hi)
            outs[("s5_snap", j)] = (hr_snap, hi_snap)
            return glu_res(y.reshape(M, D), wsel("s5_w_glu", j, pm), x, tn=tn)
        s0, conv_rows = gdn_state[j]
        nq = qkv_dim // tn
        nz = val_dim // tn
        qkv_raw, z = norm_mm(x, g_mix, W["gdn_w_qkvz"][j], [(F32, 0, nq), (BF16, nq, nq + nz)], tn=tn)
        gates = small_proj(x, g_mix, W["gdn_w_ba"][j], W["gdn_aux"][j], "gdn", nh=n_vh)
        C = CHUNK
        n_valid = min(C, seq_len)
        nc = seq_len // n_valid
        buf = jnp.pad(conv_rows.astype(F32), ((0, 0), (SUBLANES - (cw - 1), 0), (0, 0)))
        tl_in = min(512, seq_len)
        qkv_c = gdn_conv(qkv_raw, buf, W["gdn_conv_w8"][j], n_seq=n_seq, seq_len=seq_len, tl_in=tl_in,
                         tl_out=max(tl_in, C), n_norm_cols=2 * key_dim,
                         tc=key_dim if tl_in < C else min(512, key_dim))
        gates_c = jnp.pad(gates.reshape(n_seq * nc, n_valid, LANES),
                          ((0, 0), (0, C - n_valid), (0, 0))).reshape(n_seq * nc * C, LANES)
        tinv, amat, gcum = gdn_prep(qkv_c, gates_c, C=C, n_qk=n_qk, rep=rep, hd=gd)
        o, s_last = gdn_scan(qkv_c, z, gates_c, gcum, tinv, amat, s0.astype(F32), W["gdn_norm"][j],
                             n_seq=n_seq, nc=nc, C=C, n_valid=n_valid, n_qk=n_qk, rep=rep, hd=gd,
                             hg=min(8, n_vh))
        rows = jnp.concatenate([conv_rows.astype(F32), qkv_raw.reshape(n_seq, seq_len, qkv_dim)], axis=1)
        outs[("gdn", j)] = (s_last, rows[:, -(cw - 1):])
        return mm_res(o, W["gdn_w_out"][j], x, tn=tn)

    def channel(i, x):
        m = i // 2
        g_ffn = W["norm_ffn"][i]
        if i % 2 == 0:
            pf = precise
            h = glu_up(x, g_ffn, wsel("ffn_w_gate", m, pf), wsel("ffn_w_up", m, pf), tn=tn)
            x = mm_res(h, wsel("ffn_w_down", m, pf), x, tn=tn)
        else:
            pf = False
            comb = small_proj(x, g_ffn, W["moe_w_router"][m], W["zero_aux"], "router")
            h = glu_up(x, g_ffn, W["moe_w_gate"][m], W["moe_w_up"][m], comb, tn=tn)
            x = mm_res(h, W["moe_w_down"][m], x, tn=tn)
        p_i = p[i] if pf else p[i].astype(BF16)
        return ple(x, W["norm_ple"][i], wsel("ple_w_gate", i, pf), p_i, wsel("ple_w_up", i, pf), tn=tn)

    for i, part in steps:
        x = mixer(i, x) if part == "mixer" else channel(i, x)
    return x, outs


def kernel(x_prompt, x_sample, cache_fox_k, cache_fox_v, cache_fox_logf, state_s5_re, state_s5_im, state_gdn, state_gdn_conv, p_prompt, p_sample, norm_mix, norm_ffn, norm_ple, norm_final, fox_w_in, fox_b_f, fox_w_out, s5_a_re, s5_a_im, s5_b_re, s5_b_im, s5_c_re, s5_c_im, s5_d, s5_log_dt, s5_w_glu, gdn_w_in, gdn_conv_w, gdn_a_log, gdn_dt_bias, gdn_norm, gdn_w_out, ffn_w_gate, ffn_w_up, ffn_w_down, moe_w_router, moe_w_gate, moe_w_up, moe_w_down, ple_w_up, ple_w_gate):
    B, L, D = x_prompt.shape
    Bs, Ls, _ = x_sample.shape
    depth = norm_mix.shape[0]
    P = cache_fox_k.shape[2]
    H, hd = cache_fox_k.shape[3], cache_fox_k.shape[4]
    n_vh, gd = state_gdn.shape[2], state_gdn.shape[3]
    qkv_dim = state_gdn_conv.shape[3]
    val_dim = n_vh * gd
    n_qk = (qkv_dim - val_dim) // (2 * gd)
    cw = gdn_conv_w.shape[1]
    G, NS = s5_a_re.shape[1], s5_a_re.shape[2]
    E, _, Fe = moe_w_gate.shape[1:]
    n_fox, n_s5, n_gdn = fox_w_in.shape[0], s5_a_re.shape[0], gdn_w_in.shape[0]

    first_router = 1
    n_lo = {"fox": 1, "ffn": 1, "ple": 1, "s5": 1}

    def hi_lo(w, n):
        return w.astype(BF16), _split_weight(w[:n])[1]

    W = {"norm_mix": norm_mix, "norm_ffn": norm_ffn, "norm_ple": norm_ple, "norm_final": norm_final,
         "fox_heads": (H, hd), "n_fox": n_fox, "gdn_heads": (n_qk, n_vh // n_qk, gd), "gdn_conv_w": gdn_conv_w,
         "gdn_norm": gdn_norm, "zero_aux": jnp.zeros((SUBLANES, LANES), F32)}
    W["fox_w_qkv"], W["fox_w_qkv_lo"] = hi_lo(fox_w_in[:, :, :3 * D], n_lo["fox"])
    W["fox_w_out"], W["fox_w_out_lo"] = hi_lo(fox_w_out, n_lo["fox"])
    W["fox_w_f"] = [_pad_cols(fox_w_in[j, :, 3 * D:]) for j in range(n_fox)]
    W["fox_aux"] = [_aux_rows((0, fox_b_f[j])) for j in range(n_fox)]
    W["s5_w_glu"], W["s5_w_glu_lo"] = hi_lo(s5_w_glu, n_lo["s5"])
    W["s5_consts"] = [s5_constants(s5_a_re[j], s5_a_im[j], s5_b_re[j], s5_b_im[j], s5_c_re[j], s5_c_im[j],
                                   s5_d[j], s5_log_dt[j]) for j in range(n_s5)]
    W["gdn_w_qkvz"] = gdn_w_in[:, :, :qkv_dim + val_dim].astype(BF16)
    W["gdn_w_ba"] = [_pad_cols(gdn_w_in[j, :, qkv_dim + val_dim:]) for j in range(n_gdn)]
    W["gdn_aux"] = [_aux_rows((n_vh, gdn_dt_bias[j]), (n_vh, gdn_a_log[j])) for j in range(n_gdn)]
    W["gdn_conv_w8"] = jnp.pad(gdn_conv_w.astype(F32), ((0, 0), (0, SUBLANES - cw), (0, 0)))
    W["gdn_w_out"] = gdn_w_out.astype(BF16)
    W["ffn_w_gate"], W["ffn_w_gate_lo"] = hi_lo(ffn_w_gate, n_lo["ffn"])
    W["ffn_w_up"], W["ffn_w_up_lo"] = hi_lo(ffn_w_up, n_lo["ffn"])
    W["ffn_w_down"], W["ffn_w_down_lo"] = hi_lo(ffn_w_down, n_lo["ffn"])
    W["moe_w_router"] = [_pad_cols(moe_w_router[m]) for m in range(moe_w_router.shape[0])]
    W["moe_w_gate"] = moe_w_gate.astype(BF16)
    W["moe_w_up"] = moe_w_up.astype(BF16)
    W["moe_w_down"] = moe_w_down.astype(BF16).reshape(-1, E * Fe, D)
    W["ple_w_gate"], W["ple_w_gate_lo"] = hi_lo(ple_w_gate, n_lo["ple"])
    W["ple_w_up"], W["ple_w_up_lo"] = hi_lo(ple_w_up, n_lo["ple"])

    steps = [(i, part) for i in range(depth) for part in ("mixer", "channel")]
    steps_a, steps_b = steps[:2 * first_router + 1], steps[2 * first_router + 1:]
    pd = p_prompt.shape[-1]

    Lt = LANES
    Lh = L - Lt
    p_all = {i: p_prompt[i].reshape(B * L, pd) for i in range(depth)}
    p_tail = {i: p_prompt[i][:, Lh:].reshape(B * Lt, pd) for i in range(depth)}
    zeros_s5 = {j: (jnp.zeros((B, G * NS), F32),) * 2 for j in range(n_s5)}
    x_f, out_p = _trunk(x_prompt.reshape(B * L, D), p_all, {}, zeros_s5, {}, W,
                        n_seq=B, seq_len=L, steps=steps_a, precise=False, snap_steps=Lh)
    hist_t = {}
    for j in range(n_fox):
        if ("fox", j) in out_p:
            qkv_f = out_p[("fox_qkv", j)].reshape(B, L, 3 * D)
            hist_t[j] = (qkv_f, qkv_f, H, 2 * H, 0, out_p[("fox", j)].reshape(B, L, H)[:, :Lh], Lh)
    xt, out_t = _trunk(x_prompt[:, Lh:].reshape(B * Lt, D), p_tail, hist_t,
                       {j: out_p[("s5_snap", j)] for j in range(n_s5) if ("s5_snap", j) in out_p}, {}, W,
                       n_seq=B, seq_len=Lt, steps=steps_a, precise=True)
    x_p = lax.dynamic_update_slice(x_f.reshape(B, L, D), xt.reshape(B, Lt, D), (0, Lh, 0)).reshape(B * L, D)
    out_p.update({key: val for key, val in out_t.items() if key[0] == "s5"})
    gdn0 = (jnp.zeros((B, n_vh, gd, gd), F32), jnp.zeros((B, cw - 1, qkv_dim), F32))
    x_p, out_b = _trunk(x_p, p_all, {}, zeros_s5, {j: gdn0 for j in range(n_gdn)}, W,
                        n_seq=B, seq_len=L, steps=steps_b, precise=False, fox_kv=out_p.get("fox_kv"))
    out_p.update(out_b)

    ck = cache_fox_k.reshape(-1, P * H, hd)
    cv = cache_fox_v.reshape(-1, P * H, hd)
    s5_re = state_s5_re.reshape(n_s5, Bs, G * NS).astype(F32)
    s5_im = state_s5_im.reshape(n_s5, Bs, G * NS).astype(F32)
    hist_s = {j: (ck, cv, None, None, j * Bs, cache_fox_logf[j], P) for j in range(n_fox)}
    state_s = ({j: (s5_re[j], s5_im[j]) for j in range(n_s5)},
               {j: (state_gdn[j], state_gdn_conv[j]) for j in range(n_gdn)})
    p_s = {i: p_sample[i].reshape(Bs * Ls, pd) for i in range(depth)}
    x_s, out_s = _trunk(x_sample.reshape(Bs * Ls, D), p_s, hist_s, *state_s, W,
                        n_seq=Bs, seq_len=Ls, steps=steps_a, precise=True)
    x_s, out_sb = _trunk(x_s, p_s, hist_s, *state_s, W, n_seq=Bs, seq_len=Ls, steps=steps_b, precise=False,
                         fox_kv=out_s.get("fox_kv"))
    out_s.update(out_sb)

    def leaves(x, out, nb, sl):
        y = rmsnorm(x, norm_final).reshape(nb, sl, D)
        fk, fv = (a.reshape(n_fox, nb, sl, H, hd) for a in out["fox_kv"])
        flf = jnp.stack([out[("fox", j)].reshape(nb, sl, H) for j in range(n_fox)])
        sre = jnp.stack([out[("s5", j)][0].reshape(nb, G, NS) for j in range(n_s5)])
        sim = jnp.stack([out[("s5", j)][1].reshape(nb, G, NS) for j in range(n_s5)])
        gs = jnp.stack([out[("gdn", j)][0] for j in range(n_gdn)])
        gc = jnp.stack([out[("gdn", j)][1] for j in range(n_gdn)])
        return y, fk, fv, flf, sre, sim, gs, gc

    (y_p, fk_p, fv_p, flf_p, sre_p, sim_p, gs_p, gc_p) = leaves(x_p, out_p, B, L)
    (y_s, fk_s, fv_s, flf_s, sre_s, sim_s, gs_s, gc_s) = leaves(x_s, out_s, Bs, Ls)
    return (y_p, y_s, fk_p, fv_p, flf_p, fk_s, fv_s, flf_s, sre_p, sim_p, sre_s, sim_s,
            gs_p, gc_p, gs_s, gc_s)
```

```python
import functools
import math

import jax
import jax.numpy as jnp
from jax import lax
from jax.experimental import pallas as pl
from jax.experimental.pallas import tpu as pltpu

F32 = jnp.float32
BF16 = jnp.bfloat16
EPS = 1e-6
LANES = 128
SUBLANES = 8
VMEM_LIMIT = 56 * 1024 * 1024
CHUNK = 64
N_EXPERTS = 8
NEG_INF = float("-inf")
LOG2E = math.log2(math.e)


def _cparams(*sem):
    return pltpu.CompilerParams(dimension_semantics=sem, vmem_limit_bytes=VMEM_LIMIT)


def _rms_scale(x, g):
    ms = jnp.mean(x * x, axis=-1, keepdims=True)
    return x * lax.rsqrt(ms + EPS) * g


def _sigmoid(x):
    return 0.5 * jnp.tanh(0.5 * x) + 0.5


def _silu(x):
    return x * _sigmoid(x)


def _softplus(x):
    return jnp.maximum(x, 0.0) + jnp.log1p(jnp.exp(-jnp.abs(x)))


def _split_bf16(x):
    hi = x.astype(BF16)
    lo = (x - hi.astype(F32)).astype(BF16)
    return hi, lo


def _split_weight(w):
    w = w.astype(F32)
    hi = lax.reduce_precision(w, exponent_bits=8, mantissa_bits=7)
    return hi.astype(BF16), (w - hi).astype(BF16)


def _dot(a, b):
    return jnp.dot(a, b, preferred_element_type=F32)


def _dot_nt(a, b):
    return lax.dot_general(a, b, (((1,), (1,)), ((), ())), preferred_element_type=F32)


def _dot_tn(a, b):
    return lax.dot_general(a, b, (((0,), (0,)), ((), ())), preferred_element_type=F32)


def _parts(x, precise):
    return _split_bf16(x.astype(F32)) if precise else (x.astype(BF16),)


def _mm(xp, wp):
    if len(wp) == 1:
        return _dot(xp[0], wp[0])
    return _dot(xp[0], wp[0]) + _dot(xp[1], wp[0]) + _dot(xp[0], wp[1])


def _mm_nt(ap, bp):
    if len(ap) == 1:
        return _dot_nt(ap[0], bp[0])
    return _dot_nt(ap[0], bp[0]) + _dot_nt(ap[1], bp[0]) + _dot_nt(ap[0], bp[1])


def _read(refs):
    return tuple(r[...] for r in refs)


def _wparts(w):
    return tuple(w) if isinstance(w, (tuple, list)) else (w,)


def _row_tile(m, n_weight_parts=1):
    cap = 1024 // n_weight_parts
    return next(t for t in (1024, 512, 256, 128) if t <= cap and m % t == 0)


def _store_norm(x_ref, g_ref, xn_refs):
    xn = _rms_scale(x_ref[...], g_ref[...])
    for ref, part in zip(xn_refs, _parts(xn, len(xn_refs) == 2)):
        ref[...] = part


def _norm_mm_kernel(*refs, ranges, nw):
    x_ref, g_ref = refs[:2]
    w_refs = refs[2:2 + nw]
    outs = refs[2 + nw:2 + nw + len(ranges)]
    xn_refs = refs[2 + nw + len(ranges):]
    j = pl.program_id(1)

    @pl.when(j == 0)
    def _():
        _store_norm(x_ref, g_ref, xn_refs)

    acc = _mm(_read(xn_refs), _read(w_refs))
    for o_ref, (a, b) in zip(outs, ranges):
        @pl.when((j >= a) & (j < b))
        def _(o_ref=o_ref):
            o_ref[...] = acc.astype(o_ref.dtype)


def _range_map(i, j, *, a, n):
    return (i, jnp.clip(j - a, 0, n - 1))


def norm_mm(x, g, w, outs, *, tn):
    M, D = x.shape
    wp = _wparts(w)
    N = wp[0].shape[1]
    tm = _row_tile(M, len(wp))
    ranges = tuple((a, b) for _, a, b in outs)
    return pl.pallas_call(
        functools.partial(_norm_mm_kernel, ranges=ranges, nw=len(wp)),
        grid=(M // tm, N // tn),
        in_specs=[pl.BlockSpec((tm, D), lambda i, j: (i, 0)),
                  pl.BlockSpec((1, D), lambda i, j: (0, 0))]
                 + [pl.BlockSpec((D, tn), lambda i, j: (0, j))] * len(wp),
        out_specs=[pl.BlockSpec((tm, tn), functools.partial(_range_map, a=a, n=b - a))
                   for _, a, b in outs],
        out_shape=[jax.ShapeDtypeStruct((M, (b - a) * tn), dt) for dt, a, b in outs],
        scratch_shapes=[pltpu.VMEM((tm, D), BF16)] * len(wp),
        compiler_params=_cparams("parallel", "arbitrary"),
        name="norm_mm",
    )(x, g.reshape(1, D), *wp)


def _fox_in_kernel(*refs, nw, nt, n_heads, hd, tm, n_bufs):
    x_ref, g_ref = refs[:2]
    w_refs = refs[2:2 + nw]
    qkv_ref, k_ref, v_ref = refs[2 + nw + n_bufs:5 + nw + n_bufs]
    xn_refs = refs[5 + nw + n_bufs:]
    j = pl.program_id(1)

    @pl.when(j == 0)
    def _():
        _store_norm(x_ref, g_ref, xn_refs)

    acc = _mm(_read(xn_refs), _read(w_refs))
    qkv_ref[...] = acc.astype(qkv_ref.dtype)
    hpt = acc.shape[1] // hd
    for dst, base in ((k_ref, nt), (v_ref, 2 * nt)):
        for jj in range(nt):
            @pl.when(j == base + jj)
            def _(dst=dst, jj=jj):
                for hh in range(hpt):
                    dst[pl.ds(jj * hpt + hh, tm, stride=n_heads), :] = acc[:, hh * hd:(hh + 1) * hd]


def fox_in_proj(x, g, w, *, n_heads, hd, tn, bufs=None, slot=0, n_slots=1):
    M, D = x.shape
    wp = _wparts(w)
    nw = len(wp)
    tm = min(512, M)
    nt = D // tn
    n_bufs = 0 if bufs is None else 2
    blk0 = slot * (M // tm)
    head_spec = pl.BlockSpec((tm * n_heads, hd), lambda i, j: (blk0 + i, 0))
    head_shape = jax.ShapeDtypeStruct((n_slots * M * n_heads, hd), F32)
    return pl.pallas_call(
        functools.partial(_fox_in_kernel, nw=nw, nt=nt, n_heads=n_heads, hd=hd, tm=tm, n_bufs=n_bufs),
        grid=(M // tm, 3 * nt),
        in_specs=[pl.BlockSpec((tm, D), lambda i, j: (i, 0)),
                  pl.BlockSpec((1, D), lambda i, j: (0, 0))]
                 + [pl.BlockSpec((D, tn), lambda i, j: (0, j))] * nw
                 + [pl.BlockSpec(memory_space=pl.ANY)] * n_bufs,
        out_specs=[pl.BlockSpec((tm, tn), lambda i, j: (i, j)), head_spec, head_spec],
        out_shape=[jax.ShapeDtypeStruct((M, 3 * D), F32 if nw == 2 else BF16), head_shape, head_shape],
        scratch_shapes=[pltpu.VMEM((tm, D), BF16)] * nw,
        input_output_aliases={} if bufs is None else {2 + nw: 1, 3 + nw: 2},
        compiler_params=_cparams("parallel", "arbitrary"),
        name="fox_in_proj",
    )(x, g.reshape(1, D), *wp, *(bufs or ()))


def _glu_up_kernel(*refs, tiles_per_expert, nw):
    x_ref, g_ref = refs[:2]
    wg_refs = refs[2:2 + nw]
    wu_refs = refs[2 + nw:2 + 2 * nw]
    rest = refs[2 + 2 * nw:]
    comb_ref = rest[0] if tiles_per_expert else None
    o_ref = rest[1] if tiles_per_expert else rest[0]
    xn_refs = rest[2:] if tiles_per_expert else rest[1:]
    j = pl.program_id(1)

    @pl.when(j == 0)
    def _():
        _store_norm(x_ref, g_ref, xn_refs)

    xn = _read(xn_refs)
    h = _silu(_mm(xn, _read(wg_refs))) * _mm(xn, _read(wu_refs))
    if tiles_per_expert:
        comb = comb_ref[...]
        lane = lax.broadcasted_iota(jnp.int32, comb.shape, 1)
        col = jnp.sum(jnp.where(lane == j // tiles_per_expert, comb, 0.0), axis=1, keepdims=True)
        h = h * col
    o_ref[...] = h.astype(o_ref.dtype)


def glu_up(x, g, w_gate, w_up, comb=None, *, tn):
    M, D = x.shape
    wg, wu = _wparts(w_gate), _wparts(w_up)
    nw = len(wg)
    tm = _row_tile(M, nw)
    if comb is None:
        n_out, tpe = wg[0].shape[1], 0
        w_spec = pl.BlockSpec((D, tn), lambda i, j: (0, j))
        extra_in, extra_specs = (), []
    else:
        E, _, F = wg[0].shape
        n_out, tpe = E * F, F // tn
        w_spec = pl.BlockSpec((None, D, tn), lambda i, j: (j // tpe, 0, j % tpe))
        extra_in = (comb,)
        extra_specs = [pl.BlockSpec((tm, LANES), lambda i, j: (i, 0))]
    return pl.pallas_call(
        functools.partial(_glu_up_kernel, tiles_per_expert=tpe, nw=nw),
        grid=(M // tm, n_out // tn),
        in_specs=[pl.BlockSpec((tm, D), lambda i, j: (i, 0)),
                  pl.BlockSpec((1, D), lambda i, j: (0, 0))] + [w_spec] * (2 * nw) + extra_specs,
        out_specs=pl.BlockSpec((tm, tn), lambda i, j: (i, j)),
        out_shape=jax.ShapeDtypeStruct((M, n_out), F32 if nw == 2 else BF16),
        scratch_shapes=[pltpu.VMEM((tm, D), BF16)] * nw,
        compiler_params=_cparams("parallel", "arbitrary"),
        name="glu_up",
    )(x, g.reshape(1, D), *wg, *wu, *extra_in)


def _mm_res_kernel(*refs, nk, nw):
    x_ref = refs[0]
    w_refs = refs[1:1 + nw]
    r_ref, o_ref = refs[1 + nw:]
    k = pl.program_id(2)
    prod = _mm(_parts(x_ref[...], nw == 2), _read(w_refs))

    @pl.when(k == 0)
    def _():
        o_ref[...] = r_ref[...] + prod

    if nk > 1:
        @pl.when(k > 0)
        def _():
            o_ref[...] += prod


def mm_res(x, w, res, *, tn):
    M, K = x.shape
    wp = _wparts(w)
    N = wp[0].shape[1]
    tm = _row_tile(M, len(wp))
    tk = min(K, 4096 // len(wp))
    nk = K // tk
    return pl.pallas_call(
        functools.partial(_mm_res_kernel, nk=nk, nw=len(wp)),
        grid=(M // tm, N // tn, nk),
        in_specs=[pl.BlockSpec((tm, tk), lambda i, j, k: (i, k))]
                 + [pl.BlockSpec((tk, tn), lambda i, j, k: (k, j))] * len(wp)
                 + [pl.BlockSpec((tm, tn), lambda i, j, k: (i, j))],
        out_specs=pl.BlockSpec((tm, tn), lambda i, j, k: (i, j)),
        out_shape=jax.ShapeDtypeStruct((M, N), F32),
        compiler_params=_cparams("parallel", "parallel", "arbitrary"),
        name="mm_res",
    )(x, *wp, res)


def _glu_res_kernel(*refs, nw):
    y_ref = refs[0]
    wa_refs = refs[1:1 + nw]
    wb_refs = refs[1 + nw:1 + 2 * nw]
    r_ref, o_ref = refs[1 + 2 * nw:]
    y = _parts(y_ref[...], nw == 2)
    o_ref[...] = r_ref[...] + _mm(y, _read(wa_refs)) * _sigmoid(_mm(y, _read(wb_refs)))


def glu_res(y, w, res, *, tn):
    M, K = y.shape
    wp = _wparts(w)
    N = wp[0].shape[1] // 2
    nj = N // tn
    tm = _row_tile(M, len(wp))
    return pl.pallas_call(
        functools.partial(_glu_res_kernel, nw=len(wp)),
        grid=(M // tm, nj),
        in_specs=[pl.BlockSpec((tm, K), lambda i, j: (i, 0))]
                 + [pl.BlockSpec((K, tn), lambda i, j: (0, j))] * len(wp)
                 + [pl.BlockSpec((K, tn), lambda i, j: (0, j + nj))] * len(wp)
                 + [pl.BlockSpec((tm, tn), lambda i, j: (i, j))],
        out_specs=pl.BlockSpec((tm, tn), lambda i, j: (i, j)),
        out_shape=jax.ShapeDtypeStruct((M, N), F32),
        compiler_params=_cparams("parallel", "arbitrary"),
        name="glu_res",
    )(y, *wp, *wp, res)


def _ple_kernel(*refs, tn, nw):
    x_ref, g_ref = refs[:2]
    wg_refs = refs[2:2 + nw]
    p_ref = refs[2 + nw]
    wu_refs = refs[3 + nw:3 + 2 * nw]
    o_ref = refs[3 + 2 * nw]
    xn_refs = refs[4 + 2 * nw:]
    j = pl.program_id(1)

    @pl.when(j == 0)
    def _():
        _store_norm(x_ref, g_ref, xn_refs)

    gate = _sigmoid(_mm(_read(xn_refs), _read(wg_refs)))
    up = _mm(_parts(p_ref[...], nw == 2), _read(wu_refs))
    x_tile = x_ref[:, pl.ds(pl.multiple_of(j * tn, tn), tn)]
    o_ref[...] = x_tile + up * gate


def ple(x, g, w_gate, p, w_up, *, tn):
    M, D = x.shape
    P = p.shape[1]
    wg, wu = _wparts(w_gate), _wparts(w_up)
    nw = len(wg)
    tm = _row_tile(M, nw)
    return pl.pallas_call(
        functools.partial(_ple_kernel, tn=tn, nw=nw),
        grid=(M // tm, D // tn),
        in_specs=[pl.BlockSpec((tm, D), lambda i, j: (i, 0)),
                  pl.BlockSpec((1, D), lambda i, j: (0, 0))]
                 + [pl.BlockSpec((D, tn), lambda i, j: (0, j))] * nw
                 + [pl.BlockSpec((tm, P), lambda i, j: (i, 0))]
                 + [pl.BlockSpec((P, tn), lambda i, j: (0, j))] * nw,
        out_specs=pl.BlockSpec((tm, tn), lambda i, j: (i, j)),
        out_shape=jax.ShapeDtypeStruct((M, D), F32),
        scratch_shapes=[pltpu.VMEM((tm, D), BF16)] * nw,
        compiler_params=_cparams("parallel", "arbitrary"),
        name="ple",
    )(x, g.reshape(1, D), *wg, p, *wu)


def _rmsnorm_kernel(x_ref, g_ref, o_ref):
    o_ref[...] = _rms_scale(x_ref[...], g_ref[...]).astype(o_ref.dtype)


def rmsnorm(x, g):
    M, D = x.shape
    tm = _row_tile(M)
    return pl.pallas_call(
        _rmsnorm_kernel,
        grid=(M // tm,),
        in_specs=[pl.BlockSpec((tm, D), lambda i: (i, 0)),
                  pl.BlockSpec((1, D), lambda i: (0, 0))],
        out_specs=pl.BlockSpec((tm, D), lambda i: (i, 0)),
        out_shape=jax.ShapeDtypeStruct((M, D), F32),
        compiler_params=_cparams("parallel"),
        name="rmsnorm",
    )(x, g.reshape(1, D))


def _row_rstd_kernel(x_ref, o_ref):
    x = x_ref[...]
    rstd = lax.rsqrt(jnp.mean(x * x, axis=-1, keepdims=True) + EPS)
    o_ref[...] = jnp.broadcast_to(rstd, o_ref.shape)


def row_rstd(x):
    M, D = x.shape
    tm = _row_tile(M)
    return pl.pallas_call(
        _row_rstd_kernel,
        grid=(M // tm,),
        in_specs=[pl.BlockSpec((tm, D), lambda i: (i, 0))],
        out_specs=pl.BlockSpec((tm, LANES), lambda i: (i, 0)),
        out_shape=jax.ShapeDtypeStruct((M, LANES), F32),
        compiler_params=_cparams("parallel"),
        name="row_rstd",
    )(x)


def _small_proj_kernel(x_ref, g_ref, whi_ref, wlo_ref, aux_ref, o_ref, *, mode, nh):
    xn = _rms_scale(x_ref[...], g_ref[...])
    acc = _mm(_parts(xn, True), (whi_ref[...], wlo_ref[...]))
    lane = lax.broadcasted_iota(jnp.int32, acc.shape, 1)
    if mode == "logf":
        o_ref[...] = -_softplus(-(acc + aux_ref[0:1, :]))
    elif mode == "gdn":
        beta = _sigmoid(acc)
        gdec = -jnp.exp(aux_ref[1:2, :]) * _softplus(acc + aux_ref[0:1, :])
        o_ref[...] = jnp.where(lane < nh, beta, gdec)
    else:
        logits = jnp.where(lane < N_EXPERTS, acc, NEG_INF)
        v1 = jnp.max(logits, axis=1, keepdims=True)
        i1 = jnp.min(jnp.where(logits == v1, lane, LANES), axis=1, keepdims=True)
        rest = jnp.where(lane == i1, NEG_INF, logits)
        v2 = jnp.max(rest, axis=1, keepdims=True)
        i2 = jnp.min(jnp.where(rest == v2, lane, LANES), axis=1, keepdims=True)
        e2 = jnp.exp(v2 - v1)
        g1 = 1.0 / (1.0 + e2)
        o_ref[...] = jnp.where(lane == i1, g1, 0.0) + jnp.where(lane == i2, e2 * g1, 0.0)


def small_proj(x, g, w_parts, aux, mode, *, nh=0):
    M, D = x.shape
    tm = _row_tile(M)
    return pl.pallas_call(
        functools.partial(_small_proj_kernel, mode=mode, nh=nh),
        grid=(M // tm,),
        in_specs=[pl.BlockSpec((tm, D), lambda i: (i, 0)),
                  pl.BlockSpec((1, D), lambda i: (0, 0)),
                  pl.BlockSpec((D, LANES), lambda i: (0, 0)),
                  pl.BlockSpec((D, LANES), lambda i: (0, 0)),
                  pl.BlockSpec((SUBLANES, LANES), lambda i: (0, 0))],
        out_specs=pl.BlockSpec((tm, LANES), lambda i: (i, 0)),
        out_shape=jax.ShapeDtypeStruct((M, LANES), F32),
        compiler_params=_cparams("parallel"),
        name="small_proj_" + mode,
    )(x, g.reshape(1, D), *w_parts, aux)


def _pad_cols(w):
    return _split_weight(jnp.pad(w.astype(F32), ((0, 0), (0, LANES - w.shape[1]))))


def _aux_rows(*rows):
    out = [jnp.pad(vals.astype(F32), (start, LANES - start - vals.shape[0])) for start, vals in rows]
    out += [jnp.zeros((LANES,), F32)] * (SUBLANES - len(out))
    return jnp.stack(out)


def _lane_cumsum_kernel(x_ref, o_ref):
    x = x_ref[...]
    n = x.shape[1]
    lane = lax.broadcasted_iota(jnp.int32, x.shape, 1)
    shift = 1
    while shift < n:
        x = x + jnp.where(lane >= shift, pltpu.roll(x, shift, 1), 0.0)
        shift *= 2
    o_ref[...] = x


def lane_cumsum(x):
    R, n = x.shape
    rb = SUBLANES if R % SUBLANES == 0 else R
    return pl.pallas_call(
        _lane_cumsum_kernel,
        grid=(R // rb,),
        in_specs=[pl.BlockSpec((rb, n), lambda i: (i, 0))],
        out_specs=pl.BlockSpec((rb, n), lambda i: (i, 0)),
        out_shape=jax.ShapeDtypeStruct((R, n), F32),
        compiler_params=_cparams("parallel"),
        name="lane_cumsum",
    )(x)


def _fox_prompt_kernel(q_ref, k_ref, v_ref, fk_ref, o_ref, *, tq, nhb, hd, scale):
    qi = pl.program_id(2)

    def step(ki, carry, diagonal):
        rows = pl.ds(pl.multiple_of(ki * tq, tq), tq)
        out = []
        for h in range(nhb):
            m, l, acc = carry[h]
            cols = slice(h * hd, (h + 1) * hd)
            s = (_dot_nt(q_ref[:, cols], k_ref[rows, cols]) * (scale * LOG2E)
                 - fk_ref[h, pl.ds(ki, 1), :] * LOG2E)
            if diagonal:
                r = lax.broadcasted_iota(jnp.int32, s.shape, 0)
                c = lax.broadcasted_iota(jnp.int32, s.shape, 1)
                s = jnp.where(r >= c, s, NEG_INF)
            m_new = jnp.maximum(m, jnp.max(s, axis=1, keepdims=True))
            p = jnp.exp2(s - m_new)
            alpha = jnp.exp2(m - m_new)
            l = alpha * l + jnp.sum(p, axis=1, keepdims=True)
            acc = alpha * acc + _dot(p.astype(BF16), v_ref[rows, cols])
            out.append((m_new, l, acc))
        return tuple(out)

    init = tuple((jnp.full((tq, 1), NEG_INF, F32), jnp.zeros((tq, 1), F32), jnp.zeros((tq, hd), F32))
                 for _ in range(nhb))
    carry = lax.fori_loop(0, qi, lambda ki, c: step(ki, c, False), init)
    carry = step(qi, carry, True)
    o_ref[...] = jnp.concatenate([acc / l for _, l, acc in carry], axis=1).astype(o_ref.dtype)


def fox_attention_prompt(qkv, fk, *, B, L, H, hd, tq, nhb=2):
    nq = L // tq
    nhg = H // nhb
    return pl.pallas_call(
        functools.partial(_fox_prompt_kernel, tq=tq, nhb=nhb, hd=hd, scale=hd ** -0.5),
        grid=(B, nhg, nq),
        in_specs=[pl.BlockSpec((tq, nhb * hd), lambda b, g, i: (b * nq + i, g)),
                  pl.BlockSpec((L, nhb * hd), lambda b, g, i: (b, nhg + g)),
                  pl.BlockSpec((L, nhb * hd), lambda b, g, i: (b, 2 * nhg + g)),
                  pl.BlockSpec((nhb, nq, tq), lambda b, g, i: (b * nhg + g, 0, 0))],
        out_specs=pl.BlockSpec((tq, nhb * hd), lambda b, g, i: (b * nq + i, g)),
        out_shape=jax.ShapeDtypeStruct((B * L, H * hd), BF16),
        compiler_params=_cparams("parallel", "parallel", "arbitrary"),
        name="fox_attention_prompt",
    )(qkv, qkv, qkv, fk)


def _fox_cached_kernel(q_ref, kn_ref, vn_ref, kp_ref, vp_ref, f_ref, o_ref, *, P, nhb, hd, scale, precise,
                       head_rows):
    n = q_ref.shape[0]
    r = lax.broadcasted_iota(jnp.int32, (n, n), 0)
    c = lax.broadcasted_iota(jnp.int32, (n, n), 1)
    g = pl.program_id(1)

    def past(ref, h):
        if head_rows:
            return ref[pl.ds(g * nhb + h, P, stride=head_rows), :]
        return ref[:, h * hd:(h + 1) * hd]

    outs = []
    for h in range(nhb):
        cols = slice(h * hd, (h + 1) * hd)
        q = _parts(q_ref[:, cols], precise)
        f = f_ref[h]
        s_past = _mm_nt(q, _parts(past(kp_ref, h), precise)) * scale - f[:, :P]
        s_new = _mm_nt(q, _parts(kn_ref[:, cols], precise)) * scale - f[:, P:P + n]
        s_new = jnp.where(r >= c, s_new, NEG_INF)
        m = jnp.maximum(jnp.max(s_past, axis=1, keepdims=True), jnp.max(s_new, axis=1, keepdims=True))
        p_past = jnp.exp(s_past - m)
        p_new = jnp.exp(s_new - m)
        l = jnp.sum(p_past, axis=1, keepdims=True) + jnp.sum(p_new, axis=1, keepdims=True)
        acc = (_mm(_parts(p_past, precise), _parts(past(vp_ref, h), precise))
               + _mm(_parts(p_new, precise), _parts(vn_ref[:, cols], precise)))
        outs.append(acc / l)
    o_ref[...] = jnp.concatenate(outs, axis=1).astype(o_ref.dtype)


def fox_attention_cached(qkv, k_past, v_past, f, *, seq0, k_lane0, v_lane0, B, n, P, H, hd):
    precise = qkv.dtype == F32
    nhb = 4 if H % 4 == 0 else 1
    nhg = H // nhb
    w = nhb * hd
    head_rows = H if k_lane0 is None else 0
    if head_rows:
        past_specs = [pl.BlockSpec((None, P * H, hd), lambda b, g: (seq0 + b, 0, 0))] * 2
    else:
        past_specs = [pl.BlockSpec((None, P, w), lambda b, g: (seq0 + b, 0, k_lane0 // nhb + g)),
                      pl.BlockSpec((None, P, w), lambda b, g: (seq0 + b, 0, v_lane0 // nhb + g))]
    return pl.pallas_call(
        functools.partial(_fox_cached_kernel, P=P, nhb=nhb, hd=hd, scale=hd ** -0.5, precise=precise,
                          head_rows=head_rows),
        grid=(B, nhg),
        in_specs=[pl.BlockSpec((n, w), lambda b, g: (b, g)),
                  pl.BlockSpec((n, w), lambda b, g: (b, nhg + g)),
                  pl.BlockSpec((n, w), lambda b, g: (b, 2 * nhg + g))]
                 + past_specs
                 + [pl.BlockSpec((nhb, 1, f.shape[2]), lambda b, g: (b * nhg + g, 0, 0))],
        out_specs=pl.BlockSpec((n, w), lambda b, g: (b, g)),
        out_shape=jax.ShapeDtypeStruct((B * n, H * hd), qkv.dtype),
        compiler_params=_cparams("parallel", "arbitrary"),
        name="fox_attention_cached",
    )(qkv, qkv, qkv, k_past, v_past, f)


def _gelu_tanh(x):
    return 0.5 * x * (1.0 + jnp.tanh(math.sqrt(2.0 / math.pi) * (x + 0.044715 * (x * x * x))))


def _s5_kernel(*refs, nb, c, ns, ncm, snap_chunk):
    x_ref, rstd_ref, g_ref, bhi_ref, blo_ref = refs[:5]
    cm_refs = refs[5:5 + ncm]
    (ar_ref, ai_ref, d_ref, h0r_ref, h0i_ref, y_ref, hr_out, hi_out, hr_snap, hi_snap,
     u_sc, y_sc, bu_sc, st_sc, hr_sc, hi_sc) = refs[5 + ncm:]
    tc = pl.program_id(1)

    @pl.when(tc == 0)
    def _():
        hr_sc[...] = h0r_ref[...]
        hi_sc[...] = h0i_ref[...]

    g = g_ref[...]
    for b in range(nb):
        u_sc[pl.ds(b, c, stride=nb), :] = x_ref[b] * rstd_ref[b] * g
    u = u_sc[...]
    u_hi, u_lo = _split_bf16(u)
    bu_sc[...] = _dot(jnp.concatenate([u_hi, u_lo], axis=1), bhi_ref[0]) + _dot(u_hi, blo_ref[0])
    ar = jnp.broadcast_to(ar_ref[0], (nb, ns))
    ai = jnp.broadcast_to(ai_ref[0], (nb, ns))

    def step(t, carry):
        hr, hi = carry
        rows = pl.ds(pl.multiple_of(t * nb, nb), nb)
        nhr = ar * hr - ai * hi + bu_sc[rows, 0:ns]
        nhi = ar * hi + ai * hr + bu_sc[rows, ns:2 * ns]
        st_sc[rows, 0:ns] = nhr
        st_sc[rows, ns:2 * ns] = nhi
        return nhr, nhi

    hr, hi = lax.fori_loop(0, c, step, (hr_sc[...], hi_sc[...]))
    hr_sc[...] = hr
    hi_sc[...] = hi
    y = _mm(_parts(st_sc[...], ncm == 2), tuple(r[0] for r in cm_refs)) + d_ref[...] * u
    y_sc[...] = _gelu_tanh(y)
    for b in range(nb):
        y_ref[b] = y_sc[pl.ds(b, c, stride=nb), :].astype(y_ref.dtype)

    @pl.when(tc == snap_chunk)
    def _():
        hr_snap[...] = hr
        hi_snap[...] = hi

    @pl.when(tc == pl.num_programs(1) - 1)
    def _():
        hr_out[...] = hr
        hi_out[...] = hi


def s5_scan(x, rstd, g, h0r, h0i, consts, *, c, precise, snap_steps=None):
    bhi, blo, cm_hi, cm_lo, ar, ai, d = consts
    cms = (cm_hi, cm_lo) if precise else (cm_hi,)
    nb, L, D = x.shape
    nblk = D // LANES
    ns = ar.shape[2]
    ntc = L // c
    snap_chunk = ntc - 1 if snap_steps is None else snap_steps // c - 1
    blk = lambda g, t: (g, 0, 0)
    state_spec = pl.BlockSpec((nb, ns), lambda g, t: (0, g))
    state_shape = jax.ShapeDtypeStruct(h0r.shape, F32)
    return pl.pallas_call(
        functools.partial(_s5_kernel, nb=nb, c=c, ns=ns, ncm=len(cms), snap_chunk=snap_chunk),
        grid=(nblk, ntc),
        in_specs=[pl.BlockSpec((nb, c, LANES), lambda g, t: (0, t, g)),
                  pl.BlockSpec((nb, c, LANES), lambda g, t: (0, t, 0)),
                  pl.BlockSpec((1, LANES), lambda g, t: (0, g)),
                  pl.BlockSpec((1, 2 * LANES, 2 * ns), blk),
                  pl.BlockSpec((1, LANES, 2 * ns), blk)]
                 + [pl.BlockSpec((1, 2 * ns, LANES), blk)] * len(cms)
                 + [pl.BlockSpec((1, 1, ns), blk),
                    pl.BlockSpec((1, 1, ns), blk),
                    pl.BlockSpec((1, LANES), lambda g, t: (0, g)),
                    state_spec, state_spec],
        out_specs=[pl.BlockSpec((nb, c, LANES), lambda g, t: (0, t, g))] + [state_spec] * 4,
        out_shape=[jax.ShapeDtypeStruct((nb, L, D), F32 if precise else BF16)] + [state_shape] * 4,
        scratch_shapes=[pltpu.VMEM((c * nb, LANES), F32),
                        pltpu.VMEM((c * nb, LANES), F32),
                        pltpu.VMEM((c * nb, 2 * ns), F32),
                        pltpu.VMEM((c * nb, 2 * ns), F32),
                        pltpu.VMEM((nb, ns), F32),
                        pltpu.VMEM((nb, ns), F32)],
        compiler_params=_cparams("parallel", "arbitrary"),
        name="s5_scan",
    )(x, rstd, g.reshape(1, D), bhi, blo, *cms, ar, ai, d, h0r, h0i)


def s5_constants(a_re, a_im, b_re, b_im, c_re, c_im, d_skip, log_dt):
    G, N = a_re.shape
    gc = b_re.shape[2]
    gpb = LANES // gc
    nblk = G // gpb
    a = lax.complex(a_re.astype(F32), a_im.astype(F32))
    dt = jnp.exp(log_dt.astype(F32))[:, None]
    a_bar = jnp.exp(a * dt)
    b_bar = ((a_bar - 1.0) / a)[..., None] * lax.complex(b_re.astype(F32), b_im.astype(F32))
    eye = jnp.eye(gpb, dtype=F32)

    def in_mat(b):
        b = b.reshape(nblk, gpb, N, gc)
        return jnp.einsum("kgnc,gh->kgchn", b, eye).reshape(nblk, gpb * gc, gpb * N)

    def out_mat(cc):
        cc = cc.reshape(nblk, gpb, gc, N)
        return jnp.einsum("kgcn,gh->kgnhc", cc, eye).reshape(nblk, gpb * N, gpb * gc)

    bhi, blo = _split_weight(jnp.concatenate([in_mat(b_bar.real), in_mat(b_bar.imag)], axis=2))
    bhi = jnp.concatenate([bhi, bhi], axis=1)
    cm_hi, cm_lo = _split_weight(jnp.concatenate([out_mat(c_re.astype(F32)), -out_mat(c_im.astype(F32))], axis=1))
    ar = a_bar.real.reshape(nblk, 1, gpb * N)
    ai = a_bar.imag.reshape(nblk, 1, gpb * N)
    return bhi, blo, cm_hi, cm_lo, ar, ai, d_skip.astype(F32).reshape(1, G * gc)


def _gdn_conv_kernel(x_ref, halo_ref, buf_ref, w_ref, o_ref, ext_sc, *, tl, n_valid, tiles_per_seq, n_norm_tiles):
    r = pl.program_id(0)
    j = pl.program_id(1)
    first = (r % tiles_per_seq) == 0
    ext_sc[0:SUBLANES, :] = jnp.where(first, buf_ref[0], halo_ref[...])
    ext_sc[SUBLANES:SUBLANES + n_valid, :] = x_ref[...]
    w = w_ref[...]
    y = x_ref[...] * w[3:4, :]
    for s in (1, 2, 3):
        y = y + ext_sc[SUBLANES - s:SUBLANES - s + n_valid, :] * w[3 - s:4 - s, :]
    y = _silu(y)

    def store(val):
        if n_valid < tl:
            val = jnp.concatenate([val, jnp.zeros((tl - n_valid, val.shape[1]), F32)], axis=0)
        o_ref[...] = val.astype(o_ref.dtype)

    @pl.when(j < n_norm_tiles)
    def _():
        segs = []
        for a in range(0, y.shape[1], LANES):
            seg = y[:, a:a + LANES]
            segs.append(seg * lax.rsqrt(jnp.sum(seg * seg, axis=1, keepdims=True) + EPS))
        store(jnp.concatenate(segs, axis=1))

    @pl.when(j >= n_norm_tiles)
    def _():
        store(y)


def gdn_conv(x, buf8, w8, *, n_seq, seq_len, tl_in, tl_out, n_norm_cols, tc):
    C = x.shape[1]
    tiles_per_seq = seq_len // tl_in
    n_tiles = n_seq * tiles_per_seq
    hpt = tl_in // SUBLANES
    return pl.pallas_call(
        functools.partial(_gdn_conv_kernel, tl=tl_out, n_valid=tl_in, tiles_per_seq=tiles_per_seq,
                          n_norm_tiles=n_norm_cols // tc),
        grid=(n_tiles, C // tc),
        in_specs=[pl.BlockSpec((tl_in, tc), lambda r, j: (r, j)),
                  pl.BlockSpec((SUBLANES, tc), lambda r, j: (jnp.maximum(r * hpt - 1, 0), j)),
                  pl.BlockSpec((1, SUBLANES, tc), lambda r, j: (r // tiles_per_seq, 0, j)),
                  pl.BlockSpec((SUBLANES, tc), lambda r, j: (0, j))],
        out_specs=pl.BlockSpec((tl_out, tc), lambda r, j: (r, j)),
        out_shape=jax.ShapeDtypeStruct((n_tiles * tl_out, C), BF16),
        scratch_shapes=[pltpu.VMEM((SUBLANES + tl_in, tc), F32)],
        compiler_params=_cparams("parallel", "arbitrary"),
        name="gdn_conv",
    )(x, x, buf8, w8)


def _gdn_prep_kernel(qk_ref, gt_ref, ti_ref, a_ref, gc_ref, n_sc, nt_sc, x_sc, *, C, cps, n_qk, rep, hd, scale):
    nh = n_qk * rep
    n_sys = cps * nh
    pitch = C + SUBLANES
    r = lax.broadcasted_iota(jnp.int32, (C, C), 0)
    c = lax.broadcasted_iota(jnp.int32, (C, C), 1)
    tril = (r >= c).astype(BF16)
    lane_pad = jnp.zeros((C, LANES - C), F32)
    for cc in range(cps):
        rows = slice(cc * C, (cc + 1) * C)
        gates = gt_ref[rows, :]
        lane = lax.broadcasted_iota(jnp.int32, gates.shape, 1)
        g = jnp.where((lane >= nh) & (lane < 2 * nh), gates, 0.0)
        g_hi = g.astype(BF16)
        g_r1 = g - g_hi.astype(F32)
        g_mid = g_r1.astype(BF16)
        g_lo = (g_r1 - g_mid.astype(F32)).astype(BF16)
        gcum = _dot(tril, g_hi) + _dot(tril, g_mid) + _dot(tril, g_lo)
        gc_ref[rows, :] = gcum
        gcum_t = gcum.T
        for hq in range(n_qk):
            q = qk_ref[rows, hq * hd:(hq + 1) * hd]
            k = qk_ref[rows, (n_qk + hq) * hd:(n_qk + hq + 1) * hd]
            kk = _dot_nt(k, k)
            qk = _dot_nt(q, k) * scale
            for rr in range(rep):
                hv = hq * rep + rr
                sys = cc * nh + hv
                diff = gcum[:, nh + hv:nh + hv + 1] - gcum_t[nh + hv:nh + hv + 1, :]
                decay = jnp.exp(jnp.where(r >= c, diff, NEG_INF))
                n_val = jnp.where(r > c, gates[:, hv:hv + 1] * kk * decay, 0.0)
                n_sc[sys * pitch:sys * pitch + C, :] = jnp.concatenate([n_val, lane_pad], axis=1)
                a_ref[cc, hv] = (qk * decay).astype(a_ref.dtype)
    for i in range(C):
        slab = n_sc[pl.ds(i, n_sys, stride=pitch), :]
        nt_sc[i * C:(i + 1) * C, :] = slab.T[:C, :]
    _tri_solve(nt_sc, x_sc, C)
    row_pad = jnp.zeros((LANES - C, n_sys), F32)
    for i in range(C):
        cols = jnp.concatenate([x_sc[i * C:(i + 1) * C, :], row_pad], axis=0)
        n_sc[pl.ds(i, n_sys, stride=pitch), :] = cols.T
    for cc in range(cps):
        for hv in range(nh):
            sys = cc * nh + hv
            ti_ref[cc, hv] = n_sc[sys * pitch:sys * pitch + C, :C].astype(ti_ref.dtype)


def gdn_prep(qkv_c, gates, *, C, n_qk, rep, hd):
    rows = qkv_c.shape[0]
    nch = rows // C
    nh = n_qk * rep
    cps = LANES // nh
    assert nch % cps == 0 and cps * nh == LANES
    return pl.pallas_call(
        functools.partial(_gdn_prep_kernel, C=C, cps=cps, n_qk=n_qk, rep=rep, hd=hd, scale=hd ** -0.5),
        grid=(nch // cps,),
        in_specs=[pl.BlockSpec((cps * C, 2 * n_qk * hd), lambda n: (n, 0)),
                  pl.BlockSpec((cps * C, LANES), lambda n: (n, 0))],
        out_specs=[pl.BlockSpec((cps, nh, C, C), lambda n: (n, 0, 0, 0)),
                   pl.BlockSpec((cps, nh, C, C), lambda n: (n, 0, 0, 0)),
                   pl.BlockSpec((cps * C, LANES), lambda n: (n, 0))],
        out_shape=[jax.ShapeDtypeStruct((nch, nh, C, C), BF16),
                   jax.ShapeDtypeStruct((nch, nh, C, C), BF16),
                   jax.ShapeDtypeStruct((rows, LANES), F32)],
        scratch_shapes=[pltpu.VMEM((LANES * (C + SUBLANES), LANES), F32),
                        pltpu.VMEM((C * C, LANES), F32),
                        pltpu.VMEM((C * C, LANES), F32)],
        compiler_params=_cparams("parallel"),
        name="gdn_prep",
    )(qkv_c, gates)


def _tri_solve(n_ref, x_ref, C):
    nblk = C // SUBLANES
    lanes = n_ref.shape[1]
    sub = lax.broadcasted_iota(jnp.int32, (SUBLANES, lanes), 0)
    x_ref[...] = jnp.zeros(x_ref.shape, F32)
    for ib in range(nblk):
        def row(r, _, ib=ib):
            base = (ib * SUBLANES + r) * C
            acc = [jnp.zeros((SUBLANES, lanes), F32) for _ in range(ib)]
            acc.append(jnp.where(sub == r, 1.0, 0.0).astype(F32))
            for jb in range(ib + 1):
                for j in range(jb * SUBLANES, (jb + 1) * SUBLANES):
                    nb = jnp.broadcast_to(n_ref[pl.ds(base + j, 1), :], (SUBLANES, lanes))
                    for cb in range(jb + 1):
                        acc[cb] = acc[cb] - nb * x_ref[j * C + cb * SUBLANES:j * C + (cb + 1) * SUBLANES, :]
            for cb in range(ib + 1):
                x_ref[pl.ds(pl.multiple_of(base + cb * SUBLANES, SUBLANES), SUBLANES), :] = acc[cb]
            return 0

        lax.fori_loop(0, SUBLANES, row, 0)


def _gdn_scan_kernel(q_ref, k_ref, v_ref, z_ref, gt_ref, gc_ref, ti_ref, a_ref, s0_ref, ng_ref,
                     o_ref, s_out, s_sc, *, C, hg, rep, hd, nh, n_valid, scale):
    n = pl.program_id(2)
    hgi = pl.program_id(1)

    @pl.when(n == 0)
    def _():
        s_sc[...] = s0_ref[0]

    shift = (LANES - hgi * hg) % LANES
    gates = pltpu.roll(gt_ref[...], shift, 1)
    gcum = pltpu.roll(gc_ref[...], shift, 1)
    heads = range(hg)
    s_old = [s_sc[hh] for hh in heads]
    s_bf = [s.astype(BF16) for s in s_old]
    beta = [gates[:, hh:hh + 1] for hh in heads]
    gc = [gcum[:, nh + hh:nh + hh + 1] for hh in heads]
    eg = [jnp.exp(g) for g in gc]
    g_tot = [g[C - 1:C, :] for g in gc]
    k = [k_ref[:, (hh // rep) * hd:(hh // rep + 1) * hd].astype(F32) for hh in heads]
    q = [q_ref[:, (hh // rep) * hd:(hh // rep + 1) * hd].astype(F32) for hh in heads]
    v = [v_ref[:, hh * hd:(hh + 1) * hd].astype(F32) for hh in heads]
    uw = [_dot(ti_ref[0, hh].astype(BF16),
               jnp.concatenate([v[hh] * beta[hh], k[hh] * (beta[hh] * eg[hh])], axis=1).astype(BF16))
          for hh in heads]
    q_dec = [(q[hh] * (scale * eg[hh])).astype(BF16) for hh in heads]
    k_dec = [(k[hh] * jnp.exp(g_tot[hh] - gc[hh])).astype(BF16) for hh in heads]
    v_new = [(uw[hh][:, :hd] - _dot(uw[hh][:, hd:].astype(BF16), s_bf[hh])).astype(BF16) for hh in heads]
    o = [_dot(q_dec[hh], s_bf[hh]) + _dot(a_ref[0, hh], v_new[hh]) for hh in heads]
    s_new = [s_old[hh] * jnp.exp(g_tot[hh]) + _dot_tn(k_dec[hh], v_new[hh]) for hh in heads]
    for hh in heads:
        s_sc[hh] = s_new[hh]
    ng = ng_ref[...]
    outs = [_rms_scale(o[hh][:n_valid], ng) * _silu(z_ref[:, hh * hd:(hh + 1) * hd].astype(F32)) for hh in heads]
    o_ref[...] = jnp.concatenate(outs, axis=1).astype(o_ref.dtype)

    @pl.when(n == pl.num_programs(2) - 1)
    def _():
        s_out[0] = s_sc[...]


def gdn_scan(qkv_c, z, gates, gcum, tinv, amat, s0, norm_g, *, n_seq, nc, C, n_valid, n_qk, rep, hd, hg):
    nh = n_qk * rep
    nhg = nh // hg
    qw = (hg // rep) * hd
    k_off = n_qk * hd // qw
    v_off = 2 * n_qk * hd // (hg * hd)
    chunk = lambda s, g, n: s * nc + n
    return pl.pallas_call(
        functools.partial(_gdn_scan_kernel, C=C, hg=hg, rep=rep, hd=hd, nh=nh, n_valid=n_valid,
                          scale=hd ** -0.5),
        grid=(n_seq, nhg, nc),
        in_specs=[pl.BlockSpec((C, qw), lambda s, g, n: (chunk(s, g, n), g)),
                  pl.BlockSpec((C, qw), lambda s, g, n: (chunk(s, g, n), k_off + g)),
                  pl.BlockSpec((C, hg * hd), lambda s, g, n: (chunk(s, g, n), v_off + g)),
                  pl.BlockSpec((n_valid, hg * hd), lambda s, g, n: (chunk(s, g, n), g)),
                  pl.BlockSpec((C, LANES), lambda s, g, n: (chunk(s, g, n), 0)),
                  pl.BlockSpec((C, LANES), lambda s, g, n: (chunk(s, g, n), 0)),
                  pl.BlockSpec((1, hg, C, C), lambda s, g, n: (chunk(s, g, n), g, 0, 0)),
                  pl.BlockSpec((1, hg, C, C), lambda s, g, n: (chunk(s, g, n), g, 0, 0)),
                  pl.BlockSpec((1, hg, hd, hd), lambda s, g, n: (s, g, 0, 0)),
                  pl.BlockSpec((1, hd), lambda s, g, n: (0, 0))],
        out_specs=[pl.BlockSpec((n_valid, hg * hd), lambda s, g, n: (chunk(s, g, n), g)),
                   pl.BlockSpec((1, hg, hd, hd), lambda s, g, n: (s, g, 0, 0))],
        out_shape=[jax.ShapeDtypeStruct((n_seq * nc * n_valid, nh * hd), BF16),
                   jax.ShapeDtypeStruct((n_seq, nh, hd, hd), F32)],
        scratch_shapes=[pltpu.VMEM((hg, hd, hd), F32)],
        compiler_params=_cparams("parallel", "parallel", "arbitrary"),
        name="gdn_scan",
    )(qkv_c, qkv_c, qkv_c, z, gates, gcum, tinv, amat, s0, norm_g.reshape(1, hd))


def _trunk(x, p, fox_hist, s5_h0, gdn_state, W, *, n_seq, seq_len, steps, precise, snap_steps=None,
           fox_kv=None):
    M, D = x.shape
    H, hd = W["fox_heads"]
    n_qk, rep, gd = W["gdn_heads"]
    n_vh = n_qk * rep
    key_dim, val_dim = n_qk * gd, n_vh * gd
    qkv_dim = 2 * key_dim + val_dim
    cw = W["gdn_conv_w"].shape[1]
    tn = min(512, D)
    outs = {} if fox_kv is None else {"fox_kv": fox_kv}

    def wsel(name, idx, pr):
        hi = W[name][idx]
        return (hi, W[name + "_lo"][idx]) if pr else hi

    def mixer(i, x):
        j = i // 3
        pm = precise
        g_mix = W["norm_mix"][i]
        if i % 3 == 0:
            qkv, k_heads, v_heads = fox_in_proj(x, g_mix, wsel("fox_w_qkv", j, pm), n_heads=H, hd=hd,
                                                tn=tn if pm else min(2 * tn, D), bufs=outs.get("fox_kv"), slot=j, n_slots=W["n_fox"])
            outs["fox_kv"] = (k_heads, v_heads)
            outs[("fox_qkv", j)] = qkv
            logf = small_proj(x, g_mix, W["fox_w_f"][j], W["fox_aux"][j], "logf")[:, :H]
            lf = jnp.swapaxes(logf.reshape(n_seq, seq_len, H), 1, 2)
            if j not in fox_hist:
                tq = next(t for t in (512, 384, 256, 128, seq_len) if seq_len % t == 0)
                fk = lane_cumsum(lf.reshape(n_seq * H, seq_len)).reshape(n_seq * H, seq_len // tq, tq)
                o = fox_attention_prompt(qkv, fk, B=n_seq, L=seq_len, H=H, hd=hd, tq=tq)
            else:
                k_past, v_past, k_lane0, v_lane0, seq0, lf_past, P = fox_hist[j]
                fpad = -(-(P + seq_len) // LANES) * LANES
                lf_all = jnp.concatenate([jnp.swapaxes(lf_past.astype(F32), 1, 2), lf], axis=2)
                lf_all = jnp.pad(lf_all, ((0, 0), (0, 0), (0, fpad - (P + seq_len))))
                f = lane_cumsum(lf_all.reshape(n_seq * H, fpad)).reshape(n_seq * H, 1, fpad)
                o = fox_attention_cached(qkv, k_past, v_past, f, seq0=seq0, k_lane0=k_lane0, v_lane0=v_lane0,
                                         B=n_seq, n=seq_len, P=P, H=H, hd=hd)
            outs[("fox", j)] = logf
            return mm_res(o, wsel("fox_w_out", j, pm), x, tn=tn)
        if i % 3 == 1:
            rstd = row_rstd(x).reshape(n_seq, seq_len, LANES)
            y, hr, hi, hr_snap, hi_snap = s5_scan(x.reshape(n_seq, seq_len, D), rstd, g_mix,
                                                  s5_h0[j][0], s5_h0[j][1], W["s5_consts"][j],
                                                  c=min(CHUNK, seq_len), precise=pm, snap_steps=snap_steps)
            outs[("s5", j)] = (hr, hi)
            outs[("s5_snap", j)] = (hr_snap, hi_snap)
            return glu_res(y.reshape(M, D), wsel("s5_w_glu", j, pm), x, tn=tn)
        s0, conv_rows = gdn_state[j]
        nq = qkv_dim // tn
        nz = val_dim // tn
        qkv_raw, z = norm_mm(x, g_mix, W["gdn_w_qkvz"][j], [(F32, 0, nq), (BF16, nq, nq + nz)], tn=tn)
        gates = small_proj(x, g_mix, W["gdn_w_ba"][j], W["gdn_aux"][j], "gdn", nh=n_vh)
        C = CHUNK
        n_valid = min(C, seq_len)
        nc = seq_len // n_valid
        buf = jnp.pad(conv_rows.astype(F32), ((0, 0), (SUBLANES - (cw - 1), 0), (0, 0)))
        tl_in = min(512, seq_len)
        qkv_c = gdn_conv(qkv_raw, buf, W["gdn_conv_w8"][j], n_seq=n_seq, seq_len=seq_len, tl_in=tl_in,
                         tl_out=max(tl_in, C), n_norm_cols=2 * key_dim,
                         tc=key_dim if tl_in < C else min(512, key_dim))
        gates_c = jnp.pad(gates.reshape(n_seq * nc, n_valid, LANES),
                          ((0, 0), (0, C - n_valid), (0, 0))).reshape(n_seq * nc * C, LANES)
        tinv, amat, gcum = gdn_prep(qkv_c, gates_c, C=C, n_qk=n_qk, rep=rep, hd=gd)
        o, s_last = gdn_scan(qkv_c, z, gates_c, gcum, tinv, amat, s0.astype(F32), W["gdn_norm"][j],
                             n_seq=n_seq, nc=nc, C=C, n_valid=n_valid, n_qk=n_qk, rep=rep, hd=gd,
                             hg=min(16, n_vh))
        rows = jnp.concatenate([conv_rows.astype(F32), qkv_raw.reshape(n_seq, seq_len, qkv_dim)], axis=1)
        outs[("gdn", j)] = (s_last, rows[:, -(cw - 1):])
        return mm_res(o, W["gdn_w_out"][j], x, tn=tn)

    def channel(i, x):
        m = i // 2
        g_ffn = W["norm_ffn"][i]
        if i % 2 == 0:
            pf = precise
            h = glu_up(x, g_ffn, wsel("ffn_w_gate", m, pf), wsel("ffn_w_up", m, pf), tn=tn)
            x = mm_res(h, wsel("ffn_w_down", m, pf), x, tn=tn)
        else:
            pf = False
            comb = small_proj(x, g_ffn, W["moe_w_router"][m], W["zero_aux"], "router")
            h = glu_up(x, g_ffn, W["moe_w_gate"][m], W["moe_w_up"][m], comb, tn=tn)
            x = mm_res(h, W["moe_w_down"][m], x, tn=tn)
        p_i = p[i] if pf else p[i].astype(BF16)
        return ple(x, W["norm_ple"][i], wsel("ple_w_gate", i, pf), p_i, wsel("ple_w_up", i, pf), tn=tn)

    for i, part in steps:
        x = mixer(i, x) if part == "mixer" else channel(i, x)
    return x, outs


def kernel(x_prompt, x_sample, cache_fox_k, cache_fox_v, cache_fox_logf, state_s5_re, state_s5_im, state_gdn, state_gdn_conv, p_prompt, p_sample, norm_mix, norm_ffn, norm_ple, norm_final, fox_w_in, fox_b_f, fox_w_out, s5_a_re, s5_a_im, s5_b_re, s5_b_im, s5_c_re, s5_c_im, s5_d, s5_log_dt, s5_w_glu, gdn_w_in, gdn_conv_w, gdn_a_log, gdn_dt_bias, gdn_norm, gdn_w_out, ffn_w_gate, ffn_w_up, ffn_w_down, moe_w_router, moe_w_gate, moe_w_up, moe_w_down, ple_w_up, ple_w_gate):
    B, L, D = x_prompt.shape
    Bs, Ls, _ = x_sample.shape
    depth = norm_mix.shape[0]
    P = cache_fox_k.shape[2]
    H, hd = cache_fox_k.shape[3], cache_fox_k.shape[4]
    n_vh, gd = state_gdn.shape[2], state_gdn.shape[3]
    qkv_dim = state_gdn_conv.shape[3]
    val_dim = n_vh * gd
    n_qk = (qkv_dim - val_dim) // (2 * gd)
    cw = gdn_conv_w.shape[1]
    G, NS = s5_a_re.shape[1], s5_a_re.shape[2]
    E, _, Fe = moe_w_gate.shape[1:]
    n_fox, n_s5, n_gdn = fox_w_in.shape[0], s5_a_re.shape[0], gdn_w_in.shape[0]

    first_router = 1
    n_lo = {"fox": 1, "ffn": 1, "ple": 1, "s5": 1}

    def hi_lo(w, n):
        return w.astype(BF16), _split_weight(w[:n])[1]

    W = {"norm_mix": norm_mix, "norm_ffn": norm_ffn, "norm_ple": norm_ple, "norm_final": norm_final,
         "fox_heads": (H, hd), "n_fox": n_fox, "gdn_heads": (n_qk, n_vh // n_qk, gd), "gdn_conv_w": gdn_conv_w,
         "gdn_norm": gdn_norm, "zero_aux": jnp.zeros((SUBLANES, LANES), F32)}
    W["fox_w_qkv"], W["fox_w_qkv_lo"] = hi_lo(fox_w_in[:, :, :3 * D], n_lo["fox"])
    W["fox_w_out"], W["fox_w_out_lo"] = hi_lo(fox_w_out, n_lo["fox"])
    W["fox_w_f"] = [_pad_cols(fox_w_in[j, :, 3 * D:]) for j in range(n_fox)]
    W["fox_aux"] = [_aux_rows((0, fox_b_f[j])) for j in range(n_fox)]
    W["s5_w_glu"], W["s5_w_glu_lo"] = hi_lo(s5_w_glu, n_lo["s5"])
    W["s5_consts"] = [s5_constants(s5_a_re[j], s5_a_im[j], s5_b_re[j], s5_b_im[j], s5_c_re[j], s5_c_im[j],
                                   s5_d[j], s5_log_dt[j]) for j in range(n_s5)]
    W["gdn_w_qkvz"] = gdn_w_in[:, :, :qkv_dim + val_dim].astype(BF16)
    W["gdn_w_ba"] = [_pad_cols(gdn_w_in[j, :, qkv_dim + val_dim:]) for j in range(n_gdn)]
    W["gdn_aux"] = [_aux_rows((n_vh, gdn_dt_bias[j]), (n_vh, gdn_a_log[j])) for j in range(n_gdn)]
    W["gdn_conv_w8"] = jnp.pad(gdn_conv_w.astype(F32), ((0, 0), (0, SUBLANES - cw), (0, 0)))
    W["gdn_w_out"] = gdn_w_out.astype(BF16)
    W["ffn_w_gate"], W["ffn_w_gate_lo"] = hi_lo(ffn_w_gate, n_lo["ffn"])
    W["ffn_w_up"], W["ffn_w_up_lo"] = hi_lo(ffn_w_up, n_lo["ffn"])
    W["ffn_w_down"], W["ffn_w_down_lo"] = hi_lo(ffn_w_down, n_lo["ffn"])
    W["moe_w_router"] = [_pad_cols(moe_w_router[m]) for m in range(moe_w_router.shape[0])]
    W["moe_w_gate"] = moe_w_gate.astype(BF16)
    W["moe_w_up"] = moe_w_up.astype(BF16)
    W["moe_w_down"] = moe_w_down.astype(BF16).reshape(-1, E * Fe, D)
    W["ple_w_gate"], W["ple_w_gate_lo"] = hi_lo(ple_w_gate, n_lo["ple"])
    W["ple_w_up"], W["ple_w_up_lo"] = hi_lo(ple_w_up, n_lo["ple"])

    steps = [(i, part) for i in range(depth) for part in ("mixer", "channel")]
    steps_a, steps_b = steps[:2 * first_router + 1], steps[2 * first_router + 1:]
    pd = p_prompt.shape[-1]

    Lt = LANES
    Lh = L - Lt
    p_all = {i: p_prompt[i].reshape(B * L, pd) for i in range(depth)}
    p_tail = {i: p_prompt[i][:, Lh:].reshape(B * Lt, pd) for i in range(depth)}
    zeros_s5 = {j: (jnp.zeros((B, G * NS), F32),) * 2 for j in range(n_s5)}
    x_f, out_p = _trunk(x_prompt.reshape(B * L, D), p_all, {}, zeros_s5, {}, W,
                        n_seq=B, seq_len=L, steps=steps_a, precise=False, snap_steps=Lh)
    hist_t = {}
    for j in range(n_fox):
        if ("fox", j) in out_p:
            qkv_f = out_p[("fox_qkv", j)].reshape(B, L, 3 * D)
            hist_t[j] = (qkv_f, qkv_f, H, 2 * H, 0, out_p[("fox", j)].reshape(B, L, H)[:, :Lh], Lh)
    xt, out_t = _trunk(x_prompt[:, Lh:].reshape(B * Lt, D), p_tail, hist_t,
                       {j: out_p[("s5_snap", j)] for j in range(n_s5) if ("s5_snap", j) in out_p}, {}, W,
                       n_seq=B, seq_len=Lt, steps=steps_a, precise=True)
    x_p = lax.dynamic_update_slice(x_f.reshape(B, L, D), xt.reshape(B, Lt, D), (0, Lh, 0)).reshape(B * L, D)
    out_p.update({key: val for key, val in out_t.items() if key[0] == "s5"})
    gdn0 = (jnp.zeros((B, n_vh, gd, gd), F32), jnp.zeros((B, cw - 1, qkv_dim), F32))
    x_p, out_b = _trunk(x_p, p_all, {}, zeros_s5, {j: gdn0 for j in range(n_gdn)}, W,
                        n_seq=B, seq_len=L, steps=steps_b, precise=False, fox_kv=out_p.get("fox_kv"))
    out_p.update(out_b)

    ck = cache_fox_k.reshape(-1, P * H, hd)
    cv = cache_fox_v.reshape(-1, P * H, hd)
    s5_re = state_s5_re.reshape(n_s5, Bs, G * NS).astype(F32)
    s5_im = state_s5_im.reshape(n_s5, Bs, G * NS).astype(F32)
    hist_s = {j: (ck, cv, None, None, j * Bs, cache_fox_logf[j], P) for j in range(n_fox)}
    state_s = ({j: (s5_re[j], s5_im[j]) for j in range(n_s5)},
               {j: (state_gdn[j], state_gdn_conv[j]) for j in range(n_gdn)})
    p_s = {i: p_sample[i].reshape(Bs * Ls, pd) for i in range(depth)}
    x_s, out_s = _trunk(x_sample.reshape(Bs * Ls, D), p_s, hist_s, *state_s, W,
                        n_seq=Bs, seq_len=Ls, steps=steps_a, precise=True)
    x_s, out_sb = _trunk(x_s, p_s, hist_s, *state_s, W, n_seq=Bs, seq_len=Ls, steps=steps_b, precise=False,
                         fox_kv=out_s.get("fox_kv"))
    out_s.update(out_sb)

    def leaves(x, out, nb, sl):
        y = rmsnorm(x, norm_final).reshape(nb, sl, D)
        fk, fv = (a.reshape(n_fox, nb, sl, H, hd) for a in out["fox_kv"])
        flf = jnp.stack([out[("fox", j)].reshape(nb, sl, H) for j in range(n_fox)])
        sre = jnp.stack([out[("s5", j)][0].reshape(nb, G, NS) for j in range(n_s5)])
        sim = jnp.stack([out[("s5", j)][1].reshape(nb, G, NS) for j in range(n_s5)])
        gs = jnp.stack([out[("gdn", j)][0] for j in range(n_gdn)])
        gc = jnp.stack([out[("gdn", j)][1] for j in range(n_gdn)])
        return y, fk, fv, flf, sre, sim, gs, gc

    (y_p, fk_p, fv_p, flf_p, sre_p, sim_p, gs_p, gc_p) = leaves(x_p, out_p, B, L)
    (y_s, fk_s, fv_s, flf_s, sre_s, sim_s, gs_s, gc_s) = leaves(x_s, out_s, Bs, Ls)
    return (y_p, y_s, fk_p, fv_p, flf_p, fk_s, fv_s, flf_s, sre_p, sim_p, sre_s, sim_s,
            gs_p, gc_p, gs_s, gc_s)
```

```python
import functools
import math

import jax
import jax.numpy as jnp
from jax import lax
from jax.experimental import pallas as pl
from jax.experimental.pallas import tpu as pltpu

F32 = jnp.float32
BF16 = jnp.bfloat16
EPS = 1e-6
LANES = 128
SUBLANES = 8
VMEM_LIMIT = 56 * 1024 * 1024
CHUNK = 64
N_EXPERTS = 8
NEG_INF = float("-inf")
LOG2E = math.log2(math.e)


def _cparams(*sem):
    return pltpu.CompilerParams(dimension_semantics=sem, vmem_limit_bytes=VMEM_LIMIT)


def _rms_scale(x, g):
    ms = jnp.mean(x * x, axis=-1, keepdims=True)
    return x * lax.rsqrt(ms + EPS) * g


def _sigmoid(x):
    return 0.5 * jnp.tanh(0.5 * x) + 0.5


def _silu(x):
    return x * _sigmoid(x)


def _softplus(x):
    return jnp.maximum(x, 0.0) + jnp.log1p(jnp.exp(-jnp.abs(x)))


def _split_bf16(x):
    hi = x.astype(BF16)
    lo = (x - hi.astype(F32)).astype(BF16)
    return hi, lo


def _split_weight(w):
    w = w.astype(F32)
    hi = lax.reduce_precision(w, exponent_bits=8, mantissa_bits=7)
    return hi.astype(BF16), (w - hi).astype(BF16)


def _dot(a, b):
    return jnp.dot(a, b, preferred_element_type=F32)


def _dot_nt(a, b):
    return lax.dot_general(a, b, (((1,), (1,)), ((), ())), preferred_element_type=F32)


def _dot_tn(a, b):
    return lax.dot_general(a, b, (((0,), (0,)), ((), ())), preferred_element_type=F32)


def _parts(x, precise):
    return _split_bf16(x.astype(F32)) if precise else (x.astype(BF16),)


def _mm(xp, wp):
    if len(wp) == 1:
        return _dot(xp[0], wp[0])
    return _dot(xp[0], wp[0]) + _dot(xp[1], wp[0]) + _dot(xp[0], wp[1])


def _mm_nt(ap, bp):
    if len(ap) == 1:
        return _dot_nt(ap[0], bp[0])
    return _dot_nt(ap[0], bp[0]) + _dot_nt(ap[1], bp[0]) + _dot_nt(ap[0], bp[1])


def _read(refs):
    return tuple(r[...] for r in refs)


def _wparts(w):
    return tuple(w) if isinstance(w, (tuple, list)) else (w,)


def _row_tile(m, n_weight_parts=1):
    cap = 1024 // n_weight_parts
    return next(t for t in (1024, 512, 256, 128) if t <= cap and m % t == 0)


def _store_norm(x_ref, g_ref, xn_refs):
    xn = _rms_scale(x_ref[...], g_ref[...])
    for ref, part in zip(xn_refs, _parts(xn, len(xn_refs) == 2)):
        ref[...] = part


def _norm_mm_kernel(*refs, ranges, nw):
    x_ref, g_ref = refs[:2]
    w_refs = refs[2:2 + nw]
    outs = refs[2 + nw:2 + nw + len(ranges)]
    xn_refs = refs[2 + nw + len(ranges):]
    j = pl.program_id(1)

    @pl.when(j == 0)
    def _():
        _store_norm(x_ref, g_ref, xn_refs)

    acc = _mm(_read(xn_refs), _read(w_refs))
    for o_ref, (a, b) in zip(outs, ranges):
        @pl.when((j >= a) & (j < b))
        def _(o_ref=o_ref):
            o_ref[...] = acc.astype(o_ref.dtype)


def _range_map(i, j, *, a, n):
    return (i, jnp.clip(j - a, 0, n - 1))


def norm_mm(x, g, w, outs, *, tn):
    M, D = x.shape
    wp = _wparts(w)
    N = wp[0].shape[1]
    tm = _row_tile(M, len(wp))
    ranges = tuple((a, b) for _, a, b in outs)
    return pl.pallas_call(
        functools.partial(_norm_mm_kernel, ranges=ranges, nw=len(wp)),
        grid=(M // tm, N // tn),
        in_specs=[pl.BlockSpec((tm, D), lambda i, j: (i, 0)),
                  pl.BlockSpec((1, D), lambda i, j: (0, 0))]
                 + [pl.BlockSpec((D, tn), lambda i, j: (0, j))] * len(wp),
        out_specs=[pl.BlockSpec((tm, tn), functools.partial(_range_map, a=a, n=b - a))
                   for _, a, b in outs],
        out_shape=[jax.ShapeDtypeStruct((M, (b - a) * tn), dt) for dt, a, b in outs],
        scratch_shapes=[pltpu.VMEM((tm, D), BF16)] * len(wp),
        compiler_params=_cparams("parallel", "arbitrary"),
        name="norm_mm",
    )(x, g.reshape(1, D), *wp)


def _fox_in_kernel(*refs, nw, nt, n_heads, hd, tm, n_bufs):
    x_ref, g_ref = refs[:2]
    w_refs = refs[2:2 + nw]
    qkv_ref, k_ref, v_ref = refs[2 + nw + n_bufs:5 + nw + n_bufs]
    xn_refs = refs[5 + nw + n_bufs:]
    j = pl.program_id(1)

    @pl.when(j == 0)
    def _():
        _store_norm(x_ref, g_ref, xn_refs)

    acc = _mm(_read(xn_refs), _read(w_refs))
    qkv_ref[...] = acc.astype(qkv_ref.dtype)
    hpt = acc.shape[1] // hd
    for dst, base in ((k_ref, nt), (v_ref, 2 * nt)):
        for jj in range(nt):
            @pl.when(j == base + jj)
            def _(dst=dst, jj=jj):
                for hh in range(hpt):
                    dst[pl.ds(jj * hpt + hh, tm, stride=n_heads), :] = acc[:, hh * hd:(hh + 1) * hd]


def fox_in_proj(x, g, w, *, n_heads, hd, tn, bufs=None, slot=0, n_slots=1):
    M, D = x.shape
    wp = _wparts(w)
    nw = len(wp)
    tm = min(512, M)
    nt = D // tn
    n_bufs = 0 if bufs is None else 2
    blk0 = slot * (M // tm)
    head_spec = pl.BlockSpec((tm * n_heads, hd), lambda i, j: (blk0 + i, 0))
    head_shape = jax.ShapeDtypeStruct((n_slots * M * n_heads, hd), F32)
    return pl.pallas_call(
        functools.partial(_fox_in_kernel, nw=nw, nt=nt, n_heads=n_heads, hd=hd, tm=tm, n_bufs=n_bufs),
        grid=(M // tm, 3 * nt),
        in_specs=[pl.BlockSpec((tm, D), lambda i, j: (i, 0)),
                  pl.BlockSpec((1, D), lambda i, j: (0, 0))]
                 + [pl.BlockSpec((D, tn), lambda i, j: (0, j))] * nw
                 + [pl.BlockSpec(memory_space=pl.ANY)] * n_bufs,
        out_specs=[pl.BlockSpec((tm, tn), lambda i, j: (i, j)), head_spec, head_spec],
        out_shape=[jax.ShapeDtypeStruct((M, 3 * D), F32 if nw == 2 else BF16), head_shape, head_shape],
        scratch_shapes=[pltpu.VMEM((tm, D), BF16)] * nw,
        input_output_aliases={} if bufs is None else {2 + nw: 1, 3 + nw: 2},
        compiler_params=_cparams("parallel", "arbitrary"),
        name="fox_in_proj",
    )(x, g.reshape(1, D), *wp, *(bufs or ()))


def _glu_up_kernel(*refs, tiles_per_expert, nw):
    x_ref, g_ref = refs[:2]
    wg_refs = refs[2:2 + nw]
    wu_refs = refs[2 + nw:2 + 2 * nw]
    rest = refs[2 + 2 * nw:]
    comb_ref = rest[0] if tiles_per_expert else None
    o_ref = rest[1] if tiles_per_expert else rest[0]
    xn_refs = rest[2:] if tiles_per_expert else rest[1:]
    j = pl.program_id(1)

    @pl.when(j == 0)
    def _():
        _store_norm(x_ref, g_ref, xn_refs)

    xn = _read(xn_refs)
    h = _silu(_mm(xn, _read(wg_refs))) * _mm(xn, _read(wu_refs))
    if tiles_per_expert:
        comb = comb_ref[...]
        lane = lax.broadcasted_iota(jnp.int32, comb.shape, 1)
        col = jnp.sum(jnp.where(lane == j // tiles_per_expert, comb, 0.0), axis=1, keepdims=True)
        h = h * col
    o_ref[...] = h.astype(o_ref.dtype)


def glu_up(x, g, w_gate, w_up, comb=None, *, tn):
    M, D = x.shape
    wg, wu = _wparts(w_gate), _wparts(w_up)
    nw = len(wg)
    tm = _row_tile(M, nw)
    if comb is None:
        n_out, tpe = wg[0].shape[1], 0
        w_spec = pl.BlockSpec((D, tn), lambda i, j: (0, j))
        extra_in, extra_specs = (), []
    else:
        E, _, F = wg[0].shape
        n_out, tpe = E * F, F // tn
        w_spec = pl.BlockSpec((None, D, tn), lambda i, j: (j // tpe, 0, j % tpe))
        extra_in = (comb,)
        extra_specs = [pl.BlockSpec((tm, LANES), lambda i, j: (i, 0))]
    return pl.pallas_call(
        functools.partial(_glu_up_kernel, tiles_per_expert=tpe, nw=nw),
        grid=(M // tm, n_out // tn),
        in_specs=[pl.BlockSpec((tm, D), lambda i, j: (i, 0)),
                  pl.BlockSpec((1, D), lambda i, j: (0, 0))] + [w_spec] * (2 * nw) + extra_specs,
        out_specs=pl.BlockSpec((tm, tn), lambda i, j: (i, j)),
        out_shape=jax.ShapeDtypeStruct((M, n_out), F32 if nw == 2 else BF16),
        scratch_shapes=[pltpu.VMEM((tm, D), BF16)] * nw,
        compiler_params=_cparams("parallel", "arbitrary"),
        name="glu_up",
    )(x, g.reshape(1, D), *wg, *wu, *extra_in)


def _mm_res_kernel(*refs, nk, nw):
    x_ref = refs[0]
    w_refs = refs[1:1 + nw]
    r_ref, o_ref = refs[1 + nw:]
    k = pl.program_id(2)
    prod = _mm(_parts(x_ref[...], nw == 2), _read(w_refs))

    @pl.when(k == 0)
    def _():
        o_ref[...] = r_ref[...] + prod

    if nk > 1:
        @pl.when(k > 0)
        def _():
            o_ref[...] += prod


def mm_res(x, w, res, *, tn):
    M, K = x.shape
    wp = _wparts(w)
    N = wp[0].shape[1]
    tm = _row_tile(M, len(wp))
    tk = min(K, 4096 // len(wp))
    nk = K // tk
    return pl.pallas_call(
        functools.partial(_mm_res_kernel, nk=nk, nw=len(wp)),
        grid=(M // tm, N // tn, nk),
        in_specs=[pl.BlockSpec((tm, tk), lambda i, j, k: (i, k))]
                 + [pl.BlockSpec((tk, tn), lambda i, j, k: (k, j))] * len(wp)
                 + [pl.BlockSpec((tm, tn), lambda i, j, k: (i, j))],
        out_specs=pl.BlockSpec((tm, tn), lambda i, j, k: (i, j)),
        out_shape=jax.ShapeDtypeStruct((M, N), F32),
        compiler_params=_cparams("parallel", "parallel", "arbitrary"),
        name="mm_res",
    )(x, *wp, res)


def _glu_res_kernel(*refs, nw):
    y_ref = refs[0]
    wa_refs = refs[1:1 + nw]
    wb_refs = refs[1 + nw:1 + 2 * nw]
    r_ref, o_ref = refs[1 + 2 * nw:]
    y = _parts(y_ref[...], nw == 2)
    o_ref[...] = r_ref[...] + _mm(y, _read(wa_refs)) * _sigmoid(_mm(y, _read(wb_refs)))


def glu_res(y, w, res, *, tn):
    M, K = y.shape
    wp = _wparts(w)
    N = wp[0].shape[1] // 2
    nj = N // tn
    tm = _row_tile(M, len(wp))
    return pl.pallas_call(
        functools.partial(_glu_res_kernel, nw=len(wp)),
        grid=(M // tm, nj),
        in_specs=[pl.BlockSpec((tm, K), lambda i, j: (i, 0))]
                 + [pl.BlockSpec((K, tn), lambda i, j: (0, j))] * len(wp)
                 + [pl.BlockSpec((K, tn), lambda i, j: (0, j + nj))] * len(wp)
                 + [pl.BlockSpec((tm, tn), lambda i, j: (i, j))],
        out_specs=pl.BlockSpec((tm, tn), lambda i, j: (i, j)),
        out_shape=jax.ShapeDtypeStruct((M, N), F32),
        compiler_params=_cparams("parallel", "arbitrary"),
        name="glu_res",
    )(y, *wp, *wp, res)


def _ple_kernel(*refs, tn, nw):
    x_ref, g_ref = refs[:2]
    wg_refs = refs[2:2 + nw]
    p_ref = refs[2 + nw]
    wu_refs = refs[3 + nw:3 + 2 * nw]
    o_ref = refs[3 + 2 * nw]
    xn_refs = refs[4 + 2 * nw:]
    j = pl.program_id(1)

    @pl.when(j == 0)
    def _():
        _store_norm(x_ref, g_ref, xn_refs)

    gate = _sigmoid(_mm(_read(xn_refs), _read(wg_refs)))
    up = _mm(_parts(p_ref[...], nw == 2), _read(wu_refs))
    x_tile = x_ref[:, pl.ds(pl.multiple_of(j * tn, tn), tn)]
    o_ref[...] = x_tile + up * gate


def ple(x, g, w_gate, p, w_up, *, tn):
    M, D = x.shape
    P = p.shape[1]
    wg, wu = _wparts(w_gate), _wparts(w_up)
    nw = len(wg)
    tm = _row_tile(M, nw)
    return pl.pallas_call(
        functools.partial(_ple_kernel, tn=tn, nw=nw),
        grid=(M // tm, D // tn),
        in_specs=[pl.BlockSpec((tm, D), lambda i, j: (i, 0)),
                  pl.BlockSpec((1, D), lambda i, j: (0, 0))]
                 + [pl.BlockSpec((D, tn), lambda i, j: (0, j))] * nw
                 + [pl.BlockSpec((tm, P), lambda i, j: (i, 0))]
                 + [pl.BlockSpec((P, tn), lambda i, j: (0, j))] * nw,
        out_specs=pl.BlockSpec((tm, tn), lambda i, j: (i, j)),
        out_shape=jax.ShapeDtypeStruct((M, D), F32),
        scratch_shapes=[pltpu.VMEM((tm, D), BF16)] * nw,
        compiler_params=_cparams("parallel", "arbitrary"),
        name="ple",
    )(x, g.reshape(1, D), *wg, p, *wu)


def _rmsnorm_kernel(x_ref, g_ref, o_ref):
    o_ref[...] = _rms_scale(x_ref[...], g_ref[...]).astype(o_ref.dtype)


def rmsnorm(x, g):
    M, D = x.shape
    tm = _row_tile(M)
    return pl.pallas_call(
        _rmsnorm_kernel,
        grid=(M // tm,),
        in_specs=[pl.BlockSpec((tm, D), lambda i: (i, 0)),
                  pl.BlockSpec((1, D), lambda i: (0, 0))],
        out_specs=pl.BlockSpec((tm, D), lambda i: (i, 0)),
        out_shape=jax.ShapeDtypeStruct((M, D), F32),
        compiler_params=_cparams("parallel"),
        name="rmsnorm",
    )(x, g.reshape(1, D))


def _row_rstd_kernel(x_ref, o_ref):
    x = x_ref[...]
    rstd = lax.rsqrt(jnp.mean(x * x, axis=-1, keepdims=True) + EPS)
    o_ref[...] = jnp.broadcast_to(rstd, o_ref.shape)


def row_rstd(x):
    M, D = x.shape
    tm = _row_tile(M)
    return pl.pallas_call(
        _row_rstd_kernel,
        grid=(M // tm,),
        in_specs=[pl.BlockSpec((tm, D), lambda i: (i, 0))],
        out_specs=pl.BlockSpec((tm, LANES), lambda i: (i, 0)),
        out_shape=jax.ShapeDtypeStruct((M, LANES), F32),
        compiler_params=_cparams("parallel"),
        name="row_rstd",
    )(x)


def _small_proj_kernel(x_ref, g_ref, whi_ref, wlo_ref, aux_ref, o_ref, *, mode, nh):
    xn = _rms_scale(x_ref[...], g_ref[...])
    acc = _mm(_parts(xn, True), (whi_ref[...], wlo_ref[...]))
    lane = lax.broadcasted_iota(jnp.int32, acc.shape, 1)
    if mode == "logf":
        o_ref[...] = -_softplus(-(acc + aux_ref[0:1, :]))
    elif mode == "gdn":
        beta = _sigmoid(acc)
        gdec = -jnp.exp(aux_ref[1:2, :]) * _softplus(acc + aux_ref[0:1, :])
        o_ref[...] = jnp.where(lane < nh, beta, gdec)
    else:
        logits = jnp.where(lane < N_EXPERTS, acc, NEG_INF)
        v1 = jnp.max(logits, axis=1, keepdims=True)
        i1 = jnp.min(jnp.where(logits == v1, lane, LANES), axis=1, keepdims=True)
        rest = jnp.where(lane == i1, NEG_INF, logits)
        v2 = jnp.max(rest, axis=1, keepdims=True)
        i2 = jnp.min(jnp.where(rest == v2, lane, LANES), axis=1, keepdims=True)
        e2 = jnp.exp(v2 - v1)
        g1 = 1.0 / (1.0 + e2)
        o_ref[...] = jnp.where(lane == i1, g1, 0.0) + jnp.where(lane == i2, e2 * g1, 0.0)


def small_proj(x, g, w_parts, aux, mode, *, nh=0):
    M, D = x.shape
    tm = _row_tile(M)
    return pl.pallas_call(
        functools.partial(_small_proj_kernel, mode=mode, nh=nh),
        grid=(M // tm,),
        in_specs=[pl.BlockSpec((tm, D), lambda i: (i, 0)),
                  pl.BlockSpec((1, D), lambda i: (0, 0)),
                  pl.BlockSpec((D, LANES), lambda i: (0, 0)),
                  pl.BlockSpec((D, LANES), lambda i: (0, 0)),
                  pl.BlockSpec((SUBLANES, LANES), lambda i: (0, 0))],
        out_specs=pl.BlockSpec((tm, LANES), lambda i: (i, 0)),
        out_shape=jax.ShapeDtypeStruct((M, LANES), F32),
        compiler_params=_cparams("parallel"),
        name="small_proj_" + mode,
    )(x, g.reshape(1, D), *w_parts, aux)


def _pad_cols(w):
    return _split_weight(jnp.pad(w.astype(F32), ((0, 0), (0, LANES - w.shape[1]))))


def _aux_rows(*rows):
    out = [jnp.pad(vals.astype(F32), (start, LANES - start - vals.shape[0])) for start, vals in rows]
    out += [jnp.zeros((LANES,), F32)] * (SUBLANES - len(out))
    return jnp.stack(out)


def _lane_cumsum_kernel(x_ref, o_ref):
    x = x_ref[...]
    n = x.shape[1]
    lane = lax.broadcasted_iota(jnp.int32, x.shape, 1)
    shift = 1
    while shift < n:
        x = x + jnp.where(lane >= shift, pltpu.roll(x, shift, 1), 0.0)
        shift *= 2
    o_ref[...] = x


def lane_cumsum(x):
    R, n = x.shape
    rb = SUBLANES if R % SUBLANES == 0 else R
    return pl.pallas_call(
        _lane_cumsum_kernel,
        grid=(R // rb,),
        in_specs=[pl.BlockSpec((rb, n), lambda i: (i, 0))],
        out_specs=pl.BlockSpec((rb, n), lambda i: (i, 0)),
        out_shape=jax.ShapeDtypeStruct((R, n), F32),
        compiler_params=_cparams("parallel"),
        name="lane_cumsum",
    )(x)


def _fox_prompt_kernel(q_ref, k_ref, v_ref, fk_ref, o_ref, *, tq, nhb, hd, scale):
    qi = pl.program_id(2)

    def step(ki, carry, diagonal):
        rows = pl.ds(pl.multiple_of(ki * tq, tq), tq)
        out = []
        for h in range(nhb):
            m, l, acc = carry[h]
            cols = slice(h * hd, (h + 1) * hd)
            s = (_dot_nt(q_ref[:, cols], k_ref[rows, cols]) * (scale * LOG2E)
                 - fk_ref[h, pl.ds(ki, 1), :] * LOG2E)
            if diagonal:
                r = lax.broadcasted_iota(jnp.int32, s.shape, 0)
                c = lax.broadcasted_iota(jnp.int32, s.shape, 1)
                s = jnp.where(r >= c, s, NEG_INF)
            m_new = jnp.maximum(m, jnp.max(s, axis=1, keepdims=True))
            p = jnp.exp2(s - m_new)
            alpha = jnp.exp2(m - m_new)
            l = alpha * l + jnp.sum(p, axis=1, keepdims=True)
            acc = alpha * acc + _dot(p.astype(BF16), v_ref[rows, cols])
            out.append((m_new, l, acc))
        return tuple(out)

    init = tuple((jnp.full((tq, 1), NEG_INF, F32), jnp.zeros((tq, 1), F32), jnp.zeros((tq, hd), F32))
                 for _ in range(nhb))
    carry = lax.fori_loop(0, qi, lambda ki, c: step(ki, c, False), init)
    carry = step(qi, carry, True)
    o_ref[...] = jnp.concatenate([acc / l for _, l, acc in carry], axis=1).astype(o_ref.dtype)


def fox_attention_prompt(qkv, fk, *, B, L, H, hd, tq, nhb=2):
    nq = L // tq
    nhg = H // nhb
    return pl.pallas_call(
        functools.partial(_fox_prompt_kernel, tq=tq, nhb=nhb, hd=hd, scale=hd ** -0.5),
        grid=(B, nhg, nq),
        in_specs=[pl.BlockSpec((tq, nhb * hd), lambda b, g, i: (b * nq + i, g)),
                  pl.BlockSpec((L, nhb * hd), lambda b, g, i: (b, nhg + g)),
                  pl.BlockSpec((L, nhb * hd), lambda b, g, i: (b, 2 * nhg + g)),
                  pl.BlockSpec((nhb, nq, tq), lambda b, g, i: (b * nhg + g, 0, 0))],
        out_specs=pl.BlockSpec((tq, nhb * hd), lambda b, g, i: (b * nq + i, g)),
        out_shape=jax.ShapeDtypeStruct((B * L, H * hd), BF16),
        compiler_params=_cparams("parallel", "parallel", "arbitrary"),
        name="fox_attention_prompt",
    )(qkv, qkv, qkv, fk)


def _fox_cached_kernel(q_ref, kn_ref, vn_ref, kp_ref, vp_ref, f_ref, o_ref, *, P, nhb, hd, scale, precise,
                       head_rows):
    n = q_ref.shape[0]
    r = lax.broadcasted_iota(jnp.int32, (n, n), 0)
    c = lax.broadcasted_iota(jnp.int32, (n, n), 1)
    g = pl.program_id(1)

    def past(ref, h):
        if head_rows:
            return ref[pl.ds(g * nhb + h, P, stride=head_rows), :]
        return ref[:, h * hd:(h + 1) * hd]

    outs = []
    for h in range(nhb):
        cols = slice(h * hd, (h + 1) * hd)
        q = _parts(q_ref[:, cols], precise)
        f = f_ref[h]
        s_past = _mm_nt(q, _parts(past(kp_ref, h), precise)) * scale - f[:, :P]
        s_new = _mm_nt(q, _parts(kn_ref[:, cols], precise)) * scale - f[:, P:P + n]
        s_new = jnp.where(r >= c, s_new, NEG_INF)
        m = jnp.maximum(jnp.max(s_past, axis=1, keepdims=True), jnp.max(s_new, axis=1, keepdims=True))
        p_past = jnp.exp(s_past - m)
        p_new = jnp.exp(s_new - m)
        l = jnp.sum(p_past, axis=1, keepdims=True) + jnp.sum(p_new, axis=1, keepdims=True)
        acc = (_mm(_parts(p_past, precise), _parts(past(vp_ref, h), precise))
               + _mm(_parts(p_new, precise), _parts(vn_ref[:, cols], precise)))
        outs.append(acc / l)
    o_ref[...] = jnp.concatenate(outs, axis=1).astype(o_ref.dtype)


def fox_attention_cached(qkv, k_past, v_past, f, *, seq0, k_lane0, v_lane0, B, n, P, H, hd):
    precise = qkv.dtype == F32
    nhb = 4 if H % 4 == 0 else 1
    nhg = H // nhb
    w = nhb * hd
    head_rows = H if k_lane0 is None else 0
    if head_rows:
        past_specs = [pl.BlockSpec((None, P * H, hd), lambda b, g: (seq0 + b, 0, 0))] * 2
    else:
        past_specs = [pl.BlockSpec((None, P, w), lambda b, g: (seq0 + b, 0, k_lane0 // nhb + g)),
                      pl.BlockSpec((None, P, w), lambda b, g: (seq0 + b, 0, v_lane0 // nhb + g))]
    return pl.pallas_call(
        functools.partial(_fox_cached_kernel, P=P, nhb=nhb, hd=hd, scale=hd ** -0.5, precise=precise,
                          head_rows=head_rows),
        grid=(B, nhg),
        in_specs=[pl.BlockSpec((n, w), lambda b, g: (b, g)),
                  pl.BlockSpec((n, w), lambda b, g: (b, nhg + g)),
                  pl.BlockSpec((n, w), lambda b, g: (b, 2 * nhg + g))]
                 + past_specs
                 + [pl.BlockSpec((nhb, 1, f.shape[2]), lambda b, g: (b * nhg + g, 0, 0))],
        out_specs=pl.BlockSpec((n, w), lambda b, g: (b, g)),
        out_shape=jax.ShapeDtypeStruct((B * n, H * hd), qkv.dtype),
        compiler_params=_cparams("parallel", "arbitrary"),
        name="fox_attention_cached",
    )(qkv, qkv, qkv, k_past, v_past, f)


def _gelu_tanh(x):
    return 0.5 * x * (1.0 + jnp.tanh(math.sqrt(2.0 / math.pi) * (x + 0.044715 * (x * x * x))))


def _s5_kernel(*refs, nb, c, ns, ncm, snap_chunk):
    x_ref, rstd_ref, g_ref, bhi_ref, blo_ref = refs[:5]
    cm_refs = refs[5:5 + ncm]
    (ar_ref, ai_ref, d_ref, h0r_ref, h0i_ref, y_ref, hr_out, hi_out, hr_snap, hi_snap,
     u_sc, y_sc, bu_sc, st_sc, hr_sc, hi_sc) = refs[5 + ncm:]
    tc = pl.program_id(1)

    @pl.when(tc == 0)
    def _():
        hr_sc[...] = h0r_ref[...]
        hi_sc[...] = h0i_ref[...]

    g = g_ref[...]
    for b in range(nb):
        u_sc[pl.ds(b, c, stride=nb), :] = x_ref[b] * rstd_ref[b] * g
    u = u_sc[...]
    u_hi, u_lo = _split_bf16(u)
    bu_sc[...] = _dot(jnp.concatenate([u_hi, u_lo], axis=1), bhi_ref[0]) + _dot(u_hi, blo_ref[0])
    ar = jnp.broadcast_to(ar_ref[0], (nb, ns))
    ai = jnp.broadcast_to(ai_ref[0], (nb, ns))

    def step(t, carry):
        hr, hi = carry
        rows = pl.ds(pl.multiple_of(t * nb, nb), nb)
        nhr = ar * hr - ai * hi + bu_sc[rows, 0:ns]
        nhi = ar * hi + ai * hr + bu_sc[rows, ns:2 * ns]
        st_sc[rows, 0:ns] = nhr
        st_sc[rows, ns:2 * ns] = nhi
        return nhr, nhi

    hr, hi = lax.fori_loop(0, c, step, (hr_sc[...], hi_sc[...]))
    hr_sc[...] = hr
    hi_sc[...] = hi
    y = _mm(_parts(st_sc[...], ncm == 2), tuple(r[0] for r in cm_refs)) + d_ref[...] * u
    y_sc[...] = _gelu_tanh(y)
    for b in range(nb):
        y_ref[b] = y_sc[pl.ds(b, c, stride=nb), :].astype(y_ref.dtype)

    @pl.when(tc == snap_chunk)
    def _():
        hr_snap[...] = hr
        hi_snap[...] = hi

    @pl.when(tc == pl.num_programs(1) - 1)
    def _():
        hr_out[...] = hr
        hi_out[...] = hi


def s5_scan(x, rstd, g, h0r, h0i, consts, *, c, precise, snap_steps=None):
    bhi, blo, cm_hi, cm_lo, ar, ai, d = consts
    cms = (cm_hi, cm_lo) if precise else (cm_hi,)
    nb, L, D = x.shape
    nblk = D // LANES
    ns = ar.shape[2]
    ntc = L // c
    snap_chunk = ntc - 1 if snap_steps is None else snap_steps // c - 1
    blk = lambda g, t: (g, 0, 0)
    state_spec = pl.BlockSpec((nb, ns), lambda g, t: (0, g))
    state_shape = jax.ShapeDtypeStruct(h0r.shape, F32)
    return pl.pallas_call(
        functools.partial(_s5_kernel, nb=nb, c=c, ns=ns, ncm=len(cms), snap_chunk=snap_chunk),
        grid=(nblk, ntc),
        in_specs=[pl.BlockSpec((nb, c, LANES), lambda g, t: (0, t, g)),
                  pl.BlockSpec((nb, c, LANES), lambda g, t: (0, t, 0)),
                  pl.BlockSpec((1, LANES), lambda g, t: (0, g)),
                  pl.BlockSpec((1, 2 * LANES, 2 * ns), blk),
                  pl.BlockSpec((1, LANES, 2 * ns), blk)]
                 + [pl.BlockSpec((1, 2 * ns, LANES), blk)] * len(cms)
                 + [pl.BlockSpec((1, 1, ns), blk),
                    pl.BlockSpec((1, 1, ns), blk),
                    pl.BlockSpec((1, LANES), lambda g, t: (0, g)),
                    state_spec, state_spec],
        out_specs=[pl.BlockSpec((nb, c, LANES), lambda g, t: (0, t, g))] + [state_spec] * 4,
        out_shape=[jax.ShapeDtypeStruct((nb, L, D), F32 if precise else BF16)] + [state_shape] * 4,
        scratch_shapes=[pltpu.VMEM((c * nb, LANES), F32),
                        pltpu.VMEM((c * nb, LANES), F32),
                        pltpu.VMEM((c * nb, 2 * ns), F32),
                        pltpu.VMEM((c * nb, 2 * ns), F32),
                        pltpu.VMEM((nb, ns), F32),
                        pltpu.VMEM((nb, ns), F32)],
        compiler_params=_cparams("parallel", "arbitrary"),
        name="s5_scan",
    )(x, rstd, g.reshape(1, D), bhi, blo, *cms, ar, ai, d, h0r, h0i)


def s5_constants(a_re, a_im, b_re, b_im, c_re, c_im, d_skip, log_dt):
    G, N = a_re.shape
    gc = b_re.shape[2]
    gpb = LANES // gc
    nblk = G // gpb
    a = lax.complex(a_re.astype(F32), a_im.astype(F32))
    dt = jnp.exp(log_dt.astype(F32))[:, None]
    a_bar = jnp.exp(a * dt)
    b_bar = ((a_bar - 1.0) / a)[..., None] * lax.complex(b_re.astype(F32), b_im.astype(F32))
    eye = jnp.eye(gpb, dtype=F32)

    def in_mat(b):
        b = b.reshape(nblk, gpb, N, gc)
        return jnp.einsum("kgnc,gh->kgchn", b, eye).reshape(nblk, gpb * gc, gpb * N)

    def out_mat(cc):
        cc = cc.reshape(nblk, gpb, gc, N)
        return jnp.einsum("kgcn,gh->kgnhc", cc, eye).reshape(nblk, gpb * N, gpb * gc)

    bhi, blo = _split_weight(jnp.concatenate([in_mat(b_bar.real), in_mat(b_bar.imag)], axis=2))
    bhi = jnp.concatenate([bhi, bhi], axis=1)
    cm_hi, cm_lo = _split_weight(jnp.concatenate([out_mat(c_re.astype(F32)), -out_mat(c_im.astype(F32))], axis=1))
    ar = a_bar.real.reshape(nblk, 1, gpb * N)
    ai = a_bar.imag.reshape(nblk, 1, gpb * N)
    return bhi, blo, cm_hi, cm_lo, ar, ai, d_skip.astype(F32).reshape(1, G * gc)


def _gdn_conv_kernel(x_ref, halo_ref, buf_ref, w_ref, o_ref, ext_sc, *, tl, n_valid, tiles_per_seq, n_norm_tiles):
    r = pl.program_id(0)
    j = pl.program_id(1)
    first = (r % tiles_per_seq) == 0
    ext_sc[0:SUBLANES, :] = jnp.where(first, buf_ref[0], halo_ref[...])
    ext_sc[SUBLANES:SUBLANES + n_valid, :] = x_ref[...]
    w = w_ref[...]
    y = x_ref[...] * w[3:4, :]
    for s in (1, 2, 3):
        y = y + ext_sc[SUBLANES - s:SUBLANES - s + n_valid, :] * w[3 - s:4 - s, :]
    y = _silu(y)

    def store(val):
        if n_valid < tl:
            val = jnp.concatenate([val, jnp.zeros((tl - n_valid, val.shape[1]), F32)], axis=0)
        o_ref[...] = val.astype(o_ref.dtype)

    @pl.when(j < n_norm_tiles)
    def _():
        segs = []
        for a in range(0, y.shape[1], LANES):
            seg = y[:, a:a + LANES]
            segs.append(seg * lax.rsqrt(jnp.sum(seg * seg, axis=1, keepdims=True) + EPS))
        store(jnp.concatenate(segs, axis=1))

    @pl.when(j >= n_norm_tiles)
    def _():
        store(y)


def gdn_conv(x, buf8, w8, *, n_seq, seq_len, tl_in, tl_out, n_norm_cols, tc):
    C = x.shape[1]
    tiles_per_seq = seq_len // tl_in
    n_tiles = n_seq * tiles_per_seq
    hpt = tl_in // SUBLANES
    return pl.pallas_call(
        functools.partial(_gdn_conv_kernel, tl=tl_out, n_valid=tl_in, tiles_per_seq=tiles_per_seq,
                          n_norm_tiles=n_norm_cols // tc),
        grid=(n_tiles, C // tc),
        in_specs=[pl.BlockSpec((tl_in, tc), lambda r, j: (r, j)),
                  pl.BlockSpec((SUBLANES, tc), lambda r, j: (jnp.maximum(r * hpt - 1, 0), j)),
                  pl.BlockSpec((1, SUBLANES, tc), lambda r, j: (r // tiles_per_seq, 0, j)),
                  pl.BlockSpec((SUBLANES, tc), lambda r, j: (0, j))],
        out_specs=pl.BlockSpec((tl_out, tc), lambda r, j: (r, j)),
        out_shape=jax.ShapeDtypeStruct((n_tiles * tl_out, C), BF16),
        scratch_shapes=[pltpu.VMEM((SUBLANES + tl_in, tc), F32)],
        compiler_params=_cparams("parallel", "arbitrary"),
        name="gdn_conv",
    )(x, x, buf8, w8)


def _gdn_prep_kernel(qk_ref, gt_ref, ti_ref, a_ref, gc_ref, n_sc, nt_sc, x_sc, *, C, cps, n_qk, rep, hd, scale):
    nh = n_qk * rep
    n_sys = cps * nh
    pitch = C + SUBLANES
    r = lax.broadcasted_iota(jnp.int32, (C, C), 0)
    c = lax.broadcasted_iota(jnp.int32, (C, C), 1)
    tril = (r >= c).astype(BF16)
    lane_pad = jnp.zeros((C, LANES - C), F32)
    for cc in range(cps):
        rows = slice(cc * C, (cc + 1) * C)
        gates = gt_ref[rows, :]
        lane = lax.broadcasted_iota(jnp.int32, gates.shape, 1)
        g = jnp.where((lane >= nh) & (lane < 2 * nh), gates, 0.0)
        g_hi = g.astype(BF16)
        g_r1 = g - g_hi.astype(F32)
        g_mid = g_r1.astype(BF16)
        g_lo = (g_r1 - g_mid.astype(F32)).astype(BF16)
        gcum = _dot(tril, g_hi) + _dot(tril, g_mid) + _dot(tril, g_lo)
        gc_ref[rows, :] = gcum
        gcum_t = gcum.T
        for hq in range(n_qk):
            q = qk_ref[rows, hq * hd:(hq + 1) * hd]
            k = qk_ref[rows, (n_qk + hq) * hd:(n_qk + hq + 1) * hd]
            kk = _dot_nt(k, k)
            qk = _dot_nt(q, k) * scale
            for rr in range(rep):
                hv = hq * rep + rr
                sys = cc * nh + hv
                diff = gcum[:, nh + hv:nh + hv + 1] - gcum_t[nh + hv:nh + hv + 1, :]
                decay = jnp.exp(jnp.where(r >= c, diff, NEG_INF))
                n_val = jnp.where(r > c, gates[:, hv:hv + 1] * kk * decay, 0.0)
                n_sc[sys * pitch:sys * pitch + C, :] = jnp.concatenate([n_val, lane_pad], axis=1)
                a_ref[cc, hv] = (qk * decay).astype(a_ref.dtype)
    for i in range(C):
        slab = n_sc[pl.ds(i, n_sys, stride=pitch), :]
        nt_sc[i * C:(i + 1) * C, :] = slab.T[:C, :]
    _tri_solve(nt_sc, x_sc, C)
    row_pad = jnp.zeros((LANES - C, n_sys), F32)
    for i in range(C):
        cols = jnp.concatenate([x_sc[i * C:(i + 1) * C, :], row_pad], axis=0)
        n_sc[pl.ds(i, n_sys, stride=pitch), :] = cols.T
    for cc in range(cps):
        for hv in range(nh):
            sys = cc * nh + hv
            ti_ref[cc, hv] = n_sc[sys * pitch:sys * pitch + C, :C].astype(ti_ref.dtype)


def gdn_prep(qkv_c, gates, *, C, n_qk, rep, hd):
    rows = qkv_c.shape[0]
    nch = rows // C
    nh = n_qk * rep
    cps = LANES // nh
    assert nch % cps == 0 and cps * nh == LANES
    return pl.pallas_call(
        functools.partial(_gdn_prep_kernel, C=C, cps=cps, n_qk=n_qk, rep=rep, hd=hd, scale=hd ** -0.5),
        grid=(nch // cps,),
        in_specs=[pl.BlockSpec((cps * C, 2 * n_qk * hd), lambda n: (n, 0)),
                  pl.BlockSpec((cps * C, LANES), lambda n: (n, 0))],
        out_specs=[pl.BlockSpec((cps, nh, C, C), lambda n: (n, 0, 0, 0)),
                   pl.BlockSpec((cps, nh, C, C), lambda n: (n, 0, 0, 0)),
                   pl.BlockSpec((cps * C, LANES), lambda n: (n, 0))],
        out_shape=[jax.ShapeDtypeStruct((nch, nh, C, C), BF16),
                   jax.ShapeDtypeStruct((nch, nh, C, C), BF16),
                   jax.ShapeDtypeStruct((rows, LANES), F32)],
        scratch_shapes=[pltpu.VMEM((LANES * (C + SUBLANES), LANES), F32),
                        pltpu.VMEM((C * C, LANES), F32),
                        pltpu.VMEM((C * C, LANES), F32)],
        compiler_params=_cparams("parallel"),
        name="gdn_prep",
    )(qkv_c, gates)


def _tri_solve(n_ref, x_ref, C):
    nblk = C // SUBLANES
    lanes = n_ref.shape[1]
    sub = lax.broadcasted_iota(jnp.int32, (SUBLANES, lanes), 0)
    x_ref[...] = jnp.zeros(x_ref.shape, F32)
    for ib in range(nblk):
        def row(r, _, ib=ib):
            base = (ib * SUBLANES + r) * C
            acc = [jnp.zeros((SUBLANES, lanes), F32) for _ in range(ib)]
            acc.append(jnp.where(sub == r, 1.0, 0.0).astype(F32))
            for jb in range(ib + 1):
                for j in range(jb * SUBLANES, (jb + 1) * SUBLANES):
                    nb = jnp.broadcast_to(n_ref[pl.ds(base + j, 1), :], (SUBLANES, lanes))
                    for cb in range(jb + 1):
                        acc[cb] = acc[cb] - nb * x_ref[j * C + cb * SUBLANES:j * C + (cb + 1) * SUBLANES, :]
            for cb in range(ib + 1):
                x_ref[pl.ds(pl.multiple_of(base + cb * SUBLANES, SUBLANES), SUBLANES), :] = acc[cb]
            return 0

        lax.fori_loop(0, SUBLANES, row, 0)


def _gdn_scan_kernel(q_ref, k_ref, v_ref, z_ref, gt_ref, gc_ref, ti_ref, a_ref, s0_ref, ng_ref,
                     o_ref, s_out, s_sc, *, C, hg, rep, hd, nh, n_valid, scale):
    n = pl.program_id(2)
    hgi = pl.program_id(1)

    @pl.when(n == 0)
    def _():
        s_sc[...] = s0_ref[0]

    shift = (LANES - hgi * hg) % LANES
    gates = pltpu.roll(gt_ref[...], shift, 1)
    gcum = pltpu.roll(gc_ref[...], shift, 1)
    heads = range(hg)
    s_old = [s_sc[hh] for hh in heads]
    s_bf = [s.astype(BF16) for s in s_old]
    beta = [gates[:, hh:hh + 1] for hh in heads]
    gc = [gcum[:, nh + hh:nh + hh + 1] for hh in heads]
    eg = [jnp.exp(g) for g in gc]
    g_tot = [g[C - 1:C, :] for g in gc]
    k = [k_ref[:, (hh // rep) * hd:(hh // rep + 1) * hd].astype(F32) for hh in heads]
    q = [q_ref[:, (hh // rep) * hd:(hh // rep + 1) * hd].astype(F32) for hh in heads]
    v = [v_ref[:, hh * hd:(hh + 1) * hd].astype(F32) for hh in heads]
    uw = [_dot(ti_ref[0, hh].astype(BF16),
               jnp.concatenate([v[hh] * beta[hh], k[hh] * (beta[hh] * eg[hh])], axis=1).astype(BF16))
          for hh in heads]
    q_dec = [(q[hh] * (scale * eg[hh])).astype(BF16) for hh in heads]
    k_dec = [(k[hh] * jnp.exp(g_tot[hh] - gc[hh])).astype(BF16) for hh in heads]
    v_new = [(uw[hh][:, :hd] - _dot(uw[hh][:, hd:].astype(BF16), s_bf[hh])).astype(BF16) for hh in heads]
    o = [_dot(q_dec[hh], s_bf[hh]) + _dot(a_ref[0, hh], v_new[hh]) for hh in heads]
    s_new = [s_old[hh] * jnp.exp(g_tot[hh]) + _dot_tn(k_dec[hh], v_new[hh]) for hh in heads]
    for hh in heads:
        s_sc[hh] = s_new[hh]
    ng = ng_ref[...]
    outs = [_rms_scale(o[hh][:n_valid], ng) * _silu(z_ref[:, hh * hd:(hh + 1) * hd].astype(F32)) for hh in heads]
    o_ref[...] = jnp.concatenate(outs, axis=1).astype(o_ref.dtype)

    @pl.when(n == pl.num_programs(2) - 1)
    def _():
        s_out[0] = s_sc[...]


def gdn_scan(qkv_c, z, gates, gcum, tinv, amat, s0, norm_g, *, n_seq, nc, C, n_valid, n_qk, rep, hd, hg):
    nh = n_qk * rep
    nhg = nh // hg
    qw = (hg // rep) * hd
    k_off = n_qk * hd // qw
    v_off = 2 * n_qk * hd // (hg * hd)
    chunk = lambda s, g, n: s * nc + n
    return pl.pallas_call(
        functools.partial(_gdn_scan_kernel, C=C, hg=hg, rep=rep, hd=hd, nh=nh, n_valid=n_valid,
                          scale=hd ** -0.5),
        grid=(n_seq, nhg, nc),
        in_specs=[pl.BlockSpec((C, qw), lambda s, g, n: (chunk(s, g, n), g)),
                  pl.BlockSpec((C, qw), lambda s, g, n: (chunk(s, g, n), k_off + g)),
                  pl.BlockSpec((C, hg * hd), lambda s, g, n: (chunk(s, g, n), v_off + g)),
                  pl.BlockSpec((n_valid, hg * hd), lambda s, g, n: (chunk(s, g, n), g)),
                  pl.BlockSpec((C, LANES), lambda s, g, n: (chunk(s, g, n), 0)),
                  pl.BlockSpec((C, LANES), lambda s, g, n: (chunk(s, g, n), 0)),
                  pl.BlockSpec((1, hg, C, C), lambda s, g, n: (chunk(s, g, n), g, 0, 0)),
                  pl.BlockSpec((1, hg, C, C), lambda s, g, n: (chunk(s, g, n), g, 0, 0)),
                  pl.BlockSpec((1, hg, hd, hd), lambda s, g, n: (s, g, 0, 0)),
                  pl.BlockSpec((1, hd), lambda s, g, n: (0, 0))],
        out_specs=[pl.BlockSpec((n_valid, hg * hd), lambda s, g, n: (chunk(s, g, n), g)),
                   pl.BlockSpec((1, hg, hd, hd), lambda s, g, n: (s, g, 0, 0))],
        out_shape=[jax.ShapeDtypeStruct((n_seq * nc * n_valid, nh * hd), BF16),
                   jax.ShapeDtypeStruct((n_seq, nh, hd, hd), F32)],
        scratch_shapes=[pltpu.VMEM((hg, hd, hd), F32)],
        compiler_params=_cparams("parallel", "parallel", "arbitrary"),
        name="gdn_scan",
    )(qkv_c, qkv_c, qkv_c, z, gates, gcum, tinv, amat, s0, norm_g.reshape(1, hd))


def _trunk(x, p, fox_hist, s5_h0, gdn_state, W, *, n_seq, seq_len, steps, precise, snap_steps=None,
           fox_kv=None):
    M, D = x.shape
    H, hd = W["fox_heads"]
    n_qk, rep, gd = W["gdn_heads"]
    n_vh = n_qk * rep
    key_dim, val_dim = n_qk * gd, n_vh * gd
    qkv_dim = 2 * key_dim + val_dim
    cw = W["gdn_conv_w"].shape[1]
    tn = min(512, D)
    outs = {} if fox_kv is None else {"fox_kv": fox_kv}

    def wsel(name, idx, pr):
        hi = W[name][idx]
        return (hi, W[name + "_lo"][idx]) if pr else hi

    def mixer(i, x):
        j = i // 3
        pm = precise
        g_mix = W["norm_mix"][i]
        if i % 3 == 0:
            qkv, k_heads, v_heads = fox_in_proj(x, g_mix, wsel("fox_w_qkv", j, pm), n_heads=H, hd=hd,
                                                tn=tn if pm else min(2 * tn, D), bufs=outs.get("fox_kv"), slot=j, n_slots=W["n_fox"])
            outs["fox_kv"] = (k_heads, v_heads)
            outs[("fox_qkv", j)] = qkv
            logf = small_proj(x, g_mix, W["fox_w_f"][j], W["fox_aux"][j], "logf")[:, :H]
            lf = jnp.swapaxes(logf.reshape(n_seq, seq_len, H), 1, 2)
            if j not in fox_hist:
                tq = next(t for t in (512, 384, 256, 128, seq_len) if seq_len % t == 0)
                fk = lane_cumsum(lf.reshape(n_seq * H, seq_len)).reshape(n_seq * H, seq_len // tq, tq)
                o = fox_attention_prompt(qkv, fk, B=n_seq, L=seq_len, H=H, hd=hd, tq=tq)
            else:
                k_past, v_past, k_lane0, v_lane0, seq0, lf_past, P = fox_hist[j]
                fpad = -(-(P + seq_len) // LANES) * LANES
                lf_all = jnp.concatenate([jnp.swapaxes(lf_past.astype(F32), 1, 2), lf], axis=2)
                lf_all = jnp.pad(lf_all, ((0, 0), (0, 0), (0, fpad - (P + seq_len))))
                f = lane_cumsum(lf_all.reshape(n_seq * H, fpad)).reshape(n_seq * H, 1, fpad)
                o = fox_attention_cached(qkv, k_past, v_past, f, seq0=seq0, k_lane0=k_lane0, v_lane0=v_lane0,
                                         B=n_seq, n=seq_len, P=P, H=H, hd=hd)
            outs[("fox", j)] = logf
            return mm_res(o, wsel("fox_w_out", j, pm), x, tn=tn)
        if i % 3 == 1:
            rstd = row_rstd(x).reshape(n_seq, seq_len, LANES)
            y, hr, hi, hr_snap, hi_snap = s5_scan(x.reshape(n_seq, seq_len, D), rstd, g_mix,
                                                  s5_h0[j][0], s5_h0[j][1], W["s5_consts"][j],
                                                  c=min(CHUNK, seq_len), precise=pm, snap_steps=snap_steps)
            outs[("s5", j)] = (hr, hi)
            outs[("s5_snap", j)] = (hr_snap, hi_snap)
            return glu_res(y.reshape(M, D), wsel("s5_w_glu", j, pm), x, tn=tn)
        s0, conv_rows = gdn_state[j]
        nq = qkv_dim // tn
        nz = val_dim // tn
        qkv_raw, z = norm_mm(x, g_mix, W["gdn_w_qkvz"][j], [(F32, 0, nq), (BF16, nq, nq + nz)], tn=tn)
        gates = small_proj(x, g_mix, W["gdn_w_ba"][j], W["gdn_aux"][j], "gdn", nh=n_vh)
        C = CHUNK
        n_valid = min(C, seq_len)
        nc = seq_len // n_valid
        buf = jnp.pad(conv_rows.astype(F32), ((0, 0), (SUBLANES - (cw - 1), 0), (0, 0)))
        tl_in = min(512, seq_len)
        qkv_c = gdn_conv(qkv_raw, buf, W["gdn_conv_w8"][j], n_seq=n_seq, seq_len=seq_len, tl_in=tl_in,
                         tl_out=max(tl_in, C), n_norm_cols=2 * key_dim,
                         tc=key_dim if tl_in < C else min(512, key_dim))
        gates_c = jnp.pad(gates.reshape(n_seq * nc, n_valid, LANES),
                          ((0, 0), (0, C - n_valid), (0, 0))).reshape(n_seq * nc * C, LANES)
        tinv, amat, gcum = gdn_prep(qkv_c, gates_c, C=C, n_qk=n_qk, rep=rep, hd=gd)
        o, s_last = gdn_scan(qkv_c, z, gates_c, gcum, tinv, amat, s0.astype(F32), W["gdn_norm"][j],
                             n_seq=n_seq, nc=nc, C=C, n_valid=n_valid, n_qk=n_qk, rep=rep, hd=gd,
                             hg=min(32, n_vh))
        rows = jnp.concatenate([conv_rows.astype(F32), qkv_raw.reshape(n_seq, seq_len, qkv_dim)], axis=1)
        outs[("gdn", j)] = (s_last, rows[:, -(cw - 1):])
        return mm_res(o, W["gdn_w_out"][j], x, tn=tn)

    def channel(i, x):
        m = i // 2
        g_ffn = W["norm_ffn"][i]
        if i % 2 == 0:
            pf = precise
            h = glu_up(x, g_ffn, wsel("ffn_w_gate", m, pf), wsel("ffn_w_up", m, pf), tn=tn)
            x = mm_res(h, wsel("ffn_w_down", m, pf), x, tn=tn)
        else:
            pf = False
            comb = small_proj(x, g_ffn, W["moe_w_router"][m], W["zero_aux"], "router")
            h = glu_up(x, g_ffn, W["moe_w_gate"][m], W["moe_w_up"][m], comb, tn=tn)
            x = mm_res(h, W["moe_w_down"][m], x, tn=tn)
        p_i = p[i] if pf else p[i].astype(BF16)
        return ple(x, W["norm_ple"][i], wsel("ple_w_gate", i, pf), p_i, wsel("ple_w_up", i, pf), tn=tn)

    for i, part in steps:
        x = mixer(i, x) if part == "mixer" else channel(i, x)
    return x, outs


def kernel(x_prompt, x_sample, cache_fox_k, cache_fox_v, cache_fox_logf, state_s5_re, state_s5_im, state_gdn, state_gdn_conv, p_prompt, p_sample, norm_mix, norm_ffn, norm_ple, norm_final, fox_w_in, fox_b_f, fox_w_out, s5_a_re, s5_a_im, s5_b_re, s5_b_im, s5_c_re, s5_c_im, s5_d, s5_log_dt, s5_w_glu, gdn_w_in, gdn_conv_w, gdn_a_log, gdn_dt_bias, gdn_norm, gdn_w_out, ffn_w_gate, ffn_w_up, ffn_w_down, moe_w_router, moe_w_gate, moe_w_up, moe_w_down, ple_w_up, ple_w_gate):
    B, L, D = x_prompt.shape
    Bs, Ls, _ = x_sample.shape
    depth = norm_mix.shape[0]
    P = cache_fox_k.shape[2]
    H, hd = cache_fox_k.shape[3], cache_fox_k.shape[4]
    n_vh, gd = state_gdn.shape[2], state_gdn.shape[3]
    qkv_dim = state_gdn_conv.shape[3]
    val_dim = n_vh * gd
    n_qk = (qkv_dim - val_dim) // (2 * gd)
    cw = gdn_conv_w.shape[1]
    G, NS = s5_a_re.shape[1], s5_a_re.shape[2]
    E, _, Fe = moe_w_gate.shape[1:]
    n_fox, n_s5, n_gdn = fox_w_in.shape[0], s5_a_re.shape[0], gdn_w_in.shape[0]

    first_router = 1
    n_lo = {"fox": 1, "ffn": 1, "ple": 1, "s5": 1}

    def hi_lo(w, n):
        return w.astype(BF16), _split_weight(w[:n])[1]

    W = {"norm_mix": norm_mix, "norm_ffn": norm_ffn, "norm_ple": norm_ple, "norm_final": norm_final,
         "fox_heads": (H, hd), "n_fox": n_fox, "gdn_heads": (n_qk, n_vh // n_qk, gd), "gdn_conv_w": gdn_conv_w,
         "gdn_norm": gdn_norm, "zero_aux": jnp.zeros((SUBLANES, LANES), F32)}
    W["fox_w_qkv"], W["fox_w_qkv_lo"] = hi_lo(fox_w_in[:, :, :3 * D], n_lo["fox"])
    W["fox_w_out"], W["fox_w_out_lo"] = hi_lo(fox_w_out, n_lo["fox"])
    W["fox_w_f"] = [_pad_cols(fox_w_in[j, :, 3 * D:]) for j in range(n_fox)]
    W["fox_aux"] = [_aux_rows((0, fox_b_f[j])) for j in range(n_fox)]
    W["s5_w_glu"], W["s5_w_glu_lo"] = hi_lo(s5_w_glu, n_lo["s5"])
    W["s5_consts"] = [s5_constants(s5_a_re[j], s5_a_im[j], s5_b_re[j], s5_b_im[j], s5_c_re[j], s5_c_im[j],
                                   s5_d[j], s5_log_dt[j]) for j in range(n_s5)]
    W["gdn_w_qkvz"] = gdn_w_in[:, :, :qkv_dim + val_dim].astype(BF16)
    W["gdn_w_ba"] = [_pad_cols(gdn_w_in[j, :, qkv_dim + val_dim:]) for j in range(n_gdn)]
    W["gdn_aux"] = [_aux_rows((n_vh, gdn_dt_bias[j]), (n_vh, gdn_a_log[j])) for j in range(n_gdn)]
    W["gdn_conv_w8"] = jnp.pad(gdn_conv_w.astype(F32), ((0, 0), (0, SUBLANES - cw), (0, 0)))
    W["gdn_w_out"] = gdn_w_out.astype(BF16)
    W["ffn_w_gate"], W["ffn_w_gate_lo"] = hi_lo(ffn_w_gate, n_lo["ffn"])
    W["ffn_w_up"], W["ffn_w_up_lo"] = hi_lo(ffn_w_up, n_lo["ffn"])
    W["ffn_w_down"], W["ffn_w_down_lo"] = hi_lo(ffn_w_down, n_lo["ffn"])
    W["moe_w_router"] = [_pad_cols(moe_w_router[m]) for m in range(moe_w_router.shape[0])]
    W["moe_w_gate"] = moe_w_gate.astype(BF16)
    W["moe_w_up"] = moe_w_up.astype(BF16)
    W["moe_w_down"] = moe_w_down.astype(BF16).reshape(-1, E * Fe, D)
    W["ple_w_gate"], W["ple_w_gate_lo"] = hi_lo(ple_w_gate, n_lo["ple"])
    W["ple_w_up"], W["ple_w_up_lo"] = hi_lo(ple_w_up, n_lo["ple"])

    steps = [(i, part) for i in range(depth) for part in ("mixer", "channel")]
    steps_a, steps_b = steps[:2 * first_router + 1], steps[2 * first_router + 1:]
    pd = p_prompt.shape[-1]

    Lt = LANES
    Lh = L - Lt
    p_all = {i: p_prompt[i].reshape(B * L, pd) for i in range(depth)}
    p_tail = {i: p_prompt[i][:, Lh:].reshape(B * Lt, pd) for i in range(depth)}
    zeros_s5 = {j: (jnp.zeros((B, G * NS), F32),) * 2 for j in range(n_s5)}
    x_f, out_p = _trunk(x_prompt.reshape(B * L, D), p_all, {}, zeros_s5, {}, W,
                        n_seq=B, seq_len=L, steps=steps_a, precise=False, snap_steps=Lh)
    hist_t = {}
    for j in range(n_fox):
        if ("fox", j) in out_p:
            qkv_f = out_p[("fox_qkv", j)].reshape(B, L, 3 * D)
            hist_t[j] = (qkv_f, qkv_f, H, 2 * H, 0, out_p[("fox", j)].reshape(B, L, H)[:, :Lh], Lh)
    xt, out_t = _trunk(x_prompt[:, Lh:].reshape(B * Lt, D), p_tail, hist_t,
                       {j: out_p[("s5_snap", j)] for j in range(n_s5) if ("s5_snap", j) in out_p}, {}, W,
                       n_seq=B, seq_len=Lt, steps=steps_a, precise=True)
    x_p = lax.dynamic_update_slice(x_f.reshape(B, L, D), xt.reshape(B, Lt, D), (0, Lh, 0)).reshape(B * L, D)
    out_p.update({key: val for key, val in out_t.items() if key[0] == "s5"})
    gdn0 = (jnp.zeros((B, n_vh, gd, gd), F32), jnp.zeros((B, cw - 1, qkv_dim), F32))
    x_p, out_b = _trunk(x_p, p_all, {}, zeros_s5, {j: gdn0 for j in range(n_gdn)}, W,
                        n_seq=B, seq_len=L, steps=steps_b, precise=False, fox_kv=out_p.get("fox_kv"))
    out_p.update(out_b)

    ck = cache_fox_k.reshape(-1, P * H, hd)
    cv = cache_fox_v.reshape(-1, P * H, hd)
    s5_re = state_s5_re.reshape(n_s5, Bs, G * NS).astype(F32)
    s5_im = state_s5_im.reshape(n_s5, Bs, G * NS).astype(F32)
    hist_s = {j: (ck, cv, None, None, j * Bs, cache_fox_logf[j], P) for j in range(n_fox)}
    state_s = ({j: (s5_re[j], s5_im[j]) for j in range(n_s5)},
               {j: (state_gdn[j], state_gdn_conv[j]) for j in range(n_gdn)})
    p_s = {i: p_sample[i].reshape(Bs * Ls, pd) for i in range(depth)}
    x_s, out_s = _trunk(x_sample.reshape(Bs * Ls, D), p_s, hist_s, *state_s, W,
                        n_seq=Bs, seq_len=Ls, steps=steps_a, precise=True)
    x_s, out_sb = _trunk(x_s, p_s, hist_s, *state_s, W, n_seq=Bs, seq_len=Ls, steps=steps_b, precise=False,
                         fox_kv=out_s.get("fox_kv"))
    out_s.update(out_sb)

    def leaves(x, out, nb, sl):
        y = rmsnorm(x, norm_final).reshape(nb, sl, D)
        fk, fv = (a.reshape(n_fox, nb, sl, H, hd) for a in out["fox_kv"])
        flf = jnp.stack([out[("fox", j)].reshape(nb, sl, H) for j in range(n_fox)])
        sre = jnp.stack([out[("s5", j)][0].reshape(nb, G, NS) for j in range(n_s5)])
        sim = jnp.stack([out[("s5", j)][1].reshape(nb, G, NS) for j in range(n_s5)])
        gs = jnp.stack([out[("gdn", j)][0] for j in range(n_gdn)])
        gc = jnp.stack([out[("gdn", j)][1] for j in range(n_gdn)])
        return y, fk, fv, flf, sre, sim, gs, gc

    (y_p, fk_p, fv_p, flf_p, sre_p, sim_p, gs_p, gc_p) = leaves(x_p, out_p, B, L)
    (y_s, fk_s, fv_s, flf_s, sre_s, sim_s, gs_s, gc_s) = leaves(x_s, out_s, Bs, Ls)
    return (y_p, y_s, fk_p, fv_p, flf_p, fk_s, fv_s, flf_s, sre_p, sim_p, sre_s, sim_s,
            gs_p, gc_p, gs_s, gc_s)
```

```python
import functools
import math

import jax
import jax.numpy as jnp
from jax import lax
from jax.experimental import pallas as pl
from jax.experimental.pallas import tpu as pltpu

F32 = jnp.float32
BF16 = jnp.bfloat16
EPS = 1e-6
LANES = 128
SUBLANES = 8
VMEM_LIMIT = 56 * 1024 * 1024
CHUNK = 64
N_EXPERTS = 8
NEG_INF = float("-inf")
LOG2E = math.log2(math.e)


def _cparams(*sem):
    return pltpu.CompilerParams(dimension_semantics=sem, vmem_limit_bytes=VMEM_LIMIT)


def _rms_scale(x, g):
    ms = jnp.mean(x * x, axis=-1, keepdims=True)
    return x * lax.rsqrt(ms + EPS) * g


def _sigmoid(x):
    return 0.5 * jnp.tanh(0.5 * x) + 0.5


def _silu(x):
    return x * _sigmoid(x)


def _softplus(x):
    return jnp.maximum(x, 0.0) + jnp.log1p(jnp.exp(-jnp.abs(x)))


def _split_bf16(x):
    hi = x.astype(BF16)
    lo = (x - hi.astype(F32)).astype(BF16)
    return hi, lo


def _split_weight(w):
    w = w.astype(F32)
    hi = lax.reduce_precision(w, exponent_bits=8, mantissa_bits=7)
    return hi.astype(BF16), (w - hi).astype(BF16)


def _dot(a, b):
    return jnp.dot(a, b, preferred_element_type=F32)


def _dot_nt(a, b):
    return lax.dot_general(a, b, (((1,), (1,)), ((), ())), preferred_element_type=F32)


def _dot_tn(a, b):
    return lax.dot_general(a, b, (((0,), (0,)), ((), ())), preferred_element_type=F32)


def _parts(x, precise):
    return _split_bf16(x.astype(F32)) if precise else (x.astype(BF16),)


def _mm(xp, wp):
    if len(wp) == 1:
        return _dot(xp[0], wp[0])
    return _dot(xp[0], wp[0]) + _dot(xp[1], wp[0]) + _dot(xp[0], wp[1])


def _mm_nt(ap, bp):
    if len(ap) == 1:
        return _dot_nt(ap[0], bp[0])
    return _dot_nt(ap[0], bp[0]) + _dot_nt(ap[1], bp[0]) + _dot_nt(ap[0], bp[1])


def _read(refs):
    return tuple(r[...] for r in refs)


def _wparts(w):
    return tuple(w) if isinstance(w, (tuple, list)) else (w,)


def _row_tile(m, n_weight_parts=1):
    cap = 1024 // n_weight_parts
    return next(t for t in (1024, 512, 256, 128) if t <= cap and m % t == 0)


def _store_norm(x_ref, g_ref, xn_refs):
    xn = _rms_scale(x_ref[...], g_ref[...])
    for ref, part in zip(xn_refs, _parts(xn, len(xn_refs) == 2)):
        ref[...] = part


def _norm_mm_kernel(*refs, ranges, nw):
    x_ref, g_ref = refs[:2]
    w_refs = refs[2:2 + nw]
    outs = refs[2 + nw:2 + nw + len(ranges)]
    xn_refs = refs[2 + nw + len(ranges):]
    j = pl.program_id(1)

    @pl.when(j == 0)
    def _():
        _store_norm(x_ref, g_ref, xn_refs)

    acc = _mm(_read(xn_refs), _read(w_refs))
    for o_ref, (a, b) in zip(outs, ranges):
        @pl.when((j >= a) & (j < b))
        def _(o_ref=o_ref):
            o_ref[...] = acc.astype(o_ref.dtype)


def _range_map(i, j, *, a, n):
    return (i, jnp.clip(j - a, 0, n - 1))


def norm_mm(x, g, w, outs, *, tn):
    M, D = x.shape
    wp = _wparts(w)
    N = wp[0].shape[1]
    tm = _row_tile(M, len(wp))
    ranges = tuple((a, b) for _, a, b in outs)
    return pl.pallas_call(
        functools.partial(_norm_mm_kernel, ranges=ranges, nw=len(wp)),
        grid=(M // tm, N // tn),
        in_specs=[pl.BlockSpec((tm, D), lambda i, j: (i, 0)),
                  pl.BlockSpec((1, D), lambda i, j: (0, 0))]
                 + [pl.BlockSpec((D, tn), lambda i, j: (0, j))] * len(wp),
        out_specs=[pl.BlockSpec((tm, tn), functools.partial(_range_map, a=a, n=b - a))
                   for _, a, b in outs],
        out_shape=[jax.ShapeDtypeStruct((M, (b - a) * tn), dt) for dt, a, b in outs],
        scratch_shapes=[pltpu.VMEM((tm, D), BF16)] * len(wp),
        compiler_params=_cparams("parallel", "arbitrary"),
        name="norm_mm",
    )(x, g.reshape(1, D), *wp)


def _fox_in_kernel(*refs, nw, nt, n_heads, hd, tm, n_bufs):
    x_ref, g_ref = refs[:2]
    w_refs = refs[2:2 + nw]
    fhi_ref, flo_ref, aux_ref = refs[2 + nw:5 + nw]
    qkv_ref, k_ref, v_ref, lf_ref = refs[5 + nw + n_bufs:9 + nw + n_bufs]
    xn_refs = refs[9 + nw + n_bufs:]
    j = pl.program_id(1)

    @pl.when(j == 0)
    def _():
        _store_norm(x_ref, g_ref, xn_refs)
        xn = _rms_scale(x_ref[...], g_ref[...])
        gate = _mm(_parts(xn, True), (fhi_ref[...], flo_ref[...])) + aux_ref[0:1, :]
        lf_ref[...] = -_softplus(-gate)

    acc = _mm(_read(xn_refs), _read(w_refs))
    qkv_ref[...] = acc.astype(qkv_ref.dtype)
    hpt = acc.shape[1] // hd
    for dst, base in ((k_ref, nt), (v_ref, 2 * nt)):
        for jj in range(nt):
            @pl.when(j == base + jj)
            def _(dst=dst, jj=jj):
                for hh in range(hpt):
                    dst[pl.ds(jj * hpt + hh, tm, stride=n_heads), :] = acc[:, hh * hd:(hh + 1) * hd]


def fox_in_proj(x, g, w, w_f, aux_f, *, n_heads, hd, tn, bufs=None, slot=0, n_slots=1):
    M, D = x.shape
    wp = _wparts(w)
    nw = len(wp)
    tm = min(512, M)
    nt = D // tn
    n_bufs = 0 if bufs is None else 2
    blk0 = slot * (M // tm)
    head_spec = pl.BlockSpec((tm * n_heads, hd), lambda i, j: (blk0 + i, 0))
    head_shape = jax.ShapeDtypeStruct((n_slots * M * n_heads, hd), F32)
    return pl.pallas_call(
        functools.partial(_fox_in_kernel, nw=nw, nt=nt, n_heads=n_heads, hd=hd, tm=tm, n_bufs=n_bufs),
        grid=(M // tm, 3 * nt),
        in_specs=[pl.BlockSpec((tm, D), lambda i, j: (i, 0)),
                  pl.BlockSpec((1, D), lambda i, j: (0, 0))]
                 + [pl.BlockSpec((D, tn), lambda i, j: (0, j))] * nw
                 + [pl.BlockSpec((D, LANES), lambda i, j: (0, 0))] * 2
                 + [pl.BlockSpec((SUBLANES, LANES), lambda i, j: (0, 0))]
                 + [pl.BlockSpec(memory_space=pl.ANY)] * n_bufs,
        out_specs=[pl.BlockSpec((tm, tn), lambda i, j: (i, j)), head_spec, head_spec,
                   pl.BlockSpec((tm, LANES), lambda i, j: (i, 0))],
        out_shape=[jax.ShapeDtypeStruct((M, 3 * D), F32 if nw == 2 else BF16), head_shape, head_shape,
                   jax.ShapeDtypeStruct((M, LANES), F32)],
        scratch_shapes=[pltpu.VMEM((tm, D), BF16)] * nw,
        input_output_aliases={} if bufs is None else {5 + nw: 1, 6 + nw: 2},
        compiler_params=_cparams("parallel", "arbitrary"),
        name="fox_in_proj",
    )(x, g.reshape(1, D), *wp, *w_f, aux_f, *(bufs or ()))


def _glu_up_kernel(*refs, tiles_per_expert, nw):
    x_ref, g_ref = refs[:2]
    wg_refs = refs[2:2 + nw]
    wu_refs = refs[2 + nw:2 + 2 * nw]
    rest = refs[2 + 2 * nw:]
    comb_ref = rest[0] if tiles_per_expert else None
    o_ref = rest[1] if tiles_per_expert else rest[0]
    xn_refs = rest[2:] if tiles_per_expert else rest[1:]
    j = pl.program_id(1)

    @pl.when(j == 0)
    def _():
        _store_norm(x_ref, g_ref, xn_refs)

    xn = _read(xn_refs)
    h = _silu(_mm(xn, _read(wg_refs))) * _mm(xn, _read(wu_refs))
    if tiles_per_expert:
        comb = comb_ref[...]
        lane = lax.broadcasted_iota(jnp.int32, comb.shape, 1)
        col = jnp.sum(jnp.where(lane == j // tiles_per_expert, comb, 0.0), axis=1, keepdims=True)
        h = h * col
    o_ref[...] = h.astype(o_ref.dtype)


def glu_up(x, g, w_gate, w_up, comb=None, *, tn):
    M, D = x.shape
    wg, wu = _wparts(w_gate), _wparts(w_up)
    nw = len(wg)
    tm = _row_tile(M, nw)
    if comb is None:
        n_out, tpe = wg[0].shape[1], 0
        w_spec = pl.BlockSpec((D, tn), lambda i, j: (0, j))
        extra_in, extra_specs = (), []
    else:
        E, _, F = wg[0].shape
        n_out, tpe = E * F, F // tn
        w_spec = pl.BlockSpec((None, D, tn), lambda i, j: (j // tpe, 0, j % tpe))
        extra_in = (comb,)
        extra_specs = [pl.BlockSpec((tm, LANES), lambda i, j: (i, 0))]
    return pl.pallas_call(
        functools.partial(_glu_up_kernel, tiles_per_expert=tpe, nw=nw),
        grid=(M // tm, n_out // tn),
        in_specs=[pl.BlockSpec((tm, D), lambda i, j: (i, 0)),
                  pl.BlockSpec((1, D), lambda i, j: (0, 0))] + [w_spec] * (2 * nw) + extra_specs,
        out_specs=pl.BlockSpec((tm, tn), lambda i, j: (i, j)),
        out_shape=jax.ShapeDtypeStruct((M, n_out), F32 if nw == 2 else BF16),
        scratch_shapes=[pltpu.VMEM((tm, D), BF16)] * nw,
        compiler_params=_cparams("parallel", "arbitrary"),
        name="glu_up",
    )(x, g.reshape(1, D), *wg, *wu, *extra_in)


def _mm_res_kernel(*refs, nk, nw):
    x_ref = refs[0]
    w_refs = refs[1:1 + nw]
    r_ref, o_ref = refs[1 + nw:]
    k = pl.program_id(2)
    prod = _mm(_parts(x_ref[...], nw == 2), _read(w_refs))

    @pl.when(k == 0)
    def _():
        o_ref[...] = r_ref[...] + prod

    if nk > 1:
        @pl.when(k > 0)
        def _():
            o_ref[...] += prod


def mm_res(x, w, res, *, tn):
    M, K = x.shape
    wp = _wparts(w)
    N = wp[0].shape[1]
    tm = _row_tile(M, len(wp))
    tk = min(K, 4096 // len(wp))
    nk = K // tk
    return pl.pallas_call(
        functools.partial(_mm_res_kernel, nk=nk, nw=len(wp)),
        grid=(M // tm, N // tn, nk),
        in_specs=[pl.BlockSpec((tm, tk), lambda i, j, k: (i, k))]
                 + [pl.BlockSpec((tk, tn), lambda i, j, k: (k, j))] * len(wp)
                 + [pl.BlockSpec((tm, tn), lambda i, j, k: (i, j))],
        out_specs=pl.BlockSpec((tm, tn), lambda i, j, k: (i, j)),
        out_shape=jax.ShapeDtypeStruct((M, N), F32),
        compiler_params=_cparams("parallel", "parallel", "arbitrary"),
        name="mm_res",
    )(x, *wp, res)


def _glu_res_kernel(*refs, nw):
    y_ref = refs[0]
    wa_refs = refs[1:1 + nw]
    wb_refs = refs[1 + nw:1 + 2 * nw]
    r_ref, o_ref = refs[1 + 2 * nw:]
    y = _parts(y_ref[...], nw == 2)
    o_ref[...] = r_ref[...] + _mm(y, _read(wa_refs)) * _sigmoid(_mm(y, _read(wb_refs)))


def glu_res(y, w, res, *, tn):
    M, K = y.shape
    wp = _wparts(w)
    N = wp[0].shape[1] // 2
    nj = N // tn
    tm = _row_tile(M, len(wp))
    return pl.pallas_call(
        functools.partial(_glu_res_kernel, nw=len(wp)),
        grid=(M // tm, nj),
        in_specs=[pl.BlockSpec((tm, K), lambda i, j: (i, 0))]
                 + [pl.BlockSpec((K, tn), lambda i, j: (0, j))] * len(wp)
                 + [pl.BlockSpec((K, tn), lambda i, j: (0, j + nj))] * len(wp)
                 + [pl.BlockSpec((tm, tn), lambda i, j: (i, j))],
        out_specs=pl.BlockSpec((tm, tn), lambda i, j: (i, j)),
        out_shape=jax.ShapeDtypeStruct((M, N), F32),
        compiler_params=_cparams("parallel", "arbitrary"),
        name="glu_res",
    )(y, *wp, *wp, res)


def _ple_kernel(*refs, tn, nw):
    x_ref, g_ref = refs[:2]
    wg_refs = refs[2:2 + nw]
    p_ref = refs[2 + nw]
    wu_refs = refs[3 + nw:3 + 2 * nw]
    o_ref = refs[3 + 2 * nw]
    xn_refs = refs[4 + 2 * nw:]
    j = pl.program_id(1)

    @pl.when(j == 0)
    def _():
        _store_norm(x_ref, g_ref, xn_refs)

    gate = _sigmoid(_mm(_read(xn_refs), _read(wg_refs)))
    up = _mm(_parts(p_ref[...], nw == 2), _read(wu_refs))
    x_tile = x_ref[:, pl.ds(pl.multiple_of(j * tn, tn), tn)]
    o_ref[...] = x_tile + up * gate


def ple(x, g, w_gate, p, w_up, *, tn):
    M, D = x.shape
    P = p.shape[1]
    wg, wu = _wparts(w_gate), _wparts(w_up)
    nw = len(wg)
    tm = _row_tile(M, nw)
    return pl.pallas_call(
        functools.partial(_ple_kernel, tn=tn, nw=nw),
        grid=(M // tm, D // tn),
        in_specs=[pl.BlockSpec((tm, D), lambda i, j: (i, 0)),
                  pl.BlockSpec((1, D), lambda i, j: (0, 0))]
                 + [pl.BlockSpec((D, tn), lambda i, j: (0, j))] * nw
                 + [pl.BlockSpec((tm, P), lambda i, j: (i, 0))]
                 + [pl.BlockSpec((P, tn), lambda i, j: (0, j))] * nw,
        out_specs=pl.BlockSpec((tm, tn), lambda i, j: (i, j)),
        out_shape=jax.ShapeDtypeStruct((M, D), F32),
        scratch_shapes=[pltpu.VMEM((tm, D), BF16)] * nw,
        compiler_params=_cparams("parallel", "arbitrary"),
        name="ple",
    )(x, g.reshape(1, D), *wg, p, *wu)


def _rmsnorm_kernel(x_ref, g_ref, o_ref):
    o_ref[...] = _rms_scale(x_ref[...], g_ref[...]).astype(o_ref.dtype)


def rmsnorm(x, g):
    M, D = x.shape
    tm = _row_tile(M)
    return pl.pallas_call(
        _rmsnorm_kernel,
        grid=(M // tm,),
        in_specs=[pl.BlockSpec((tm, D), lambda i: (i, 0)),
                  pl.BlockSpec((1, D), lambda i: (0, 0))],
        out_specs=pl.BlockSpec((tm, D), lambda i: (i, 0)),
        out_shape=jax.ShapeDtypeStruct((M, D), F32),
        compiler_params=_cparams("parallel"),
        name="rmsnorm",
    )(x, g.reshape(1, D))


def _row_rstd_kernel(x_ref, o_ref):
    x = x_ref[...]
    rstd = lax.rsqrt(jnp.mean(x * x, axis=-1, keepdims=True) + EPS)
    o_ref[...] = jnp.broadcast_to(rstd, o_ref.shape)


def row_rstd(x):
    M, D = x.shape
    tm = _row_tile(M)
    return pl.pallas_call(
        _row_rstd_kernel,
        grid=(M // tm,),
        in_specs=[pl.BlockSpec((tm, D), lambda i: (i, 0))],
        out_specs=pl.BlockSpec((tm, LANES), lambda i: (i, 0)),
        out_shape=jax.ShapeDtypeStruct((M, LANES), F32),
        compiler_params=_cparams("parallel"),
        name="row_rstd",
    )(x)


def _small_proj_kernel(x_ref, g_ref, whi_ref, wlo_ref, aux_ref, o_ref, *, mode, nh):
    xn = _rms_scale(x_ref[...], g_ref[...])
    acc = _mm(_parts(xn, True), (whi_ref[...], wlo_ref[...]))
    lane = lax.broadcasted_iota(jnp.int32, acc.shape, 1)
    if mode == "logf":
        o_ref[...] = -_softplus(-(acc + aux_ref[0:1, :]))
    elif mode == "gdn":
        beta = _sigmoid(acc)
        gdec = -jnp.exp(aux_ref[1:2, :]) * _softplus(acc + aux_ref[0:1, :])
        o_ref[...] = jnp.where(lane < nh, beta, gdec)
    else:
        logits = jnp.where(lane < N_EXPERTS, acc, NEG_INF)
        v1 = jnp.max(logits, axis=1, keepdims=True)
        i1 = jnp.min(jnp.where(logits == v1, lane, LANES), axis=1, keepdims=True)
        rest = jnp.where(lane == i1, NEG_INF, logits)
        v2 = jnp.max(rest, axis=1, keepdims=True)
        i2 = jnp.min(jnp.where(rest == v2, lane, LANES), axis=1, keepdims=True)
        e2 = jnp.exp(v2 - v1)
        g1 = 1.0 / (1.0 + e2)
        o_ref[...] = jnp.where(lane == i1, g1, 0.0) + jnp.where(lane == i2, e2 * g1, 0.0)


def small_proj(x, g, w_parts, aux, mode, *, nh=0):
    M, D = x.shape
    tm = _row_tile(M)
    return pl.pallas_call(
        functools.partial(_small_proj_kernel, mode=mode, nh=nh),
        grid=(M // tm,),
        in_specs=[pl.BlockSpec((tm, D), lambda i: (i, 0)),
                  pl.BlockSpec((1, D), lambda i: (0, 0)),
                  pl.BlockSpec((D, LANES), lambda i: (0, 0)),
                  pl.BlockSpec((D, LANES), lambda i: (0, 0)),
                  pl.BlockSpec((SUBLANES, LANES), lambda i: (0, 0))],
        out_specs=pl.BlockSpec((tm, LANES), lambda i: (i, 0)),
        out_shape=jax.ShapeDtypeStruct((M, LANES), F32),
        compiler_params=_cparams("parallel"),
        name="small_proj_" + mode,
    )(x, g.reshape(1, D), *w_parts, aux)


def _pad_cols(w):
    return _split_weight(jnp.pad(w.astype(F32), ((0, 0), (0, LANES - w.shape[1]))))


def _aux_rows(*rows):
    out = [jnp.pad(vals.astype(F32), (start, LANES - start - vals.shape[0])) for start, vals in rows]
    out += [jnp.zeros((LANES,), F32)] * (SUBLANES - len(out))
    return jnp.stack(out)


def _lane_cumsum_kernel(x_ref, o_ref):
    x = x_ref[...]
    n = x.shape[1]
    lane = lax.broadcasted_iota(jnp.int32, x.shape, 1)
    shift = 1
    while shift < n:
        x = x + jnp.where(lane >= shift, pltpu.roll(x, shift, 1), 0.0)
        shift *= 2
    o_ref[...] = x


def lane_cumsum(x):
    R, n = x.shape
    rb = SUBLANES if R % SUBLANES == 0 else R
    return pl.pallas_call(
        _lane_cumsum_kernel,
        grid=(R // rb,),
        in_specs=[pl.BlockSpec((rb, n), lambda i: (i, 0))],
        out_specs=pl.BlockSpec((rb, n), lambda i: (i, 0)),
        out_shape=jax.ShapeDtypeStruct((R, n), F32),
        compiler_params=_cparams("parallel"),
        name="lane_cumsum",
    )(x)


def _fox_prompt_kernel(q_ref, k_ref, v_ref, fk_ref, o_ref, *, tq, nhb, hd, scale):
    qi = pl.program_id(2)

    def step(ki, carry, diagonal):
        rows = pl.ds(pl.multiple_of(ki * tq, tq), tq)
        out = []
        for h in range(nhb):
            m, l, acc = carry[h]
            cols = slice(h * hd, (h + 1) * hd)
            s = (_dot_nt(q_ref[:, cols], k_ref[rows, cols]) * (scale * LOG2E)
                 - fk_ref[h, pl.ds(ki, 1), :] * LOG2E)
            if diagonal:
                r = lax.broadcasted_iota(jnp.int32, s.shape, 0)
                c = lax.broadcasted_iota(jnp.int32, s.shape, 1)
                s = jnp.where(r >= c, s, NEG_INF)
            m_new = jnp.maximum(m, jnp.max(s, axis=1, keepdims=True))
            p = jnp.exp2(s - m_new)
            alpha = jnp.exp2(m - m_new)
            l = alpha * l + jnp.sum(p, axis=1, keepdims=True)
            acc = alpha * acc + _dot(p.astype(BF16), v_ref[rows, cols])
            out.append((m_new, l, acc))
        return tuple(out)

    init = tuple((jnp.full((tq, 1), NEG_INF, F32), jnp.zeros((tq, 1), F32), jnp.zeros((tq, hd), F32))
                 for _ in range(nhb))
    carry = lax.fori_loop(0, qi, lambda ki, c: step(ki, c, False), init)
    carry = step(qi, carry, True)
    o_ref[...] = jnp.concatenate([acc / l for _, l, acc in carry], axis=1).astype(o_ref.dtype)


def fox_attention_prompt(qkv, fk, *, B, L, H, hd, tq, nhb=2):
    nq = L // tq
    nhg = H // nhb
    return pl.pallas_call(
        functools.partial(_fox_prompt_kernel, tq=tq, nhb=nhb, hd=hd, scale=hd ** -0.5),
        grid=(B, nhg, nq),
        in_specs=[pl.BlockSpec((tq, nhb * hd), lambda b, g, i: (b * nq + i, g)),
                  pl.BlockSpec((L, nhb * hd), lambda b, g, i: (b, nhg + g)),
                  pl.BlockSpec((L, nhb * hd), lambda b, g, i: (b, 2 * nhg + g)),
                  pl.BlockSpec((nhb, nq, tq), lambda b, g, i: (b * nhg + g, 0, 0))],
        out_specs=pl.BlockSpec((tq, nhb * hd), lambda b, g, i: (b * nq + i, g)),
        out_shape=jax.ShapeDtypeStruct((B * L, H * hd), BF16),
        compiler_params=_cparams("parallel", "parallel", "arbitrary"),
        name="fox_attention_prompt",
    )(qkv, qkv, qkv, fk)


def _fox_cached_kernel(q_ref, kn_ref, vn_ref, kp_ref, vp_ref, f_ref, o_ref, *, P, nhb, hd, scale, precise,
                       head_rows):
    n = q_ref.shape[0]
    r = lax.broadcasted_iota(jnp.int32, (n, n), 0)
    c = lax.broadcasted_iota(jnp.int32, (n, n), 1)
    g = pl.program_id(1)

    def past(ref, h):
        if head_rows:
            return ref[pl.ds(g * nhb + h, P, stride=head_rows), :]
        return ref[:, h * hd:(h + 1) * hd]

    outs = []
    for h in range(nhb):
        cols = slice(h * hd, (h + 1) * hd)
        q = _parts(q_ref[:, cols], precise)
        f = f_ref[h]
        s_past = _mm_nt(q, _parts(past(kp_ref, h), precise)) * scale - f[:, :P]
        s_new = _mm_nt(q, _parts(kn_ref[:, cols], precise)) * scale - f[:, P:P + n]
        s_new = jnp.where(r >= c, s_new, NEG_INF)
        m = jnp.maximum(jnp.max(s_past, axis=1, keepdims=True), jnp.max(s_new, axis=1, keepdims=True))
        p_past = jnp.exp(s_past - m)
        p_new = jnp.exp(s_new - m)
        l = jnp.sum(p_past, axis=1, keepdims=True) + jnp.sum(p_new, axis=1, keepdims=True)
        acc = (_mm(_parts(p_past, precise), _parts(past(vp_ref, h), precise))
               + _mm(_parts(p_new, precise), _parts(vn_ref[:, cols], precise)))
        outs.append(acc / l)
    o_ref[...] = jnp.concatenate(outs, axis=1).astype(o_ref.dtype)


def fox_attention_cached(qkv, k_past, v_past, f, *, seq0, k_lane0, v_lane0, B, n, P, H, hd):
    precise = qkv.dtype == F32
    nhb = 4 if H % 4 == 0 else 1
    nhg = H // nhb
    w = nhb * hd
    head_rows = H if k_lane0 is None else 0
    if head_rows:
        past_specs = [pl.BlockSpec((None, P * H, hd), lambda b, g: (seq0 + b, 0, 0))] * 2
    else:
        past_specs = [pl.BlockSpec((None, P, w), lambda b, g: (seq0 + b, 0, k_lane0 // nhb + g)),
                      pl.BlockSpec((None, P, w), lambda b, g: (seq0 + b, 0, v_lane0 // nhb + g))]
    return pl.pallas_call(
        functools.partial(_fox_cached_kernel, P=P, nhb=nhb, hd=hd, scale=hd ** -0.5, precise=precise,
                          head_rows=head_rows),
        grid=(B, nhg),
        in_specs=[pl.BlockSpec((n, w), lambda b, g: (b, g)),
                  pl.BlockSpec((n, w), lambda b, g: (b, nhg + g)),
                  pl.BlockSpec((n, w), lambda b, g: (b, 2 * nhg + g))]
                 + past_specs
                 + [pl.BlockSpec((nhb, 1, f.shape[2]), lambda b, g: (b * nhg + g, 0, 0))],
        out_specs=pl.BlockSpec((n, w), lambda b, g: (b, g)),
        out_shape=jax.ShapeDtypeStruct((B * n, H * hd), qkv.dtype),
        compiler_params=_cparams("parallel", "arbitrary"),
        name="fox_attention_cached",
    )(qkv, qkv, qkv, k_past, v_past, f)


def _gelu_tanh(x):
    return 0.5 * x * (1.0 + jnp.tanh(math.sqrt(2.0 / math.pi) * (x + 0.044715 * (x * x * x))))


def _s5_kernel(*refs, nb, c, ns, ncm, snap_chunk):
    x_ref, rstd_ref, g_ref, bhi_ref, blo_ref = refs[:5]
    cm_refs = refs[5:5 + ncm]
    (ar_ref, ai_ref, d_ref, h0r_ref, h0i_ref, y_ref, hr_out, hi_out, hr_snap, hi_snap,
     u_sc, y_sc, bu_sc, st_sc, hr_sc, hi_sc) = refs[5 + ncm:]
    tc = pl.program_id(1)

    @pl.when(tc == 0)
    def _():
        hr_sc[...] = h0r_ref[...]
        hi_sc[...] = h0i_ref[...]

    g = g_ref[...]
    for b in range(nb):
        u_sc[pl.ds(b, c, stride=nb), :] = x_ref[b] * rstd_ref[b] * g
    u = u_sc[...]
    u_hi, u_lo = _split_bf16(u)
    bu_sc[...] = _dot(jnp.concatenate([u_hi, u_lo], axis=1), bhi_ref[0]) + _dot(u_hi, blo_ref[0])
    ar = jnp.broadcast_to(ar_ref[0], (nb, ns))
    ai = jnp.broadcast_to(ai_ref[0], (nb, ns))

    def step(t, carry):
        hr, hi = carry
        rows = pl.ds(pl.multiple_of(t * nb, nb), nb)
        nhr = ar * hr - ai * hi + bu_sc[rows, 0:ns]
        nhi = ar * hi + ai * hr + bu_sc[rows, ns:2 * ns]
        st_sc[rows, 0:ns] = nhr
        st_sc[rows, ns:2 * ns] = nhi
        return nhr, nhi

    hr, hi = lax.fori_loop(0, c, step, (hr_sc[...], hi_sc[...]))
    hr_sc[...] = hr
    hi_sc[...] = hi
    y = _mm(_parts(st_sc[...], ncm == 2), tuple(r[0] for r in cm_refs)) + d_ref[...] * u
    y_sc[...] = _gelu_tanh(y)
    for b in range(nb):
        y_ref[b] = y_sc[pl.ds(b, c, stride=nb), :].astype(y_ref.dtype)

    @pl.when(tc == snap_chunk)
    def _():
        hr_snap[...] = hr
        hi_snap[...] = hi

    @pl.when(tc == pl.num_programs(1) - 1)
    def _():
        hr_out[...] = hr
        hi_out[...] = hi


def s5_scan(x, rstd, g, h0r, h0i, consts, *, c, precise, snap_steps=None):
    bhi, blo, cm_hi, cm_lo, ar, ai, d = consts
    cms = (cm_hi, cm_lo) if precise else (cm_hi,)
    nb, L, D = x.shape
    nblk = D // LANES
    ns = ar.shape[2]
    ntc = L // c
    snap_chunk = ntc - 1 if snap_steps is None else snap_steps // c - 1
    blk = lambda g, t: (g, 0, 0)
    state_spec = pl.BlockSpec((nb, ns), lambda g, t: (0, g))
    state_shape = jax.ShapeDtypeStruct(h0r.shape, F32)
    return pl.pallas_call(
        functools.partial(_s5_kernel, nb=nb, c=c, ns=ns, ncm=len(cms), snap_chunk=snap_chunk),
        grid=(nblk, ntc),
        in_specs=[pl.BlockSpec((nb, c, LANES), lambda g, t: (0, t, g)),
                  pl.BlockSpec((nb, c, LANES), lambda g, t: (0, t, 0)),
                  pl.BlockSpec((1, LANES), lambda g, t: (0, g)),
                  pl.BlockSpec((1, 2 * LANES, 2 * ns), blk),
                  pl.BlockSpec((1, LANES, 2 * ns), blk)]
                 + [pl.BlockSpec((1, 2 * ns, LANES), blk)] * len(cms)
                 + [pl.BlockSpec((1, 1, ns), blk),
                    pl.BlockSpec((1, 1, ns), blk),
                    pl.BlockSpec((1, LANES), lambda g, t: (0, g)),
                    state_spec, state_spec],
        out_specs=[pl.BlockSpec((nb, c, LANES), lambda g, t: (0, t, g))] + [state_spec] * 4,
        out_shape=[jax.ShapeDtypeStruct((nb, L, D), F32 if precise else BF16)] + [state_shape] * 4,
        scratch_shapes=[pltpu.VMEM((c * nb, LANES), F32),
                        pltpu.VMEM((c * nb, LANES), F32),
                        pltpu.VMEM((c * nb, 2 * ns), F32),
                        pltpu.VMEM((c * nb, 2 * ns), F32),
                        pltpu.VMEM((nb, ns), F32),
                        pltpu.VMEM((nb, ns), F32)],
        compiler_params=_cparams("parallel", "arbitrary"),
        name="s5_scan",
    )(x, rstd, g.reshape(1, D), bhi, blo, *cms, ar, ai, d, h0r, h0i)


def s5_constants(a_re, a_im, b_re, b_im, c_re, c_im, d_skip, log_dt):
    G, N = a_re.shape
    gc = b_re.shape[2]
    gpb = LANES // gc
    nblk = G // gpb
    a = lax.complex(a_re.astype(F32), a_im.astype(F32))
    dt = jnp.exp(log_dt.astype(F32))[:, None]
    a_bar = jnp.exp(a * dt)
    b_bar = ((a_bar - 1.0) / a)[..., None] * lax.complex(b_re.astype(F32), b_im.astype(F32))
    eye = jnp.eye(gpb, dtype=F32)

    def in_mat(b):
        b = b.reshape(nblk, gpb, N, gc)
        return jnp.einsum("kgnc,gh->kgchn", b, eye).reshape(nblk, gpb * gc, gpb * N)

    def out_mat(cc):
        cc = cc.reshape(nblk, gpb, gc, N)
        return jnp.einsum("kgcn,gh->kgnhc", cc, eye).reshape(nblk, gpb * N, gpb * gc)

    bhi, blo = _split_weight(jnp.concatenate([in_mat(b_bar.real), in_mat(b_bar.imag)], axis=2))
    bhi = jnp.concatenate([bhi, bhi], axis=1)
    cm_hi, cm_lo = _split_weight(jnp.concatenate([out_mat(c_re.astype(F32)), -out_mat(c_im.astype(F32))], axis=1))
    ar = a_bar.real.reshape(nblk, 1, gpb * N)
    ai = a_bar.imag.reshape(nblk, 1, gpb * N)
    return bhi, blo, cm_hi, cm_lo, ar, ai, d_skip.astype(F32).reshape(1, G * gc)


def _gdn_conv_kernel(x_ref, halo_ref, buf_ref, w_ref, o_ref, ext_sc, *, tl, n_valid, tiles_per_seq, n_norm_tiles):
    r = pl.program_id(0)
    j = pl.program_id(1)
    first = (r % tiles_per_seq) == 0
    ext_sc[0:SUBLANES, :] = jnp.where(first, buf_ref[0], halo_ref[...])
    ext_sc[SUBLANES:SUBLANES + n_valid, :] = x_ref[...]
    w = w_ref[...]
    y = x_ref[...] * w[3:4, :]
    for s in (1, 2, 3):
        y = y + ext_sc[SUBLANES - s:SUBLANES - s + n_valid, :] * w[3 - s:4 - s, :]
    y = _silu(y)

    def store(val):
        if n_valid < tl:
            val = jnp.concatenate([val, jnp.zeros((tl - n_valid, val.shape[1]), F32)], axis=0)
        o_ref[...] = val.astype(o_ref.dtype)

    @pl.when(j < n_norm_tiles)
    def _():
        segs = []
        for a in range(0, y.shape[1], LANES):
            seg = y[:, a:a + LANES]
            segs.append(seg * lax.rsqrt(jnp.sum(seg * seg, axis=1, keepdims=True) + EPS))
        store(jnp.concatenate(segs, axis=1))

    @pl.when(j >= n_norm_tiles)
    def _():
        store(y)


def gdn_conv(x, buf8, w8, *, n_seq, seq_len, tl_in, tl_out, n_norm_cols, tc):
    C = x.shape[1]
    tiles_per_seq = seq_len // tl_in
    n_tiles = n_seq * tiles_per_seq
    hpt = tl_in // SUBLANES
    return pl.pallas_call(
        functools.partial(_gdn_conv_kernel, tl=tl_out, n_valid=tl_in, tiles_per_seq=tiles_per_seq,
                          n_norm_tiles=n_norm_cols // tc),
        grid=(n_tiles, C // tc),
        in_specs=[pl.BlockSpec((tl_in, tc), lambda r, j: (r, j)),
                  pl.BlockSpec((SUBLANES, tc), lambda r, j: (jnp.maximum(r * hpt - 1, 0), j)),
                  pl.BlockSpec((1, SUBLANES, tc), lambda r, j: (r // tiles_per_seq, 0, j)),
                  pl.BlockSpec((SUBLANES, tc), lambda r, j: (0, j))],
        out_specs=pl.BlockSpec((tl_out, tc), lambda r, j: (r, j)),
        out_shape=jax.ShapeDtypeStruct((n_tiles * tl_out, C), BF16),
        scratch_shapes=[pltpu.VMEM((SUBLANES + tl_in, tc), F32)],
        compiler_params=_cparams("parallel", "arbitrary"),
        name="gdn_conv",
    )(x, x, buf8, w8)


def _gdn_prep_kernel(qk_ref, gt_ref, ti_ref, a_ref, gc_ref, n_sc, nt_sc, x_sc, *, C, cps, n_qk, rep, hd, scale):
    nh = n_qk * rep
    n_sys = cps * nh
    pitch = C + SUBLANES
    r = lax.broadcasted_iota(jnp.int32, (C, C), 0)
    c = lax.broadcasted_iota(jnp.int32, (C, C), 1)
    tril = (r >= c).astype(BF16)
    lane_pad = jnp.zeros((C, LANES - C), F32)
    for cc in range(cps):
        rows = slice(cc * C, (cc + 1) * C)
        gates = gt_ref[rows, :]
        lane = lax.broadcasted_iota(jnp.int32, gates.shape, 1)
        g = jnp.where((lane >= nh) & (lane < 2 * nh), gates, 0.0)
        g_hi = g.astype(BF16)
        g_r1 = g - g_hi.astype(F32)
        g_mid = g_r1.astype(BF16)
        g_lo = (g_r1 - g_mid.astype(F32)).astype(BF16)
        gcum = _dot(tril, g_hi) + _dot(tril, g_mid) + _dot(tril, g_lo)
        gc_ref[rows, :] = gcum
        gcum_t = gcum.T
        for hq in range(n_qk):
            q = qk_ref[rows, hq * hd:(hq + 1) * hd]
            k = qk_ref[rows, (n_qk + hq) * hd:(n_qk + hq + 1) * hd]
            kk = _dot_nt(k, k)
            qk = _dot_nt(q, k) * scale
            for rr in range(rep):
                hv = hq * rep + rr
                sys = cc * nh + hv
                diff = gcum[:, nh + hv:nh + hv + 1] - gcum_t[nh + hv:nh + hv + 1, :]
                decay = jnp.exp(jnp.where(r >= c, diff, NEG_INF))
                n_val = jnp.where(r > c, gates[:, hv:hv + 1] * kk * decay, 0.0)
                n_sc[sys * pitch:sys * pitch + C, :] = jnp.concatenate([n_val, lane_pad], axis=1)
                a_ref[cc, hv] = (qk * decay).astype(a_ref.dtype)
    for i in range(C):
        slab = n_sc[pl.ds(i, n_sys, stride=pitch), :]
        nt_sc[i * C:(i + 1) * C, :] = slab.T[:C, :]
    _tri_solve(nt_sc, x_sc, C)
    row_pad = jnp.zeros((LANES - C, n_sys), F32)
    for i in range(C):
        cols = jnp.concatenate([x_sc[i * C:(i + 1) * C, :], row_pad], axis=0)
        n_sc[pl.ds(i, n_sys, stride=pitch), :] = cols.T
    for cc in range(cps):
        for hv in range(nh):
            sys = cc * nh + hv
            ti_ref[cc, hv] = n_sc[sys * pitch:sys * pitch + C, :C].astype(ti_ref.dtype)


def gdn_prep(qkv_c, gates, *, C, n_qk, rep, hd):
    rows = qkv_c.shape[0]
    nch = rows // C
    nh = n_qk * rep
    cps = LANES // nh
    assert nch % cps == 0 and cps * nh == LANES
    return pl.pallas_call(
        functools.partial(_gdn_prep_kernel, C=C, cps=cps, n_qk=n_qk, rep=rep, hd=hd, scale=hd ** -0.5),
        grid=(nch // cps,),
        in_specs=[pl.BlockSpec((cps * C, 2 * n_qk * hd), lambda n: (n, 0)),
                  pl.BlockSpec((cps * C, LANES), lambda n: (n, 0))],
        out_specs=[pl.BlockSpec((cps, nh, C, C), lambda n: (n, 0, 0, 0)),
                   pl.BlockSpec((cps, nh, C, C), lambda n: (n, 0, 0, 0)),
                   pl.BlockSpec((cps * C, LANES), lambda n: (n, 0))],
        out_shape=[jax.ShapeDtypeStruct((nch, nh, C, C), BF16),
                   jax.ShapeDtypeStruct((nch, nh, C, C), BF16),
                   jax.ShapeDtypeStruct((rows, LANES), F32)],
        scratch_shapes=[pltpu.VMEM((LANES * (C + SUBLANES), LANES), F32),
                        pltpu.VMEM((C * C, LANES), F32),
                        pltpu.VMEM((C * C, LANES), F32)],
        compiler_params=_cparams("parallel"),
        name="gdn_prep",
    )(qkv_c, gates)


def _tri_solve(n_ref, x_ref, C):
    nblk = C // SUBLANES
    lanes = n_ref.shape[1]
    sub = lax.broadcasted_iota(jnp.int32, (SUBLANES, lanes), 0)
    x_ref[...] = jnp.zeros(x_ref.shape, F32)
    for ib in range(nblk):
        def row(r, _, ib=ib):
            base = (ib * SUBLANES + r) * C
            acc = [jnp.zeros((SUBLANES, lanes), F32) for _ in range(ib)]
            acc.append(jnp.where(sub == r, 1.0, 0.0).astype(F32))
            for jb in range(ib + 1):
                for j in range(jb * SUBLANES, (jb + 1) * SUBLANES):
                    nb = jnp.broadcast_to(n_ref[pl.ds(base + j, 1), :], (SUBLANES, lanes))
                    for cb in range(jb + 1):
                        acc[cb] = acc[cb] - nb * x_ref[j * C + cb * SUBLANES:j * C + (cb + 1) * SUBLANES, :]
            for cb in range(ib + 1):
                x_ref[pl.ds(pl.multiple_of(base + cb * SUBLANES, SUBLANES), SUBLANES), :] = acc[cb]
            return 0

        lax.fori_loop(0, SUBLANES, row, 0)


def _gdn_scan_kernel(q_ref, k_ref, v_ref, z_ref, gt_ref, gc_ref, ti_ref, a_ref, s0_ref, ng_ref,
                     o_ref, s_out, s_sc, *, C, hg, rep, hd, nh, n_valid, scale):
    n = pl.program_id(2)
    hgi = pl.program_id(1)

    @pl.when(n == 0)
    def _():
        s_sc[...] = s0_ref[0]

    shift = (LANES - hgi * hg) % LANES
    gates = pltpu.roll(gt_ref[...], shift, 1)
    gcum = pltpu.roll(gc_ref[...], shift, 1)
    heads = range(hg)
    s_old = [s_sc[hh] for hh in heads]
    s_bf = [s.astype(BF16) for s in s_old]
    beta = [gates[:, hh:hh + 1] for hh in heads]
    gc = [gcum[:, nh + hh:nh + hh + 1] for hh in heads]
    eg = [jnp.exp(g) for g in gc]
    g_tot = [g[C - 1:C, :] for g in gc]
    k = [k_ref[:, (hh // rep) * hd:(hh // rep + 1) * hd].astype(F32) for hh in heads]
    q = [q_ref[:, (hh // rep) * hd:(hh // rep + 1) * hd].astype(F32) for hh in heads]
    v = [v_ref[:, hh * hd:(hh + 1) * hd].astype(F32) for hh in heads]
    uw = [_dot(ti_ref[0, hh].astype(BF16),
               jnp.concatenate([v[hh] * beta[hh], k[hh] * (beta[hh] * eg[hh])], axis=1).astype(BF16))
          for hh in heads]
    q_dec = [(q[hh] * (scale * eg[hh])).astype(BF16) for hh in heads]
    k_dec = [(k[hh] * jnp.exp(g_tot[hh] - gc[hh])).astype(BF16) for hh in heads]
    v_new = [(uw[hh][:, :hd] - _dot(uw[hh][:, hd:].astype(BF16), s_bf[hh])).astype(BF16) for hh in heads]
    o = [_dot(q_dec[hh], s_bf[hh]) + _dot(a_ref[0, hh], v_new[hh]) for hh in heads]
    s_new = [s_old[hh] * jnp.exp(g_tot[hh]) + _dot_tn(k_dec[hh], v_new[hh]) for hh in heads]
    for hh in heads:
        s_sc[hh] = s_new[hh]
    ng = ng_ref[...]
    outs = [_rms_scale(o[hh][:n_valid], ng) * _silu(z_ref[:, hh * hd:(hh + 1) * hd].astype(F32)) for hh in heads]
    o_ref[...] = jnp.concatenate(outs, axis=1).astype(o_ref.dtype)

    @pl.when(n == pl.num_programs(2) - 1)
    def _():
        s_out[0] = s_sc[...]


def gdn_scan(qkv_c, z, gates, gcum, tinv, amat, s0, norm_g, *, n_seq, nc, C, n_valid, n_qk, rep, hd, hg):
    nh = n_qk * rep
    nhg = nh // hg
    qw = (hg // rep) * hd
    k_off = n_qk * hd // qw
    v_off = 2 * n_qk * hd // (hg * hd)
    chunk = lambda s, g, n: s * nc + n
    return pl.pallas_call(
        functools.partial(_gdn_scan_kernel, C=C, hg=hg, rep=rep, hd=hd, nh=nh, n_valid=n_valid,
                          scale=hd ** -0.5),
        grid=(n_seq, nhg, nc),
        in_specs=[pl.BlockSpec((C, qw), lambda s, g, n: (chunk(s, g, n), g)),
                  pl.BlockSpec((C, qw), lambda s, g, n: (chunk(s, g, n), k_off + g)),
                  pl.BlockSpec((C, hg * hd), lambda s, g, n: (chunk(s, g, n), v_off + g)),
                  pl.BlockSpec((n_valid, hg * hd), lambda s, g, n: (chunk(s, g, n), g)),
                  pl.BlockSpec((C, LANES), lambda s, g, n: (chunk(s, g, n), 0)),
                  pl.BlockSpec((C, LANES), lambda s, g, n: (chunk(s, g, n), 0)),
                  pl.BlockSpec((1, hg, C, C), lambda s, g, n: (chunk(s, g, n), g, 0, 0)),
                  pl.BlockSpec((1, hg, C, C), lambda s, g, n: (chunk(s, g, n), g, 0, 0)),
                  pl.BlockSpec((1, hg, hd, hd), lambda s, g, n: (s, g, 0, 0)),
                  pl.BlockSpec((1, hd), lambda s, g, n: (0, 0))],
        out_specs=[pl.BlockSpec((n_valid, hg * hd), lambda s, g, n: (chunk(s, g, n), g)),
                   pl.BlockSpec((1, hg, hd, hd), lambda s, g, n: (s, g, 0, 0))],
        out_shape=[jax.ShapeDtypeStruct((n_seq * nc * n_valid, nh * hd), BF16),
                   jax.ShapeDtypeStruct((n_seq, nh, hd, hd), F32)],
        scratch_shapes=[pltpu.VMEM((hg, hd, hd), F32)],
        compiler_params=_cparams("parallel", "parallel", "arbitrary"),
        name="gdn_scan",
    )(qkv_c, qkv_c, qkv_c, z, gates, gcum, tinv, amat, s0, norm_g.reshape(1, hd))


def _trunk(x, p, fox_hist, s5_h0, gdn_state, W, *, n_seq, seq_len, steps, precise, snap_steps=None,
           fox_kv=None):
    M, D = x.shape
    H, hd = W["fox_heads"]
    n_qk, rep, gd = W["gdn_heads"]
    n_vh = n_qk * rep
    key_dim, val_dim = n_qk * gd, n_vh * gd
    qkv_dim = 2 * key_dim + val_dim
    cw = W["gdn_conv_w"].shape[1]
    tn = min(512, D)
    outs = {} if fox_kv is None else {"fox_kv": fox_kv}

    def wsel(name, idx, pr):
        hi = W[name][idx]
        return (hi, W[name + "_lo"][idx]) if pr else hi

    def mixer(i, x):
        j = i // 3
        pm = precise
        g_mix = W["norm_mix"][i]
        if i % 3 == 0:
            qkv, k_heads, v_heads, logf = fox_in_proj(
                x, g_mix, wsel("fox_w_qkv", j, pm), W["fox_w_f"][j], W["fox_aux"][j], n_heads=H, hd=hd,
                tn=tn if pm else min(2 * tn, D), bufs=outs.get("fox_kv"), slot=j, n_slots=W["n_fox"])
            outs["fox_kv"] = (k_heads, v_heads)
            outs[("fox_qkv", j)] = qkv
            logf = logf[:, :H]
            lf = jnp.swapaxes(logf.reshape(n_seq, seq_len, H), 1, 2)
            if j not in fox_hist:
                tq = next(t for t in (512, 384, 256, 128, seq_len) if seq_len % t == 0)
                fk = lane_cumsum(lf.reshape(n_seq * H, seq_len)).reshape(n_seq * H, seq_len // tq, tq)
                o = fox_attention_prompt(qkv, fk, B=n_seq, L=seq_len, H=H, hd=hd, tq=tq)
            else:
                k_past, v_past, k_lane0, v_lane0, seq0, lf_past, P = fox_hist[j]
                fpad = -(-(P + seq_len) // LANES) * LANES
                lf_all = jnp.concatenate([jnp.swapaxes(lf_past.astype(F32), 1, 2), lf], axis=2)
                lf_all = jnp.pad(lf_all, ((0, 0), (0, 0), (0, fpad - (P + seq_len))))
                f = lane_cumsum(lf_all.reshape(n_seq * H, fpad)).reshape(n_seq * H, 1, fpad)
                o = fox_attention_cached(qkv, k_past, v_past, f, seq0=seq0, k_lane0=k_lane0, v_lane0=v_lane0,
                                         B=n_seq, n=seq_len, P=P, H=H, hd=hd)
            outs[("fox", j)] = logf
            return mm_res(o, wsel("fox_w_out", j, pm), x, tn=tn)
        if i % 3 == 1:
            rstd = row_rstd(x).reshape(n_seq, seq_len, LANES)
            y, hr, hi, hr_snap, hi_snap = s5_scan(x.reshape(n_seq, seq_len, D), rstd, g_mix,
                                                  s5_h0[j][0], s5_h0[j][1], W["s5_consts"][j],
                                                  c=min(CHUNK, seq_len), precise=pm, snap_steps=snap_steps)
            outs[("s5", j)] = (hr, hi)
            outs[("s5_snap", j)] = (hr_snap, hi_snap)
            return glu_res(y.reshape(M, D), wsel("s5_w_glu", j, pm), x, tn=tn)
        s0, conv_rows = gdn_state[j]
        nq = qkv_dim // tn
        nz = val_dim // tn
        qkv_raw, z = norm_mm(x, g_mix, W["gdn_w_qkvz"][j], [(F32, 0, nq), (BF16, nq, nq + nz)], tn=tn)
        gates = small_proj(x, g_mix, W["gdn_w_ba"][j], W["gdn_aux"][j], "gdn", nh=n_vh)
        C = CHUNK
        n_valid = min(C, seq_len)
        nc = seq_len // n_valid
        buf = jnp.pad(conv_rows.astype(F32), ((0, 0), (SUBLANES - (cw - 1), 0), (0, 0)))
        tl_in = min(512, seq_len)
        qkv_c = gdn_conv(qkv_raw, buf, W["gdn_conv_w8"][j], n_seq=n_seq, seq_len=seq_len, tl_in=tl_in,
                         tl_out=max(tl_in, C), n_norm_cols=2 * key_dim,
                         tc=key_dim if tl_in < C else min(512, key_dim))
        gates_c = jnp.pad(gates.reshape(n_seq * nc, n_valid, LANES),
                          ((0, 0), (0, C - n_valid), (0, 0))).reshape(n_seq * nc * C, LANES)
        tinv, amat, gcum = gdn_prep(qkv_c, gates_c, C=C, n_qk=n_qk, rep=rep, hd=gd)
        o, s_last = gdn_scan(qkv_c, z, gates_c, gcum, tinv, amat, s0.astype(F32), W["gdn_norm"][j],
                             n_seq=n_seq, nc=nc, C=C, n_valid=n_valid, n_qk=n_qk, rep=rep, hd=gd,
                             hg=min(32, n_vh))
        rows = jnp.concatenate([conv_rows.astype(F32), qkv_raw.reshape(n_seq, seq_len, qkv_dim)], axis=1)
        outs[("gdn", j)] = (s_last, rows[:, -(cw - 1):])
        return mm_res(o, W["gdn_w_out"][j], x, tn=tn)

    def channel(i, x):
        m = i // 2
        g_ffn = W["norm_ffn"][i]
        if i % 2 == 0:
            pf = precise
            h = glu_up(x, g_ffn, wsel("ffn_w_gate", m, pf), wsel("ffn_w_up", m, pf), tn=tn)
            x = mm_res(h, wsel("ffn_w_down", m, pf), x, tn=tn)
        else:
            pf = False
            comb = small_proj(x, g_ffn, W["moe_w_router"][m], W["zero_aux"], "router")
            h = glu_up(x, g_ffn, W["moe_w_gate"][m], W["moe_w_up"][m], comb, tn=tn)
            x = mm_res(h, W["moe_w_down"][m], x, tn=tn)
        p_i = p[i] if pf else p[i].astype(BF16)
        return ple(x, W["norm_ple"][i], wsel("ple_w_gate", i, pf), p_i, wsel("ple_w_up", i, pf), tn=tn)

    for i, part in steps:
        x = mixer(i, x) if part == "mixer" else channel(i, x)
    return x, outs


def kernel(x_prompt, x_sample, cache_fox_k, cache_fox_v, cache_fox_logf, state_s5_re, state_s5_im, state_gdn, state_gdn_conv, p_prompt, p_sample, norm_mix, norm_ffn, norm_ple, norm_final, fox_w_in, fox_b_f, fox_w_out, s5_a_re, s5_a_im, s5_b_re, s5_b_im, s5_c_re, s5_c_im, s5_d, s5_log_dt, s5_w_glu, gdn_w_in, gdn_conv_w, gdn_a_log, gdn_dt_bias, gdn_norm, gdn_w_out, ffn_w_gate, ffn_w_up, ffn_w_down, moe_w_router, moe_w_gate, moe_w_up, moe_w_down, ple_w_up, ple_w_gate):
    B, L, D = x_prompt.shape
    Bs, Ls, _ = x_sample.shape
    depth = norm_mix.shape[0]
    P = cache_fox_k.shape[2]
    H, hd = cache_fox_k.shape[3], cache_fox_k.shape[4]
    n_vh, gd = state_gdn.shape[2], state_gdn.shape[3]
    qkv_dim = state_gdn_conv.shape[3]
    val_dim = n_vh * gd
    n_qk = (qkv_dim - val_dim) // (2 * gd)
    cw = gdn_conv_w.shape[1]
    G, NS = s5_a_re.shape[1], s5_a_re.shape[2]
    E, _, Fe = moe_w_gate.shape[1:]
    n_fox, n_s5, n_gdn = fox_w_in.shape[0], s5_a_re.shape[0], gdn_w_in.shape[0]

    first_router = 1
    n_lo = {"fox": 1, "ffn": 1, "ple": 1, "s5": 1}

    def hi_lo(w, n):
        return w.astype(BF16), _split_weight(w[:n])[1]

    W = {"norm_mix": norm_mix, "norm_ffn": norm_ffn, "norm_ple": norm_ple, "norm_final": norm_final,
         "fox_heads": (H, hd), "n_fox": n_fox, "gdn_heads": (n_qk, n_vh // n_qk, gd), "gdn_conv_w": gdn_conv_w,
         "gdn_norm": gdn_norm, "zero_aux": jnp.zeros((SUBLANES, LANES), F32)}
    W["fox_w_qkv"], W["fox_w_qkv_lo"] = hi_lo(fox_w_in[:, :, :3 * D], n_lo["fox"])
    W["fox_w_out"], W["fox_w_out_lo"] = hi_lo(fox_w_out, n_lo["fox"])
    W["fox_w_f"] = [_pad_cols(fox_w_in[j, :, 3 * D:]) for j in range(n_fox)]
    W["fox_aux"] = [_aux_rows((0, fox_b_f[j])) for j in range(n_fox)]
    W["s5_w_glu"], W["s5_w_glu_lo"] = hi_lo(s5_w_glu, n_lo["s5"])
    W["s5_consts"] = [s5_constants(s5_a_re[j], s5_a_im[j], s5_b_re[j], s5_b_im[j], s5_c_re[j], s5_c_im[j],
                                   s5_d[j], s5_log_dt[j]) for j in range(n_s5)]
    W["gdn_w_qkvz"] = gdn_w_in[:, :, :qkv_dim + val_dim].astype(BF16)
    W["gdn_w_ba"] = [_pad_cols(gdn_w_in[j, :, qkv_dim + val_dim:]) for j in range(n_gdn)]
    W["gdn_aux"] = [_aux_rows((n_vh, gdn_dt_bias[j]), (n_vh, gdn_a_log[j])) for j in range(n_gdn)]
    W["gdn_conv_w8"] = jnp.pad(gdn_conv_w.astype(F32), ((0, 0), (0, SUBLANES - cw), (0, 0)))
    W["gdn_w_out"] = gdn_w_out.astype(BF16)
    W["ffn_w_gate"], W["ffn_w_gate_lo"] = hi_lo(ffn_w_gate, n_lo["ffn"])
    W["ffn_w_up"], W["ffn_w_up_lo"] = hi_lo(ffn_w_up, n_lo["ffn"])
    W["ffn_w_down"], W["ffn_w_down_lo"] = hi_lo(ffn_w_down, n_lo["ffn"])
    W["moe_w_router"] = [_pad_cols(moe_w_router[m]) for m in range(moe_w_router.shape[0])]
    W["moe_w_gate"] = moe_w_gate.astype(BF16)
    W["moe_w_up"] = moe_w_up.astype(BF16)
    W["moe_w_down"] = moe_w_down.astype(BF16).reshape(-1, E * Fe, D)
    W["ple_w_gate"], W["ple_w_gate_lo"] = hi_lo(ple_w_gate, n_lo["ple"])
    W["ple_w_up"], W["ple_w_up_lo"] = hi_lo(ple_w_up, n_lo["ple"])

    steps = [(i, part) for i in range(depth) for part in ("mixer", "channel")]
    steps_a, steps_b = steps[:2 * first_router + 1], steps[2 * first_router + 1:]
    pd = p_prompt.shape[-1]

    Lt = LANES
    Lh = L - Lt
    p_all = {i: p_prompt[i].reshape(B * L, pd) for i in range(depth)}
    p_tail = {i: p_prompt[i][:, Lh:].reshape(B * Lt, pd) for i in range(depth)}
    zeros_s5 = {j: (jnp.zeros((B, G * NS), F32),) * 2 for j in range(n_s5)}
    x_f, out_p = _trunk(x_prompt.reshape(B * L, D), p_all, {}, zeros_s5, {}, W,
                        n_seq=B, seq_len=L, steps=steps_a, precise=False, snap_steps=Lh)
    hist_t = {}
    for j in range(n_fox):
        if ("fox", j) in out_p:
            qkv_f = out_p[("fox_qkv", j)].reshape(B, L, 3 * D)
            hist_t[j] = (qkv_f, qkv_f, H, 2 * H, 0, out_p[("fox", j)].reshape(B, L, H)[:, :Lh], Lh)
    xt, out_t = _trunk(x_prompt[:, Lh:].reshape(B * Lt, D), p_tail, hist_t,
                       {j: out_p[("s5_snap", j)] for j in range(n_s5) if ("s5_snap", j) in out_p}, {}, W,
                       n_seq=B, seq_len=Lt, steps=steps_a, precise=True)
    x_p = lax.dynamic_update_slice(x_f.reshape(B, L, D), xt.reshape(B, Lt, D), (0, Lh, 0)).reshape(B * L, D)
    out_p.update({key: val for key, val in out_t.items() if key[0] == "s5"})
    gdn0 = (jnp.zeros((B, n_vh, gd, gd), F32), jnp.zeros((B, cw - 1, qkv_dim), F32))
    x_p, out_b = _trunk(x_p, p_all, {}, zeros_s5, {j: gdn0 for j in range(n_gdn)}, W,
                        n_seq=B, seq_len=L, steps=steps_b, precise=False, fox_kv=out_p.get("fox_kv"))
    out_p.update(out_b)

    ck = cache_fox_k.reshape(-1, P * H, hd)
    cv = cache_fox_v.reshape(-1, P * H, hd)
    s5_re = state_s5_re.reshape(n_s5, Bs, G * NS).astype(F32)
    s5_im = state_s5_im.reshape(n_s5, Bs, G * NS).astype(F32)
    hist_s = {j: (ck, cv, None, None, j * Bs, cache_fox_logf[j], P) for j in range(n_fox)}
    state_s = ({j: (s5_re[j], s5_im[j]) for j in range(n_s5)},
               {j: (state_gdn[j], state_gdn_conv[j]) for j in range(n_gdn)})
    p_s = {i: p_sample[i].reshape(Bs * Ls, pd) for i in range(depth)}
    x_s, out_s = _trunk(x_sample.reshape(Bs * Ls, D), p_s, hist_s, *state_s, W,
                        n_seq=Bs, seq_len=Ls, steps=steps_a, precise=True)
    x_s, out_sb = _trunk(x_s, p_s, hist_s, *state_s, W, n_seq=Bs, seq_len=Ls, steps=steps_b, precise=False,
                         fox_kv=out_s.get("fox_kv"))
    out_s.update(out_sb)

    def leaves(x, out, nb, sl):
        y = rmsnorm(x, norm_final).reshape(nb, sl, D)
        fk, fv = (a.reshape(n_fox, nb, sl, H, hd) for a in out["fox_kv"])
        flf = jnp.stack([out[("fox", j)].reshape(nb, sl, H) for j in range(n_fox)])
        sre = jnp.stack([out[("s5", j)][0].reshape(nb, G, NS) for j in range(n_s5)])
        sim = jnp.stack([out[("s5", j)][1].reshape(nb, G, NS) for j in range(n_s5)])
        gs = jnp.stack([out[("gdn", j)][0] for j in range(n_gdn)])
        gc = jnp.stack([out[("gdn", j)][1] for j in range(n_gdn)])
        return y, fk, fv, flf, sre, sim, gs, gc

    (y_p, fk_p, fv_p, flf_p, sre_p, sim_p, gs_p, gc_p) = leaves(x_p, out_p, B, L)
    (y_s, fk_s, fv_s, flf_s, sre_s, sim_s, gs_s, gc_s) = leaves(x_s, out_s, Bs, Ls)
    return (y_p, y_s, fk_p, fv_p, flf_p, fk_s, fv_s, flf_s, sre_p, sim_p, sre_s, sim_s,
            gs_p, gc_p, gs_s, gc_s)
```
